```python
import jax
import jax.numpy as jnp
from jax import lax
import numpy as np


D_MODEL = 1024
BATCH = 2
SEQ = 8192
DEPTH = 1
DEC_BATCH = 128
DEC_SEQ = 8
PAST_LEN = 16384
PAGE_SIZE = 128

D_ATTN = D_MODEL // 2
D_CONV = D_MODEL - D_ATTN
HEAD_DIM = 64
N_HEADS = D_ATTN // HEAD_DIM
N_KV_HEADS = 2
KV_DIM = N_KV_HEADS * HEAD_DIM
WINDOW = 128
BLOCK = 128
CONV_WIDTH = 31
CONV_GROUPS = 8
MEM_LEN = 256
MEM_HEADS = 4
MEM_HEAD_DIM = D_MODEL // MEM_HEADS
N_GROUPS = 4
EXPERTS_PER_GROUP = 8
N_EXPERTS = N_GROUPS * EXPERTS_PER_GROUP
TOP_K = 2
D_EXPERT = D_MODEL // 4
D_IN = D_ATTN + 2 * KV_DIM + 2 * D_CONV
EPS = 1e-6
NEG_INF = -1e30

kernel_name = 'hymba_swa_conformer_hmoe_step'


def rmsnorm(x, g):
    xf = x.astype(jnp.float32)
    y = xf * lax.rsqrt(jnp.mean(xf * xf, axis=-1, keepdims=True) + EPS)
    return (y * g.astype(jnp.float32)).astype(x.dtype)


def layernorm(x, g, b):
    xf = x.astype(jnp.float32)
    mu = jnp.mean(xf, axis=-1, keepdims=True)
    xc = xf - mu
    y = xc * lax.rsqrt(jnp.mean(xc * xc, axis=-1, keepdims=True) + EPS)
    return (y * g.astype(jnp.float32) + b.astype(jnp.float32)).astype(x.dtype)


def alibi_slopes(n):
    return jnp.exp2(-8.0 * jnp.arange(1, n + 1, dtype=jnp.float32) / n)


def mixer_projections(x, norm_g, w_in, q_norm_g, k_norm_g):
    h = rmsnorm(x, norm_g)
    p = h @ w_in
    q, k, v, ua, ub = jnp.split(p, [D_ATTN, D_ATTN + KV_DIM, D_ATTN + 2 * KV_DIM, D_ATTN + 2 * KV_DIM + D_CONV], axis=-1)
    lead = x.shape[:-1]
    q = rmsnorm(q.reshape(*lead, N_HEADS, HEAD_DIM), q_norm_g)
    k = rmsnorm(k.reshape(*lead, N_KV_HEADS, HEAD_DIM), k_norm_g)
    v = v.reshape(*lead, N_KV_HEADS, HEAD_DIM)
    u = ua * jax.nn.sigmoid(ub)
    return q, k, v, u


def swa_attend(q, k, v, q_pos, k_pos, sinks):
    *lead, tq, h, d = q.shape
    kvh = k.shape[-2]
    g = h // kvh
    qg = q.reshape(*lead, tq, kvh, g, d)
    s = jnp.einsum('...qhgd,...shd->...hgqs', qg, k).astype(jnp.float32) * (d ** -0.5)
    dist = q_pos[..., :, None] - k_pos[..., None, :]
    mask = (dist >= 0) & (dist < WINDOW) & (k_pos[..., None, :] >= 0)
    slopes = alibi_slopes(h).reshape(kvh, g, 1, 1)
    s = s - slopes * dist[..., None, None, :, :].astype(jnp.float32)
    s = jnp.where(mask[..., None, None, :, :], s, NEG_INF)
    sink = sinks.astype(jnp.float32).reshape(kvh, g, 1, 1)
    m = jnp.maximum(jnp.max(s, axis=-1, keepdims=True), sink)
    p = jnp.exp(s - m)
    denom = jnp.sum(p, axis=-1, keepdims=True) + jnp.exp(sink - m)
    o = jnp.einsum('...hgqs,...shd->...qhgd', (p / denom).astype(v.dtype), v)
    return o.reshape(*lead, tq, h * d)


def swa_prompt(q, k, v, sinks):
    b, s = q.shape[:2]
    nb = s // BLOCK
    qb = q.reshape(b, nb, BLOCK, N_HEADS, HEAD_DIM)
    kb = k.reshape(b, nb, BLOCK, N_KV_HEADS, HEAD_DIM)
    vb = v.reshape(b, nb, BLOCK, N_KV_HEADS, HEAD_DIM)
    pad = ((0, 0), (1, 0), (0, 0), (0, 0), (0, 0))
    k_band = jnp.concatenate([jnp.pad(kb[:, :-1], pad), kb], axis=2)
    v_band = jnp.concatenate([jnp.pad(vb[:, :-1], pad), vb], axis=2)
    starts = jnp.arange(nb, dtype=jnp.int32)[:, None] * BLOCK
    q_pos = starts + jnp.arange(BLOCK, dtype=jnp.int32)[None, :]
    k_pos = starts - BLOCK + jnp.arange(2 * BLOCK, dtype=jnp.int32)[None, :]
    o = swa_attend(qb, k_band, v_band, q_pos, k_pos, sinks)
    return o.reshape(b, s, D_ATTN)


def conv_module(u, past, dw_w, dw_b, ln_g, ln_b):
    full = jnp.concatenate([past, u], axis=1)
    y = lax.conv_general_dilated(full, dw_w[:, None, :], window_strides=(1,), padding='VALID',
                                 dimension_numbers=('NWC', 'WIO', 'NWC'), feature_group_count=D_CONV)
    y = jax.nn.silu(layernorm(y + dw_b, ln_g, ln_b))
    return y, full[:, -(CONV_WIDTH - 1):]


def memory_kv(mem, norm_mem_g, w_mk, w_mv, mk_norm_g):
    h = rmsnorm(mem, norm_mem_g)
    n, m = mem.shape[:2]
    k = rmsnorm((h @ w_mk).reshape(n, m, MEM_HEADS, MEM_HEAD_DIM), mk_norm_g)
    v = (h @ w_mv).reshape(n, m, MEM_HEADS, MEM_HEAD_DIM)
    return k, v


def cross_attend(x, mem_k, mem_v, norm_xa_g, w_mq, mq_norm_g, w_mo):
    h = rmsnorm(x, norm_xa_g)
    n, t = x.shape[:2]
    q = rmsnorm((h @ w_mq).reshape(n, t, MEM_HEADS, MEM_HEAD_DIM), mq_norm_g)
    s = jnp.einsum('nqhd,nmhd->nhqm', q, mem_k).astype(jnp.float32) * (MEM_HEAD_DIM ** -0.5)
    p = jax.nn.softmax(s, axis=-1).astype(mem_v.dtype)
    o = jnp.einsum('nhqm,nmhd->nqhd', p, mem_v).reshape(n, t, MEM_HEADS * MEM_HEAD_DIM)
    return o @ w_mo


def hier_moe(h, w_rg, b_rg, w_re, b_re, w_gate, w_up, w_down):
    shp = h.shape
    t = h.reshape(-1, D_MODEL)
    lg = (t @ w_rg).astype(jnp.float32) + b_rg.astype(jnp.float32)
    pg = jax.nn.softmax(lg, axis=-1)
    gsel = jnp.argmax(lg, axis=-1)
    pg_sel = jnp.take_along_axis(pg, gsel[:, None], axis=-1)
    le = ((t @ w_re).astype(jnp.float32) + b_re.astype(jnp.float32)).reshape(-1, N_GROUPS, EXPERTS_PER_GROUP)
    le_sel = jnp.take_along_axis(le, gsel[:, None, None], axis=1)[:, 0]
    top_v, top_i = lax.top_k(le_sel, TOP_K)
    gate = jax.nn.softmax(top_v, axis=-1) * pg_sel
    eidx = gsel[:, None] * EXPERTS_PER_GROUP + top_i
    comb = jnp.einsum('tk,tke->te', gate, jax.nn.one_hot(eidx, N_EXPERTS, dtype=jnp.float32))
    a = jnp.einsum('td,edf->tef', t, w_gate)
    b = jnp.einsum('td,edf->tef', t, w_up)
    act = jax.nn.silu(a) * b * comb[:, :, None].astype(t.dtype)
    y = jnp.einsum('tef,efd->td', act, w_down)
    return y.reshape(shp)


def layer_tail(x, attn, conv, mem_k, mem_v, w_out, norm_xa_g, w_mq, mq_norm_g, w_mo,
               norm_ffn_g, w_rg, b_rg, w_re, b_re, w_gate, w_up, w_down):
    x = x + jnp.concatenate([attn, conv], axis=-1) @ w_out
    x = x + cross_attend(x, mem_k, mem_v, norm_xa_g, w_mq, mq_norm_g, w_mo)
    x = x + hier_moe(rmsnorm(x, norm_ffn_g), w_rg, b_rg, w_re, b_re, w_gate, w_up, w_down)
    return x


def setup_inputs(seed: int = 0) -> dict:
    key = jax.random.key(seed)
    ks = iter(jax.random.split(key, 40))
    f32 = jnp.float32

    def nrm(shape, scale):
        return jax.random.normal(next(ks), shape, f32) * scale

    def gain(shape):
        return 1.0 + nrm(shape, 0.01)

    w_buf = min(WINDOW, PAST_LEN)
    return {
        'x_prompt': nrm((BATCH, SEQ, D_MODEL), 1.0),
        'x_sample': nrm((DEC_BATCH, DEC_SEQ, D_MODEL), 1.0),
        'cache_swa_k': nrm((DEPTH, DEC_BATCH, w_buf, N_KV_HEADS, HEAD_DIM), 1.0),
        'cache_swa_v': nrm((DEPTH, DEC_BATCH, w_buf, N_KV_HEADS, HEAD_DIM), 1.0),
        'state_conv': nrm((DEPTH, DEC_BATCH, CONV_WIDTH - 1, D_CONV), 0.5),
        'cache_mem_k': nrm((DEPTH, DEC_BATCH, MEM_LEN, MEM_HEADS, MEM_HEAD_DIM), 1.0),
        'cache_mem_v': nrm((DEPTH, DEC_BATCH, MEM_LEN, MEM_HEADS, MEM_HEAD_DIM), 1.0),
        'mem_prompt': nrm((BATCH, MEM_LEN, D_MODEL), 1.0),
        'norm_mix_g': gain((DEPTH, D_MODEL)),
        'w_in': nrm((DEPTH, D_MODEL, D_IN), D_MODEL ** -0.5),
        'q_norm_g': gain((DEPTH, HEAD_DIM)),
        'k_norm_g': gain((DEPTH, HEAD_DIM)),
        'attn_sinks': nrm((DEPTH, N_HEADS), 0.5),
        'conv_dw_w': nrm((DEPTH, CONV_WIDTH, D_CONV), CONV_WIDTH ** -0.5),
        'conv_dw_b': nrm((DEPTH, D_CONV), 0.01),
        'conv_ln_g': gain((DEPTH, D_CONV)),
        'conv_ln_b': nrm((DEPTH, D_CONV), 0.01),
        'w_out': nrm((DEPTH, D_ATTN + D_CONV, D_MODEL), (D_ATTN + D_CONV) ** -0.5),
        'norm_xa_g': gain((DEPTH, D_MODEL)),
        'norm_mem_g': gain((DEPTH, D_MODEL)),
        'w_mq': nrm((DEPTH, D_MODEL, MEM_HEADS * MEM_HEAD_DIM), D_MODEL ** -0.5),
        'w_mk': nrm((DEPTH, D_MODEL, MEM_HEADS * MEM_HEAD_DIM), D_MODEL ** -0.5),
        'w_mv': nrm((DEPTH, D_MODEL, MEM_HEADS * MEM_HEAD_DIM), D_MODEL ** -0.5),
        'mq_norm_g': gain((DEPTH, MEM_HEAD_DIM)),
        'mk_norm_g': gain((DEPTH, MEM_HEAD_DIM)),
        'w_mo': nrm((DEPTH, MEM_HEADS * MEM_HEAD_DIM, D_MODEL), (MEM_HEADS * MEM_HEAD_DIM) ** -0.5),
        'norm_ffn_g': gain((DEPTH, D_MODEL)),
        'w_router_group': nrm((DEPTH, D_MODEL, N_GROUPS), D_MODEL ** -0.5),
        'b_router_group': nrm((DEPTH, N_GROUPS), 0.01),
        'w_router_expert': nrm((DEPTH, D_MODEL, N_EXPERTS), D_MODEL ** -0.5),
        'b_router_expert': nrm((DEPTH, N_EXPERTS), 0.01),
        'w_exp_gate': nrm((DEPTH, N_EXPERTS, D_MODEL, D_EXPERT), D_MODEL ** -0.5),
        'w_exp_up': nrm((DEPTH, N_EXPERTS, D_MODEL, D_EXPERT), D_MODEL ** -0.5),
        'w_exp_down': nrm((DEPTH, N_EXPERTS, D_EXPERT, D_MODEL), D_EXPERT ** -0.5),
    }


def reference(x_prompt, x_sample, cache_swa_k, cache_swa_v, state_conv, cache_mem_k, cache_mem_v, mem_prompt,
              norm_mix_g, w_in, q_norm_g, k_norm_g, attn_sinks, conv_dw_w, conv_dw_b, conv_ln_g, conv_ln_b, w_out,
              norm_xa_g, norm_mem_g, w_mq, w_mk, w_mv, mq_norm_g, mk_norm_g, w_mo,
              norm_ffn_g, w_router_group, b_router_group, w_router_expert, b_router_expert,
              w_exp_gate, w_exp_up, w_exp_down):
    xp = x_prompt
    xs = x_sample
    w_buf = cache_swa_k.shape[2]
    wp = min(WINDOW, xp.shape[1])
    s_pos = PAST_LEN + jnp.arange(xs.shape[1], dtype=jnp.int32)
    kv_pos = jnp.concatenate([PAST_LEN - w_buf + jnp.arange(w_buf, dtype=jnp.int32), s_pos])
    kp_l, vp_l, cp_l, mkp_l, mvp_l, ks_l, vs_l, cs_l = [], [], [], [], [], [], [], []
    for l in range(DEPTH):
        tail = (w_out[l], norm_xa_g[l], w_mq[l], mq_norm_g[l], w_mo[l], norm_ffn_g[l],
                w_router_group[l], b_router_group[l], w_router_expert[l], b_router_expert[l],
                w_exp_gate[l], w_exp_up[l], w_exp_down[l])
        conv_p = (conv_dw_w[l], conv_dw_b[l], conv_ln_g[l], conv_ln_b[l])

        q, k, v, u = mixer_projections(xp, norm_mix_g[l], w_in[l], q_norm_g[l], k_norm_g[l])
        attn = swa_prompt(q, k, v, attn_sinks[l])
        conv0 = jnp.zeros((xp.shape[0], CONV_WIDTH - 1, D_CONV), u.dtype)
        conv, conv_new = conv_module(u, conv0, *conv_p)
        mk, mv = memory_kv(mem_prompt, norm_mem_g[l], w_mk[l], w_mv[l], mk_norm_g[l])
        xp = layer_tail(xp, attn, conv, mk, mv, *tail)
        kp_l.append(k[:, -wp:])
        vp_l.append(v[:, -wp:])
        cp_l.append(conv_new)
        mkp_l.append(mk)
        mvp_l.append(mv)

        q, k, v, u = mixer_projections(xs, norm_mix_g[l], w_in[l], q_norm_g[l], k_norm_g[l])
        k_all = jnp.concatenate([cache_swa_k[l], k], axis=1)
        v_all = jnp.concatenate([cache_swa_v[l], v], axis=1)
        attn = swa_attend(q, k_all, v_all, s_pos, kv_pos, attn_sinks[l])
        conv, conv_new = conv_module(u, state_conv[l], *conv_p)
        xs = layer_tail(xs, attn, conv, cache_mem_k[l], cache_mem_v[l], *tail)
        ks_l.append(k_all[:, -w_buf:])
        vs_l.append(v_all[:, -w_buf:])
        cs_l.append(conv_new)

    swa_k_prompt = jnp.stack(kp_l, axis=0)
    swa_v_prompt = jnp.stack(vp_l, axis=0)
    conv_prompt = jnp.stack(cp_l, axis=0)
    mem_k_prompt = jnp.stack(mkp_l, axis=0)
    mem_v_prompt = jnp.stack(mvp_l, axis=0)
    swa_k_sample = jnp.stack(ks_l, axis=0)
    swa_v_sample = jnp.stack(vs_l, axis=0)
    conv_sample = jnp.stack(cs_l, axis=0)
    return (xp, xs, swa_k_prompt, swa_v_prompt, conv_prompt, mem_k_prompt, mem_v_prompt,
            swa_k_sample, swa_v_sample, conv_sample)
```

```python
import functools

import numpy as np
import jax
import jax.numpy as jnp
from jax import lax
from jax.experimental import pallas as pl
from jax.experimental.pallas import tpu as pltpu

F32 = jnp.float32
BF16 = jnp.bfloat16

D_MODEL = 1024
D_ATTN = 512
D_CONV = 512
HEAD_DIM = 64
N_HEADS = 8
N_KV_HEADS = 2
KV_DIM = N_KV_HEADS * HEAD_DIM
HEADS_PER_KV = N_HEADS // N_KV_HEADS
WINDOW = 128
BLOCK = 128
CONV_WIDTH = 31
CONV_PAST = CONV_WIDTH - 1
MEM_HEADS = 4
MEM_HEAD_DIM = 256
N_GROUPS = 4
EXPERTS_PER_GROUP = 8
N_EXPERTS = 32
D_EXPERT = 256
D_IN = D_ATTN + 2 * KV_DIM + 2 * D_CONV
EPS = 1e-6
NEG_INF = -1e30

LANES = 128
HALF = LANES // 2
CONV_PAD = 32
VMEM_LIMIT = 56 * 1024 * 1024

TM_PROMPT = 256
SEQ_PER_STEP = 16
MEM_SEQ_PER_STEP = 4
TM_MOE = 1024


def _rms(x, g):
    ms = jnp.mean(x * x, axis=-1, keepdims=True)
    return x * lax.rsqrt(ms + EPS) * g


def _group_mean_sq(x, bd_ref, width):
    x2 = x * x
    hi = x2.astype(BF16)
    lo = (x2 - hi.astype(F32)).astype(BF16)
    bd = bd_ref[...]
    s = jnp.dot(hi, bd, preferred_element_type=F32) + jnp.dot(lo, bd, preferred_element_type=F32)
    return s * (1.0 / width)


def _mixer_proj(x, ng_ref, win_ref, bdq_ref, bdk_ref, gqlo_ref, gqhi_ref, gk_ref):
    h = _rms(x, ng_ref[...]).astype(BF16)
    p = jnp.dot(h, win_ref[...], preferred_element_type=F32)
    q = p[:, :D_ATTN]
    k = p[:, D_ATTN:D_ATTN + KV_DIM]
    v = p[:, D_ATTN + KV_DIM:D_ATTN + 2 * KV_DIM]
    ua = p[:, D_ATTN + 2 * KV_DIM:D_ATTN + 2 * KV_DIM + D_CONV]
    ub = p[:, D_ATTN + 2 * KV_DIM + D_CONV:]
    qn = q * lax.rsqrt(_group_mean_sq(q, bdq_ref, HEAD_DIM) + EPS)
    q_lo = qn * gqlo_ref[...]
    q_hi = qn * gqhi_ref[...]
    kn = k * lax.rsqrt(_group_mean_sq(k, bdk_ref, HEAD_DIM) + EPS) * gk_ref[...]
    u = ua * jax.nn.sigmoid(ub)
    return q_lo, q_hi, kn, v, u


def _dup_halves(x):
    lo = lax.broadcasted_iota(jnp.int32, x.shape, 1) < HALF
    xr = pltpu.roll(x, HALF, axis=1)
    return jnp.where(lo, x, xr), jnp.where(lo, xr, x)


def _sink_softmax(s, sink):
    m = jnp.maximum(jnp.max(s, axis=-1, keepdims=True), sink)
    p = jnp.exp(s - m)
    denom = jnp.sum(p, axis=-1, keepdims=True) + jnp.exp(sink - m)
    return p * (1.0 / denom)


def _conv_ln_silu(y, cb_ref, lg_ref, lb_ref):
    y = y + cb_ref[...]
    mu = jnp.mean(y, axis=-1, keepdims=True)
    yc = y - mu
    yn = yc * lax.rsqrt(jnp.mean(yc * yc, axis=-1, keepdims=True) + EPS)
    z = yn * lg_ref[...] + lb_ref[...]
    return z * jax.nn.sigmoid(z)


def _mem_query(x1, xag_ref, wmq_ref, gmq_ref):
    h = _rms(x1, xag_ref[...]).astype(BF16)
    q = jnp.dot(h, wmq_ref[...], preferred_element_type=F32)
    parts = []
    for hd in range(MEM_HEADS):
        qh = q[:, hd * MEM_HEAD_DIM:(hd + 1) * MEM_HEAD_DIM]
        parts.append(qh * lax.rsqrt(jnp.mean(qh * qh, axis=-1, keepdims=True) + EPS))
    return jnp.concatenate(parts, axis=1) * gmq_ref[...]


def _mem_attend(q, mk, mv):
    outs = []
    for hd in range(MEM_HEADS):
        sl = slice(hd * MEM_HEAD_DIM, (hd + 1) * MEM_HEAD_DIM)
        s = lax.dot_general(q[:, sl].astype(BF16), mk[:, sl], (((1,), (1,)), ((), ())),
                            preferred_element_type=F32)
        m = jnp.max(s, axis=-1, keepdims=True)
        p = jnp.exp(s - m)
        p = p * (1.0 / jnp.sum(p, axis=-1, keepdims=True))
        outs.append(jnp.dot(p.astype(BF16), mv[:, sl], preferred_element_type=F32))
    return jnp.concatenate(outs, axis=1)


def _cross_tail(o, x1, wmo_ref, fg_ref, wr_ref, br_ref):
    x2 = x1 + jnp.dot(o.astype(BF16), wmo_ref[...], preferred_element_type=F32)
    hn = _rms(x2, fg_ref[...]).astype(BF16)
    logits = jnp.dot(hn, wr_ref[...], preferred_element_type=F32) + br_ref[...]
    lane = lax.broadcasted_iota(jnp.int32, logits.shape, 1)
    is_g = (lane >= N_EXPERTS) & (lane < N_EXPERTS + N_GROUPS)
    lg = jnp.where(is_g, logits, NEG_INF)
    gmax = jnp.max(lg, axis=-1, keepdims=True)
    gsel = jnp.min(jnp.where(lg == gmax, lane, 2 * LANES), axis=-1, keepdims=True) - N_EXPERTS
    pg_sel = 1.0 / jnp.sum(jnp.exp(lg - gmax), axis=-1, keepdims=True)
    in_grp = (lane >= gsel * EXPERTS_PER_GROUP) & (lane < (gsel + 1) * EXPERTS_PER_GROUP)
    le = jnp.where(in_grp, logits, NEG_INF)
    top1 = jnp.max(le, axis=-1, keepdims=True)
    idx1 = jnp.min(jnp.where(le == top1, lane, 2 * LANES), axis=-1, keepdims=True)
    le2 = jnp.where(lane == idx1, NEG_INF, le)
    top2 = jnp.max(le2, axis=-1, keepdims=True)
    idx2 = jnp.min(jnp.where(le2 == top2, lane, 2 * LANES), axis=-1, keepdims=True)
    e2 = jnp.exp(top2 - top1)
    inv = 1.0 / (1.0 + e2)
    gate1 = pg_sel * inv
    gate2 = pg_sel * (e2 * inv)
    comb = jnp.where(lane == idx1, gate1, 0.0) + jnp.where(lane == idx2, gate2, 0.0)
    return x2, hn, comb


def _mixer_prompt_kernel(sink_ref, x_ref, ng_ref, win_ref, bdq_ref, bdk_ref, gqlo_ref, gqhi_ref, gk_ref,
                         bias_ref, cw_ref, cb_ref, lg_ref, lb_ref, wout_ref,
                         x1_ref, ko_ref, vo_ref, co_ref,
                         kband, vband, uext, *, tm, nt):
    t = pl.program_id(1)

    @pl.when(t == 0)
    def _():
        kband[0:BLOCK, :] = jnp.zeros((BLOCK, KV_DIM), F32)
        vband[0:BLOCK, :] = jnp.zeros((BLOCK, KV_DIM), F32)
        uext[0:CONV_PAD, :] = jnp.zeros((CONV_PAD, D_CONV), F32)

    x = x_ref[...]
    q_lo, q_hi, kn, v, u = _mixer_proj(x, ng_ref, win_ref, bdq_ref, bdk_ref, gqlo_ref, gqhi_ref, gk_ref)
    kband[BLOCK:BLOCK + tm, :] = kn
    vband[BLOCK:BLOCK + tm, :] = v
    uext[CONV_PAD:CONV_PAD + tm, :] = u

    @pl.when(t == nt - 1)
    def _():
        ko_ref[...] = kn[tm - BLOCK:, :]
        vo_ref[...] = v[tm - BLOCK:, :]
        co_ref[...] = uext[pl.ds(CONV_PAD + tm - CONV_PAST, CONV_PAST), :]

    kd = [a.astype(BF16) for a in _dup_halves(kband[...])]
    vd = _dup_halves(vband[...])
    lo = lax.broadcasted_iota(jnp.int32, vd[0].shape, 1) < HALF
    v_lo = [jnp.where(lo, a, 0.0).astype(BF16) for a in vd]
    v_hi = [jnp.where(lo, 0.0, a).astype(BF16) for a in vd]
    q_lo = q_lo.astype(BF16)
    q_hi = q_hi.astype(BF16)
    attn_blocks = []
    for j in range(tm // BLOCK):
        rows = slice(j * BLOCK, (j + 1) * BLOCK)
        keys = slice(j * BLOCK, j * BLOCK + 2 * BLOCK)
        bias_base = jnp.where(t == 0, N_HEADS, 0) if j == 0 else 0
        tiles = []
        for c in range(N_KV_HEADS):
            q4 = jnp.concatenate(
                [(q_lo if (HEADS_PER_KV * c + a) % 2 == 0 else q_hi)[rows,
                  ((HEADS_PER_KV * c + a) // 2) * LANES:((HEADS_PER_KV * c + a) // 2 + 1) * LANES]
                 for a in range(HEADS_PER_KV)], axis=0)
            s_all = lax.dot_general(q4, kd[c][keys], (((1,), (1,)), ((), ())), preferred_element_type=F32)
            ps = []
            for a in range(HEADS_PER_KV):
                hd = HEADS_PER_KV * c + a
                s = s_all[a * BLOCK:(a + 1) * BLOCK] + bias_ref[bias_base + hd]
                ps.append(_sink_softmax(s, sink_ref[hd]).astype(BF16))
            vstack = jnp.concatenate([v_lo[c][keys], v_hi[c][keys]], axis=0)
            for i2 in range(HEADS_PER_KV // 2):
                pp = jnp.concatenate([ps[2 * i2], ps[2 * i2 + 1]], axis=1)
                tiles.append(jnp.dot(pp, vstack, preferred_element_type=F32))
        attn_blocks.append(jnp.concatenate(tiles, axis=1))
    attn = jnp.concatenate(attn_blocks, axis=0)

    off = CONV_PAD - CONV_PAST
    chunks = []
    for lc in range(D_CONV // LANES):
        ls = slice(lc * LANES, (lc + 1) * LANES)
        acc = jnp.zeros((tm, LANES), F32)
        for j in range(CONV_WIDTH):
            acc = acc + cw_ref[j:j + 1, ls] * uext[pl.ds(off + j, tm), ls]
        chunks.append(acc)
    y = _conv_ln_silu(jnp.concatenate(chunks, axis=1), cb_ref, lg_ref, lb_ref)

    x1_ref[...] = (x + jnp.dot(attn.astype(BF16), wout_ref[0:D_ATTN, :], preferred_element_type=F32)
                   + jnp.dot(y.astype(BF16), wout_ref[D_ATTN:, :], preferred_element_type=F32))

    kband[0:BLOCK, :] = kband[tm:tm + BLOCK, :]
    vband[0:BLOCK, :] = vband[tm:tm + BLOCK, :]
    uext[0:CONV_PAD, :] = uext[tm:tm + CONV_PAD, :]


def _const_spec(shape):
    nd = len(shape)
    return pl.BlockSpec(shape, lambda *_: (0,) * nd)


def _mixer_prompt(x, sinks, consts, tm):
    b, s, d = x.shape
    nt = s // tm
    (ng, win, bdq, bdk, gqlo, gqhi, gk, bias_p, cw, cb, lg, lb, wout) = consts
    kern = functools.partial(_mixer_prompt_kernel, tm=tm, nt=nt)
    cspecs = [_const_spec(a.shape) for a in consts]
    return pl.pallas_call(
        kern,
        grid=(b, nt),
        in_specs=[pl.BlockSpec(memory_space=pltpu.SMEM),
                  pl.BlockSpec((None, tm, d), lambda i, j: (i, j, 0))] + cspecs,
        out_specs=[pl.BlockSpec((None, tm, d), lambda i, j: (i, j, 0)),
                   pl.BlockSpec((None, BLOCK, KV_DIM), lambda i, j: (i, 0, 0)),
                   pl.BlockSpec((None, BLOCK, KV_DIM), lambda i, j: (i, 0, 0)),
                   pl.BlockSpec((None, CONV_PAST, D_CONV), lambda i, j: (i, 0, 0))],
        out_shape=[jax.ShapeDtypeStruct((b, s, d), F32),
                   jax.ShapeDtypeStruct((b, BLOCK, KV_DIM), F32),
                   jax.ShapeDtypeStruct((b, BLOCK, KV_DIM), F32),
                   jax.ShapeDtypeStruct((b, CONV_PAST, D_CONV), F32)],
        scratch_shapes=[pltpu.VMEM((BLOCK + tm, KV_DIM), F32),
                        pltpu.VMEM((BLOCK + tm, KV_DIM), F32),
                        pltpu.VMEM((CONV_PAD + tm, D_CONV), F32)],
        compiler_params=pltpu.CompilerParams(dimension_semantics=("arbitrary", "arbitrary"),
                                             vmem_limit_bytes=VMEM_LIMIT),
        name="mixer_prompt",
    )(sinks, x, *consts)


def _mixer_sample_kernel(x_ref, ck_ref, cv_ref, st_ref, ng_ref, win_ref, bdq_ref, bdk_ref, gqlo_ref, gqhi_ref,
                         gk_ref, bias_ref, sinkrow_ref, cw_ref, cb_ref, lg_ref, lb_ref, wout_ref,
                         xag_ref, wmq_ref, gmq_ref,
                         x1_ref, qm_ref, ko_ref, vo_ref, co_ref,
                         kall, vall, full, attn_scr, *, g, ts):
    wb = ck_ref.shape[1]
    x = x_ref[...]
    q_lo, q_hi, kn, v, u = _mixer_proj(x, ng_ref, win_ref, bdq_ref, bdk_ref, gqlo_ref, gqhi_ref, gk_ref)
    kall[:, 0:wb, :] = ck_ref[...]
    kall[:, wb:wb + ts, :] = kn.reshape(g, ts, KV_DIM)
    vall[:, 0:wb, :] = cv_ref[...]
    vall[:, wb:wb + ts, :] = v.reshape(g, ts, KV_DIM)
    ko_ref[...] = kall[:, ts:wb + ts, :]
    vo_ref[...] = vall[:, ts:wb + ts, :]
    full[:, 0:CONV_PAST, :] = st_ref[...]
    full[:, CONV_PAST:CONV_PAST + ts, :] = u.reshape(g, ts, D_CONV)
    co_ref[...] = full[:, ts:ts + CONV_PAST, :]

    lo8 = lax.broadcasted_iota(jnp.int32, (ts, LANES), 1) < HALF
    for n in range(g):
        rows = slice(n * ts, (n + 1) * ts)
        kd = [a.astype(BF16) for a in _dup_halves(kall[n])]
        vd = [a.astype(BF16) for a in _dup_halves(vall[n])]
        tiles = []
        for c in range(N_KV_HEADS):
            q4 = jnp.concatenate(
                [(q_lo if (HEADS_PER_KV * c + a) % 2 == 0 else q_hi)[rows,
                  ((HEADS_PER_KV * c + a) // 2) * LANES:((HEADS_PER_KV * c + a) // 2 + 1) * LANES]
                 for a in range(HEADS_PER_KV)], axis=0).astype(BF16)
            s = lax.dot_general(q4, kd[c], (((1,), (1,)), ((), ())), preferred_element_type=F32)
            p = _sink_softmax(s + bias_ref[c], sinkrow_ref[c]).astype(BF16)
            r = jnp.dot(p, vd[c], preferred_element_type=F32)
            for i2 in range(HEADS_PER_KV // 2):
                tiles.append(jnp.where(lo8, r[(2 * i2) * ts:(2 * i2 + 1) * ts],
                                       r[(2 * i2 + 1) * ts:(2 * i2 + 2) * ts]))
        attn_scr[rows, :] = jnp.concatenate(tiles, axis=1)

    acc = jnp.zeros((g, ts, D_CONV), F32)
    for j in range(CONV_WIDTH):
        acc = acc + cw_ref[j:j + 1, :] * full[:, j:j + ts, :]
    y = _conv_ln_silu(acc.reshape(g * ts, D_CONV), cb_ref, lg_ref, lb_ref)

    x1 = (x + jnp.dot(attn_scr[...].astype(BF16), wout_ref[0:D_ATTN, :], preferred_element_type=F32)
          + jnp.dot(y.astype(BF16), wout_ref[D_ATTN:, :], preferred_element_type=F32))
    x1_ref[...] = x1
    qm_ref[...] = _mem_query(x1, xag_ref, wmq_ref, gmq_ref)


def _mixer_sample(xs2d, ck, cv, st, consts, g, ts):
    n, d = xs2d.shape
    nb, wb = ck.shape[0], ck.shape[1]
    rows = g * ts
    cspecs = [_const_spec(a.shape) for a in consts]
    kern = functools.partial(_mixer_sample_kernel, g=g, ts=ts)
    return pl.pallas_call(
        kern,
        grid=(nb // g,),
        in_specs=[pl.BlockSpec((rows, d), lambda i: (i, 0)),
                  pl.BlockSpec((g, wb, KV_DIM), lambda i: (i, 0, 0)),
                  pl.BlockSpec((g, wb, KV_DIM), lambda i: (i, 0, 0)),
                  pl.BlockSpec((g, CONV_PAST, D_CONV), lambda i: (i, 0, 0))] + cspecs,
        out_specs=[pl.BlockSpec((rows, d), lambda i: (i, 0)),
                   pl.BlockSpec((rows, d), lambda i: (i, 0)),
                   pl.BlockSpec((g, wb, KV_DIM), lambda i: (i, 0, 0)),
                   pl.BlockSpec((g, wb, KV_DIM), lambda i: (i, 0, 0)),
                   pl.BlockSpec((g, CONV_PAST, D_CONV), lambda i: (i, 0, 0))],
        out_shape=[jax.ShapeDtypeStruct((n, d), F32),
                   jax.ShapeDtypeStruct((n, d), F32),
                   jax.ShapeDtypeStruct((nb, wb, KV_DIM), F32),
                   jax.ShapeDtypeStruct((nb, wb, KV_DIM), F32),
                   jax.ShapeDtypeStruct((nb, CONV_PAST, D_CONV), F32)],
        scratch_shapes=[pltpu.VMEM((g, wb + ts, KV_DIM), F32),
                        pltpu.VMEM((g, wb + ts, KV_DIM), F32),
                        pltpu.VMEM((g, CONV_PAST + ts + 2, D_CONV), F32),
                        pltpu.VMEM((rows, D_ATTN), F32)],
        compiler_params=pltpu.CompilerParams(dimension_semantics=("arbitrary",),
                                             vmem_limit_bytes=VMEM_LIMIT),
        name="mixer_sample",
    )(xs2d, ck, cv, st, *consts)


def _memory_kv_kernel(mem_ref, g_ref, wmk_ref, wmv_ref, gk_ref, k_ref, v_ref):
    h = _rms(mem_ref[...], g_ref[...]).astype(BF16)
    k = jnp.dot(h, wmk_ref[...], preferred_element_type=F32)
    parts = []
    for hd in range(MEM_HEADS):
        kh = k[:, hd * MEM_HEAD_DIM:(hd + 1) * MEM_HEAD_DIM]
        parts.append(kh * lax.rsqrt(jnp.mean(kh * kh, axis=-1, keepdims=True) + EPS))
    k_ref[...] = jnp.concatenate(parts, axis=1) * gk_ref[...]
    v_ref[...] = jnp.dot(h, wmv_ref[...], preferred_element_type=F32)


def _memory_kv(mem2d, g, wmk, wmv, gk, tm):
    n, d = mem2d.shape
    consts = (g, wmk, wmv, gk)
    return pl.pallas_call(
        _memory_kv_kernel,
        grid=(n // tm,),
        in_specs=[pl.BlockSpec((tm, d), lambda i: (i, 0))] + [_const_spec(a.shape) for a in consts],
        out_specs=[pl.BlockSpec((tm, d), lambda i: (i, 0)), pl.BlockSpec((tm, d), lambda i: (i, 0))],
        out_shape=[jax.ShapeDtypeStruct((n, d), F32), jax.ShapeDtypeStruct((n, d), F32)],
        compiler_params=pltpu.CompilerParams(dimension_semantics=("arbitrary",),
                                             vmem_limit_bytes=VMEM_LIMIT),
        name="memory_kv",
    )(mem2d, *consts)


def _cross_prompt_kernel(x1_ref, mk_ref, mv_ref, xag_ref, wmq_ref, gmq_ref, wmo_ref, fg_ref, wr_ref, br_ref,
                         x2_ref, hn_ref, comb_ref):
    x1 = x1_ref[...]
    q = _mem_query(x1, xag_ref, wmq_ref, gmq_ref)
    o = _mem_attend(q, mk_ref[...].astype(BF16), mv_ref[...].astype(BF16))
    x2, hn, comb = _cross_tail(o, x1, wmo_ref, fg_ref, wr_ref, br_ref)
    x2_ref[...] = x2
    hn_ref[...] = hn
    comb_ref[...] = comb


def _cross_prompt(x1, mk, mv, consts, tm):
    b, s, d = x1.shape
    m = mk.shape[1]
    cspecs = [_const_spec(a.shape) for a in consts]
    return pl.pallas_call(
        _cross_prompt_kernel,
        grid=(b, s // tm),
        in_specs=[pl.BlockSpec((None, tm, d), lambda i, j: (i, j, 0)),
                  pl.BlockSpec((None, m, d), lambda i, j: (i, 0, 0)),
                  pl.BlockSpec((None, m, d), lambda i, j: (i, 0, 0))] + cspecs,
        out_specs=[pl.BlockSpec((None, tm, d), lambda i, j: (i, j, 0)),
                   pl.BlockSpec((None, tm, d), lambda i, j: (i, j, 0)),
                   pl.BlockSpec((None, tm, LANES), lambda i, j: (i, j, 0))],
        out_shape=[jax.ShapeDtypeStruct((b, s, d), F32),
                   jax.ShapeDtypeStruct((b, s, d), BF16),
                   jax.ShapeDtypeStruct((b, s, LANES), F32)],
        compiler_params=pltpu.CompilerParams(dimension_semantics=("arbitrary", "arbitrary"),
                                             vmem_limit_bytes=VMEM_LIMIT),
        name="cross_prompt",
    )(x1, mk, mv, *consts)


def _mem_attend_sample_kernel(qm_ref, mk_ref, mv_ref, o_ref, *, g, ts):
    for n in range(g):
        o_ref[n * ts:(n + 1) * ts, :] = _mem_attend(qm_ref[n * ts:(n + 1) * ts, :],
                                                    mk_ref[n].astype(BF16), mv_ref[n].astype(BF16))


def _mem_attend_sample(qm, mk, mv, g, ts):
    n, d = qm.shape
    nb, m = mk.shape[0], mk.shape[1]
    kern = functools.partial(_mem_attend_sample_kernel, g=g, ts=ts)
    return pl.pallas_call(
        kern,
        grid=(nb // g,),
        in_specs=[pl.BlockSpec((g * ts, d), lambda i: (i, 0)),
                  pl.BlockSpec((g, m, d), lambda i: (i, 0, 0)),
                  pl.BlockSpec((g, m, d), lambda i: (i, 0, 0))],
        out_specs=pl.BlockSpec((g * ts, d), lambda i: (i, 0)),
        out_shape=jax.ShapeDtypeStruct((n, d), F32),
        compiler_params=pltpu.CompilerParams(dimension_semantics=("arbitrary",),
                                             vmem_limit_bytes=VMEM_LIMIT),
        name="mem_attend_sample",
    )(qm, mk, mv)


def _cross_tail_kernel(o_ref, x1_ref, wmo_ref, fg_ref, wr_ref, br_ref, x2_ref, hn_ref, comb_ref):
    x2, hn, comb = _cross_tail(o_ref[...], x1_ref[...], wmo_ref, fg_ref, wr_ref, br_ref)
    x2_ref[...] = x2
    hn_ref[...] = hn
    comb_ref[...] = comb


def _cross_tail_call(o, x1, consts, tm):
    n, d = x1.shape
    cspecs = [_const_spec(a.shape) for a in consts]
    return pl.pallas_call(
        _cross_tail_kernel,
        grid=(n // tm,),
        in_specs=[pl.BlockSpec((tm, d), lambda i: (i, 0)), pl.BlockSpec((tm, d), lambda i: (i, 0))] + cspecs,
        out_specs=[pl.BlockSpec((tm, d), lambda i: (i, 0)),
                   pl.BlockSpec((tm, d), lambda i: (i, 0)),
                   pl.BlockSpec((tm, LANES), lambda i: (i, 0))],
        out_shape=[jax.ShapeDtypeStruct((n, d), F32),
                   jax.ShapeDtypeStruct((n, d), BF16),
                   jax.ShapeDtypeStruct((n, LANES), F32)],
        compiler_params=pltpu.CompilerParams(dimension_semantics=("arbitrary",),
                                             vmem_limit_bytes=VMEM_LIMIT),
        name="cross_tail",
    )(o, x1, *consts)


def _moe_dense_kernel(hn_ref, comb_ref, x2_ref, wg_ref, wu_ref, wd_ref, out_ref):
    e = pl.program_id(1)

    @pl.when(e == 0)
    def _():
        out_ref[...] = x2_ref[...]

    hn = hn_ref[...]
    a = jnp.dot(hn, wg_ref[...], preferred_element_type=F32)
    b = jnp.dot(hn, wu_ref[...], preferred_element_type=F32)
    comb = comb_ref[...]
    lane = lax.broadcasted_iota(jnp.int32, comb.shape, 1)
    ce = jnp.sum(jnp.where(lane == e, comb, 0.0), axis=-1, keepdims=True)
    act = (a * jax.nn.sigmoid(a)) * b * ce
    out_ref[...] += jnp.dot(act.astype(BF16), wd_ref[...], preferred_element_type=F32)


def _moe_dense(hn, comb, x2, wg, wu, wd, tm):
    n, d = x2.shape
    ne, _, f = wg.shape
    return pl.pallas_call(
        _moe_dense_kernel,
        grid=(n // tm, ne),
        in_specs=[pl.BlockSpec((tm, d), lambda i, e: (i, 0)),
                  pl.BlockSpec((tm, LANES), lambda i, e: (i, 0)),
                  pl.BlockSpec((tm, d), lambda i, e: (i, 0)),
                  pl.BlockSpec((None, d, f), lambda i, e: (e, 0, 0)),
                  pl.BlockSpec((None, d, f), lambda i, e: (e, 0, 0)),
                  pl.BlockSpec((None, f, d), lambda i, e: (e, 0, 0))],
        out_specs=pl.BlockSpec((tm, d), lambda i, e: (i, 0)),
        out_shape=jax.ShapeDtypeStruct((n, d), F32),
        compiler_params=pltpu.CompilerParams(dimension_semantics=("arbitrary", "arbitrary"),
                                             vmem_limit_bytes=VMEM_LIMIT),
        name="moe_dense",
    )(hn, comb, x2, wg, wu, wd)


def _block_diag(n, width):
    idx = np.arange(n) // width
    return jnp.asarray((idx[:, None] == idx[None, :]).astype(np.float32), dtype=BF16)


def _alibi_slopes():
    return np.exp2(-8.0 * np.arange(1, N_HEADS + 1, dtype=np.float32) / N_HEADS).astype(np.float32)


def _prompt_bias():
    i = np.arange(BLOCK)[:, None]
    s = np.arange(2 * BLOCK)[None, :]
    dist = (i + BLOCK - s).astype(np.float32)
    mask = (dist >= 0) & (dist < WINDOW)
    first = mask & (s >= BLOCK)
    slopes = _alibi_slopes()[:, None, None]
    reg = np.where(mask[None], -slopes * dist[None], np.float32(NEG_INF))
    fst = np.where(first[None], -slopes * dist[None], np.float32(NEG_INF))
    return jnp.asarray(np.concatenate([reg, fst], axis=0).astype(np.float32))


def _sample_bias(ts, wb):
    i = np.arange(ts)[:, None]
    s = np.arange(wb + ts)[None, :]
    dist = (i + wb - s).astype(np.float32)
    mask = (dist >= 0) & (dist < WINDOW)
    slopes = _alibi_slopes()[:, None, None]
    b = np.where(mask[None], -slopes * dist[None], np.float32(NEG_INF)).astype(np.float32)
    return jnp.asarray(b.reshape(N_KV_HEADS, HEADS_PER_KV * ts, wb + ts))


def kernel(x_prompt, x_sample, cache_swa_k, cache_swa_v, state_conv, cache_mem_k, cache_mem_v, mem_prompt, norm_mix_g, w_in, q_norm_g, k_norm_g, attn_sinks, conv_dw_w, conv_dw_b, conv_ln_g, conv_ln_b, w_out, norm_xa_g, norm_mem_g, w_mq, w_mk, w_mv, mq_norm_g, mk_norm_g, w_mo, norm_ffn_g, w_router_group, b_router_group, w_router_expert, b_router_expert, w_exp_gate, w_exp_up, w_exp_down):
    depth = w_in.shape[0]
    bp, sp, d = x_prompt.shape
    nb, ts, _ = x_sample.shape
    wb = cache_swa_k.shape[2]
    mlen = mem_prompt.shape[1]
    assert d == D_MODEL and wb == WINDOW and sp % TM_PROMPT == 0 and nb % SEQ_PER_STEP == 0

    bdq = _block_diag(D_ATTN, HEAD_DIM)
    bdk = _block_diag(KV_DIM, HEAD_DIM)
    bias_p = _prompt_bias()
    bias_s = _sample_bias(ts, wb)
    lane_lo = (np.arange(D_ATTN) % LANES) < HALF
    row = lambda a: a.reshape(1, -1).astype(F32)

    xp = x_prompt
    xs = x_sample.reshape(nb * ts, d)
    kp_l, vp_l, cp_l, mkp_l, mvp_l, ks_l, vs_l, cs_l = [], [], [], [], [], [], [], []
    for l in range(depth):
        gq = jnp.tile(q_norm_g[l].astype(F32), N_HEADS) * (HEAD_DIM ** -0.5)
        gqlo = jnp.where(lane_lo, gq, 0.0).reshape(1, -1)
        gqhi = jnp.where(lane_lo, 0.0, gq).reshape(1, -1)
        gk = jnp.tile(k_norm_g[l].astype(F32), N_KV_HEADS).reshape(1, -1)
        sinks = attn_sinks[l].astype(F32)
        sinkrow = jnp.repeat(sinks, ts).reshape(N_KV_HEADS, HEADS_PER_KV * ts, 1)
        win = w_in[l].astype(BF16)
        wout = w_out[l].astype(BF16)
        wmq = w_mq[l].astype(BF16)
        wmo = w_mo[l].astype(BF16)
        gmq = (jnp.tile(mq_norm_g[l].astype(F32), MEM_HEADS) * (MEM_HEAD_DIM ** -0.5)).reshape(1, -1)
        gmk = jnp.tile(mk_norm_g[l].astype(F32), MEM_HEADS).reshape(1, -1)
        w_r = jnp.concatenate([w_router_expert[l], w_router_group[l],
                               jnp.zeros((d, LANES - N_EXPERTS - N_GROUPS), F32)], axis=1).astype(BF16)
        b_r = jnp.concatenate([b_router_expert[l], b_router_group[l],
                               jnp.zeros((LANES - N_EXPERTS - N_GROUPS,), F32)]).reshape(1, -1).astype(F32)
        wg = w_exp_gate[l].astype(BF16)
        wu = w_exp_up[l].astype(BF16)
        wd = w_exp_down[l].astype(BF16)

        mix_consts = (row(norm_mix_g[l]), win, bdq, bdk, gqlo, gqhi, gk)
        conv_consts = (conv_dw_w[l].astype(F32), row(conv_dw_b[l]), row(conv_ln_g[l]), row(conv_ln_b[l]), wout)
        tail_consts = (wmo, row(norm_ffn_g[l]), w_r, b_r)

        x1p, kp, vp, cp = _mixer_prompt(xp, sinks, mix_consts + (bias_p,) + conv_consts, TM_PROMPT)
        mk, mv = _memory_kv(mem_prompt.reshape(bp * mlen, d), row(norm_mem_g[l]),
                            w_mk[l].astype(BF16), w_mv[l].astype(BF16), gmk, min(256, bp * mlen))
        mk = mk.reshape(bp, mlen, d)
        mv = mv.reshape(bp, mlen, d)
        x2p, hnp, combp = _cross_prompt(x1p, mk, mv, (row(norm_xa_g[l]), wmq, gmq) + tail_consts, TM_PROMPT)
        tmo = min(TM_MOE, bp * sp)
        xp = _moe_dense(hnp.reshape(bp * sp, d), combp.reshape(bp * sp, LANES), x2p.reshape(bp * sp, d),
                        wg, wu, wd, tmo).reshape(bp, sp, d)
        kp_l.append(kp.reshape(bp, BLOCK, N_KV_HEADS, HEAD_DIM))
        vp_l.append(vp.reshape(bp, BLOCK, N_KV_HEADS, HEAD_DIM))
        cp_l.append(cp)
        mkp_l.append(mk.reshape(bp, mlen, MEM_HEADS, MEM_HEAD_DIM))
        mvp_l.append(mv.reshape(bp, mlen, MEM_HEADS, MEM_HEAD_DIM))

        x1s, qm, ksn, vsn, csn = _mixer_sample(
            xs, cache_swa_k[l].reshape(nb, wb, KV_DIM), cache_swa_v[l].reshape(nb, wb, KV_DIM), state_conv[l],
            mix_consts + (bias_s, sinkrow) + conv_consts + (row(norm_xa_g[l]), wmq, gmq), SEQ_PER_STEP, ts)
        o_s = _mem_attend_sample(qm, cache_mem_k[l].reshape(nb, mlen, d), cache_mem_v[l].reshape(nb, mlen, d),
                                 MEM_SEQ_PER_STEP, ts)
        tms = min(256, nb * ts)
        x2s, hns, combs = _cross_tail_call(o_s, x1s, tail_consts, tms)
        xs = _moe_dense(hns, combs, x2s, wg, wu, wd, min(TM_MOE, nb * ts))
        ks_l.append(ksn.reshape(nb, wb, N_KV_HEADS, HEAD_DIM))
        vs_l.append(vsn.reshape(nb, wb, N_KV_HEADS, HEAD_DIM))
        cs_l.append(csn)

    st = lambda xs_: jnp.stack(xs_, axis=0)
    return (xp, xs.reshape(nb, ts, d), st(kp_l), st(vp_l), st(cp_l), st(mkp_l), st(mvp_l),
            st(ks_l), st(vs_l), st(cs_l))
```

```python
import functools
import math

import numpy as np
import jax
import jax.numpy as jnp
from jax import lax
from jax.experimental import pallas as pl
from jax.experimental.pallas import tpu as pltpu
from jax.experimental.pallas import tpu_sc as plsc

F32 = jnp.float32
BF16 = jnp.bfloat16

D_MODEL = 1024
D_ATTN = 512
D_CONV = 512
HEAD_DIM = 64
N_HEADS = 8
N_KV_HEADS = 2
KV_DIM = N_KV_HEADS * HEAD_DIM
HEADS_PER_KV = N_HEADS // N_KV_HEADS
WINDOW = 128
BLOCK = 128
CONV_WIDTH = 31
CONV_PAST = CONV_WIDTH - 1
MEM_HEADS = 4
MEM_HEAD_DIM = 256
N_GROUPS = 4
EXPERTS_PER_GROUP = 8
N_EXPERTS = 32
D_EXPERT = 256
D_IN = D_ATTN + 2 * KV_DIM + 2 * D_CONV
EPS = 1e-6
NEG_INF = -1e30

LANES = 128
HALF = LANES // 2
CONV_PAD = 32
VMEM_LIMIT = 56 * 1024 * 1024

TM_PROMPT = 256
SEQ_PER_STEP = 16
MEM_SEQ_PER_STEP = 4
TOP_K = 2
SC_CORES = 2
SC_WORKERS = 32
SC_ROWS_PER_STEP = 64
TM_ROWS_PROMPT = 256
TM_ROWS_SAMPLE = 128


def _rms(x, g):
    ms = jnp.mean(x * x, axis=-1, keepdims=True)
    return x * lax.rsqrt(ms + EPS) * g


def _group_mean_sq(x, bd_ref, width):
    x2 = x * x
    hi = x2.astype(BF16)
    lo = (x2 - hi.astype(F32)).astype(BF16)
    bd = bd_ref[...]
    s = jnp.dot(hi, bd, preferred_element_type=F32) + jnp.dot(lo, bd, preferred_element_type=F32)
    return s * (1.0 / width)


def _mixer_proj(x, ng_ref, win_ref, bdq_ref, bdk_ref, gqlo_ref, gqhi_ref, gk_ref):
    h = _rms(x, ng_ref[...]).astype(BF16)
    p = jnp.dot(h, win_ref[...], preferred_element_type=F32)
    q = p[:, :D_ATTN]
    k = p[:, D_ATTN:D_ATTN + KV_DIM]
    v = p[:, D_ATTN + KV_DIM:D_ATTN + 2 * KV_DIM]
    ua = p[:, D_ATTN + 2 * KV_DIM:D_ATTN + 2 * KV_DIM + D_CONV]
    ub = p[:, D_ATTN + 2 * KV_DIM + D_CONV:]
    qn = q * lax.rsqrt(_group_mean_sq(q, bdq_ref, HEAD_DIM) + EPS)
    q_lo = qn * gqlo_ref[...]
    q_hi = qn * gqhi_ref[...]
    kn = k * lax.rsqrt(_group_mean_sq(k, bdk_ref, HEAD_DIM) + EPS) * gk_ref[...]
    u = ua * jax.nn.sigmoid(ub)
    return q_lo, q_hi, kn, v, u


def _dup_halves(x):
    lo = lax.broadcasted_iota(jnp.int32, x.shape, 1) < HALF
    xr = pltpu.roll(x, HALF, axis=1)
    return jnp.where(lo, x, xr), jnp.where(lo, xr, x)


def _sink_softmax(s, sink):
    m = jnp.maximum(jnp.max(s, axis=-1, keepdims=True), sink)
    p = jnp.exp(s - m)
    denom = jnp.sum(p, axis=-1, keepdims=True) + jnp.exp(sink - m)
    return p * (1.0 / denom)


def _conv_ln_silu(y, cb_ref, lg_ref, lb_ref):
    y = y + cb_ref[...]
    mu = jnp.mean(y, axis=-1, keepdims=True)
    yc = y - mu
    yn = yc * lax.rsqrt(jnp.mean(yc * yc, axis=-1, keepdims=True) + EPS)
    z = yn * lg_ref[...] + lb_ref[...]
    return z * jax.nn.sigmoid(z)


def _mem_query(x1, xag_ref, wmq_ref, gmq_ref):
    h = _rms(x1, xag_ref[...]).astype(BF16)
    q = jnp.dot(h, wmq_ref[...], preferred_element_type=F32)
    parts = []
    for hd in range(MEM_HEADS):
        qh = q[:, hd * MEM_HEAD_DIM:(hd + 1) * MEM_HEAD_DIM]
        parts.append(qh * lax.rsqrt(jnp.mean(qh * qh, axis=-1, keepdims=True) + EPS))
    return jnp.concatenate(parts, axis=1) * gmq_ref[...]


def _mem_attend(q, mk, mv):
    outs = []
    for hd in range(MEM_HEADS):
        sl = slice(hd * MEM_HEAD_DIM, (hd + 1) * MEM_HEAD_DIM)
        s = lax.dot_general(q[:, sl].astype(BF16), mk[:, sl], (((1,), (1,)), ((), ())),
                            preferred_element_type=F32)
        m = jnp.max(s, axis=-1, keepdims=True)
        p = jnp.exp(s - m)
        p = p * (1.0 / jnp.sum(p, axis=-1, keepdims=True))
        outs.append(jnp.dot(p.astype(BF16), mv[:, sl], preferred_element_type=F32))
    return jnp.concatenate(outs, axis=1)


def _pack_bf16_pairs(y):
    n = y.shape[1] // 2
    bits = pltpu.bitcast(y.astype(BF16).astype(F32), jnp.int32)
    return (bits[:, :n] & jnp.int32(-65536)) | lax.shift_right_logical(bits[:, n:], jnp.int32(16))


def _unpack_bf16_pairs(w):
    hi = pltpu.bitcast(w & jnp.int32(-65536), F32)
    lo = pltpu.bitcast(lax.shift_left(w, jnp.int32(16)), F32)
    return jnp.concatenate([hi, lo], axis=1)


def _cross_tail(o, x1, wmo_ref, fg_ref, wr_ref, br_ref, tri_ref, base):
    x2 = x1 + jnp.dot(o.astype(BF16), wmo_ref[...], preferred_element_type=F32)
    hn_f = _rms(x2, fg_ref[...])
    hn = hn_f.astype(BF16)
    logits = jnp.dot(hn, wr_ref[...], preferred_element_type=F32) + br_ref[...]
    lane = lax.broadcasted_iota(jnp.int32, logits.shape, 1)
    is_g = (lane >= N_EXPERTS) & (lane < N_EXPERTS + N_GROUPS)
    lg = jnp.where(is_g, logits, NEG_INF)
    gmax = jnp.max(lg, axis=-1, keepdims=True)
    gsel = jnp.min(jnp.where(lg == gmax, lane, 2 * LANES), axis=-1, keepdims=True) - N_EXPERTS
    pg_sel = 1.0 / jnp.sum(jnp.exp(lg - gmax), axis=-1, keepdims=True)
    in_grp = (lane >= gsel * EXPERTS_PER_GROUP) & (lane < (gsel + 1) * EXPERTS_PER_GROUP)
    le = jnp.where(in_grp, logits, NEG_INF)
    top1 = jnp.max(le, axis=-1, keepdims=True)
    idx1 = jnp.min(jnp.where(le == top1, lane, 2 * LANES), axis=-1, keepdims=True)
    le2 = jnp.where(lane == idx1, NEG_INF, le)
    top2 = jnp.max(le2, axis=-1, keepdims=True)
    idx2 = jnp.min(jnp.where(le2 == top2, lane, 2 * LANES), axis=-1, keepdims=True)
    e2 = jnp.exp(top2 - top1)
    inv = 1.0 / (1.0 + e2)
    gate1 = pg_sel * inv
    gate2 = pg_sel * (e2 * inv)
    used = jnp.where((lane == idx1) | (lane == idx2), 1.0, 0.0)
    before = jnp.dot(tri_ref[...], used.astype(BF16), preferred_element_type=F32) + base
    rank1 = jnp.sum(jnp.where(lane == idx1, before, 0.0), axis=-1, keepdims=True)
    rank2 = jnp.sum(jnp.where(lane == idx2, before, 0.0), axis=-1, keepdims=True)
    route = jnp.zeros_like(logits)
    for pos, val in enumerate((idx1.astype(F32), idx2.astype(F32), gate1, gate2, rank1, rank2)):
        route = jnp.where(lane == pos, val, route)
    return x2, _pack_bf16_pairs(hn_f), route, base + jnp.sum(used, axis=0, keepdims=True)


def _mixer_prompt_kernel(sink_ref, x_ref, ng_ref, win_ref, bdq_ref, bdk_ref, gqlo_ref, gqhi_ref, gk_ref,
                         bias_ref, cw_ref, cb_ref, lg_ref, lb_ref, wout_ref,
                         x1_ref, ko_ref, vo_ref, co_ref,
                         kband, vband, uext, *, tm, nt):
    t = pl.program_id(1)

    @pl.when(t == 0)
    def _():
        kband[0:BLOCK, :] = jnp.zeros((BLOCK, KV_DIM), F32)
        vband[0:BLOCK, :] = jnp.zeros((BLOCK, KV_DIM), F32)
        uext[0:CONV_PAD, :] = jnp.zeros((CONV_PAD, D_CONV), F32)

    x = x_ref[...]
    q_lo, q_hi, kn, v, u = _mixer_proj(x, ng_ref, win_ref, bdq_ref, bdk_ref, gqlo_ref, gqhi_ref, gk_ref)
    kband[BLOCK:BLOCK + tm, :] = kn
    vband[BLOCK:BLOCK + tm, :] = v
    uext[CONV_PAD:CONV_PAD + tm, :] = u

    @pl.when(t == nt - 1)
    def _():
        ko_ref[...] = kn[tm - BLOCK:, :]
        vo_ref[...] = v[tm - BLOCK:, :]
        co_ref[...] = uext[pl.ds(CONV_PAD + tm - CONV_PAST, CONV_PAST), :]

    kd = [a.astype(BF16) for a in _dup_halves(kband[...])]
    vd = _dup_halves(vband[...])
    lo = lax.broadcasted_iota(jnp.int32, vd[0].shape, 1) < HALF
    v_lo = [jnp.where(lo, a, 0.0).astype(BF16) for a in vd]
    v_hi = [jnp.where(lo, 0.0, a).astype(BF16) for a in vd]
    q_lo = q_lo.astype(BF16)
    q_hi = q_hi.astype(BF16)
    attn_blocks = []
    for j in range(tm // BLOCK):
        rows = slice(j * BLOCK, (j + 1) * BLOCK)
        keys = slice(j * BLOCK, j * BLOCK + 2 * BLOCK)
        bias_base = jnp.where(t == 0, N_HEADS, 0) if j == 0 else 0
        tiles = []
        for c in range(N_KV_HEADS):
            q4 = jnp.concatenate(
                [(q_lo if (HEADS_PER_KV * c + a) % 2 == 0 else q_hi)[rows,
                  ((HEADS_PER_KV * c + a) // 2) * LANES:((HEADS_PER_KV * c + a) // 2 + 1) * LANES]
                 for a in range(HEADS_PER_KV)], axis=0)
            s_all = lax.dot_general(q4, kd[c][keys], (((1,), (1,)), ((), ())), preferred_element_type=F32)
            ps = []
            for a in range(HEADS_PER_KV):
                hd = HEADS_PER_KV * c + a
                s = s_all[a * BLOCK:(a + 1) * BLOCK] + bias_ref[bias_base + hd]
                ps.append(_sink_softmax(s, sink_ref[hd]).astype(BF16))
            vstack = jnp.concatenate([v_lo[c][keys], v_hi[c][keys]], axis=0)
            for i2 in range(HEADS_PER_KV // 2):
                pp = jnp.concatenate([ps[2 * i2], ps[2 * i2 + 1]], axis=1)
                tiles.append(jnp.dot(pp, vstack, preferred_element_type=F32))
        attn_blocks.append(jnp.concatenate(tiles, axis=1))
    attn = jnp.concatenate(attn_blocks, axis=0)

    off = CONV_PAD - CONV_PAST
    chunks = []
    for lc in range(D_CONV // LANES):
        ls = slice(lc * LANES, (lc + 1) * LANES)
        acc = jnp.zeros((tm, LANES), F32)
        for j in range(CONV_WIDTH):
            acc = acc + cw_ref[j:j + 1, ls] * uext[pl.ds(off + j, tm), ls]
        chunks.append(acc)
    y = _conv_ln_silu(jnp.concatenate(chunks, axis=1), cb_ref, lg_ref, lb_ref)

    x1_ref[...] = (x + jnp.dot(attn.astype(BF16), wout_ref[0:D_ATTN, :], preferred_element_type=F32)
                   + jnp.dot(y.astype(BF16), wout_ref[D_ATTN:, :], preferred_element_type=F32))

    kband[0:BLOCK, :] = kband[tm:tm + BLOCK, :]
    vband[0:BLOCK, :] = vband[tm:tm + BLOCK, :]
    uext[0:CONV_PAD, :] = uext[tm:tm + CONV_PAD, :]


def _const_spec(shape):
    nd = len(shape)
    return pl.BlockSpec(shape, lambda *_: (0,) * nd)


def _mixer_prompt(x, sinks, consts, tm):
    b, s, d = x.shape
    nt = s // tm
    (ng, win, bdq, bdk, gqlo, gqhi, gk, bias_p, cw, cb, lg, lb, wout) = consts
    kern = functools.partial(_mixer_prompt_kernel, tm=tm, nt=nt)
    cspecs = [_const_spec(a.shape) for a in consts]
    return pl.pallas_call(
        kern,
        grid=(b, nt),
        in_specs=[pl.BlockSpec(memory_space=pltpu.SMEM),
                  pl.BlockSpec((None, tm, d), lambda i, j: (i, j, 0))] + cspecs,
        out_specs=[pl.BlockSpec((None, tm, d), lambda i, j: (i, j, 0)),
                   pl.BlockSpec((None, BLOCK, KV_DIM), lambda i, j: (i, 0, 0)),
                   pl.BlockSpec((None, BLOCK, KV_DIM), lambda i, j: (i, 0, 0)),
                   pl.BlockSpec((None, CONV_PAST, D_CONV), lambda i, j: (i, 0, 0))],
        out_shape=[jax.ShapeDtypeStruct((b, s, d), F32),
                   jax.ShapeDtypeStruct((b, BLOCK, KV_DIM), F32),
                   jax.ShapeDtypeStruct((b, BLOCK, KV_DIM), F32),
                   jax.ShapeDtypeStruct((b, CONV_PAST, D_CONV), F32)],
        scratch_shapes=[pltpu.VMEM((BLOCK + tm, KV_DIM), F32),
                        pltpu.VMEM((BLOCK + tm, KV_DIM), F32),
                        pltpu.VMEM((CONV_PAD + tm, D_CONV), F32)],
        compiler_params=pltpu.CompilerParams(dimension_semantics=("arbitrary", "arbitrary"),
                                             vmem_limit_bytes=VMEM_LIMIT),
        name="mixer_prompt",
    )(sinks, x, *consts)


def _mixer_sample_kernel(x_ref, ck_ref, cv_ref, st_ref, ng_ref, win_ref, bdq_ref, bdk_ref, gqlo_ref, gqhi_ref,
                         gk_ref, bias_ref, sinkrow_ref, cw_ref, cb_ref, lg_ref, lb_ref, wout_ref,
                         xag_ref, wmq_ref, gmq_ref,
                         x1_ref, qm_ref, ko_ref, vo_ref, co_ref,
                         kall, vall, full, attn_scr, *, g, ts):
    wb = ck_ref.shape[1]
    x = x_ref[...]
    q_lo, q_hi, kn, v, u = _mixer_proj(x, ng_ref, win_ref, bdq_ref, bdk_ref, gqlo_ref, gqhi_ref, gk_ref)
    kall[:, 0:wb, :] = ck_ref[...]
    kall[:, wb:wb + ts, :] = kn.reshape(g, ts, KV_DIM)
    vall[:, 0:wb, :] = cv_ref[...]
    vall[:, wb:wb + ts, :] = v.reshape(g, ts, KV_DIM)
    ko_ref[...] = kall[:, ts:wb + ts, :]
    vo_ref[...] = vall[:, ts:wb + ts, :]
    full[:, 0:CONV_PAST, :] = st_ref[...]
    full[:, CONV_PAST:CONV_PAST + ts, :] = u.reshape(g, ts, D_CONV)
    co_ref[...] = full[:, ts:ts + CONV_PAST, :]

    lo8 = lax.broadcasted_iota(jnp.int32, (ts, LANES), 1) < HALF
    for n in range(g):
        rows = slice(n * ts, (n + 1) * ts)
        kd = [a.astype(BF16) for a in _dup_halves(kall[n])]
        vd = [a.astype(BF16) for a in _dup_halves(vall[n])]
        tiles = []
        for c in range(N_KV_HEADS):
            q4 = jnp.concatenate(
                [(q_lo if (HEADS_PER_KV * c + a) % 2 == 0 else q_hi)[rows,
                  ((HEADS_PER_KV * c + a) // 2) * LANES:((HEADS_PER_KV * c + a) // 2 + 1) * LANES]
                 for a in range(HEADS_PER_KV)], axis=0).astype(BF16)
            s = lax.dot_general(q4, kd[c], (((1,), (1,)), ((), ())), preferred_element_type=F32)
            p = _sink_softmax(s + bias_ref[c], sinkrow_ref[c]).astype(BF16)
            r = jnp.dot(p, vd[c], preferred_element_type=F32)
            for i2 in range(HEADS_PER_KV // 2):
                tiles.append(jnp.where(lo8, r[(2 * i2) * ts:(2 * i2 + 1) * ts],
                                       r[(2 * i2 + 1) * ts:(2 * i2 + 2) * ts]))
        attn_scr[rows, :] = jnp.concatenate(tiles, axis=1)

    acc = jnp.zeros((g, ts, D_CONV), F32)
    for j in range(CONV_WIDTH):
        acc = acc + cw_ref[j:j + 1, :] * full[:, j:j + ts, :]
    y = _conv_ln_silu(acc.reshape(g * ts, D_CONV), cb_ref, lg_ref, lb_ref)

    x1 = (x + jnp.dot(attn_scr[...].astype(BF16), wout_ref[0:D_ATTN, :], preferred_element_type=F32)
          + jnp.dot(y.astype(BF16), wout_ref[D_ATTN:, :], preferred_element_type=F32))
    x1_ref[...] = x1
    qm_ref[...] = _mem_query(x1, xag_ref, wmq_ref, gmq_ref)


def _mixer_sample(xs2d, ck, cv, st, consts, g, ts):
    n, d = xs2d.shape
    nb, wb = ck.shape[0], ck.shape[1]
    rows = g * ts
    cspecs = [_const_spec(a.shape) for a in consts]
    kern = functools.partial(_mixer_sample_kernel, g=g, ts=ts)
    return pl.pallas_call(
        kern,
        grid=(nb // g,),
        in_specs=[pl.BlockSpec((rows, d), lambda i: (i, 0)),
                  pl.BlockSpec((g, wb, KV_DIM), lambda i: (i, 0, 0)),
                  pl.BlockSpec((g, wb, KV_DIM), lambda i: (i, 0, 0)),
                  pl.BlockSpec((g, CONV_PAST, D_CONV), lambda i: (i, 0, 0))] + cspecs,
        out_specs=[pl.BlockSpec((rows, d), lambda i: (i, 0)),
                   pl.BlockSpec((rows, d), lambda i: (i, 0)),
                   pl.BlockSpec((g, wb, KV_DIM), lambda i: (i, 0, 0)),
                   pl.BlockSpec((g, wb, KV_DIM), lambda i: (i, 0, 0)),
                   pl.BlockSpec((g, CONV_PAST, D_CONV), lambda i: (i, 0, 0))],
        out_shape=[jax.ShapeDtypeStruct((n, d), F32),
                   jax.ShapeDtypeStruct((n, d), F32),
                   jax.ShapeDtypeStruct((nb, wb, KV_DIM), F32),
                   jax.ShapeDtypeStruct((nb, wb, KV_DIM), F32),
                   jax.ShapeDtypeStruct((nb, CONV_PAST, D_CONV), F32)],
        scratch_shapes=[pltpu.VMEM((g, wb + ts, KV_DIM), F32),
                        pltpu.VMEM((g, wb + ts, KV_DIM), F32),
                        pltpu.VMEM((g, CONV_PAST + ts + 2, D_CONV), F32),
                        pltpu.VMEM((rows, D_ATTN), F32)],
        compiler_params=pltpu.CompilerParams(dimension_semantics=("arbitrary",),
                                             vmem_limit_bytes=VMEM_LIMIT),
        name="mixer_sample",
    )(xs2d, ck, cv, st, *consts)


def _memory_kv_kernel(mem_ref, g_ref, wmk_ref, wmv_ref, gk_ref, k_ref, v_ref):
    h = _rms(mem_ref[...], g_ref[...]).astype(BF16)
    k = jnp.dot(h, wmk_ref[...], preferred_element_type=F32)
    parts = []
    for hd in range(MEM_HEADS):
        kh = k[:, hd * MEM_HEAD_DIM:(hd + 1) * MEM_HEAD_DIM]
        parts.append(kh * lax.rsqrt(jnp.mean(kh * kh, axis=-1, keepdims=True) + EPS))
    k_ref[...] = jnp.concatenate(parts, axis=1) * gk_ref[...]
    v_ref[...] = jnp.dot(h, wmv_ref[...], preferred_element_type=F32)


def _memory_kv(mem2d, g, wmk, wmv, gk, tm):
    n, d = mem2d.shape
    consts = (g, wmk, wmv, gk)
    return pl.pallas_call(
        _memory_kv_kernel,
        grid=(n // tm,),
        in_specs=[pl.BlockSpec((tm, d), lambda i: (i, 0))] + [_const_spec(a.shape) for a in consts],
        out_specs=[pl.BlockSpec((tm, d), lambda i: (i, 0)), pl.BlockSpec((tm, d), lambda i: (i, 0))],
        out_shape=[jax.ShapeDtypeStruct((n, d), F32), jax.ShapeDtypeStruct((n, d), F32)],
        compiler_params=pltpu.CompilerParams(dimension_semantics=("arbitrary",),
                                             vmem_limit_bytes=VMEM_LIMIT),
        name="memory_kv",
    )(mem2d, *consts)


def _cross_prompt_kernel(x1_ref, mk_ref, mv_ref, xag_ref, wmq_ref, gmq_ref, wmo_ref, fg_ref, wr_ref, br_ref, tri_ref,
                         x2_ref, hn_ref, route_ref, cnt_ref, base):
    @pl.when((pl.program_id(0) == 0) & (pl.program_id(1) == 0))
    def _():
        base[...] = jnp.zeros_like(base)

    x1 = x1_ref[...]
    q = _mem_query(x1, xag_ref, wmq_ref, gmq_ref)
    o = _mem_attend(q, mk_ref[...].astype(BF16), mv_ref[...].astype(BF16))
    x2, hn, route, new_base = _cross_tail(o, x1, wmo_ref, fg_ref, wr_ref, br_ref, tri_ref, base[...])
    x2_ref[...] = x2
    hn_ref[...] = hn
    route_ref[...] = route
    base[...] = new_base
    cnt_ref[...] = new_base


def _cross_prompt(x1, mk, mv, consts, tm):
    b, s, d = x1.shape
    m = mk.shape[1]
    cspecs = [_const_spec(a.shape) for a in consts]
    return pl.pallas_call(
        _cross_prompt_kernel,
        grid=(b, s // tm),
        in_specs=[pl.BlockSpec((None, tm, d), lambda i, j: (i, j, 0)),
                  pl.BlockSpec((None, m, d), lambda i, j: (i, 0, 0)),
                  pl.BlockSpec((None, m, d), lambda i, j: (i, 0, 0))] + cspecs,
        out_specs=[pl.BlockSpec((None, tm, d), lambda i, j: (i, j, 0)),
                   pl.BlockSpec((None, tm, d // 2), lambda i, j: (i, j, 0)),
                   pl.BlockSpec((None, tm, LANES), lambda i, j: (i, j, 0)),
                   pl.BlockSpec((1, LANES), lambda i, j: (0, 0))],
        out_shape=[jax.ShapeDtypeStruct((b, s, d), F32),
                   jax.ShapeDtypeStruct((b, s, d // 2), jnp.int32),
                   jax.ShapeDtypeStruct((b, s, LANES), F32),
                   jax.ShapeDtypeStruct((1, LANES), F32)],
        scratch_shapes=[pltpu.VMEM((1, LANES), F32)],
        compiler_params=pltpu.CompilerParams(dimension_semantics=("arbitrary", "arbitrary"),
                                             vmem_limit_bytes=VMEM_LIMIT),
        name="cross_prompt",
    )(x1, mk, mv, *consts)


def _mem_attend_sample_kernel(qm_ref, mk_ref, mv_ref, o_ref, *, g, ts):
    for n in range(g):
        o_ref[n * ts:(n + 1) * ts, :] = _mem_attend(qm_ref[n * ts:(n + 1) * ts, :], mk_ref[n], mv_ref[n])


def _mem_attend_sample(qm, mk, mv, g, ts):
    n, d = qm.shape
    nb, m = mk.shape[0], mk.shape[1]
    kern = functools.partial(_mem_attend_sample_kernel, g=g, ts=ts)
    return pl.pallas_call(
        kern,
        grid=(nb // g,),
        in_specs=[pl.BlockSpec((g * ts, d), lambda i: (i, 0)),
                  pl.BlockSpec((g, m, d), lambda i: (i, 0, 0)),
                  pl.BlockSpec((g, m, d), lambda i: (i, 0, 0))],
        out_specs=pl.BlockSpec((g * ts, d), lambda i: (i, 0)),
        out_shape=jax.ShapeDtypeStruct((n, d), F32),
        compiler_params=pltpu.CompilerParams(dimension_semantics=("arbitrary",),
                                             vmem_limit_bytes=VMEM_LIMIT),
        name="mem_attend_sample",
    )(qm, mk, mv)


def _cross_tail_kernel(o_ref, x1_ref, wmo_ref, fg_ref, wr_ref, br_ref, tri_ref,
                       x2_ref, hn_ref, route_ref, cnt_ref, base):
    @pl.when(pl.program_id(0) == 0)
    def _():
        base[...] = jnp.zeros_like(base)

    x2, hn, route, new_base = _cross_tail(o_ref[...], x1_ref[...], wmo_ref, fg_ref, wr_ref, br_ref, tri_ref,
                                          base[...])
    x2_ref[...] = x2
    hn_ref[...] = hn
    route_ref[...] = route
    base[...] = new_base
    cnt_ref[...] = new_base


def _cross_tail_call(o, x1, consts, tm):
    n, d = x1.shape
    cspecs = [_const_spec(a.shape) for a in consts]
    return pl.pallas_call(
        _cross_tail_kernel,
        grid=(n // tm,),
        in_specs=[pl.BlockSpec((tm, d), lambda i: (i, 0)), pl.BlockSpec((tm, d), lambda i: (i, 0))] + cspecs,
        out_specs=[pl.BlockSpec((tm, d), lambda i: (i, 0)),
                   pl.BlockSpec((tm, d // 2), lambda i: (i, 0)),
                   pl.BlockSpec((tm, LANES), lambda i: (i, 0)),
                   pl.BlockSpec((1, LANES), lambda i: (0, 0))],
        out_shape=[jax.ShapeDtypeStruct((n, d), F32),
                   jax.ShapeDtypeStruct((n, d // 2), jnp.int32),
                   jax.ShapeDtypeStruct((n, LANES), F32),
                   jax.ShapeDtypeStruct((1, LANES), F32)],
        scratch_shapes=[pltpu.VMEM((1, LANES), F32)],
        compiler_params=pltpu.CompilerParams(dimension_semantics=("arbitrary",),
                                             vmem_limit_bytes=VMEM_LIMIT),
        name="cross_tail",
    )(o, x1, *consts)


def _sc_gather_rows(table, idx):
    nrows = idx.shape[0]
    _, width = table.shape
    per_worker = nrows // SC_WORKERS
    assert nrows % (8 * SC_WORKERS) == 0 and per_worker % SC_ROWS_PER_STEP == 0
    mesh = plsc.VectorSubcoreMesh(core_axis_name="c", subcore_axis_name="s")

    @functools.partial(
        pl.kernel, mesh=mesh, out_type=jax.ShapeDtypeStruct((nrows, width), table.dtype),
        scratch_types=[pltpu.VMEM((SC_ROWS_PER_STEP,), jnp.int32),
                       pltpu.VMEM((SC_ROWS_PER_STEP, width), table.dtype),
                       pltpu.SemaphoreType.DMA])
    def gather(table_hbm, idx_hbm, out_hbm, idx_v, rows_v, sem):
        wid = lax.axis_index("s") * SC_CORES + lax.axis_index("c")
        base = wid * per_worker

        @pl.loop(0, per_worker // SC_ROWS_PER_STEP)
        def _(i):
            off = base + i * SC_ROWS_PER_STEP
            pltpu.sync_copy(idx_hbm.at[pl.ds(off, SC_ROWS_PER_STEP)], idx_v)
            pltpu.async_copy(table_hbm.at[idx_v], rows_v, sem).wait()
            pltpu.sync_copy(rows_v, out_hbm.at[pl.ds(off, SC_ROWS_PER_STEP)])

    return gather(table, idx)


def _moe_grouped_kernel(te_ref, nv_ref, xs_ref, wg_ref, wu_ref, wd_ref, ys_ref, wg_b, wu_b, wd_b):
    i = pl.program_id(0)

    @pl.when(i < nv_ref[0])
    def _():
        @pl.when((i == 0) | (te_ref[i] != te_ref[jnp.maximum(i - 1, 0)]))
        def _():
            wg_b[...] = wg_ref[...].astype(BF16)
            wu_b[...] = wu_ref[...].astype(BF16)
            wd_b[...] = wd_ref[...].astype(BF16)

        x = _unpack_bf16_pairs(xs_ref[...]).astype(BF16)
        a = jnp.dot(x, wg_b[...], preferred_element_type=F32)
        b = jnp.dot(x, wu_b[...], preferred_element_type=F32)
        act = (a * jax.nn.sigmoid(a)) * b
        y = jnp.dot(act.astype(BF16), wd_b[...], preferred_element_type=F32)
        ys_ref[...] = _pack_bf16_pairs(y)


def _moe_grouped(tile_expert, n_valid, xs, wg, wu, wd, tm):
    p, half = xs.shape
    _, d, f = wg.shape
    grid_spec = pltpu.PrefetchScalarGridSpec(
        num_scalar_prefetch=2,
        grid=(p // tm,),
        in_specs=[pl.BlockSpec((tm, half), lambda i, te, nv: (i, 0)),
                  pl.BlockSpec((None, d, f), lambda i, te, nv: (te[i], 0, 0)),
                  pl.BlockSpec((None, d, f), lambda i, te, nv: (te[i], 0, 0)),
                  pl.BlockSpec((None, f, d), lambda i, te, nv: (te[i], 0, 0))],
        out_specs=pl.BlockSpec((tm, half), lambda i, te, nv: (i, 0)),
        scratch_shapes=[pltpu.VMEM((d, f), BF16), pltpu.VMEM((d, f), BF16), pltpu.VMEM((f, d), BF16)])
    return pl.pallas_call(
        _moe_grouped_kernel,
        grid_spec=grid_spec,
        out_shape=jax.ShapeDtypeStruct((p, half), jnp.int32),
        compiler_params=pltpu.CompilerParams(dimension_semantics=("arbitrary",),
                                             vmem_limit_bytes=VMEM_LIMIT),
        name="moe_grouped",
    )(tile_expert, n_valid, xs, wg, wu, wd)


def _moe_combine_kernel(x2_ref, route_ref, y0_ref, y1_ref, out_ref):
    route = route_ref[...]
    out_ref[...] = (x2_ref[...] + route[:, 2:3] * _unpack_bf16_pairs(y0_ref[...])
                    + route[:, 3:4] * _unpack_bf16_pairs(y1_ref[...]))


def _moe_combine(x2, route, gathered, tm):
    n, d = x2.shape
    nt = n // tm
    return pl.pallas_call(
        _moe_combine_kernel,
        grid=(nt,),
        in_specs=[pl.BlockSpec((tm, d), lambda i: (i, 0)),
                  pl.BlockSpec((tm, LANES), lambda i: (i, 0)),
                  pl.BlockSpec((tm, d // 2), lambda i: (i, 0)),
                  pl.BlockSpec((tm, d // 2), lambda i: (i + nt, 0))],
        out_specs=pl.BlockSpec((tm, d), lambda i: (i, 0)),
        out_shape=jax.ShapeDtypeStruct((n, d), F32),
        compiler_params=pltpu.CompilerParams(dimension_semantics=("arbitrary",),
                                             vmem_limit_bytes=VMEM_LIMIT),
        name="moe_combine",
    )(x2, route, gathered, gathered)


def _moe(x2, hn_packed, route, counts, wg, wu, wd, tm_rows, tm_tok):
    n = x2.shape[0]
    pad_to = math.lcm(tm_rows, 8 * SC_WORKERS, SC_WORKERS * SC_ROWS_PER_STEP)
    p = -(-(TOP_K * n + N_EXPERTS * tm_rows) // pad_to) * pad_to
    n_tiles = p // tm_rows
    cnt = counts[0, :N_EXPERTS].astype(jnp.int32)
    tiles_per_e = (cnt + tm_rows - 1) // tm_rows
    tile_end = jnp.cumsum(tiles_per_e)
    row_start = (tile_end - tiles_per_e) * tm_rows
    n_valid = tile_end[-1:]
    tile_expert = jnp.minimum(jnp.searchsorted(tile_end, jnp.arange(n_tiles, dtype=jnp.int32), side="right"),
                              N_EXPERTS - 1).astype(jnp.int32)
    tile_expert = jnp.where(jnp.arange(n_tiles) < n_valid[0], tile_expert,
                            tile_expert[jnp.maximum(n_valid[0] - 1, 0)])
    eidx = route[:, 0:TOP_K].astype(jnp.int32)
    pos = (row_start[eidx] + route[:, 4:4 + TOP_K].astype(jnp.int32)).T.reshape(-1)
    tok = jnp.tile(jnp.arange(n, dtype=jnp.int32), TOP_K)
    src = jnp.zeros((p,), jnp.int32).at[pos].set(tok)

    xs = _sc_gather_rows(hn_packed, src)
    ys = _moe_grouped(tile_expert, n_valid.astype(jnp.int32), xs, wg, wu, wd, tm_rows)
    back = _sc_gather_rows(ys, pos)
    return _moe_combine(x2, route, back, tm_tok)


def _block_diag(n, width):
    idx = np.arange(n) // width
    return jnp.asarray((idx[:, None] == idx[None, :]).astype(np.float32), dtype=BF16)


def _strict_lower(n):
    r = np.arange(n)
    return jnp.asarray((r[:, None] > r[None, :]).astype(np.float32), dtype=BF16)


def _alibi_slopes():
    return np.exp2(-8.0 * np.arange(1, N_HEADS + 1, dtype=np.float32) / N_HEADS).astype(np.float32)


def _prompt_bias():
    i = np.arange(BLOCK)[:, None]
    s = np.arange(2 * BLOCK)[None, :]
    dist = (i + BLOCK - s).astype(np.float32)
    mask = (dist >= 0) & (dist < WINDOW)
    first = mask & (s >= BLOCK)
    slopes = _alibi_slopes()[:, None, None]
    reg = np.where(mask[None], -slopes * dist[None], np.float32(NEG_INF))
    fst = np.where(first[None], -slopes * dist[None], np.float32(NEG_INF))
    return jnp.asarray(np.concatenate([reg, fst], axis=0).astype(np.float32))


def _sample_bias(ts, wb):
    i = np.arange(ts)[:, None]
    s = np.arange(wb + ts)[None, :]
    dist = (i + wb - s).astype(np.float32)
    mask = (dist >= 0) & (dist < WINDOW)
    slopes = _alibi_slopes()[:, None, None]
    b = np.where(mask[None], -slopes * dist[None], np.float32(NEG_INF)).astype(np.float32)
    return jnp.asarray(b.reshape(N_KV_HEADS, HEADS_PER_KV * ts, wb + ts))


def kernel(x_prompt, x_sample, cache_swa_k, cache_swa_v, state_conv, cache_mem_k, cache_mem_v, mem_prompt, norm_mix_g, w_in, q_norm_g, k_norm_g, attn_sinks, conv_dw_w, conv_dw_b, conv_ln_g, conv_ln_b, w_out, norm_xa_g, norm_mem_g, w_mq, w_mk, w_mv, mq_norm_g, mk_norm_g, w_mo, norm_ffn_g, w_router_group, b_router_group, w_router_expert, b_router_expert, w_exp_gate, w_exp_up, w_exp_down):
    depth = w_in.shape[0]
    bp, sp, d = x_prompt.shape
    nb, ts, _ = x_sample.shape
    wb = cache_swa_k.shape[2]
    mlen = mem_prompt.shape[1]
    assert d == D_MODEL and wb == WINDOW and sp % TM_PROMPT == 0 and nb % SEQ_PER_STEP == 0

    bdq = _block_diag(D_ATTN, HEAD_DIM)
    bdk = _block_diag(KV_DIM, HEAD_DIM)
    bias_p = _prompt_bias()
    bias_s = _sample_bias(ts, wb)
    lane_lo = (np.arange(D_ATTN) % LANES) < HALF
    row = lambda a: a.reshape(1, -1).astype(F32)

    xp = x_prompt
    xs = x_sample.reshape(nb * ts, d)
    kp_l, vp_l, cp_l, mkp_l, mvp_l, ks_l, vs_l, cs_l = [], [], [], [], [], [], [], []
    for l in range(depth):
        gq = jnp.tile(q_norm_g[l].astype(F32), N_HEADS) * (HEAD_DIM ** -0.5)
        gqlo = jnp.where(lane_lo, gq, 0.0).reshape(1, -1)
        gqhi = jnp.where(lane_lo, 0.0, gq).reshape(1, -1)
        gk = jnp.tile(k_norm_g[l].astype(F32), N_KV_HEADS).reshape(1, -1)
        sinks = attn_sinks[l].astype(F32)
        sinkrow = jnp.repeat(sinks, ts).reshape(N_KV_HEADS, HEADS_PER_KV * ts, 1)
        win = w_in[l].astype(BF16)
        wout = w_out[l].astype(BF16)
        wmq = w_mq[l].astype(BF16)
        wmo = w_mo[l].astype(BF16)
        gmq = (jnp.tile(mq_norm_g[l].astype(F32), MEM_HEADS) * (MEM_HEAD_DIM ** -0.5)).reshape(1, -1)
        gmk = jnp.tile(mk_norm_g[l].astype(F32), MEM_HEADS).reshape(1, -1)
        w_r = jnp.concatenate([w_router_expert[l], w_router_group[l],
                               jnp.zeros((d, LANES - N_EXPERTS - N_GROUPS), F32)], axis=1).astype(BF16)
        b_r = jnp.concatenate([b_router_expert[l], b_router_group[l],
                               jnp.zeros((LANES - N_EXPERTS - N_GROUPS,), F32)]).reshape(1, -1).astype(F32)
        wg, wu, wd = w_exp_gate[l], w_exp_up[l], w_exp_down[l]

        mix_consts = (row(norm_mix_g[l]), win, bdq, bdk, gqlo, gqhi, gk)
        conv_consts = (conv_dw_w[l].astype(F32), row(conv_dw_b[l]), row(conv_ln_g[l]), row(conv_ln_b[l]), wout)
        tail_consts = (wmo, row(norm_ffn_g[l]), w_r, b_r)

        x1p, kp, vp, cp = _mixer_prompt(xp, sinks, mix_consts + (bias_p,) + conv_consts, TM_PROMPT)
        mk, mv = _memory_kv(mem_prompt.reshape(bp * mlen, d), row(norm_mem_g[l]),
                            w_mk[l].astype(BF16), w_mv[l].astype(BF16), gmk, min(256, bp * mlen))
        mk = mk.reshape(bp, mlen, d)
        mv = mv.reshape(bp, mlen, d)
        x2p, hnp, routep, cntp = _cross_prompt(x1p, mk, mv, (row(norm_xa_g[l]), wmq, gmq) + tail_consts
                                               + (_strict_lower(TM_PROMPT),), TM_PROMPT)
        xp = _moe(x2p.reshape(bp * sp, d), hnp.reshape(bp * sp, d // 2), routep.reshape(bp * sp, LANES), cntp,
                  wg, wu, wd, TM_ROWS_PROMPT, TM_PROMPT).reshape(bp, sp, d)
        kp_l.append(kp.reshape(bp, BLOCK, N_KV_HEADS, HEAD_DIM))
        vp_l.append(vp.reshape(bp, BLOCK, N_KV_HEADS, HEAD_DIM))
        cp_l.append(cp)
        mkp_l.append(mk.reshape(bp, mlen, MEM_HEADS, MEM_HEAD_DIM))
        mvp_l.append(mv.reshape(bp, mlen, MEM_HEADS, MEM_HEAD_DIM))

        x1s, qm, ksn, vsn, csn = _mixer_sample(
            xs, cache_swa_k[l].reshape(nb, wb, KV_DIM), cache_swa_v[l].reshape(nb, wb, KV_DIM), state_conv[l],
            mix_consts + (bias_s, sinkrow) + conv_consts + (row(norm_xa_g[l]), wmq, gmq), SEQ_PER_STEP, ts)
        o_s = _mem_attend_sample(qm, cache_mem_k[l].reshape(nb, mlen, d).astype(BF16),
                                 cache_mem_v[l].reshape(nb, mlen, d).astype(BF16), MEM_SEQ_PER_STEP, ts)
        tms = min(256, nb * ts)
        x2s, hns, routes, cnts = _cross_tail_call(o_s, x1s, tail_consts + (_strict_lower(tms),), tms)
        xs = _moe(x2s, hns, routes, cnts, wg, wu, wd, TM_ROWS_SAMPLE, tms)
        ks_l.append(ksn.reshape(nb, wb, N_KV_HEADS, HEAD_DIM))
        vs_l.append(vsn.reshape(nb, wb, N_KV_HEADS, HEAD_DIM))
        cs_l.append(csn)

    st = lambda xs_: jnp.stack(xs_, axis=0)
    return (xp, xs.reshape(nb, ts, d), st(kp_l), st(vp_l), st(cp_l), st(mkp_l), st(mvp_l),
            st(ks_l), st(vs_l), st(cs_l))
```

```python
import functools

import numpy as np
import jax
import jax.numpy as jnp
from jax import lax
from jax.experimental import pallas as pl
from jax.experimental.pallas import tpu as pltpu
from jax.experimental.pallas import tpu_sc as plsc

F32 = jnp.float32
BF16 = jnp.bfloat16

D_MODEL = 1024
D_ATTN = 512
D_CONV = 512
HEAD_DIM = 64
N_HEADS = 8
N_KV_HEADS = 2
KV_DIM = N_KV_HEADS * HEAD_DIM
HEADS_PER_KV = N_HEADS // N_KV_HEADS
WINDOW = 128
BLOCK = 128
CONV_WIDTH = 31
CONV_PAST = CONV_WIDTH - 1
MEM_HEADS = 4
MEM_HEAD_DIM = 256
N_GROUPS = 4
EXPERTS_PER_GROUP = 8
N_EXPERTS = 32
D_EXPERT = 256
D_IN = D_ATTN + 2 * KV_DIM + 2 * D_CONV
EPS = 1e-6
NEG_INF = -1e30

LANES = 128
HALF = LANES // 2
CONV_PAD = 32
VMEM_LIMIT = 56 * 1024 * 1024

TM_PROMPT = 256
SEQ_PER_STEP = 16
MEM_SEQ_PER_STEP = 4
TOP_K = 2
SC_CORES = 2
SC_WORKERS = 32
SC_ROWS_PER_STEP = 64
TM_ROWS_PROMPT = 256
TM_ROWS_SAMPLE = 128


def _rms(x, g):
    ms = jnp.mean(x * x, axis=-1, keepdims=True)
    return x * lax.rsqrt(ms + EPS) * g


def _group_mean_sq(x, bd_ref, width):
    x2 = x * x
    hi = x2.astype(BF16)
    lo = (x2 - hi.astype(F32)).astype(BF16)
    bd = bd_ref[...]
    s = jnp.dot(hi, bd, preferred_element_type=F32) + jnp.dot(lo, bd, preferred_element_type=F32)
    return s * (1.0 / width)


def _mixer_proj(x, ng_ref, win_ref, bdq_ref, bdk_ref, gqlo_ref, gqhi_ref, gk_ref):
    h = _rms(x, ng_ref[...]).astype(BF16)
    p = jnp.dot(h, win_ref[...], preferred_element_type=F32)
    q = p[:, :D_ATTN]
    k = p[:, D_ATTN:D_ATTN + KV_DIM]
    v = p[:, D_ATTN + KV_DIM:D_ATTN + 2 * KV_DIM]
    ua = p[:, D_ATTN + 2 * KV_DIM:D_ATTN + 2 * KV_DIM + D_CONV]
    ub = p[:, D_ATTN + 2 * KV_DIM + D_CONV:]
    qn = q * lax.rsqrt(_group_mean_sq(q, bdq_ref, HEAD_DIM) + EPS)
    q_lo = qn * gqlo_ref[...]
    q_hi = qn * gqhi_ref[...]
    kn = k * lax.rsqrt(_group_mean_sq(k, bdk_ref, HEAD_DIM) + EPS) * gk_ref[...]
    u = ua * jax.nn.sigmoid(ub)
    return q_lo, q_hi, kn, v, u


def _dup_halves(x):
    lo = lax.broadcasted_iota(jnp.int32, x.shape, 1) < HALF
    xr = pltpu.roll(x, HALF, axis=1)
    return jnp.where(lo, x, xr), jnp.where(lo, xr, x)


def _sink_softmax(s, sink):
    m = jnp.maximum(jnp.max(s, axis=-1, keepdims=True), sink)
    p = jnp.exp(s - m)
    denom = jnp.sum(p, axis=-1, keepdims=True) + jnp.exp(sink - m)
    return p * (1.0 / denom)


def _conv_ln_silu(y, cb_ref, lg_ref, lb_ref):
    y = y + cb_ref[...]
    mu = jnp.mean(y, axis=-1, keepdims=True)
    yc = y - mu
    yn = yc * lax.rsqrt(jnp.mean(yc * yc, axis=-1, keepdims=True) + EPS)
    z = yn * lg_ref[...] + lb_ref[...]
    return z * jax.nn.sigmoid(z)


def _mem_query(x1, xag_ref, wmq_ref, gmq_ref):
    h = _rms(x1, xag_ref[...]).astype(BF16)
    q = jnp.dot(h, wmq_ref[...], preferred_element_type=F32)
    parts = []
    for hd in range(MEM_HEADS):
        qh = q[:, hd * MEM_HEAD_DIM:(hd + 1) * MEM_HEAD_DIM]
        parts.append(qh * lax.rsqrt(jnp.mean(qh * qh, axis=-1, keepdims=True) + EPS))
    return jnp.concatenate(parts, axis=1) * gmq_ref[...]


def _mem_attend(q, k_heads, v_heads):
    outs = []
    for hd in range(MEM_HEADS):
        sl = slice(hd * MEM_HEAD_DIM, (hd + 1) * MEM_HEAD_DIM)
        s = lax.dot_general(q[:, sl].astype(BF16), k_heads[hd], (((1,), (1,)), ((), ())),
                            preferred_element_type=F32)
        m = jnp.max(s, axis=-1, keepdims=True)
        p = jnp.exp(s - m)
        p = p * (1.0 / jnp.sum(p, axis=-1, keepdims=True))
        outs.append(jnp.dot(p.astype(BF16), v_heads[hd], preferred_element_type=F32))
    return jnp.concatenate(outs, axis=1)


def _split_heads(x):
    return [x[:, hd * MEM_HEAD_DIM:(hd + 1) * MEM_HEAD_DIM] for hd in range(MEM_HEADS)]


def _pack_bf16_pairs(y):
    n = y.shape[1] // 2
    bits = pltpu.bitcast(y.astype(BF16).astype(F32), jnp.int32)
    return (bits[:, :n] & jnp.int32(-65536)) | lax.shift_right_logical(bits[:, n:], jnp.int32(16))


def _unpack_bf16_pairs(w):
    hi = pltpu.bitcast(w & jnp.int32(-65536), F32)
    lo = pltpu.bitcast(lax.shift_left(w, jnp.int32(16)), F32)
    return jnp.concatenate([hi, lo], axis=1)


def _cross_tail(o, x1, wmo_ref, fg_ref, wr_ref, br_ref, tri_ref, base):
    x2 = x1 + jnp.dot(o.astype(BF16), wmo_ref[...], preferred_element_type=F32)
    hn_f = _rms(x2, fg_ref[...])
    hn = hn_f.astype(BF16)
    logits = jnp.dot(hn, wr_ref[...], preferred_element_type=F32) + br_ref[...]
    lane = lax.broadcasted_iota(jnp.int32, logits.shape, 1)
    is_g = (lane >= N_EXPERTS) & (lane < N_EXPERTS + N_GROUPS)
    lg = jnp.where(is_g, logits, NEG_INF)
    gmax = jnp.max(lg, axis=-1, keepdims=True)
    gsel = jnp.min(jnp.where(lg == gmax, lane, 2 * LANES), axis=-1, keepdims=True) - N_EXPERTS
    pg_sel = 1.0 / jnp.sum(jnp.exp(lg - gmax), axis=-1, keepdims=True)
    in_grp = (lane >= gsel * EXPERTS_PER_GROUP) & (lane < (gsel + 1) * EXPERTS_PER_GROUP)
    le = jnp.where(in_grp, logits, NEG_INF)
    top1 = jnp.max(le, axis=-1, keepdims=True)
    idx1 = jnp.min(jnp.where(le == top1, lane, 2 * LANES), axis=-1, keepdims=True)
    le2 = jnp.where(lane == idx1, NEG_INF, le)
    top2 = jnp.max(le2, axis=-1, keepdims=True)
    idx2 = jnp.min(jnp.where(le2 == top2, lane, 2 * LANES), axis=-1, keepdims=True)
    e2 = jnp.exp(top2 - top1)
    inv = 1.0 / (1.0 + e2)
    gate1 = pg_sel * inv
    gate2 = pg_sel * (e2 * inv)
    used = jnp.where((lane == idx1) | (lane == idx2), 1.0, 0.0)
    before = jnp.dot(tri_ref[...], used.astype(BF16), preferred_element_type=F32) + base
    rank1 = jnp.sum(jnp.where(lane == idx1, before, 0.0), axis=-1, keepdims=True)
    rank2 = jnp.sum(jnp.where(lane == idx2, before, 0.0), axis=-1, keepdims=True)
    route = jnp.zeros_like(logits)
    for pos, val in enumerate((idx1.astype(F32), idx2.astype(F32), gate1, gate2, rank1, rank2)):
        route = jnp.where(lane == pos, val, route)
    return x2, _pack_bf16_pairs(hn_f), route, base + jnp.sum(used, axis=0, keepdims=True)


def _mixer_prompt_kernel(sink_ref, x_ref, ng_ref, win_ref, bdq_ref, bdk_ref, gqlo_ref, gqhi_ref, gk_ref,
                         bias_ref, cw_ref, cb_ref, lg_ref, lb_ref, wout_ref,
                         x1_ref, ko_ref, vo_ref, co_ref,
                         kband, vband, uext, *, tm, nt):
    t = pl.program_id(1)

    @pl.when(t == 0)
    def _():
        kband[0:BLOCK, :] = jnp.zeros((BLOCK, KV_DIM), F32)
        vband[0:BLOCK, :] = jnp.zeros((BLOCK, KV_DIM), F32)
        uext[0:CONV_PAD, :] = jnp.zeros((CONV_PAD, D_CONV), F32)

    x = x_ref[...]
    q_lo, q_hi, kn, v, u = _mixer_proj(x, ng_ref, win_ref, bdq_ref, bdk_ref, gqlo_ref, gqhi_ref, gk_ref)
    kband[BLOCK:BLOCK + tm, :] = kn
    vband[BLOCK:BLOCK + tm, :] = v
    uext[CONV_PAD:CONV_PAD + tm, :] = u

    @pl.when(t == nt - 1)
    def _():
        ko_ref[...] = kn[tm - BLOCK:, :]
        vo_ref[...] = v[tm - BLOCK:, :]
        co_ref[...] = uext[pl.ds(CONV_PAD + tm - CONV_PAST, CONV_PAST), :]

    kd = [a.astype(BF16) for a in _dup_halves(kband[...])]
    vd = _dup_halves(vband[...])
    lo = lax.broadcasted_iota(jnp.int32, vd[0].shape, 1) < HALF
    v_lo = [jnp.where(lo, a, 0.0).astype(BF16) for a in vd]
    v_hi = [jnp.where(lo, 0.0, a).astype(BF16) for a in vd]
    q_lo = q_lo.astype(BF16)
    q_hi = q_hi.astype(BF16)
    attn_blocks = []
    for j in range(tm // BLOCK):
        rows = slice(j * BLOCK, (j + 1) * BLOCK)
        keys = slice(j * BLOCK, j * BLOCK + 2 * BLOCK)
        bias_base = jnp.where(t == 0, N_HEADS, 0) if j == 0 else 0
        tiles = []
        for c in range(N_KV_HEADS):
            q4 = jnp.concatenate(
                [(q_lo if (HEADS_PER_KV * c + a) % 2 == 0 else q_hi)[rows,
                  ((HEADS_PER_KV * c + a) // 2) * LANES:((HEADS_PER_KV * c + a) // 2 + 1) * LANES]
                 for a in range(HEADS_PER_KV)], axis=0)
            s_all = lax.dot_general(q4, kd[c][keys], (((1,), (1,)), ((), ())), preferred_element_type=F32)
            ps = []
            for a in range(HEADS_PER_KV):
                hd = HEADS_PER_KV * c + a
                s = s_all[a * BLOCK:(a + 1) * BLOCK] + bias_ref[bias_base + hd]
                ps.append(_sink_softmax(s, sink_ref[hd]).astype(BF16))
            vstack = jnp.concatenate([v_lo[c][keys], v_hi[c][keys]], axis=0)
            for i2 in range(HEADS_PER_KV // 2):
                pp = jnp.concatenate([ps[2 * i2], ps[2 * i2 + 1]], axis=1)
                tiles.append(jnp.dot(pp, vstack, preferred_element_type=F32))
        attn_blocks.append(jnp.concatenate(tiles, axis=1))
    attn = jnp.concatenate(attn_blocks, axis=0)

    off = CONV_PAD - CONV_PAST
    chunks = []
    for lc in range(D_CONV // LANES):
        ls = slice(lc * LANES, (lc + 1) * LANES)
        acc = jnp.zeros((tm, LANES), F32)
        for j in range(CONV_WIDTH):
            acc = acc + cw_ref[j:j + 1, ls] * uext[pl.ds(off + j, tm), ls]
        chunks.append(acc)
    y = _conv_ln_silu(jnp.concatenate(chunks, axis=1), cb_ref, lg_ref, lb_ref)

    x1_ref[...] = (x + jnp.dot(attn.astype(BF16), wout_ref[0:D_ATTN, :], preferred_element_type=F32)
                   + jnp.dot(y.astype(BF16), wout_ref[D_ATTN:, :], preferred_element_type=F32))

    kband[0:BLOCK, :] = kband[tm:tm + BLOCK, :]
    vband[0:BLOCK, :] = vband[tm:tm + BLOCK, :]
    uext[0:CONV_PAD, :] = uext[tm:tm + CONV_PAD, :]


def _const_spec(shape):
    nd = len(shape)
    return pl.BlockSpec(shape, lambda *_: (0,) * nd)


def _mixer_prompt(x, sinks, consts, tm):
    b, s, d = x.shape
    nt = s // tm
    (ng, win, bdq, bdk, gqlo, gqhi, gk, bias_p, cw, cb, lg, lb, wout) = consts
    kern = functools.partial(_mixer_prompt_kernel, tm=tm, nt=nt)
    cspecs = [_const_spec(a.shape) for a in consts]
    return pl.pallas_call(
        kern,
        grid=(b, nt),
        in_specs=[pl.BlockSpec(memory_space=pltpu.SMEM),
                  pl.BlockSpec((None, tm, d), lambda i, j: (i, j, 0))] + cspecs,
        out_specs=[pl.BlockSpec((None, tm, d), lambda i, j: (i, j, 0)),
                   pl.BlockSpec((None, BLOCK, KV_DIM), lambda i, j: (i, 0, 0)),
                   pl.BlockSpec((None, BLOCK, KV_DIM), lambda i, j: (i, 0, 0)),
                   pl.BlockSpec((None, CONV_PAST, D_CONV), lambda i, j: (i, 0, 0))],
        out_shape=[jax.ShapeDtypeStruct((b, s, d), F32),
                   jax.ShapeDtypeStruct((b, BLOCK, KV_DIM), F32),
                   jax.ShapeDtypeStruct((b, BLOCK, KV_DIM), F32),
                   jax.ShapeDtypeStruct((b, CONV_PAST, D_CONV), F32)],
        scratch_shapes=[pltpu.VMEM((BLOCK + tm, KV_DIM), F32),
                        pltpu.VMEM((BLOCK + tm, KV_DIM), F32),
                        pltpu.VMEM((CONV_PAD + tm, D_CONV), F32)],
        compiler_params=pltpu.CompilerParams(dimension_semantics=("arbitrary", "arbitrary"),
                                             vmem_limit_bytes=VMEM_LIMIT),
        name="mixer_prompt",
    )(sinks, x, *consts)


def _mixer_sample_kernel(x_ref, ck_ref, cv_ref, st_ref, ng_ref, win_ref, bdq_ref, bdk_ref, gqlo_ref, gqhi_ref,
                         gk_ref, bias_ref, sinkrow_ref, cw_ref, cb_ref, lg_ref, lb_ref, wout_ref,
                         xag_ref, wmq_ref, gmq_ref,
                         x1_ref, qm_ref, ko_ref, vo_ref, co_ref,
                         kall, vall, full, attn_scr, *, g, ts):
    wb = ck_ref.shape[1]
    x = x_ref[...]
    q_lo, q_hi, kn, v, u = _mixer_proj(x, ng_ref, win_ref, bdq_ref, bdk_ref, gqlo_ref, gqhi_ref, gk_ref)
    kall[:, 0:wb, :] = ck_ref[...]
    kall[:, wb:wb + ts, :] = kn.reshape(g, ts, KV_DIM)
    vall[:, 0:wb, :] = cv_ref[...]
    vall[:, wb:wb + ts, :] = v.reshape(g, ts, KV_DIM)
    ko_ref[...] = kall[:, ts:wb + ts, :]
    vo_ref[...] = vall[:, ts:wb + ts, :]
    full[:, 0:CONV_PAST, :] = st_ref[...]
    full[:, CONV_PAST:CONV_PAST + ts, :] = u.reshape(g, ts, D_CONV)
    co_ref[...] = full[:, ts:ts + CONV_PAST, :]

    lo8 = lax.broadcasted_iota(jnp.int32, (ts, LANES), 1) < HALF
    for n in range(g):
        rows = slice(n * ts, (n + 1) * ts)
        kd = [a.astype(BF16) for a in _dup_halves(kall[n])]
        vd = [a.astype(BF16) for a in _dup_halves(vall[n])]
        tiles = []
        for c in range(N_KV_HEADS):
            q4 = jnp.concatenate(
                [(q_lo if (HEADS_PER_KV * c + a) % 2 == 0 else q_hi)[rows,
                  ((HEADS_PER_KV * c + a) // 2) * LANES:((HEADS_PER_KV * c + a) // 2 + 1) * LANES]
                 for a in range(HEADS_PER_KV)], axis=0).astype(BF16)
            s = lax.dot_general(q4, kd[c], (((1,), (1,)), ((), ())), preferred_element_type=F32)
            p = _sink_softmax(s + bias_ref[c], sinkrow_ref[c]).astype(BF16)
            r = jnp.dot(p, vd[c], preferred_element_type=F32)
            for i2 in range(HEADS_PER_KV // 2):
                tiles.append(jnp.where(lo8, r[(2 * i2) * ts:(2 * i2 + 1) * ts],
                                       r[(2 * i2 + 1) * ts:(2 * i2 + 2) * ts]))
        attn_scr[rows, :] = jnp.concatenate(tiles, axis=1)

    acc = jnp.zeros((g, ts, D_CONV), F32)
    for j in range(CONV_WIDTH):
        acc = acc + cw_ref[j:j + 1, :] * full[:, j:j + ts, :]
    y = _conv_ln_silu(acc.reshape(g * ts, D_CONV), cb_ref, lg_ref, lb_ref)

    x1 = (x + jnp.dot(attn_scr[...].astype(BF16), wout_ref[0:D_ATTN, :], preferred_element_type=F32)
          + jnp.dot(y.astype(BF16), wout_ref[D_ATTN:, :], preferred_element_type=F32))
    x1_ref[...] = x1
    qm_ref[...] = _mem_query(x1, xag_ref, wmq_ref, gmq_ref)


def _mixer_sample(xs2d, ck, cv, st, consts, g, ts):
    n, d = xs2d.shape
    nb, wb = ck.shape[0], ck.shape[1]
    rows = g * ts
    cspecs = [_const_spec(a.shape) for a in consts]
    kern = functools.partial(_mixer_sample_kernel, g=g, ts=ts)
    return pl.pallas_call(
        kern,
        grid=(nb // g,),
        in_specs=[pl.BlockSpec((rows, d), lambda i: (i, 0)),
                  pl.BlockSpec((g, wb, KV_DIM), lambda i: (i, 0, 0)),
                  pl.BlockSpec((g, wb, KV_DIM), lambda i: (i, 0, 0)),
                  pl.BlockSpec((g, CONV_PAST, D_CONV), lambda i: (i, 0, 0))] + cspecs,
        out_specs=[pl.BlockSpec((rows, d), lambda i: (i, 0)),
                   pl.BlockSpec((rows, d), lambda i: (i, 0)),
                   pl.BlockSpec((g, wb, KV_DIM), lambda i: (i, 0, 0)),
                   pl.BlockSpec((g, wb, KV_DIM), lambda i: (i, 0, 0)),
                   pl.BlockSpec((g, CONV_PAST, D_CONV), lambda i: (i, 0, 0))],
        out_shape=[jax.ShapeDtypeStruct((n, d), F32),
                   jax.ShapeDtypeStruct((n, d), F32),
                   jax.ShapeDtypeStruct((nb, wb, KV_DIM), F32),
                   jax.ShapeDtypeStruct((nb, wb, KV_DIM), F32),
                   jax.ShapeDtypeStruct((nb, CONV_PAST, D_CONV), F32)],
        scratch_shapes=[pltpu.VMEM((g, wb + ts, KV_DIM), F32),
                        pltpu.VMEM((g, wb + ts, KV_DIM), F32),
                        pltpu.VMEM((g, CONV_PAST + ts + 2, D_CONV), F32),
                        pltpu.VMEM((rows, D_ATTN), F32)],
        compiler_params=pltpu.CompilerParams(dimension_semantics=("arbitrary",),
                                             vmem_limit_bytes=VMEM_LIMIT),
        name="mixer_sample",
    )(xs2d, ck, cv, st, *consts)


def _memory_kv_kernel(mem_ref, g_ref, wmk_ref, wmv_ref, gk_ref, k_ref, v_ref):
    h = _rms(mem_ref[...], g_ref[...]).astype(BF16)
    k = jnp.dot(h, wmk_ref[...], preferred_element_type=F32)
    parts = []
    for hd in range(MEM_HEADS):
        kh = k[:, hd * MEM_HEAD_DIM:(hd + 1) * MEM_HEAD_DIM]
        parts.append(kh * lax.rsqrt(jnp.mean(kh * kh, axis=-1, keepdims=True) + EPS))
    k_ref[...] = jnp.concatenate(parts, axis=1) * gk_ref[...]
    v_ref[...] = jnp.dot(h, wmv_ref[...], preferred_element_type=F32)


def _memory_kv(mem2d, g, wmk, wmv, gk, tm):
    n, d = mem2d.shape
    consts = (g, wmk, wmv, gk)
    return pl.pallas_call(
        _memory_kv_kernel,
        grid=(n // tm,),
        in_specs=[pl.BlockSpec((tm, d), lambda i: (i, 0))] + [_const_spec(a.shape) for a in consts],
        out_specs=[pl.BlockSpec((tm, d), lambda i: (i, 0)), pl.BlockSpec((tm, d), lambda i: (i, 0))],
        out_shape=[jax.ShapeDtypeStruct((n, d), F32), jax.ShapeDtypeStruct((n, d), F32)],
        compiler_params=pltpu.CompilerParams(dimension_semantics=("arbitrary",),
                                             vmem_limit_bytes=VMEM_LIMIT),
        name="memory_kv",
    )(mem2d, *consts)


def _cross_prompt_kernel(x1_ref, mk_ref, mv_ref, xag_ref, wmq_ref, gmq_ref, wmo_ref, fg_ref, wr_ref, br_ref, tri_ref,
                         x2_ref, hn_ref, route_ref, cnt_ref, base):
    @pl.when((pl.program_id(0) == 0) & (pl.program_id(1) == 0))
    def _():
        base[...] = jnp.zeros_like(base)

    x1 = x1_ref[...]
    q = _mem_query(x1, xag_ref, wmq_ref, gmq_ref)
    o = _mem_attend(q, _split_heads(mk_ref[...].astype(BF16)), _split_heads(mv_ref[...].astype(BF16)))
    x2, hn, route, new_base = _cross_tail(o, x1, wmo_ref, fg_ref, wr_ref, br_ref, tri_ref, base[...])
    x2_ref[...] = x2
    hn_ref[...] = hn
    route_ref[...] = route
    base[...] = new_base
    cnt_ref[...] = new_base


def _cross_prompt(x1, mk, mv, consts, tm):
    b, s, d = x1.shape
    m = mk.shape[1]
    cspecs = [_const_spec(a.shape) for a in consts]
    return pl.pallas_call(
        _cross_prompt_kernel,
        grid=(b, s // tm),
        in_specs=[pl.BlockSpec((None, tm, d), lambda i, j: (i, j, 0)),
                  pl.BlockSpec((None, m, d), lambda i, j: (i, 0, 0)),
                  pl.BlockSpec((None, m, d), lambda i, j: (i, 0, 0))] + cspecs,
        out_specs=[pl.BlockSpec((None, tm, d), lambda i, j: (i, j, 0)),
                   pl.BlockSpec((None, tm, d // 2), lambda i, j: (i, j, 0)),
                   pl.BlockSpec((None, tm, LANES), lambda i, j: (i, j, 0)),
                   pl.BlockSpec((1, LANES), lambda i, j: (0, 0))],
        out_shape=[jax.ShapeDtypeStruct((b, s, d), F32),
                   jax.ShapeDtypeStruct((b, s, d // 2), jnp.int32),
                   jax.ShapeDtypeStruct((b, s, LANES), F32),
                   jax.ShapeDtypeStruct((1, LANES), F32)],
        scratch_shapes=[pltpu.VMEM((1, LANES), F32)],
        compiler_params=pltpu.CompilerParams(dimension_semantics=("arbitrary", "arbitrary"),
                                             vmem_limit_bytes=VMEM_LIMIT),
        name="cross_prompt",
    )(x1, mk, mv, *consts)


def _mem_attend_sample_kernel(qm_ref, *refs, g, ts):
    k_refs, v_refs, o_ref = refs[:g], refs[g:2 * g], refs[2 * g]
    for n in range(g):
        k_heads = [k_refs[n][:, hd, :].astype(BF16) for hd in range(MEM_HEADS)]
        v_heads = [v_refs[n][:, hd, :].astype(BF16) for hd in range(MEM_HEADS)]
        o_ref[n * ts:(n + 1) * ts, :] = _mem_attend(qm_ref[n * ts:(n + 1) * ts, :], k_heads, v_heads)


def _mem_attend_sample(qm, mk, mv, g, ts):
    n, d = qm.shape
    nb, m = mk.shape[0], mk.shape[1]
    kern = functools.partial(_mem_attend_sample_kernel, g=g, ts=ts)

    def seq_spec(j):
        return pl.BlockSpec((None, m, MEM_HEADS, MEM_HEAD_DIM), lambda i: (g * i + j, 0, 0, 0))

    return pl.pallas_call(
        kern,
        grid=(nb // g,),
        in_specs=[pl.BlockSpec((g * ts, d), lambda i: (i, 0))] + [seq_spec(j) for j in range(g)] * 2,
        out_specs=pl.BlockSpec((g * ts, d), lambda i: (i, 0)),
        out_shape=jax.ShapeDtypeStruct((n, d), F32),
        compiler_params=pltpu.CompilerParams(dimension_semantics=("arbitrary",),
                                             vmem_limit_bytes=VMEM_LIMIT),
        name="mem_attend_sample",
    )(qm, *([mk] * g), *([mv] * g))


def _cross_tail_kernel(o_ref, x1_ref, wmo_ref, fg_ref, wr_ref, br_ref, tri_ref,
                       x2_ref, hn_ref, route_ref, cnt_ref, base):
    @pl.when(pl.program_id(0) == 0)
    def _():
        base[...] = jnp.zeros_like(base)

    x2, hn, route, new_base = _cross_tail(o_ref[...], x1_ref[...], wmo_ref, fg_ref, wr_ref, br_ref, tri_ref,
                                          base[...])
    x2_ref[...] = x2
    hn_ref[...] = hn
    route_ref[...] = route
    base[...] = new_base
    cnt_ref[...] = new_base


def _cross_tail_call(o, x1, consts, tm):
    n, d = x1.shape
    cspecs = [_const_spec(a.shape) for a in consts]
    return pl.pallas_call(
        _cross_tail_kernel,
        grid=(n // tm,),
        in_specs=[pl.BlockSpec((tm, d), lambda i: (i, 0)), pl.BlockSpec((tm, d), lambda i: (i, 0))] + cspecs,
        out_specs=[pl.BlockSpec((tm, d), lambda i: (i, 0)),
                   pl.BlockSpec((tm, d // 2), lambda i: (i, 0)),
                   pl.BlockSpec((tm, LANES), lambda i: (i, 0)),
                   pl.BlockSpec((1, LANES), lambda i: (0, 0))],
        out_shape=[jax.ShapeDtypeStruct((n, d), F32),
                   jax.ShapeDtypeStruct((n, d // 2), jnp.int32),
                   jax.ShapeDtypeStruct((n, LANES), F32),
                   jax.ShapeDtypeStruct((1, LANES), F32)],
        scratch_shapes=[pltpu.VMEM((1, LANES), F32)],
        compiler_params=pltpu.CompilerParams(dimension_semantics=("arbitrary",),
                                             vmem_limit_bytes=VMEM_LIMIT),
        name="cross_tail",
    )(o, x1, *consts)


def _sc_rows_per_step(per_worker):
    step = min(SC_ROWS_PER_STEP, per_worker)
    assert per_worker % step == 0 and step % 8 == 0
    return step


def _sc_gather_rows(table, idx):
    nrows = idx.shape[0]
    _, width = table.shape
    assert nrows % (8 * SC_WORKERS) == 0
    per_worker = nrows // SC_WORKERS
    step = _sc_rows_per_step(per_worker)
    mesh = plsc.VectorSubcoreMesh(core_axis_name="c", subcore_axis_name="s")

    @functools.partial(
        pl.kernel, mesh=mesh, out_type=jax.ShapeDtypeStruct((nrows, width), table.dtype),
        scratch_types=[pltpu.VMEM((step,), jnp.int32),
                       pltpu.VMEM((step, width), table.dtype),
                       pltpu.SemaphoreType.DMA])
    def gather(table_hbm, idx_hbm, out_hbm, idx_v, rows_v, sem):
        wid = lax.axis_index("s") * SC_CORES + lax.axis_index("c")
        base = wid * per_worker

        @pl.loop(0, per_worker // step)
        def _(i):
            off = base + i * step
            pltpu.sync_copy(idx_hbm.at[pl.ds(off, step)], idx_v)
            pltpu.async_copy(table_hbm.at[idx_v], rows_v, sem).wait()
            pltpu.sync_copy(rows_v, out_hbm.at[pl.ds(off, step)])

    return gather(table, idx)


def _sc_scatter_rows(table, pos, nrows_out):
    n, width = table.shape
    assert n % (8 * SC_WORKERS) == 0 and pos.shape == (TOP_K * n,)
    per_worker = n // SC_WORKERS
    step = _sc_rows_per_step(per_worker)
    mesh = plsc.VectorSubcoreMesh(core_axis_name="c", subcore_axis_name="s")

    @functools.partial(
        pl.kernel, mesh=mesh, out_type=jax.ShapeDtypeStruct((nrows_out, width), table.dtype),
        scratch_types=[pltpu.VMEM((step,), jnp.int32)] * TOP_K
        + [pltpu.VMEM((step, width), table.dtype), pltpu.SemaphoreType.DMA])
    def scatter(table_hbm, pos_hbm, out_hbm, *scratch):
        idx_vs, rows_v, sem = scratch[:TOP_K], scratch[TOP_K], scratch[TOP_K + 1]
        wid = lax.axis_index("s") * SC_CORES + lax.axis_index("c")
        base = wid * per_worker

        @pl.loop(0, per_worker // step)
        def _(i):
            off = base + i * step
            for k in range(TOP_K):
                pltpu.sync_copy(pos_hbm.at[pl.ds(k * n + off, step)], idx_vs[k])
            pltpu.sync_copy(table_hbm.at[pl.ds(off, step)], rows_v)
            for k in range(TOP_K):
                pltpu.async_copy(rows_v, out_hbm.at[idx_vs[k]], sem).wait()

    return scatter(table, pos)


W_CHUNKS = 4


def _moe_grouped_kernel(te_ref, nv_ref, rv_ref, xs_ref, *refs):
    w_refs, ys_ref = refs[:3 * W_CHUNKS], refs[3 * W_CHUNKS]
    wg_b, wu_b, wd_b = refs[3 * W_CHUNKS + 1:]
    i = pl.program_id(0)

    @pl.when(i < nv_ref[0])
    def _():
        @pl.when((i == 0) | (te_ref[i] != te_ref[jnp.maximum(i - 1, 0)]))
        def _():
            for m, dst in enumerate((wg_b, wu_b, wd_b)):
                rows = dst.shape[0] // W_CHUNKS
                for c in range(W_CHUNKS):
                    dst[c * rows:(c + 1) * rows, :] = w_refs[m * W_CHUNKS + c][...].astype(BF16)

        xs = xs_ref[...]
        row = lax.broadcasted_iota(jnp.int32, xs.shape, 0)
        x = _unpack_bf16_pairs(jnp.where(row < rv_ref[i], xs, 0)).astype(BF16)
        a = jnp.dot(x, wg_b[...], preferred_element_type=F32)
        b = jnp.dot(x, wu_b[...], preferred_element_type=F32)
        act = (a * jax.nn.sigmoid(a)) * b
        y = jnp.dot(act.astype(BF16), wd_b[...], preferred_element_type=F32)
        ys_ref[...] = _pack_bf16_pairs(y)


def _moe_grouped(tile_expert, n_valid, rows_valid, xs, wg, wu, wd, tm):
    p, half = xs.shape
    ne, d, f = wg.shape

    def chunk_specs(rows, cols):
        return [pl.BlockSpec((None, None, rows // W_CHUNKS, cols), functools.partial(
            lambda c, i, te, nv, rv: (te[i], c, 0, 0), c)) for c in range(W_CHUNKS)]

    grid_spec = pltpu.PrefetchScalarGridSpec(
        num_scalar_prefetch=3,
        grid=(p // tm,),
        in_specs=[pl.BlockSpec((tm, half), lambda i, te, nv, rv: (i, 0))]
        + chunk_specs(d, f) + chunk_specs(d, f) + chunk_specs(f, d),
        out_specs=pl.BlockSpec((tm, half), lambda i, te, nv, rv: (i, 0)),
        scratch_shapes=[pltpu.VMEM((d, f), BF16), pltpu.VMEM((d, f), BF16), pltpu.VMEM((f, d), BF16)])
    chunked = lambda w: [w.reshape(ne, W_CHUNKS, w.shape[1] // W_CHUNKS, w.shape[2])] * W_CHUNKS
    return pl.pallas_call(
        _moe_grouped_kernel,
        grid_spec=grid_spec,
        out_shape=jax.ShapeDtypeStruct((p, half), jnp.int32),
        compiler_params=pltpu.CompilerParams(dimension_semantics=("arbitrary",),
                                             vmem_limit_bytes=VMEM_LIMIT),
        name="moe_grouped",
    )(tile_expert, n_valid, rows_valid, xs, *chunked(wg), *chunked(wu), *chunked(wd))


def _moe_combine_kernel(x2_ref, route_ref, y0_ref, y1_ref, out_ref):
    route = route_ref[...]
    out_ref[...] = (x2_ref[...] + route[:, 2:3] * _unpack_bf16_pairs(y0_ref[...])
                    + route[:, 3:4] * _unpack_bf16_pairs(y1_ref[...]))


def _moe_combine(x2, route, gathered, tm):
    n, d = x2.shape
    nt = n // tm
    return pl.pallas_call(
        _moe_combine_kernel,
        grid=(nt,),
        in_specs=[pl.BlockSpec((tm, d), lambda i: (i, 0)),
                  pl.BlockSpec((tm, LANES), lambda i: (i, 0)),
                  pl.BlockSpec((tm, d // 2), lambda i: (i, 0)),
                  pl.BlockSpec((tm, d // 2), lambda i: (i + nt, 0))],
        out_specs=pl.BlockSpec((tm, d), lambda i: (i, 0)),
        out_shape=jax.ShapeDtypeStruct((n, d), F32),
        compiler_params=pltpu.CompilerParams(dimension_semantics=("arbitrary",),
                                             vmem_limit_bytes=VMEM_LIMIT),
        name="moe_combine",
    )(x2, route, gathered, gathered)


def _moe(x2, hn_packed, route, counts, wg, wu, wd, tm_rows, tm_tok):
    n = x2.shape[0]
    p = (TOP_K * n // tm_rows + N_EXPERTS) * tm_rows
    n_tiles = p // tm_rows
    cnt = counts[0, :N_EXPERTS].astype(jnp.int32)
    tiles_per_e = (cnt + tm_rows - 1) // tm_rows
    e_ids = jnp.arange(N_EXPERTS, dtype=jnp.int32)
    tile_end = jnp.sum(jnp.where(e_ids[None, :] <= e_ids[:, None], tiles_per_e[None, :], 0), axis=1)
    tile_start = tile_end - tiles_per_e
    n_valid = tile_end[-1:]
    tile_ids = jnp.arange(n_tiles, dtype=jnp.int32)
    tile_expert = jnp.minimum(jnp.sum((tile_end[None, :] <= tile_ids[:, None]).astype(jnp.int32), axis=1),
                              N_EXPERTS - 1)
    mine = tile_expert[:, None] == e_ids[None, :]
    rows_left = jnp.sum(jnp.where(mine, cnt - (tile_ids[:, None] - tile_start) * tm_rows, 0), axis=1)
    rows_valid = jnp.clip(rows_left, 0, tm_rows).astype(jnp.int32)
    eidx = route[:, 0:TOP_K].astype(jnp.int32)
    row_start = jnp.sum(jnp.where(eidx[:, :, None] == e_ids, tile_start * tm_rows, 0), axis=2)
    pos = (row_start + route[:, 4:4 + TOP_K].astype(jnp.int32)).T.reshape(-1)

    xs = _sc_scatter_rows(hn_packed, pos, p)
    ys = _moe_grouped(tile_expert, n_valid, rows_valid, xs, wg, wu, wd, tm_rows)
    back = _sc_gather_rows(ys, pos)
    return _moe_combine(x2, route, back, tm_tok)


def _block_diag(n, width):
    idx = np.arange(n) // width
    return jnp.asarray((idx[:, None] == idx[None, :]).astype(np.float32), dtype=BF16)


def _strict_lower(n):
    r = np.arange(n)
    return jnp.asarray((r[:, None] > r[None, :]).astype(np.float32), dtype=BF16)


def _alibi_slopes():
    return np.exp2(-8.0 * np.arange(1, N_HEADS + 1, dtype=np.float32) / N_HEADS).astype(np.float32)


def _prompt_bias():
    i = np.arange(BLOCK)[:, None]
    s = np.arange(2 * BLOCK)[None, :]
    dist = (i + BLOCK - s).astype(np.float32)
    mask = (dist >= 0) & (dist < WINDOW)
    first = mask & (s >= BLOCK)
    slopes = _alibi_slopes()[:, None, None]
    reg = np.where(mask[None], -slopes * dist[None], np.float32(NEG_INF))
    fst = np.where(first[None], -slopes * dist[None], np.float32(NEG_INF))
    return jnp.asarray(np.concatenate([reg, fst], axis=0).astype(np.float32))


def _sample_bias(ts, wb):
    i = np.arange(ts)[:, None]
    s = np.arange(wb + ts)[None, :]
    dist = (i + wb - s).astype(np.float32)
    mask = (dist >= 0) & (dist < WINDOW)
    slopes = _alibi_slopes()[:, None, None]
    b = np.where(mask[None], -slopes * dist[None], np.float32(NEG_INF)).astype(np.float32)
    return jnp.asarray(b.reshape(N_KV_HEADS, HEADS_PER_KV * ts, wb + ts))


def kernel(x_prompt, x_sample, cache_swa_k, cache_swa_v, state_conv, cache_mem_k, cache_mem_v, mem_prompt, norm_mix_g, w_in, q_norm_g, k_norm_g, attn_sinks, conv_dw_w, conv_dw_b, conv_ln_g, conv_ln_b, w_out, norm_xa_g, norm_mem_g, w_mq, w_mk, w_mv, mq_norm_g, mk_norm_g, w_mo, norm_ffn_g, w_router_group, b_router_group, w_router_expert, b_router_expert, w_exp_gate, w_exp_up, w_exp_down):
    depth = w_in.shape[0]
    bp, sp, d = x_prompt.shape
    nb, ts, _ = x_sample.shape
    wb = cache_swa_k.shape[2]
    mlen = mem_prompt.shape[1]
    assert d == D_MODEL and wb == WINDOW and sp % TM_PROMPT == 0 and nb % SEQ_PER_STEP == 0

    bdq = _block_diag(D_ATTN, HEAD_DIM)
    bdk = _block_diag(KV_DIM, HEAD_DIM)
    bias_p = _prompt_bias()
    bias_s = _sample_bias(ts, wb)
    lane_lo = (np.arange(D_ATTN) % LANES) < HALF
    row = lambda a: a.reshape(1, -1).astype(F32)

    xp = x_prompt
    xs = x_sample.reshape(nb * ts, d)
    kp_l, vp_l, cp_l, mkp_l, mvp_l, ks_l, vs_l, cs_l = [], [], [], [], [], [], [], []
    for l in range(depth):
        gq = jnp.tile(q_norm_g[l].astype(F32), N_HEADS) * (HEAD_DIM ** -0.5)
        gqlo = jnp.where(lane_lo, gq, 0.0).reshape(1, -1)
        gqhi = jnp.where(lane_lo, 0.0, gq).reshape(1, -1)
        gk = jnp.tile(k_norm_g[l].astype(F32), N_KV_HEADS).reshape(1, -1)
        sinks = attn_sinks[l].astype(F32)
        sinkrow = jnp.repeat(sinks, ts).reshape(N_KV_HEADS, HEADS_PER_KV * ts, 1)
        win = w_in[l].astype(BF16)
        wout = w_out[l].astype(BF16)
        wmq = w_mq[l].astype(BF16)
        wmo = w_mo[l].astype(BF16)
        gmq = (jnp.tile(mq_norm_g[l].astype(F32), MEM_HEADS) * (MEM_HEAD_DIM ** -0.5)).reshape(1, -1)
        gmk = jnp.tile(mk_norm_g[l].astype(F32), MEM_HEADS).reshape(1, -1)
        w_r = jnp.concatenate([w_router_expert[l], w_router_group[l],
                               jnp.zeros((d, LANES - N_EXPERTS - N_GROUPS), F32)], axis=1).astype(BF16)
        b_r = jnp.concatenate([b_router_expert[l], b_router_group[l],
                               jnp.zeros((LANES - N_EXPERTS - N_GROUPS,), F32)]).reshape(1, -1).astype(F32)
        wg, wu, wd = w_exp_gate[l], w_exp_up[l], w_exp_down[l]

        mix_consts = (row(norm_mix_g[l]), win, bdq, bdk, gqlo, gqhi, gk)
        conv_consts = (conv_dw_w[l].astype(F32), row(conv_dw_b[l]), row(conv_ln_g[l]), row(conv_ln_b[l]), wout)
        tail_consts = (wmo, row(norm_ffn_g[l]), w_r, b_r)

        x1p, kp, vp, cp = _mixer_prompt(xp, sinks, mix_consts + (bias_p,) + conv_consts, TM_PROMPT)
        mk, mv = _memory_kv(mem_prompt.reshape(bp * mlen, d), row(norm_mem_g[l]),
                            w_mk[l].astype(BF16), w_mv[l].astype(BF16), gmk, min(256, bp * mlen))
        mk = mk.reshape(bp, mlen, d)
        mv = mv.reshape(bp, mlen, d)
        x2p, hnp, routep, cntp = _cross_prompt(x1p, mk, mv, (row(norm_xa_g[l]), wmq, gmq) + tail_consts
                                               + (_strict_lower(TM_PROMPT),), TM_PROMPT)
        xp = _moe(x2p.reshape(bp * sp, d), hnp.reshape(bp * sp, d // 2), routep.reshape(bp * sp, LANES), cntp,
                  wg, wu, wd, TM_ROWS_PROMPT, TM_PROMPT).reshape(bp, sp, d)
        kp_l.append(kp.reshape(bp, BLOCK, N_KV_HEADS, HEAD_DIM))
        vp_l.append(vp.reshape(bp, BLOCK, N_KV_HEADS, HEAD_DIM))
        cp_l.append(cp)
        mkp_l.append(mk.reshape(bp, mlen, MEM_HEADS, MEM_HEAD_DIM))
        mvp_l.append(mv.reshape(bp, mlen, MEM_HEADS, MEM_HEAD_DIM))

        x1s, qm, ksn, vsn, csn = _mixer_sample(
            xs, cache_swa_k[l].reshape(nb, wb, KV_DIM), cache_swa_v[l].reshape(nb, wb, KV_DIM), state_conv[l],
            mix_consts + (bias_s, sinkrow) + conv_consts + (row(norm_xa_g[l]), wmq, gmq), SEQ_PER_STEP, ts)
        o_s = _mem_attend_sample(qm, cache_mem_k[l], cache_mem_v[l], MEM_SEQ_PER_STEP, ts)
        tms = min(256, nb * ts)
        x2s, hns, routes, cnts = _cross_tail_call(o_s, x1s, tail_consts + (_strict_lower(tms),), tms)
        xs = _moe(x2s, hns, routes, cnts, wg, wu, wd, TM_ROWS_SAMPLE, tms)
        ks_l.append(ksn.reshape(nb, wb, N_KV_HEADS, HEAD_DIM))
        vs_l.append(vsn.reshape(nb, wb, N_KV_HEADS, HEAD_DIM))
        cs_l.append(csn)

    st = lambda xs_: jnp.stack(xs_, axis=0)
    return (xp, xs.reshape(nb, ts, d), st(kp_l), st(vp_l), st(cp_l), st(mkp_l), st(mvp_l),
            st(ks_l), st(vs_l), st(cs_l))
```

```python
import functools

import numpy as np
import jax
import jax.numpy as jnp
from jax import lax
from jax.experimental import pallas as pl
from jax.experimental.pallas import tpu as pltpu
from jax.experimental.pallas import tpu_sc as plsc

F32 = jnp.float32
BF16 = jnp.bfloat16

D_MODEL = 1024
D_ATTN = 512
D_CONV = 512
HEAD_DIM = 64
N_HEADS = 8
N_KV_HEADS = 2
KV_DIM = N_KV_HEADS * HEAD_DIM
HEADS_PER_KV = N_HEADS // N_KV_HEADS
WINDOW = 128
BLOCK = 128
CONV_WIDTH = 31
CONV_PAST = CONV_WIDTH - 1
MEM_HEADS = 4
MEM_HEAD_DIM = 256
N_GROUPS = 4
EXPERTS_PER_GROUP = 8
N_EXPERTS = 32
D_EXPERT = 256
D_IN = D_ATTN + 2 * KV_DIM + 2 * D_CONV
EPS = 1e-6
NEG_INF = -1e30

LANES = 128
HALF = LANES // 2
SUBLANES = 8
CONV_PAD = 32
CONV_ROWS = 64
VMEM_LIMIT = 56 * 1024 * 1024

TM_PROMPT = 256
SEQ_PER_STEP = 16
MEM_SEQ_PER_STEP = 4
TOP_K = 2
SC_CORES = 2
SC_WORKERS = 32
SC_ROWS_PER_STEP = 64
TM_ROWS_PROMPT = 512
TM_ROWS_SAMPLE = 128
TM_COMBINE = 1024


def _rms(x, g):
    ms = jnp.mean(x * x, axis=-1, keepdims=True)
    return x * lax.rsqrt(ms + EPS) * g


def _group_mean_sq(x, bd_ref, width):
    x2 = x * x
    hi = x2.astype(BF16)
    lo = (x2 - hi.astype(F32)).astype(BF16)
    bd = bd_ref[...]
    s = jnp.dot(hi, bd, preferred_element_type=F32) + jnp.dot(lo, bd, preferred_element_type=F32)
    return s * (1.0 / width)


def _mixer_proj(x, ng_ref, win_ref, bdq_ref, bdk_ref, gqlo_ref, gqhi_ref, gk_ref):
    h = _rms(x, ng_ref[...]).astype(BF16)
    p = jnp.dot(h, win_ref[...], preferred_element_type=F32)
    q = p[:, :D_ATTN]
    k = p[:, D_ATTN:D_ATTN + KV_DIM]
    v = p[:, D_ATTN + KV_DIM:D_ATTN + 2 * KV_DIM]
    ua = p[:, D_ATTN + 2 * KV_DIM:D_ATTN + 2 * KV_DIM + D_CONV]
    ub = p[:, D_ATTN + 2 * KV_DIM + D_CONV:]
    qn = q * lax.rsqrt(_group_mean_sq(q, bdq_ref, HEAD_DIM) + EPS)
    q_lo = qn * gqlo_ref[...]
    q_hi = qn * gqhi_ref[...]
    kn = k * lax.rsqrt(_group_mean_sq(k, bdk_ref, HEAD_DIM) + EPS) * gk_ref[...]
    u = ua * jax.nn.sigmoid(ub)
    return q_lo, q_hi, kn, v, u


def _dup_halves(x):
    lo = lax.broadcasted_iota(jnp.int32, x.shape, 1) < HALF
    xr = pltpu.roll(x, HALF, axis=1)
    return jnp.where(lo, x, xr), jnp.where(lo, xr, x)


def _sink_softmax(s, sink):
    m = jnp.maximum(jnp.max(s, axis=-1, keepdims=True), sink)
    p = jnp.exp(s - m)
    denom = jnp.sum(p, axis=-1, keepdims=True) + jnp.exp(sink - m)
    return p * (1.0 / denom)


def _conv_ln_silu(y, cb_ref, lg_ref, lb_ref):
    y = y + cb_ref[...]
    mu = jnp.mean(y, axis=-1, keepdims=True)
    yc = y - mu
    yn = yc * lax.rsqrt(jnp.mean(yc * yc, axis=-1, keepdims=True) + EPS)
    z = yn * lg_ref[...] + lb_ref[...]
    return z * jax.nn.sigmoid(z)


def _mem_query(x1, xag_ref, wmq_ref, gmq_ref):
    h = _rms(x1, xag_ref[...]).astype(BF16)
    q = jnp.dot(h, wmq_ref[...], preferred_element_type=F32)
    parts = []
    for hd in range(MEM_HEADS):
        qh = q[:, hd * MEM_HEAD_DIM:(hd + 1) * MEM_HEAD_DIM]
        parts.append(qh * lax.rsqrt(jnp.mean(qh * qh, axis=-1, keepdims=True) + EPS))
    return jnp.concatenate(parts, axis=1) * gmq_ref[...]


def _mem_attend(q, k_heads, v_heads):
    outs = []
    for hd in range(MEM_HEADS):
        sl = slice(hd * MEM_HEAD_DIM, (hd + 1) * MEM_HEAD_DIM)
        s = lax.dot_general(q[:, sl].astype(BF16), k_heads[hd], (((1,), (1,)), ((), ())),
                            preferred_element_type=F32)
        m = jnp.max(s, axis=-1, keepdims=True)
        p = jnp.exp(s - m)
        p = p * (1.0 / jnp.sum(p, axis=-1, keepdims=True))
        outs.append(jnp.dot(p.astype(BF16), v_heads[hd], preferred_element_type=F32))
    return jnp.concatenate(outs, axis=1)


def _split_heads(x):
    return [x[:, hd * MEM_HEAD_DIM:(hd + 1) * MEM_HEAD_DIM] for hd in range(MEM_HEADS)]


def _pack_bf16_pairs(y):
    n = y.shape[1] // 2
    bits = pltpu.bitcast(y.astype(BF16).astype(F32), jnp.int32)
    return (bits[:, :n] & jnp.int32(-65536)) | lax.shift_right_logical(bits[:, n:], jnp.int32(16))


def _unpack_bf16_pairs(w):
    hi = pltpu.bitcast(w & jnp.int32(-65536), F32)
    lo = pltpu.bitcast(lax.shift_left(w, jnp.int32(16)), F32)
    return jnp.concatenate([hi, lo], axis=1)


def _cross_tail(o, x1, wmo_ref, fg_ref, wr_ref, br_ref, tri_ref, base):
    x2 = x1 + jnp.dot(o.astype(BF16), wmo_ref[...], preferred_element_type=F32)
    hn_f = _rms(x2, fg_ref[...])
    hn = hn_f.astype(BF16)
    logits = jnp.dot(hn, wr_ref[...], preferred_element_type=F32) + br_ref[...]
    lane = lax.broadcasted_iota(jnp.int32, logits.shape, 1)
    is_g = (lane >= N_EXPERTS) & (lane < N_EXPERTS + N_GROUPS)
    lg = jnp.where(is_g, logits, NEG_INF)
    gmax = jnp.max(lg, axis=-1, keepdims=True)
    gsel = jnp.min(jnp.where(lg == gmax, lane, 2 * LANES), axis=-1, keepdims=True) - N_EXPERTS
    pg_sel = 1.0 / jnp.sum(jnp.exp(lg - gmax), axis=-1, keepdims=True)
    in_grp = (lane >= gsel * EXPERTS_PER_GROUP) & (lane < (gsel + 1) * EXPERTS_PER_GROUP)
    le = jnp.where(in_grp, logits, NEG_INF)
    top1 = jnp.max(le, axis=-1, keepdims=True)
    idx1 = jnp.min(jnp.where(le == top1, lane, 2 * LANES), axis=-1, keepdims=True)
    le2 = jnp.where(lane == idx1, NEG_INF, le)
    top2 = jnp.max(le2, axis=-1, keepdims=True)
    idx2 = jnp.min(jnp.where(le2 == top2, lane, 2 * LANES), axis=-1, keepdims=True)
    e2 = jnp.exp(top2 - top1)
    inv = 1.0 / (1.0 + e2)
    gate1 = pg_sel * inv
    gate2 = pg_sel * (e2 * inv)
    used = jnp.where((lane == idx1) | (lane == idx2), 1.0, 0.0)
    before = jnp.dot(tri_ref[...], used.astype(BF16), preferred_element_type=F32) + base
    rank1 = jnp.sum(jnp.where(lane == idx1, before, 0.0), axis=-1, keepdims=True)
    rank2 = jnp.sum(jnp.where(lane == idx2, before, 0.0), axis=-1, keepdims=True)
    route = jnp.zeros_like(logits)
    for pos, val in enumerate((idx1.astype(F32), idx2.astype(F32), gate1, gate2, rank1, rank2)):
        route = jnp.where(lane == pos, val, route)
    return x2, _pack_bf16_pairs(hn_f), route, base + jnp.sum(used, axis=0, keepdims=True)


def _mixer_prompt_kernel(sink_ref, x_ref, ng_ref, win_ref, bdq_ref, bdk_ref, gqlo_ref, gqhi_ref, gk_ref,
                         bias_ref, cw_ref, cb_ref, lg_ref, lb_ref, wout_ref,
                         x1_ref, ko_ref, vo_ref, co_ref,
                         kband, vband, uext, ushift, *, tm, nt):
    t = pl.program_id(1)

    @pl.when(t == 0)
    def _():
        kband[0:BLOCK, :] = jnp.zeros((BLOCK, KV_DIM), F32)
        vband[0:BLOCK, :] = jnp.zeros((BLOCK, KV_DIM), F32)
        uext[0:CONV_PAD, :] = jnp.zeros((CONV_PAD, D_CONV), F32)

    x = x_ref[...]
    q_lo, q_hi, kn, v, u = _mixer_proj(x, ng_ref, win_ref, bdq_ref, bdk_ref, gqlo_ref, gqhi_ref, gk_ref)
    kband[BLOCK:BLOCK + tm, :] = kn
    vband[BLOCK:BLOCK + tm, :] = v
    uext[CONV_PAD:CONV_PAD + tm, :] = u

    @pl.when(t == nt - 1)
    def _():
        ko_ref[...] = kn[tm - BLOCK:, :]
        vo_ref[...] = v[tm - BLOCK:, :]
        co_ref[...] = uext[pl.ds(CONV_PAD + tm - CONV_PAST, CONV_PAST), :]

    kd = [a.astype(BF16) for a in _dup_halves(kband[...])]
    vd = _dup_halves(vband[...])
    lo = lax.broadcasted_iota(jnp.int32, vd[0].shape, 1) < HALF
    v_lo = [jnp.where(lo, a, 0.0).astype(BF16) for a in vd]
    v_hi = [jnp.where(lo, 0.0, a).astype(BF16) for a in vd]
    q_lo = q_lo.astype(BF16)
    q_hi = q_hi.astype(BF16)
    attn_blocks = []
    for j in range(tm // BLOCK):
        rows = slice(j * BLOCK, (j + 1) * BLOCK)
        keys = slice(j * BLOCK, j * BLOCK + 2 * BLOCK)
        bias_base = jnp.where(t == 0, N_HEADS, 0) if j == 0 else 0
        tiles = []
        for c in range(N_KV_HEADS):
            q4 = jnp.concatenate(
                [(q_lo if (HEADS_PER_KV * c + a) % 2 == 0 else q_hi)[rows,
                  ((HEADS_PER_KV * c + a) // 2) * LANES:((HEADS_PER_KV * c + a) // 2 + 1) * LANES]
                 for a in range(HEADS_PER_KV)], axis=0)
            s_all = lax.dot_general(q4, kd[c][keys], (((1,), (1,)), ((), ())), preferred_element_type=F32)
            ps = []
            for a in range(HEADS_PER_KV):
                hd = HEADS_PER_KV * c + a
                s = s_all[a * BLOCK:(a + 1) * BLOCK] + bias_ref[bias_base + hd]
                ps.append(_sink_softmax(s, sink_ref[hd]).astype(BF16))
            vstack = jnp.concatenate([v_lo[c][keys], v_hi[c][keys]], axis=0)
            for i2 in range(HEADS_PER_KV // 2):
                pp = jnp.concatenate([ps[2 * i2], ps[2 * i2 + 1]], axis=1)
                tiles.append(jnp.dot(pp, vstack, preferred_element_type=F32))
        attn_blocks.append(jnp.concatenate(tiles, axis=1))
    attn = jnp.concatenate(attn_blocks, axis=0)

    off = CONV_PAD - CONV_PAST
    span = tm + CONV_PAD - SUBLANES
    for s in range(1, SUBLANES):
        ushift[s - 1] = uext[pl.ds(s, span), :]
    chunks = []
    for lc in range(D_CONV // LANES):
        ls = slice(lc * LANES, (lc + 1) * LANES)
        accs = [jnp.zeros((CONV_ROWS, LANES), F32) for _ in range(tm // CONV_ROWS)]
        for j in range(CONV_WIDTH):
            s, a = (off + j) % SUBLANES, (off + j) // SUBLANES
            wj = cw_ref[j:j + 1, ls]
            for rc in range(tm // CONV_ROWS):
                r0 = rc * CONV_ROWS + a * SUBLANES
                tap = uext[r0:r0 + CONV_ROWS, ls] if s == 0 else ushift[s - 1, r0:r0 + CONV_ROWS, ls]
                accs[rc] = accs[rc] + wj * tap
        chunks.append(jnp.concatenate(accs, axis=0))
    y = _conv_ln_silu(jnp.concatenate(chunks, axis=1), cb_ref, lg_ref, lb_ref)

    x1_ref[...] = (x + jnp.dot(attn.astype(BF16), wout_ref[0:D_ATTN, :], preferred_element_type=F32)
                   + jnp.dot(y.astype(BF16), wout_ref[D_ATTN:, :], preferred_element_type=F32))

    kband[0:BLOCK, :] = kband[tm:tm + BLOCK, :]
    vband[0:BLOCK, :] = vband[tm:tm + BLOCK, :]
    uext[0:CONV_PAD, :] = uext[tm:tm + CONV_PAD, :]


def _const_spec(shape):
    nd = len(shape)
    return pl.BlockSpec(shape, lambda *_: (0,) * nd)


def _mixer_prompt(x, sinks, consts, tm):
    b, s, d = x.shape
    nt = s // tm
    (ng, win, bdq, bdk, gqlo, gqhi, gk, bias_p, cw, cb, lg, lb, wout) = consts
    kern = functools.partial(_mixer_prompt_kernel, tm=tm, nt=nt)
    cspecs = [_const_spec(a.shape) for a in consts]
    return pl.pallas_call(
        kern,
        grid=(b, nt),
        in_specs=[pl.BlockSpec(memory_space=pltpu.SMEM),
                  pl.BlockSpec((None, tm, d), lambda i, j: (i, j, 0))] + cspecs,
        out_specs=[pl.BlockSpec((None, tm, d), lambda i, j: (i, j, 0)),
                   pl.BlockSpec((None, BLOCK, KV_DIM), lambda i, j: (i, 0, 0)),
                   pl.BlockSpec((None, BLOCK, KV_DIM), lambda i, j: (i, 0, 0)),
                   pl.BlockSpec((None, CONV_PAST, D_CONV), lambda i, j: (i, 0, 0))],
        out_shape=[jax.ShapeDtypeStruct((b, s, d), F32),
                   jax.ShapeDtypeStruct((b, BLOCK, KV_DIM), F32),
                   jax.ShapeDtypeStruct((b, BLOCK, KV_DIM), F32),
                   jax.ShapeDtypeStruct((b, CONV_PAST, D_CONV), F32)],
        scratch_shapes=[pltpu.VMEM((BLOCK + tm, KV_DIM), F32),
                        pltpu.VMEM((BLOCK + tm, KV_DIM), F32),
                        pltpu.VMEM((CONV_PAD + tm, D_CONV), F32),
                        pltpu.VMEM((SUBLANES - 1, tm + CONV_PAD - SUBLANES, D_CONV), F32)],
        compiler_params=pltpu.CompilerParams(dimension_semantics=("arbitrary", "arbitrary"),
                                             vmem_limit_bytes=VMEM_LIMIT),
        name="mixer_prompt",
    )(sinks, x, *consts)


def _mixer_sample_kernel(x_ref, ck_ref, cv_ref, st_ref, ng_ref, win_ref, bdq_ref, bdk_ref, gqlo_ref, gqhi_ref,
                         gk_ref, bias_ref, sinkrow_ref, cw_ref, cb_ref, lg_ref, lb_ref, wout_ref,
                         xag_ref, wmq_ref, gmq_ref,
                         x1_ref, qm_ref, ko_ref, vo_ref, co_ref,
                         kall, vall, full, attn_scr, *, g, ts):
    wb = ck_ref.shape[1]
    x = x_ref[...]
    q_lo, q_hi, kn, v, u = _mixer_proj(x, ng_ref, win_ref, bdq_ref, bdk_ref, gqlo_ref, gqhi_ref, gk_ref)
    kall[:, 0:wb, :] = ck_ref[...]
    kall[:, wb:wb + ts, :] = kn.reshape(g, ts, KV_DIM)
    vall[:, 0:wb, :] = cv_ref[...]
    vall[:, wb:wb + ts, :] = v.reshape(g, ts, KV_DIM)
    ko_ref[...] = kall[:, ts:wb + ts, :]
    vo_ref[...] = vall[:, ts:wb + ts, :]
    full[:, 0:CONV_PAST, :] = st_ref[...]
    full[:, CONV_PAST:CONV_PAST + ts, :] = u.reshape(g, ts, D_CONV)
    co_ref[...] = full[:, ts:ts + CONV_PAST, :]

    lo8 = lax.broadcasted_iota(jnp.int32, (ts, LANES), 1) < HALF
    for n in range(g):
        rows = slice(n * ts, (n + 1) * ts)
        kd = [a.astype(BF16) for a in _dup_halves(kall[n])]
        vd = [a.astype(BF16) for a in _dup_halves(vall[n])]
        tiles = []
        for c in range(N_KV_HEADS):
            q4 = jnp.concatenate(
                [(q_lo if (HEADS_PER_KV * c + a) % 2 == 0 else q_hi)[rows,
                  ((HEADS_PER_KV * c + a) // 2) * LANES:((HEADS_PER_KV * c + a) // 2 + 1) * LANES]
                 for a in range(HEADS_PER_KV)], axis=0).astype(BF16)
            s = lax.dot_general(q4, kd[c], (((1,), (1,)), ((), ())), preferred_element_type=F32)
            p = _sink_softmax(s + bias_ref[c], sinkrow_ref[c]).astype(BF16)
            r = jnp.dot(p, vd[c], preferred_element_type=F32)
            for i2 in range(HEADS_PER_KV // 2):
                tiles.append(jnp.where(lo8, r[(2 * i2) * ts:(2 * i2 + 1) * ts],
                                       r[(2 * i2 + 1) * ts:(2 * i2 + 2) * ts]))
        attn_scr[rows, :] = jnp.concatenate(tiles, axis=1)

    acc = jnp.zeros((g, ts, D_CONV), F32)
    for j in range(CONV_WIDTH):
        acc = acc + cw_ref[j:j + 1, :] * full[:, j:j + ts, :]
    y = _conv_ln_silu(acc.reshape(g * ts, D_CONV), cb_ref, lg_ref, lb_ref)

    x1 = (x + jnp.dot(attn_scr[...].astype(BF16), wout_ref[0:D_ATTN, :], preferred_element_type=F32)
          + jnp.dot(y.astype(BF16), wout_ref[D_ATTN:, :], preferred_element_type=F32))
    x1_ref[...] = x1
    qm_ref[...] = _mem_query(x1, xag_ref, wmq_ref, gmq_ref)


def _mixer_sample(xs2d, ck, cv, st, consts, g, ts):
    n, d = xs2d.shape
    nb, wb = ck.shape[0], ck.shape[1]
    rows = g * ts
    cspecs = [_const_spec(a.shape) for a in consts]
    kern = functools.partial(_mixer_sample_kernel, g=g, ts=ts)
    return pl.pallas_call(
        kern,
        grid=(nb // g,),
        in_specs=[pl.BlockSpec((rows, d), lambda i: (i, 0)),
                  pl.BlockSpec((g, wb, KV_DIM), lambda i: (i, 0, 0)),
                  pl.BlockSpec((g, wb, KV_DIM), lambda i: (i, 0, 0)),
                  pl.BlockSpec((g, CONV_PAST, D_CONV), lambda i: (i, 0, 0))] + cspecs,
        out_specs=[pl.BlockSpec((rows, d), lambda i: (i, 0)),
                   pl.BlockSpec((rows, d), lambda i: (i, 0)),
                   pl.BlockSpec((g, wb, KV_DIM), lambda i: (i, 0, 0)),
                   pl.BlockSpec((g, wb, KV_DIM), lambda i: (i, 0, 0)),
                   pl.BlockSpec((g, CONV_PAST, D_CONV), lambda i: (i, 0, 0))],
        out_shape=[jax.ShapeDtypeStruct((n, d), F32),
                   jax.ShapeDtypeStruct((n, d), F32),
                   jax.ShapeDtypeStruct((nb, wb, KV_DIM), F32),
                   jax.ShapeDtypeStruct((nb, wb, KV_DIM), F32),
                   jax.ShapeDtypeStruct((nb, CONV_PAST, D_CONV), F32)],
        scratch_shapes=[pltpu.VMEM((g, wb + ts, KV_DIM), F32),
                        pltpu.VMEM((g, wb + ts, KV_DIM), F32),
                        pltpu.VMEM((g, CONV_PAST + ts + 2, D_CONV), F32),
                        pltpu.VMEM((rows, D_ATTN), F32)],
        compiler_params=pltpu.CompilerParams(dimension_semantics=("arbitrary",),
                                             vmem_limit_bytes=VMEM_LIMIT),
        name="mixer_sample",
    )(xs2d, ck, cv, st, *consts)


def _memory_kv_kernel(mem_ref, g_ref, wmk_ref, wmv_ref, gk_ref, k_ref, v_ref):
    h = _rms(mem_ref[...], g_ref[...]).astype(BF16)
    k = jnp.dot(h, wmk_ref[...], preferred_element_type=F32)
    parts = []
    for hd in range(MEM_HEADS):
        kh = k[:, hd * MEM_HEAD_DIM:(hd + 1) * MEM_HEAD_DIM]
        parts.append(kh * lax.rsqrt(jnp.mean(kh * kh, axis=-1, keepdims=True) + EPS))
    k_ref[...] = jnp.concatenate(parts, axis=1) * gk_ref[...]
    v_ref[...] = jnp.dot(h, wmv_ref[...], preferred_element_type=F32)


def _memory_kv(mem2d, g, wmk, wmv, gk, tm):
    n, d = mem2d.shape
    consts = (g, wmk, wmv, gk)
    return pl.pallas_call(
        _memory_kv_kernel,
        grid=(n // tm,),
        in_specs=[pl.BlockSpec((tm, d), lambda i: (i, 0))] + [_const_spec(a.shape) for a in consts],
        out_specs=[pl.BlockSpec((tm, d), lambda i: (i, 0)), pl.BlockSpec((tm, d), lambda i: (i, 0))],
        out_shape=[jax.ShapeDtypeStruct((n, d), F32), jax.ShapeDtypeStruct((n, d), F32)],
        compiler_params=pltpu.CompilerParams(dimension_semantics=("arbitrary",),
                                             vmem_limit_bytes=VMEM_LIMIT),
        name="memory_kv",
    )(mem2d, *consts)


def _cross_prompt_kernel(x1_ref, mk_ref, mv_ref, xag_ref, wmq_ref, gmq_ref, wmo_ref, fg_ref, wr_ref, br_ref, tri_ref,
                         x2_ref, hn_ref, route_ref, cnt_ref, base):
    @pl.when((pl.program_id(0) == 0) & (pl.program_id(1) == 0))
    def _():
        base[...] = jnp.zeros_like(base)

    x1 = x1_ref[...]
    q = _mem_query(x1, xag_ref, wmq_ref, gmq_ref)
    o = _mem_attend(q, _split_heads(mk_ref[...].astype(BF16)), _split_heads(mv_ref[...].astype(BF16)))
    x2, hn, route, new_base = _cross_tail(o, x1, wmo_ref, fg_ref, wr_ref, br_ref, tri_ref, base[...])
    x2_ref[...] = x2
    hn_ref[...] = hn
    route_ref[...] = route
    base[...] = new_base
    cnt_ref[...] = new_base


def _cross_prompt(x1, mk, mv, consts, tm):
    b, s, d = x1.shape
    m = mk.shape[1]
    cspecs = [_const_spec(a.shape) for a in consts]
    return pl.pallas_call(
        _cross_prompt_kernel,
        grid=(b, s // tm),
        in_specs=[pl.BlockSpec((None, tm, d), lambda i, j: (i, j, 0)),
                  pl.BlockSpec((None, m, d), lambda i, j: (i, 0, 0)),
                  pl.BlockSpec((None, m, d), lambda i, j: (i, 0, 0))] + cspecs,
        out_specs=[pl.BlockSpec((None, tm, d), lambda i, j: (i, j, 0)),
                   pl.BlockSpec((None, tm, d // 2), lambda i, j: (i, j, 0)),
                   pl.BlockSpec((None, tm, LANES), lambda i, j: (i, j, 0)),
                   pl.BlockSpec((1, LANES), lambda i, j: (0, 0))],
        out_shape=[jax.ShapeDtypeStruct((b, s, d), F32),
                   jax.ShapeDtypeStruct((b, s, d // 2), jnp.int32),
                   jax.ShapeDtypeStruct((b, s, LANES), F32),
                   jax.ShapeDtypeStruct((1, LANES), F32)],
        scratch_shapes=[pltpu.VMEM((1, LANES), F32)],
        compiler_params=pltpu.CompilerParams(dimension_semantics=("arbitrary", "arbitrary"),
                                             vmem_limit_bytes=VMEM_LIMIT),
        name="cross_prompt",
    )(x1, mk, mv, *consts)


def _mem_attend_sample_kernel(qm_ref, *refs, g, ts):
    k_refs, v_refs, o_ref = refs[:g], refs[g:2 * g], refs[2 * g]
    halves = MEM_HEAD_DIM // LANES
    rows_per_pos = MEM_HEADS * halves
    mlen = k_refs[0].shape[0] // rows_per_pos

    def all_heads(ref):
        return jnp.concatenate([ref[pl.ds(c * MEM_HEADS + hd, mlen, stride=rows_per_pos), :]
                                for hd in range(MEM_HEADS) for c in range(halves)], axis=1).astype(BF16)

    rows = MEM_HEADS * ts
    own = (lax.broadcasted_iota(jnp.int32, (rows, MEM_HEADS * MEM_HEAD_DIM), 1) // MEM_HEAD_DIM
           == lax.broadcasted_iota(jnp.int32, (rows, MEM_HEADS * MEM_HEAD_DIM), 0) // ts)
    for n in range(g):
        q = qm_ref[n * ts:(n + 1) * ts, :]
        q_bd = jnp.where(own, jnp.concatenate([q] * MEM_HEADS, axis=0), 0.0).astype(BF16)
        s = lax.dot_general(q_bd, all_heads(k_refs[n]), (((1,), (1,)), ((), ())), preferred_element_type=F32)
        m = jnp.max(s, axis=-1, keepdims=True)
        p = jnp.exp(s - m)
        p = p * (1.0 / jnp.sum(p, axis=-1, keepdims=True))
        r = jnp.dot(p.astype(BF16), all_heads(v_refs[n]), preferred_element_type=F32)
        o_ref[n * ts:(n + 1) * ts, :] = jnp.concatenate(
            [r[hd * ts:(hd + 1) * ts, hd * MEM_HEAD_DIM:(hd + 1) * MEM_HEAD_DIM] for hd in range(MEM_HEADS)], axis=1)


def _mem_cache_rows(cache):
    nb, mlen = cache.shape[:2]
    halves = MEM_HEAD_DIM // LANES
    return (cache.reshape(nb, mlen, MEM_HEADS, halves, LANES).transpose(0, 1, 3, 2, 4)
            .reshape(nb, mlen * halves * MEM_HEADS, LANES))


def _mem_attend_sample(qm, mk, mv, g, ts):
    n, d = qm.shape
    nb, rows = mk.shape[0], mk.shape[1]
    kern = functools.partial(_mem_attend_sample_kernel, g=g, ts=ts)

    def seq_spec(j):
        return pl.BlockSpec((None, rows, LANES), lambda i: (g * i + j, 0, 0))

    return pl.pallas_call(
        kern,
        grid=(nb // g,),
        in_specs=[pl.BlockSpec((g * ts, d), lambda i: (i, 0))] + [seq_spec(j) for j in range(g)] * 2,
        out_specs=pl.BlockSpec((g * ts, d), lambda i: (i, 0)),
        out_shape=jax.ShapeDtypeStruct((n, d), F32),
        compiler_params=pltpu.CompilerParams(dimension_semantics=("arbitrary",),
                                             vmem_limit_bytes=VMEM_LIMIT),
        name="mem_attend_sample",
    )(qm, *([mk] * g), *([mv] * g))


def _cross_tail_kernel(o_ref, x1_ref, wmo_ref, fg_ref, wr_ref, br_ref, tri_ref,
                       x2_ref, hn_ref, route_ref, cnt_ref, base):
    @pl.when(pl.program_id(0) == 0)
    def _():
        base[...] = jnp.zeros_like(base)

    x2, hn, route, new_base = _cross_tail(o_ref[...], x1_ref[...], wmo_ref, fg_ref, wr_ref, br_ref, tri_ref,
                                          base[...])
    x2_ref[...] = x2
    hn_ref[...] = hn
    route_ref[...] = route
    base[...] = new_base
    cnt_ref[...] = new_base


def _cross_tail_call(o, x1, consts, tm):
    n, d = x1.shape
    cspecs = [_const_spec(a.shape) for a in consts]
    return pl.pallas_call(
        _cross_tail_kernel,
        grid=(n // tm,),
        in_specs=[pl.BlockSpec((tm, d), lambda i: (i, 0)), pl.BlockSpec((tm, d), lambda i: (i, 0))] + cspecs,
        out_specs=[pl.BlockSpec((tm, d), lambda i: (i, 0)),
                   pl.BlockSpec((tm, d // 2), lambda i: (i, 0)),
                   pl.BlockSpec((tm, LANES), lambda i: (i, 0)),
                   pl.BlockSpec((1, LANES), lambda i: (0, 0))],
        out_shape=[jax.ShapeDtypeStruct((n, d), F32),
                   jax.ShapeDtypeStruct((n, d // 2), jnp.int32),
                   jax.ShapeDtypeStruct((n, LANES), F32),
                   jax.ShapeDtypeStruct((1, LANES), F32)],
        scratch_shapes=[pltpu.VMEM((1, LANES), F32)],
        compiler_params=pltpu.CompilerParams(dimension_semantics=("arbitrary",),
                                             vmem_limit_bytes=VMEM_LIMIT),
        name="cross_tail",
    )(o, x1, *consts)


def _sc_rows_per_step(per_worker):
    step = min(SC_ROWS_PER_STEP, per_worker)
    assert per_worker % step == 0 and step % 8 == 0
    return step


def _sc_gather_rows(table, idx):
    nrows = idx.shape[0]
    _, width = table.shape
    assert nrows % (8 * SC_WORKERS) == 0
    per_worker = nrows // SC_WORKERS
    step = _sc_rows_per_step(per_worker)
    mesh = plsc.VectorSubcoreMesh(core_axis_name="c", subcore_axis_name="s")

    @functools.partial(
        pl.kernel, mesh=mesh, out_type=jax.ShapeDtypeStruct((nrows, width), table.dtype),
        scratch_types=[pltpu.VMEM((step,), jnp.int32),
                       pltpu.VMEM((step, width), table.dtype),
                       pltpu.SemaphoreType.DMA])
    def gather(table_hbm, idx_hbm, out_hbm, idx_v, rows_v, sem):
        wid = lax.axis_index("s") * SC_CORES + lax.axis_index("c")
        base = wid * per_worker

        @pl.loop(0, per_worker // step)
        def _(i):
            off = base + i * step
            pltpu.sync_copy(idx_hbm.at[pl.ds(off, step)], idx_v)
            pltpu.async_copy(table_hbm.at[idx_v], rows_v, sem).wait()
            pltpu.sync_copy(rows_v, out_hbm.at[pl.ds(off, step)])

    return gather(table, idx)


def _sc_scatter_rows(table, pos, nrows_out):
    n, width = table.shape
    assert n % (8 * SC_WORKERS) == 0 and pos.shape == (TOP_K * n,)
    per_worker = n // SC_WORKERS
    step = _sc_rows_per_step(per_worker)
    mesh = plsc.VectorSubcoreMesh(core_axis_name="c", subcore_axis_name="s")

    @functools.partial(
        pl.kernel, mesh=mesh, out_type=jax.ShapeDtypeStruct((nrows_out, width), table.dtype),
        scratch_types=[pltpu.VMEM((step,), jnp.int32)] * TOP_K
        + [pltpu.VMEM((step, width), table.dtype), pltpu.SemaphoreType.DMA])
    def scatter(table_hbm, pos_hbm, out_hbm, *scratch):
        idx_vs, rows_v, sem = scratch[:TOP_K], scratch[TOP_K], scratch[TOP_K + 1]
        wid = lax.axis_index("s") * SC_CORES + lax.axis_index("c")
        base = wid * per_worker

        @pl.loop(0, per_worker // step)
        def _(i):
            off = base + i * step
            for k in range(TOP_K):
                pltpu.sync_copy(pos_hbm.at[pl.ds(k * n + off, step)], idx_vs[k])
            pltpu.sync_copy(table_hbm.at[pl.ds(off, step)], rows_v)
            for k in range(TOP_K):
                pltpu.async_copy(rows_v, out_hbm.at[idx_vs[k]], sem).wait()

    return scatter(table, pos)


W_CHUNKS = 4


def _moe_grouped_kernel(te_ref, nv_ref, rv_ref, xs_ref, *refs):
    w_refs, ys_ref = refs[:3 * W_CHUNKS], refs[3 * W_CHUNKS]
    wg_b, wu_b, wd_b = refs[3 * W_CHUNKS + 1:]
    i = pl.program_id(0)

    @pl.when(i < nv_ref[0])
    def _():
        @pl.when((i == 0) | (te_ref[i] != te_ref[jnp.maximum(i - 1, 0)]))
        def _():
            for m, dst in enumerate((wg_b, wu_b, wd_b)):
                rows = dst.shape[0] // W_CHUNKS
                for c in range(W_CHUNKS):
                    dst[c * rows:(c + 1) * rows, :] = w_refs[m * W_CHUNKS + c][...].astype(BF16)

        xs = xs_ref[...]
        row = lax.broadcasted_iota(jnp.int32, xs.shape, 0)
        x = _unpack_bf16_pairs(jnp.where(row < rv_ref[i], xs, 0)).astype(BF16)
        a = jnp.dot(x, wg_b[...], preferred_element_type=F32)
        b = jnp.dot(x, wu_b[...], preferred_element_type=F32)
        act = (a * jax.nn.sigmoid(a)) * b
        y = jnp.dot(act.astype(BF16), wd_b[...], preferred_element_type=F32)
        ys_ref[...] = _pack_bf16_pairs(y)


def _moe_grouped(tile_expert, n_valid, rows_valid, xs, wg, wu, wd, tm):
    p, half = xs.shape
    ne, d, f = wg.shape

    def chunk_specs(rows, cols):
        return [pl.BlockSpec((None, None, rows // W_CHUNKS, cols), functools.partial(
            lambda c, i, te, nv, rv: (te[i], c, 0, 0), c)) for c in range(W_CHUNKS)]

    grid_spec = pltpu.PrefetchScalarGridSpec(
        num_scalar_prefetch=3,
        grid=(p // tm,),
        in_specs=[pl.BlockSpec((tm, half), lambda i, te, nv, rv: (i, 0))]
        + chunk_specs(d, f) + chunk_specs(d, f) + chunk_specs(f, d),
        out_specs=pl.BlockSpec((tm, half), lambda i, te, nv, rv: (i, 0)),
        scratch_shapes=[pltpu.VMEM((d, f), BF16), pltpu.VMEM((d, f), BF16), pltpu.VMEM((f, d), BF16)])
    chunked = lambda w: [w.reshape(ne, W_CHUNKS, w.shape[1] // W_CHUNKS, w.shape[2])] * W_CHUNKS
    return pl.pallas_call(
        _moe_grouped_kernel,
        grid_spec=grid_spec,
        out_shape=jax.ShapeDtypeStruct((p, half), jnp.int32),
        compiler_params=pltpu.CompilerParams(dimension_semantics=("arbitrary",),
                                             vmem_limit_bytes=VMEM_LIMIT),
        name="moe_grouped",
    )(tile_expert, n_valid, rows_valid, xs, *chunked(wg), *chunked(wu), *chunked(wd))


def _moe_combine_kernel(x2_ref, route_ref, y0_ref, y1_ref, out_ref):
    route = route_ref[...]
    out_ref[...] = (x2_ref[...] + route[:, 2:3] * _unpack_bf16_pairs(y0_ref[...])
                    + route[:, 3:4] * _unpack_bf16_pairs(y1_ref[...]))


def _moe_combine(x2, route, gathered, tm):
    n, d = x2.shape
    nt = n // tm
    return pl.pallas_call(
        _moe_combine_kernel,
        grid=(nt,),
        in_specs=[pl.BlockSpec((tm, d), lambda i: (i, 0)),
                  pl.BlockSpec((tm, LANES), lambda i: (i, 0)),
                  pl.BlockSpec((tm, d // 2), lambda i: (i, 0)),
                  pl.BlockSpec((tm, d // 2), lambda i: (i + nt, 0))],
        out_specs=pl.BlockSpec((tm, d), lambda i: (i, 0)),
        out_shape=jax.ShapeDtypeStruct((n, d), F32),
        compiler_params=pltpu.CompilerParams(dimension_semantics=("arbitrary",),
                                             vmem_limit_bytes=VMEM_LIMIT),
        name="moe_combine",
    )(x2, route, gathered, gathered)


def _moe(x2, hn_packed, route, counts, wg, wu, wd, tm_rows):
    n = x2.shape[0]
    p = (TOP_K * n // tm_rows + N_EXPERTS) * tm_rows
    n_tiles = p // tm_rows
    cnt = counts[0, :N_EXPERTS].astype(jnp.int32)
    tiles_per_e = (cnt + tm_rows - 1) // tm_rows
    e_ids = jnp.arange(N_EXPERTS, dtype=jnp.int32)
    tile_end = jnp.sum(jnp.where(e_ids[None, :] <= e_ids[:, None], tiles_per_e[None, :], 0), axis=1)
    tile_start = tile_end - tiles_per_e
    n_valid = tile_end[-1:]
    tile_ids = jnp.arange(n_tiles, dtype=jnp.int32)
    tile_expert = jnp.minimum(jnp.sum((tile_end[None, :] <= tile_ids[:, None]).astype(jnp.int32), axis=1),
                              N_EXPERTS - 1)
    mine = tile_expert[:, None] == e_ids[None, :]
    rows_left = jnp.sum(jnp.where(mine, cnt - (tile_ids[:, None] - tile_start) * tm_rows, 0), axis=1)
    rows_valid = jnp.clip(rows_left, 0, tm_rows).astype(jnp.int32)
    eidx = route[:, 0:TOP_K].astype(jnp.int32)
    row_start = jnp.sum(jnp.where(eidx[:, :, None] == e_ids, tile_start * tm_rows, 0), axis=2)
    pos = (row_start + route[:, 4:4 + TOP_K].astype(jnp.int32)).T.reshape(-1)

    xs = _sc_scatter_rows(hn_packed, pos, p)
    ys = _moe_grouped(tile_expert, n_valid, rows_valid, xs, wg, wu, wd, tm_rows)
    back = _sc_gather_rows(ys, pos)
    return _moe_combine(x2, route, back, min(TM_COMBINE, n))


def _block_diag(n, width):
    idx = np.arange(n) // width
    return jnp.asarray((idx[:, None] == idx[None, :]).astype(np.float32), dtype=BF16)


def _strict_lower(n):
    r = np.arange(n)
    return jnp.asarray((r[:, None] > r[None, :]).astype(np.float32), dtype=BF16)


def _alibi_slopes():
    return np.exp2(-8.0 * np.arange(1, N_HEADS + 1, dtype=np.float32) / N_HEADS).astype(np.float32)


def _prompt_bias():
    i = np.arange(BLOCK)[:, None]
    s = np.arange(2 * BLOCK)[None, :]
    dist = (i + BLOCK - s).astype(np.float32)
    mask = (dist >= 0) & (dist < WINDOW)
    first = mask & (s >= BLOCK)
    slopes = _alibi_slopes()[:, None, None]
    reg = np.where(mask[None], -slopes * dist[None], np.float32(NEG_INF))
    fst = np.where(first[None], -slopes * dist[None], np.float32(NEG_INF))
    return jnp.asarray(np.concatenate([reg, fst], axis=0).astype(np.float32))


def _sample_bias(ts, wb):
    i = np.arange(ts)[:, None]
    s = np.arange(wb + ts)[None, :]
    dist = (i + wb - s).astype(np.float32)
    mask = (dist >= 0) & (dist < WINDOW)
    slopes = _alibi_slopes()[:, None, None]
    b = np.where(mask[None], -slopes * dist[None], np.float32(NEG_INF)).astype(np.float32)
    return jnp.asarray(b.reshape(N_KV_HEADS, HEADS_PER_KV * ts, wb + ts))


def kernel(x_prompt, x_sample, cache_swa_k, cache_swa_v, state_conv, cache_mem_k, cache_mem_v, mem_prompt, norm_mix_g, w_in, q_norm_g, k_norm_g, attn_sinks, conv_dw_w, conv_dw_b, conv_ln_g, conv_ln_b, w_out, norm_xa_g, norm_mem_g, w_mq, w_mk, w_mv, mq_norm_g, mk_norm_g, w_mo, norm_ffn_g, w_router_group, b_router_group, w_router_expert, b_router_expert, w_exp_gate, w_exp_up, w_exp_down):
    depth = w_in.shape[0]
    bp, sp, d = x_prompt.shape
    nb, ts, _ = x_sample.shape
    wb = cache_swa_k.shape[2]
    mlen = mem_prompt.shape[1]
    assert d == D_MODEL and wb == WINDOW and sp % TM_PROMPT == 0 and nb % SEQ_PER_STEP == 0

    bdq = _block_diag(D_ATTN, HEAD_DIM)
    bdk = _block_diag(KV_DIM, HEAD_DIM)
    bias_p = _prompt_bias()
    bias_s = _sample_bias(ts, wb)
    lane_lo = (np.arange(D_ATTN) % LANES) < HALF
    row = lambda a: a.reshape(1, -1).astype(F32)

    xp = x_prompt
    xs = x_sample.reshape(nb * ts, d)
    kp_l, vp_l, cp_l, mkp_l, mvp_l, ks_l, vs_l, cs_l = [], [], [], [], [], [], [], []
    for l in range(depth):
        gq = jnp.tile(q_norm_g[l].astype(F32), N_HEADS) * (HEAD_DIM ** -0.5)
        gqlo = jnp.where(lane_lo, gq, 0.0).reshape(1, -1)
        gqhi = jnp.where(lane_lo, 0.0, gq).reshape(1, -1)
        gk = jnp.tile(k_norm_g[l].astype(F32), N_KV_HEADS).reshape(1, -1)
        sinks = attn_sinks[l].astype(F32)
        sinkrow = jnp.repeat(sinks, ts).reshape(N_KV_HEADS, HEADS_PER_KV * ts, 1)
        win = w_in[l].astype(BF16)
        wout = w_out[l].astype(BF16)
        wmq = w_mq[l].astype(BF16)
        wmo = w_mo[l].astype(BF16)
        gmq = (jnp.tile(mq_norm_g[l].astype(F32), MEM_HEADS) * (MEM_HEAD_DIM ** -0.5)).reshape(1, -1)
        gmk = jnp.tile(mk_norm_g[l].astype(F32), MEM_HEADS).reshape(1, -1)
        w_r = jnp.concatenate([w_router_expert[l], w_router_group[l],
                               jnp.zeros((d, LANES - N_EXPERTS - N_GROUPS), F32)], axis=1).astype(BF16)
        b_r = jnp.concatenate([b_router_expert[l], b_router_group[l],
                               jnp.zeros((LANES - N_EXPERTS - N_GROUPS,), F32)]).reshape(1, -1).astype(F32)
        wg, wu, wd = w_exp_gate[l], w_exp_up[l], w_exp_down[l]

        mix_consts = (row(norm_mix_g[l]), win, bdq, bdk, gqlo, gqhi, gk)
        conv_consts = (conv_dw_w[l].astype(F32), row(conv_dw_b[l]), row(conv_ln_g[l]), row(conv_ln_b[l]), wout)
        tail_consts = (wmo, row(norm_ffn_g[l]), w_r, b_r)

        x1p, kp, vp, cp = _mixer_prompt(xp, sinks, mix_consts + (bias_p,) + conv_consts, TM_PROMPT)
        mk, mv = _memory_kv(mem_prompt.reshape(bp * mlen, d), row(norm_mem_g[l]),
                            w_mk[l].astype(BF16), w_mv[l].astype(BF16), gmk, min(256, bp * mlen))
        mk = mk.reshape(bp, mlen, d)
        mv = mv.reshape(bp, mlen, d)
        x2p, hnp, routep, cntp = _cross_prompt(x1p, mk, mv, (row(norm_xa_g[l]), wmq, gmq) + tail_consts
                                               + (_strict_lower(TM_PROMPT),), TM_PROMPT)
        xp = _moe(x2p.reshape(bp * sp, d), hnp.reshape(bp * sp, d // 2), routep.reshape(bp * sp, LANES), cntp,
                  wg, wu, wd, TM_ROWS_PROMPT).reshape(bp, sp, d)
        kp_l.append(kp.reshape(bp, BLOCK, N_KV_HEADS, HEAD_DIM))
        vp_l.append(vp.reshape(bp, BLOCK, N_KV_HEADS, HEAD_DIM))
        cp_l.append(cp)
        mkp_l.append(mk.reshape(bp, mlen, MEM_HEADS, MEM_HEAD_DIM))
        mvp_l.append(mv.reshape(bp, mlen, MEM_HEADS, MEM_HEAD_DIM))

        x1s, qm, ksn, vsn, csn = _mixer_sample(
            xs, cache_swa_k[l].reshape(nb, wb, KV_DIM), cache_swa_v[l].reshape(nb, wb, KV_DIM), state_conv[l],
            mix_consts + (bias_s, sinkrow) + conv_consts + (row(norm_xa_g[l]), wmq, gmq), SEQ_PER_STEP, ts)
        o_s = _mem_attend_sample(qm, _mem_cache_rows(cache_mem_k[l]), _mem_cache_rows(cache_mem_v[l]),
                                 MEM_SEQ_PER_STEP, ts)
        tms = min(256, nb * ts)
        x2s, hns, routes, cnts = _cross_tail_call(o_s, x1s, tail_consts + (_strict_lower(tms),), tms)
        xs = _moe(x2s, hns, routes, cnts, wg, wu, wd, TM_ROWS_SAMPLE)
        ks_l.append(ksn.reshape(nb, wb, N_KV_HEADS, HEAD_DIM))
        vs_l.append(vsn.reshape(nb, wb, N_KV_HEADS, HEAD_DIM))
        cs_l.append(csn)

    st = lambda xs_: jnp.stack(xs_, axis=0)
    return (xp, xs.reshape(nb, ts, d), st(kp_l), st(vp_l), st(cp_l), st(mkp_l), st(mvp_l),
            st(ks_l), st(vs_l), st(cs_l))
```

```python
import functools

import numpy as np
import jax
import jax.numpy as jnp
from jax import lax
from jax.experimental import pallas as pl
from jax.experimental.pallas import tpu as pltpu
from jax.experimental.pallas import tpu_sc as plsc

F32 = jnp.float32
BF16 = jnp.bfloat16

D_MODEL = 1024
D_ATTN = 512
D_CONV = 512
HEAD_DIM = 64
N_HEADS = 8
N_KV_HEADS = 2
KV_DIM = N_KV_HEADS * HEAD_DIM
HEADS_PER_KV = N_HEADS // N_KV_HEADS
WINDOW = 128
BLOCK = 128
CONV_WIDTH = 31
CONV_PAST = CONV_WIDTH - 1
MEM_HEADS = 4
MEM_HEAD_DIM = 256
N_GROUPS = 4
EXPERTS_PER_GROUP = 8
N_EXPERTS = 32
D_EXPERT = 256
D_IN = D_ATTN + 2 * KV_DIM + 2 * D_CONV
EPS = 1e-6
NEG_INF = -1e30

LANES = 128
HALF = LANES // 2
SUBLANES = 8
CONV_PAD = 32
CONV_ROWS = 64
VMEM_LIMIT = 56 * 1024 * 1024

TM_PROMPT = 512
SEQ_PER_STEP = 16
MEM_SEQ_PER_STEP = 4
TOP_K = 2
SC_CORES = 2
SC_WORKERS = 32
SC_ROWS_PER_STEP = 64
TM_ROWS_PROMPT = 512
TM_ROWS_SAMPLE = 128
TM_COMBINE = 1024


def _rms(x, g):
    ms = jnp.mean(x * x, axis=-1, keepdims=True)
    return x * lax.rsqrt(ms + EPS) * g


def _group_mean_sq(x, bd_ref, width):
    x2 = x * x
    hi = x2.astype(BF16)
    lo = (x2 - hi.astype(F32)).astype(BF16)
    bd = bd_ref[...]
    s = jnp.dot(hi, bd, preferred_element_type=F32) + jnp.dot(lo, bd, preferred_element_type=F32)
    return s * (1.0 / width)


def _mixer_proj(x, ng_ref, win_ref, bdq_ref, bdk_ref, gqlo_ref, gqhi_ref, gk_ref):
    h = _rms(x, ng_ref[...]).astype(BF16)
    p = jnp.dot(h, win_ref[...], preferred_element_type=F32)
    q = p[:, :D_ATTN]
    k = p[:, D_ATTN:D_ATTN + KV_DIM]
    v = p[:, D_ATTN + KV_DIM:D_ATTN + 2 * KV_DIM]
    ua = p[:, D_ATTN + 2 * KV_DIM:D_ATTN + 2 * KV_DIM + D_CONV]
    ub = p[:, D_ATTN + 2 * KV_DIM + D_CONV:]
    qn = q * lax.rsqrt(_group_mean_sq(q, bdq_ref, HEAD_DIM) + EPS)
    q_lo = qn * gqlo_ref[...]
    q_hi = qn * gqhi_ref[...]
    kn = k * lax.rsqrt(_group_mean_sq(k, bdk_ref, HEAD_DIM) + EPS) * gk_ref[...]
    u = ua * jax.nn.sigmoid(ub)
    return q_lo, q_hi, kn, v, u


def _dup_halves(x):
    lo = lax.broadcasted_iota(jnp.int32, x.shape, 1) < HALF
    xr = pltpu.roll(x, HALF, axis=1)
    return jnp.where(lo, x, xr), jnp.where(lo, xr, x)


def _sink_softmax(s, sink):
    m = jnp.maximum(jnp.max(s, axis=-1, keepdims=True), sink)
    p = jnp.exp(s - m)
    denom = jnp.sum(p, axis=-1, keepdims=True) + jnp.exp(sink - m)
    return p * (1.0 / denom)


def _conv_ln_silu(y, cb_ref, lg_ref, lb_ref):
    y = y + cb_ref[...]
    mu = jnp.mean(y, axis=-1, keepdims=True)
    yc = y - mu
    yn = yc * lax.rsqrt(jnp.mean(yc * yc, axis=-1, keepdims=True) + EPS)
    z = yn * lg_ref[...] + lb_ref[...]
    return z * jax.nn.sigmoid(z)


def _mem_query(x1, xag_ref, wmq_ref, gmq_ref):
    h = _rms(x1, xag_ref[...]).astype(BF16)
    q = jnp.dot(h, wmq_ref[...], preferred_element_type=F32)
    parts = []
    for hd in range(MEM_HEADS):
        qh = q[:, hd * MEM_HEAD_DIM:(hd + 1) * MEM_HEAD_DIM]
        parts.append(qh * lax.rsqrt(jnp.mean(qh * qh, axis=-1, keepdims=True) + EPS))
    return jnp.concatenate(parts, axis=1) * gmq_ref[...]


def _mem_attend(q, k_heads, v_heads):
    outs = []
    for hd in range(MEM_HEADS):
        sl = slice(hd * MEM_HEAD_DIM, (hd + 1) * MEM_HEAD_DIM)
        s = lax.dot_general(q[:, sl].astype(BF16), k_heads[hd], (((1,), (1,)), ((), ())),
                            preferred_element_type=F32)
        m = jnp.max(s, axis=-1, keepdims=True)
        p = jnp.exp(s - m)
        p = p * (1.0 / jnp.sum(p, axis=-1, keepdims=True))
        outs.append(jnp.dot(p.astype(BF16), v_heads[hd], preferred_element_type=F32))
    return jnp.concatenate(outs, axis=1)


def _split_heads(x):
    return [x[:, hd * MEM_HEAD_DIM:(hd + 1) * MEM_HEAD_DIM] for hd in range(MEM_HEADS)]


def _pack_bf16_pairs(y):
    n = y.shape[1] // 2
    bits = pltpu.bitcast(y.astype(BF16).astype(F32), jnp.int32)
    return (bits[:, :n] & jnp.int32(-65536)) | lax.shift_right_logical(bits[:, n:], jnp.int32(16))


def _unpack_bf16_pairs(w):
    hi = pltpu.bitcast(w & jnp.int32(-65536), F32)
    lo = pltpu.bitcast(lax.shift_left(w, jnp.int32(16)), F32)
    return jnp.concatenate([hi, lo], axis=1)


def _cross_tail(o, x1, wmo_ref, fg_ref, wr_ref, br_ref, tri_ref, base):
    x2 = x1 + jnp.dot(o.astype(BF16), wmo_ref[...], preferred_element_type=F32)
    hn_f = _rms(x2, fg_ref[...])
    hn = hn_f.astype(BF16)
    logits = jnp.dot(hn, wr_ref[...], preferred_element_type=F32) + br_ref[...]
    lane = lax.broadcasted_iota(jnp.int32, logits.shape, 1)
    is_g = (lane >= N_EXPERTS) & (lane < N_EXPERTS + N_GROUPS)
    lg = jnp.where(is_g, logits, NEG_INF)
    gmax = jnp.max(lg, axis=-1, keepdims=True)
    gsel = jnp.min(jnp.where(lg == gmax, lane, 2 * LANES), axis=-1, keepdims=True) - N_EXPERTS
    pg_sel = 1.0 / jnp.sum(jnp.exp(lg - gmax), axis=-1, keepdims=True)
    in_grp = (lane >= gsel * EXPERTS_PER_GROUP) & (lane < (gsel + 1) * EXPERTS_PER_GROUP)
    le = jnp.where(in_grp, logits, NEG_INF)
    top1 = jnp.max(le, axis=-1, keepdims=True)
    idx1 = jnp.min(jnp.where(le == top1, lane, 2 * LANES), axis=-1, keepdims=True)
    le2 = jnp.where(lane == idx1, NEG_INF, le)
    top2 = jnp.max(le2, axis=-1, keepdims=True)
    idx2 = jnp.min(jnp.where(le2 == top2, lane, 2 * LANES), axis=-1, keepdims=True)
    e2 = jnp.exp(top2 - top1)
    inv = 1.0 / (1.0 + e2)
    gate1 = pg_sel * inv
    gate2 = pg_sel * (e2 * inv)
    used = jnp.where((lane == idx1) | (lane == idx2), 1.0, 0.0)
    before = jnp.dot(tri_ref[...], used.astype(BF16), preferred_element_type=F32) + base
    rank1 = jnp.sum(jnp.where(lane == idx1, before, 0.0), axis=-1, keepdims=True)
    rank2 = jnp.sum(jnp.where(lane == idx2, before, 0.0), axis=-1, keepdims=True)
    route = jnp.zeros_like(logits)
    for pos, val in enumerate((idx1.astype(F32), idx2.astype(F32), gate1, gate2, rank1, rank2)):
        route = jnp.where(lane == pos, val, route)
    return x2, _pack_bf16_pairs(hn_f), route, base + jnp.sum(used, axis=0, keepdims=True)


def _mixer_prompt_kernel(sink_ref, x_ref, ng_ref, win_ref, bdq_ref, bdk_ref, gqlo_ref, gqhi_ref, gk_ref,
                         bias_ref, cw_ref, cb_ref, lg_ref, lb_ref, wout_ref,
                         x1_ref, ko_ref, vo_ref, co_ref,
                         kband, vband, uext, ushift, *, tm, nt):
    t = pl.program_id(1)

    @pl.when(t == 0)
    def _():
        kband[0:BLOCK, :] = jnp.zeros((BLOCK, KV_DIM), F32)
        vband[0:BLOCK, :] = jnp.zeros((BLOCK, KV_DIM), F32)
        uext[0:CONV_PAD, :] = jnp.zeros((CONV_PAD, D_CONV), F32)

    x = x_ref[...]
    q_lo, q_hi, kn, v, u = _mixer_proj(x, ng_ref, win_ref, bdq_ref, bdk_ref, gqlo_ref, gqhi_ref, gk_ref)
    kband[BLOCK:BLOCK + tm, :] = kn
    vband[BLOCK:BLOCK + tm, :] = v
    uext[CONV_PAD:CONV_PAD + tm, :] = u

    @pl.when(t == nt - 1)
    def _():
        ko_ref[...] = kn[tm - BLOCK:, :]
        vo_ref[...] = v[tm - BLOCK:, :]
        co_ref[...] = uext[pl.ds(CONV_PAD + tm - CONV_PAST, CONV_PAST), :]

    kd = [a.astype(BF16) for a in _dup_halves(kband[...])]
    vd = _dup_halves(vband[...])
    lo = lax.broadcasted_iota(jnp.int32, vd[0].shape, 1) < HALF
    v_lo = [jnp.where(lo, a, 0.0).astype(BF16) for a in vd]
    v_hi = [jnp.where(lo, 0.0, a).astype(BF16) for a in vd]
    q_lo = q_lo.astype(BF16)
    q_hi = q_hi.astype(BF16)
    attn_blocks = []
    for j in range(tm // BLOCK):
        rows = slice(j * BLOCK, (j + 1) * BLOCK)
        keys = slice(j * BLOCK, j * BLOCK + 2 * BLOCK)
        bias_base = jnp.where(t == 0, N_HEADS, 0) if j == 0 else 0
        tiles = []
        for c in range(N_KV_HEADS):
            q4 = jnp.concatenate(
                [(q_lo if (HEADS_PER_KV * c + a) % 2 == 0 else q_hi)[rows,
                  ((HEADS_PER_KV * c + a) // 2) * LANES:((HEADS_PER_KV * c + a) // 2 + 1) * LANES]
                 for a in range(HEADS_PER_KV)], axis=0)
            s_all = lax.dot_general(q4, kd[c][keys], (((1,), (1,)), ((), ())), preferred_element_type=F32)
            ps = []
            for a in range(HEADS_PER_KV):
                hd = HEADS_PER_KV * c + a
                s = s_all[a * BLOCK:(a + 1) * BLOCK] + bias_ref[bias_base + hd]
                ps.append(_sink_softmax(s, sink_ref[hd]).astype(BF16))
            vstack = jnp.concatenate([v_lo[c][keys], v_hi[c][keys]], axis=0)
            for i2 in range(HEADS_PER_KV // 2):
                pp = jnp.concatenate([ps[2 * i2], ps[2 * i2 + 1]], axis=1)
                tiles.append(jnp.dot(pp, vstack, preferred_element_type=F32))
        attn_blocks.append(jnp.concatenate(tiles, axis=1))
    attn = jnp.concatenate(attn_blocks, axis=0)

    off = CONV_PAD - CONV_PAST
    span = tm + CONV_PAD - SUBLANES
    for s in range(1, SUBLANES):
        ushift[s - 1] = uext[pl.ds(s, span), :]
    chunks = []
    for lc in range(D_CONV // LANES):
        ls = slice(lc * LANES, (lc + 1) * LANES)
        accs = [jnp.zeros((CONV_ROWS, LANES), F32) for _ in range(tm // CONV_ROWS)]
        for j in range(CONV_WIDTH):
            s, a = (off + j) % SUBLANES, (off + j) // SUBLANES
            wj = cw_ref[j:j + 1, ls]
            for rc in range(tm // CONV_ROWS):
                r0 = rc * CONV_ROWS + a * SUBLANES
                tap = uext[r0:r0 + CONV_ROWS, ls] if s == 0 else ushift[s - 1, r0:r0 + CONV_ROWS, ls]
                accs[rc] = accs[rc] + wj * tap
        chunks.append(jnp.concatenate(accs, axis=0))
    y = _conv_ln_silu(jnp.concatenate(chunks, axis=1), cb_ref, lg_ref, lb_ref)

    x1_ref[...] = (x + jnp.dot(attn.astype(BF16), wout_ref[0:D_ATTN, :], preferred_element_type=F32)
                   + jnp.dot(y.astype(BF16), wout_ref[D_ATTN:, :], preferred_element_type=F32))

    kband[0:BLOCK, :] = kband[tm:tm + BLOCK, :]
    vband[0:BLOCK, :] = vband[tm:tm + BLOCK, :]
    uext[0:CONV_PAD, :] = uext[tm:tm + CONV_PAD, :]


def _const_spec(shape):
    nd = len(shape)
    return pl.BlockSpec(shape, lambda *_: (0,) * nd)


def _mixer_prompt(x, sinks, consts, tm):
    b, s, d = x.shape
    nt = s // tm
    (ng, win, bdq, bdk, gqlo, gqhi, gk, bias_p, cw, cb, lg, lb, wout) = consts
    kern = functools.partial(_mixer_prompt_kernel, tm=tm, nt=nt)
    cspecs = [_const_spec(a.shape) for a in consts]
    return pl.pallas_call(
        kern,
        grid=(b, nt),
        in_specs=[pl.BlockSpec(memory_space=pltpu.SMEM),
                  pl.BlockSpec((None, tm, d), lambda i, j: (i, j, 0))] + cspecs,
        out_specs=[pl.BlockSpec((None, tm, d), lambda i, j: (i, j, 0)),
                   pl.BlockSpec((None, BLOCK, KV_DIM), lambda i, j: (i, 0, 0)),
                   pl.BlockSpec((None, BLOCK, KV_DIM), lambda i, j: (i, 0, 0)),
                   pl.BlockSpec((None, CONV_PAST, D_CONV), lambda i, j: (i, 0, 0))],
        out_shape=[jax.ShapeDtypeStruct((b, s, d), F32),
                   jax.ShapeDtypeStruct((b, BLOCK, KV_DIM), F32),
                   jax.ShapeDtypeStruct((b, BLOCK, KV_DIM), F32),
                   jax.ShapeDtypeStruct((b, CONV_PAST, D_CONV), F32)],
        scratch_shapes=[pltpu.VMEM((BLOCK + tm, KV_DIM), F32),
                        pltpu.VMEM((BLOCK + tm, KV_DIM), F32),
                        pltpu.VMEM((CONV_PAD + tm, D_CONV), F32),
                        pltpu.VMEM((SUBLANES - 1, tm + CONV_PAD - SUBLANES, D_CONV), F32)],
        compiler_params=pltpu.CompilerParams(dimension_semantics=("arbitrary", "arbitrary"),
                                             vmem_limit_bytes=VMEM_LIMIT),
        name="mixer_prompt",
    )(sinks, x, *consts)


def _mixer_sample_kernel(x_ref, ck_ref, cv_ref, st_ref, ng_ref, win_ref, bdq_ref, bdk_ref, gqlo_ref, gqhi_ref,
                         gk_ref, bias_ref, sinkrow_ref, cw_ref, cb_ref, lg_ref, lb_ref, wout_ref,
                         xag_ref, wmq_ref, gmq_ref,
                         x1_ref, qm_ref, ko_ref, vo_ref, co_ref,
                         kall, vall, qs, s_scr, r_scr, u_scr, y_scr, *, g, ts):
    wb = ck_ref.shape[1]
    x = x_ref[...]
    q_lo, q_hi, kn, v, u = _mixer_proj(x, ng_ref, win_ref, bdq_ref, bdk_ref, gqlo_ref, gqhi_ref, gk_ref)
    kall[:, 0:wb, :] = ck_ref[...]
    kall[:, wb:wb + ts, :] = kn.reshape(g, ts, KV_DIM)
    vall[:, 0:wb, :] = cv_ref[...]
    vall[:, wb:wb + ts, :] = v.reshape(g, ts, KV_DIM)
    ko_ref[...] = kall[:, ts:wb + ts, :]
    vo_ref[...] = vall[:, ts:wb + ts, :]

    for lc in range(D_CONV // LANES):
        ls = slice(lc * LANES, (lc + 1) * LANES)
        u_scr[lc] = u[:, ls]
        u_t = [u_scr[lc, pl.ds(t, g, stride=ts), :] for t in range(ts)]

        def frame(r):
            return st_ref[r, :, ls] if r < CONV_PAST else u_t[r - CONV_PAST]

        for t in range(ts):
            acc = jnp.zeros((g, LANES), F32)
            for j in range(CONV_WIDTH):
                acc = acc + cw_ref[j:j + 1, ls] * frame(t + j)
            y_scr[lc, pl.ds(t, g, stride=ts), :] = acc
        for r in range(CONV_PAST):
            co_ref[r, :, ls] = frame(r + ts)
    y = _conv_ln_silu(jnp.concatenate([y_scr[lc] for lc in range(D_CONV // LANES)], axis=1), cb_ref, lg_ref, lb_ref)

    for hd in range(N_HEADS):
        tile = (q_lo if hd % 2 == 0 else q_hi)[:, (hd // 2) * LANES:(hd // 2 + 1) * LANES]
        if hd % 2 != hd // HEADS_PER_KV:
            tile = pltpu.roll(tile, HALF, axis=1)
        qs[:, hd * ts:(hd + 1) * ts, :] = tile.reshape(g, ts, LANES)
    for n in range(g):
        s_scr[n] = lax.dot_general(qs[n].astype(BF16), kall[n].astype(BF16), (((1,), (1,)), ((), ())),
                                   preferred_element_type=F32)
    s_scr[...] = _sink_softmax(s_scr[...] + bias_ref[...], sinkrow_ref[...])
    for n in range(g):
        r_scr[n] = jnp.dot(s_scr[n].astype(BF16), vall[n].astype(BF16), preferred_element_type=F32)
    r = r_scr[...].reshape(g * N_HEADS * ts, LANES)
    r_sw = pltpu.roll(r, HALF, axis=1).reshape(g, N_HEADS * ts, LANES)
    r = r.reshape(g, N_HEADS * ts, LANES)
    lo = lax.broadcasted_iota(jnp.int32, (g * ts, LANES), 1) < HALF
    heads = [(r if hd // HEADS_PER_KV == hd % 2 else r_sw)[:, hd * ts:(hd + 1) * ts, :].reshape(g * ts, LANES)
             for hd in range(N_HEADS)]
    attn = jnp.concatenate([jnp.where(lo, heads[2 * i], heads[2 * i + 1]) for i in range(N_HEADS // 2)], axis=1)

    x1 = (x + jnp.dot(attn.astype(BF16), wout_ref[0:D_ATTN, :], preferred_element_type=F32)
          + jnp.dot(y.astype(BF16), wout_ref[D_ATTN:, :], preferred_element_type=F32))
    x1_ref[...] = x1
    qm_ref[...] = _mem_query(x1, xag_ref, wmq_ref, gmq_ref)


def _mixer_sample(xs2d, ck, cv, st, consts, g, ts):
    n, d = xs2d.shape
    nb, wb = ck.shape[0], ck.shape[1]
    rows = g * ts
    cspecs = [_const_spec(a.shape) for a in consts]
    kern = functools.partial(_mixer_sample_kernel, g=g, ts=ts)
    return pl.pallas_call(
        kern,
        grid=(nb // g,),
        in_specs=[pl.BlockSpec((rows, d), lambda i: (i, 0)),
                  pl.BlockSpec((g, wb, KV_DIM), lambda i: (i, 0, 0)),
                  pl.BlockSpec((g, wb, KV_DIM), lambda i: (i, 0, 0)),
                  pl.BlockSpec((CONV_PAST, g, D_CONV), lambda i: (0, i, 0))] + cspecs,
        out_specs=[pl.BlockSpec((rows, d), lambda i: (i, 0)),
                   pl.BlockSpec((rows, d), lambda i: (i, 0)),
                   pl.BlockSpec((g, wb, KV_DIM), lambda i: (i, 0, 0)),
                   pl.BlockSpec((g, wb, KV_DIM), lambda i: (i, 0, 0)),
                   pl.BlockSpec((CONV_PAST, g, D_CONV), lambda i: (0, i, 0))],
        out_shape=[jax.ShapeDtypeStruct((n, d), F32),
                   jax.ShapeDtypeStruct((n, d), F32),
                   jax.ShapeDtypeStruct((nb, wb, KV_DIM), F32),
                   jax.ShapeDtypeStruct((nb, wb, KV_DIM), F32),
                   jax.ShapeDtypeStruct((CONV_PAST, nb, D_CONV), F32)],
        scratch_shapes=[pltpu.VMEM((g, wb + ts, KV_DIM), F32),
                        pltpu.VMEM((g, wb + ts, KV_DIM), F32),
                        pltpu.VMEM((g, N_HEADS * ts, LANES), F32),
                        pltpu.VMEM((g, N_HEADS * ts, wb + ts), F32),
                        pltpu.VMEM((g, N_HEADS * ts, LANES), F32),
                        pltpu.VMEM((D_CONV // LANES, rows, LANES), F32),
                        pltpu.VMEM((D_CONV // LANES, rows, LANES), F32)],
        compiler_params=pltpu.CompilerParams(dimension_semantics=("arbitrary",),
                                             vmem_limit_bytes=VMEM_LIMIT),
        name="mixer_sample",
    )(xs2d, ck, cv, st, *consts)


def _memory_kv_kernel(mem_ref, g_ref, wmk_ref, wmv_ref, gk_ref, k_ref, v_ref):
    h = _rms(mem_ref[...], g_ref[...]).astype(BF16)
    k = jnp.dot(h, wmk_ref[...], preferred_element_type=F32)
    parts = []
    for hd in range(MEM_HEADS):
        kh = k[:, hd * MEM_HEAD_DIM:(hd + 1) * MEM_HEAD_DIM]
        parts.append(kh * lax.rsqrt(jnp.mean(kh * kh, axis=-1, keepdims=True) + EPS))
    k_ref[...] = jnp.concatenate(parts, axis=1) * gk_ref[...]
    v_ref[...] = jnp.dot(h, wmv_ref[...], preferred_element_type=F32)


def _memory_kv(mem2d, g, wmk, wmv, gk, tm):
    n, d = mem2d.shape
    consts = (g, wmk, wmv, gk)
    return pl.pallas_call(
        _memory_kv_kernel,
        grid=(n // tm,),
        in_specs=[pl.BlockSpec((tm, d), lambda i: (i, 0))] + [_const_spec(a.shape) for a in consts],
        out_specs=[pl.BlockSpec((tm, d), lambda i: (i, 0)), pl.BlockSpec((tm, d), lambda i: (i, 0))],
        out_shape=[jax.ShapeDtypeStruct((n, d), F32), jax.ShapeDtypeStruct((n, d), F32)],
        compiler_params=pltpu.CompilerParams(dimension_semantics=("arbitrary",),
                                             vmem_limit_bytes=VMEM_LIMIT),
        name="memory_kv",
    )(mem2d, *consts)


def _cross_prompt_kernel(x1_ref, mk_ref, mv_ref, xag_ref, wmq_ref, gmq_ref, wmo_ref, fg_ref, wr_ref, br_ref, tri_ref,
                         x2_ref, hn_ref, route_ref, cnt_ref, base):
    @pl.when((pl.program_id(0) == 0) & (pl.program_id(1) == 0))
    def _():
        base[...] = jnp.zeros_like(base)

    x1 = x1_ref[...]
    q = _mem_query(x1, xag_ref, wmq_ref, gmq_ref)
    o = _mem_attend(q, _split_heads(mk_ref[...].astype(BF16)), _split_heads(mv_ref[...].astype(BF16)))
    x2, hn, route, new_base = _cross_tail(o, x1, wmo_ref, fg_ref, wr_ref, br_ref, tri_ref, base[...])
    x2_ref[...] = x2
    hn_ref[...] = hn
    route_ref[...] = route
    base[...] = new_base
    cnt_ref[...] = new_base


def _cross_prompt(x1, mk, mv, consts, tm):
    b, s, d = x1.shape
    m = mk.shape[1]
    cspecs = [_const_spec(a.shape) for a in consts]
    return pl.pallas_call(
        _cross_prompt_kernel,
        grid=(b, s // tm),
        in_specs=[pl.BlockSpec((None, tm, d), lambda i, j: (i, j, 0)),
                  pl.BlockSpec((None, m, d), lambda i, j: (i, 0, 0)),
                  pl.BlockSpec((None, m, d), lambda i, j: (i, 0, 0))] + cspecs,
        out_specs=[pl.BlockSpec((None, tm, d), lambda i, j: (i, j, 0)),
                   pl.BlockSpec((None, tm, d // 2), lambda i, j: (i, j, 0)),
                   pl.BlockSpec((None, tm, LANES), lambda i, j: (i, j, 0)),
                   pl.BlockSpec((1, LANES), lambda i, j: (0, 0))],
        out_shape=[jax.ShapeDtypeStruct((b, s, d), F32),
                   jax.ShapeDtypeStruct((b, s, d // 2), jnp.int32),
                   jax.ShapeDtypeStruct((b, s, LANES), F32),
                   jax.ShapeDtypeStruct((1, LANES), F32)],
        scratch_shapes=[pltpu.VMEM((1, LANES), F32)],
        compiler_params=pltpu.CompilerParams(dimension_semantics=("arbitrary", "arbitrary"),
                                             vmem_limit_bytes=VMEM_LIMIT),
        name="cross_prompt",
    )(x1, mk, mv, *consts)


def _mem_attend_sample_kernel(qm_ref, *refs, g, ts):
    k_refs, v_refs, o_ref = refs[:g], refs[g:2 * g], refs[2 * g]
    halves = MEM_HEAD_DIM // LANES
    rows_per_pos = MEM_HEADS * halves
    mlen = k_refs[0].shape[0] // rows_per_pos

    def all_heads(ref):
        return jnp.concatenate([ref[pl.ds(c * MEM_HEADS + hd, mlen, stride=rows_per_pos), :]
                                for hd in range(MEM_HEADS) for c in range(halves)], axis=1).astype(BF16)

    rows = MEM_HEADS * ts
    own = (lax.broadcasted_iota(jnp.int32, (rows, MEM_HEADS * MEM_HEAD_DIM), 1) // MEM_HEAD_DIM
           == lax.broadcasted_iota(jnp.int32, (rows, MEM_HEADS * MEM_HEAD_DIM), 0) // ts)
    for n in range(g):
        q = qm_ref[n * ts:(n + 1) * ts, :]
        q_bd = jnp.where(own, jnp.concatenate([q] * MEM_HEADS, axis=0), 0.0).astype(BF16)
        s = lax.dot_general(q_bd, all_heads(k_refs[n]), (((1,), (1,)), ((), ())), preferred_element_type=F32)
        m = jnp.max(s, axis=-1, keepdims=True)
        p = jnp.exp(s - m)
        p = p * (1.0 / jnp.sum(p, axis=-1, keepdims=True))
        r = jnp.dot(p.astype(BF16), all_heads(v_refs[n]), preferred_element_type=F32)
        o_ref[n * ts:(n + 1) * ts, :] = jnp.concatenate(
            [r[hd * ts:(hd + 1) * ts, hd * MEM_HEAD_DIM:(hd + 1) * MEM_HEAD_DIM] for hd in range(MEM_HEADS)], axis=1)


def _mem_cache_rows(cache):
    nb, mlen = cache.shape[:2]
    halves = MEM_HEAD_DIM // LANES
    return (cache.reshape(nb, mlen, MEM_HEADS, halves, LANES).transpose(0, 1, 3, 2, 4)
            .reshape(nb, mlen * halves * MEM_HEADS, LANES))


def _mem_attend_sample(qm, mk, mv, g, ts):
    n, d = qm.shape
    nb, rows = mk.shape[0], mk.shape[1]
    kern = functools.partial(_mem_attend_sample_kernel, g=g, ts=ts)

    def seq_spec(j):
        return pl.BlockSpec((None, rows, LANES), lambda i: (g * i + j, 0, 0))

    return pl.pallas_call(
        kern,
        grid=(nb // g,),
        in_specs=[pl.BlockSpec((g * ts, d), lambda i: (i, 0))] + [seq_spec(j) for j in range(g)] * 2,
        out_specs=pl.BlockSpec((g * ts, d), lambda i: (i, 0)),
        out_shape=jax.ShapeDtypeStruct((n, d), F32),
        compiler_params=pltpu.CompilerParams(dimension_semantics=("arbitrary",),
                                             vmem_limit_bytes=VMEM_LIMIT),
        name="mem_attend_sample",
    )(qm, *([mk] * g), *([mv] * g))


def _cross_tail_kernel(o_ref, x1_ref, wmo_ref, fg_ref, wr_ref, br_ref, tri_ref,
                       x2_ref, hn_ref, route_ref, cnt_ref, base):
    @pl.when(pl.program_id(0) == 0)
    def _():
        base[...] = jnp.zeros_like(base)

    x2, hn, route, new_base = _cross_tail(o_ref[...], x1_ref[...], wmo_ref, fg_ref, wr_ref, br_ref, tri_ref,
                                          base[...])
    x2_ref[...] = x2
    hn_ref[...] = hn
    route_ref[...] = route
    base[...] = new_base
    cnt_ref[...] = new_base


def _cross_tail_call(o, x1, consts, tm):
    n, d = x1.shape
    cspecs = [_const_spec(a.shape) for a in consts]
    return pl.pallas_call(
        _cross_tail_kernel,
        grid=(n // tm,),
        in_specs=[pl.BlockSpec((tm, d), lambda i: (i, 0)), pl.BlockSpec((tm, d), lambda i: (i, 0))] + cspecs,
        out_specs=[pl.BlockSpec((tm, d), lambda i: (i, 0)),
                   pl.BlockSpec((tm, d // 2), lambda i: (i, 0)),
                   pl.BlockSpec((tm, LANES), lambda i: (i, 0)),
                   pl.BlockSpec((1, LANES), lambda i: (0, 0))],
        out_shape=[jax.ShapeDtypeStruct((n, d), F32),
                   jax.ShapeDtypeStruct((n, d // 2), jnp.int32),
                   jax.ShapeDtypeStruct((n, LANES), F32),
                   jax.ShapeDtypeStruct((1, LANES), F32)],
        scratch_shapes=[pltpu.VMEM((1, LANES), F32)],
        compiler_params=pltpu.CompilerParams(dimension_semantics=("arbitrary",),
                                             vmem_limit_bytes=VMEM_LIMIT),
        name="cross_tail",
    )(o, x1, *consts)


def _sc_rows_per_step(per_worker):
    step = min(SC_ROWS_PER_STEP, per_worker)
    assert per_worker % step == 0 and step % 8 == 0
    return step


def _sc_gather_rows(table, idx):
    nrows = idx.shape[0]
    _, width = table.shape
    assert nrows % (8 * SC_WORKERS) == 0
    per_worker = nrows // SC_WORKERS
    step = _sc_rows_per_step(per_worker)
    mesh = plsc.VectorSubcoreMesh(core_axis_name="c", subcore_axis_name="s")

    @functools.partial(
        pl.kernel, mesh=mesh, out_type=jax.ShapeDtypeStruct((nrows, width), table.dtype),
        scratch_types=[pltpu.VMEM((step,), jnp.int32),
                       pltpu.VMEM((step, width), table.dtype),
                       pltpu.SemaphoreType.DMA])
    def gather(table_hbm, idx_hbm, out_hbm, idx_v, rows_v, sem):
        wid = lax.axis_index("s") * SC_CORES + lax.axis_index("c")
        base = wid * per_worker

        @pl.loop(0, per_worker // step)
        def _(i):
            off = base + i * step
            pltpu.sync_copy(idx_hbm.at[pl.ds(off, step)], idx_v)
            pltpu.async_copy(table_hbm.at[idx_v], rows_v, sem).wait()
            pltpu.sync_copy(rows_v, out_hbm.at[pl.ds(off, step)])

    return gather(table, idx)


def _sc_scatter_rows(table, pos, nrows_out):
    n, width = table.shape
    assert n % (8 * SC_WORKERS) == 0 and pos.shape == (TOP_K * n,)
    per_worker = n // SC_WORKERS
    step = _sc_rows_per_step(per_worker)
    mesh = plsc.VectorSubcoreMesh(core_axis_name="c", subcore_axis_name="s")

    @functools.partial(
        pl.kernel, mesh=mesh, out_type=jax.ShapeDtypeStruct((nrows_out, width), table.dtype),
        scratch_types=[pltpu.VMEM((step,), jnp.int32)] * TOP_K
        + [pltpu.VMEM((step, width), table.dtype), pltpu.SemaphoreType.DMA])
    def scatter(table_hbm, pos_hbm, out_hbm, *scratch):
        idx_vs, rows_v, sem = scratch[:TOP_K], scratch[TOP_K], scratch[TOP_K + 1]
        wid = lax.axis_index("s") * SC_CORES + lax.axis_index("c")
        base = wid * per_worker

        @pl.loop(0, per_worker // step)
        def _(i):
            off = base + i * step
            for k in range(TOP_K):
                pltpu.sync_copy(pos_hbm.at[pl.ds(k * n + off, step)], idx_vs[k])
            pltpu.sync_copy(table_hbm.at[pl.ds(off, step)], rows_v)
            for k in range(TOP_K):
                pltpu.async_copy(rows_v, out_hbm.at[idx_vs[k]], sem).wait()

    return scatter(table, pos)


W_CHUNKS = 4


def _moe_grouped_kernel(te_ref, nv_ref, rv_ref, xs_ref, *refs):
    w_refs, ys_ref = refs[:3 * W_CHUNKS], refs[3 * W_CHUNKS]
    wg_b, wu_b, wd_b = refs[3 * W_CHUNKS + 1:]
    i = pl.program_id(0)

    @pl.when(i < nv_ref[0])
    def _():
        @pl.when((i == 0) | (te_ref[i] != te_ref[jnp.maximum(i - 1, 0)]))
        def _():
            for m, dst in enumerate((wg_b, wu_b, wd_b)):
                rows = dst.shape[0] // W_CHUNKS
                for c in range(W_CHUNKS):
                    dst[c * rows:(c + 1) * rows, :] = w_refs[m * W_CHUNKS + c][...].astype(BF16)

        xs = xs_ref[...]
        row = lax.broadcasted_iota(jnp.int32, xs.shape, 0)
        x = _unpack_bf16_pairs(jnp.where(row < rv_ref[i], xs, 0)).astype(BF16)
        a = jnp.dot(x, wg_b[...], preferred_element_type=F32)
        b = jnp.dot(x, wu_b[...], preferred_element_type=F32)
        act = (a * jax.nn.sigmoid(a)) * b
        y = jnp.dot(act.astype(BF16), wd_b[...], preferred_element_type=F32)
        ys_ref[...] = _pack_bf16_pairs(y)


def _moe_grouped(tile_expert, n_valid, rows_valid, xs, wg, wu, wd, tm):
    p, half = xs.shape
    ne, d, f = wg.shape

    def chunk_specs(rows, cols):
        return [pl.BlockSpec((None, None, rows // W_CHUNKS, cols), functools.partial(
            lambda c, i, te, nv, rv: (te[i], c, 0, 0), c)) for c in range(W_CHUNKS)]

    grid_spec = pltpu.PrefetchScalarGridSpec(
        num_scalar_prefetch=3,
        grid=(p // tm,),
        in_specs=[pl.BlockSpec((tm, half), lambda i, te, nv, rv: (i, 0))]
        + chunk_specs(d, f) + chunk_specs(d, f) + chunk_specs(f, d),
        out_specs=pl.BlockSpec((tm, half), lambda i, te, nv, rv: (i, 0)),
        scratch_shapes=[pltpu.VMEM((d, f), BF16), pltpu.VMEM((d, f), BF16), pltpu.VMEM((f, d), BF16)])
    chunked = lambda w: [w.reshape(ne, W_CHUNKS, w.shape[1] // W_CHUNKS, w.shape[2])] * W_CHUNKS
    return pl.pallas_call(
        _moe_grouped_kernel,
        grid_spec=grid_spec,
        out_shape=jax.ShapeDtypeStruct((p, half), jnp.int32),
        compiler_params=pltpu.CompilerParams(dimension_semantics=("arbitrary",),
                                             vmem_limit_bytes=VMEM_LIMIT),
        name="moe_grouped",
    )(tile_expert, n_valid, rows_valid, xs, *chunked(wg), *chunked(wu), *chunked(wd))


def _moe_combine_kernel(x2_ref, route_ref, y0_ref, y1_ref, out_ref):
    route = route_ref[...]
    out_ref[...] = (x2_ref[...] + route[:, 2:3] * _unpack_bf16_pairs(y0_ref[...])
                    + route[:, 3:4] * _unpack_bf16_pairs(y1_ref[...]))


def _moe_combine(x2, route, gathered, tm):
    n, d = x2.shape
    nt = n // tm
    return pl.pallas_call(
        _moe_combine_kernel,
        grid=(nt,),
        in_specs=[pl.BlockSpec((tm, d), lambda i: (i, 0)),
                  pl.BlockSpec((tm, LANES), lambda i: (i, 0)),
                  pl.BlockSpec((tm, d // 2), lambda i: (i, 0)),
                  pl.BlockSpec((tm, d // 2), lambda i: (i + nt, 0))],
        out_specs=pl.BlockSpec((tm, d), lambda i: (i, 0)),
        out_shape=jax.ShapeDtypeStruct((n, d), F32),
        compiler_params=pltpu.CompilerParams(dimension_semantics=("arbitrary",),
                                             vmem_limit_bytes=VMEM_LIMIT),
        name="moe_combine",
    )(x2, route, gathered, gathered)


def _moe(x2, hn_packed, route, counts, wg, wu, wd, tm_rows):
    n = x2.shape[0]
    p = (TOP_K * n // tm_rows + N_EXPERTS) * tm_rows
    n_tiles = p // tm_rows
    cnt = counts[0, :N_EXPERTS].astype(jnp.int32)
    tiles_per_e = (cnt + tm_rows - 1) // tm_rows
    e_ids = jnp.arange(N_EXPERTS, dtype=jnp.int32)
    tile_end = jnp.sum(jnp.where(e_ids[None, :] <= e_ids[:, None], tiles_per_e[None, :], 0), axis=1)
    tile_start = tile_end - tiles_per_e
    n_valid = tile_end[-1:]
    tile_ids = jnp.arange(n_tiles, dtype=jnp.int32)
    tile_expert = jnp.minimum(jnp.sum((tile_end[None, :] <= tile_ids[:, None]).astype(jnp.int32), axis=1),
                              N_EXPERTS - 1)
    mine = tile_expert[:, None] == e_ids[None, :]
    rows_left = jnp.sum(jnp.where(mine, cnt - (tile_ids[:, None] - tile_start) * tm_rows, 0), axis=1)
    rows_valid = jnp.clip(rows_left, 0, tm_rows).astype(jnp.int32)
    eidx = route[:, 0:TOP_K].astype(jnp.int32)
    row_start = jnp.sum(jnp.where(eidx[:, :, None] == e_ids, tile_start * tm_rows, 0), axis=2)
    pos = (row_start + route[:, 4:4 + TOP_K].astype(jnp.int32)).T.reshape(-1)

    xs = _sc_scatter_rows(hn_packed, pos, p)
    ys = _moe_grouped(tile_expert, n_valid, rows_valid, xs, wg, wu, wd, tm_rows)
    back = _sc_gather_rows(ys, pos)
    return _moe_combine(x2, route, back, min(TM_COMBINE, n))


def _block_diag(n, width):
    idx = np.arange(n) // width
    return jnp.asarray((idx[:, None] == idx[None, :]).astype(np.float32), dtype=BF16)


def _strict_lower(n):
    r = np.arange(n)
    return jnp.asarray((r[:, None] > r[None, :]).astype(np.float32), dtype=BF16)


def _alibi_slopes():
    return np.exp2(-8.0 * np.arange(1, N_HEADS + 1, dtype=np.float32) / N_HEADS).astype(np.float32)


def _prompt_bias():
    i = np.arange(BLOCK)[:, None]
    s = np.arange(2 * BLOCK)[None, :]
    dist = (i + BLOCK - s).astype(np.float32)
    mask = (dist >= 0) & (dist < WINDOW)
    first = mask & (s >= BLOCK)
    slopes = _alibi_slopes()[:, None, None]
    reg = np.where(mask[None], -slopes * dist[None], np.float32(NEG_INF))
    fst = np.where(first[None], -slopes * dist[None], np.float32(NEG_INF))
    return jnp.asarray(np.concatenate([reg, fst], axis=0).astype(np.float32))


def _sample_bias(ts, wb):
    i = np.arange(ts)[:, None]
    s = np.arange(wb + ts)[None, :]
    dist = (i + wb - s).astype(np.float32)
    mask = (dist >= 0) & (dist < WINDOW)
    slopes = _alibi_slopes()[:, None, None]
    b = np.where(mask[None], -slopes * dist[None], np.float32(NEG_INF)).astype(np.float32)
    return jnp.asarray(b.reshape(N_HEADS * ts, wb + ts))


def kernel(x_prompt, x_sample, cache_swa_k, cache_swa_v, state_conv, cache_mem_k, cache_mem_v, mem_prompt, norm_mix_g, w_in, q_norm_g, k_norm_g, attn_sinks, conv_dw_w, conv_dw_b, conv_ln_g, conv_ln_b, w_out, norm_xa_g, norm_mem_g, w_mq, w_mk, w_mv, mq_norm_g, mk_norm_g, w_mo, norm_ffn_g, w_router_group, b_router_group, w_router_expert, b_router_expert, w_exp_gate, w_exp_up, w_exp_down):
    depth = w_in.shape[0]
    bp, sp, d = x_prompt.shape
    nb, ts, _ = x_sample.shape
    wb = cache_swa_k.shape[2]
    mlen = mem_prompt.shape[1]
    assert d == D_MODEL and wb == WINDOW and sp % TM_PROMPT == 0 and nb % SEQ_PER_STEP == 0

    bdq = _block_diag(D_ATTN, HEAD_DIM)
    bdk = _block_diag(KV_DIM, HEAD_DIM)
    bias_p = _prompt_bias()
    bias_s = _sample_bias(ts, wb)
    lane_lo = (np.arange(D_ATTN) % LANES) < HALF
    row = lambda a: a.reshape(1, -1).astype(F32)

    xp = x_prompt
    xs = x_sample.reshape(nb * ts, d)
    kp_l, vp_l, cp_l, mkp_l, mvp_l, ks_l, vs_l, cs_l = [], [], [], [], [], [], [], []
    for l in range(depth):
        gq = jnp.tile(q_norm_g[l].astype(F32), N_HEADS) * (HEAD_DIM ** -0.5)
        gqlo = jnp.where(lane_lo, gq, 0.0).reshape(1, -1)
        gqhi = jnp.where(lane_lo, 0.0, gq).reshape(1, -1)
        gk = jnp.tile(k_norm_g[l].astype(F32), N_KV_HEADS).reshape(1, -1)
        sinks = attn_sinks[l].astype(F32)
        sinkrow = jnp.repeat(sinks, ts).reshape(N_HEADS * ts, 1)
        win = w_in[l].astype(BF16)
        wout = w_out[l].astype(BF16)
        wmq = w_mq[l].astype(BF16)
        wmo = w_mo[l].astype(BF16)
        gmq = (jnp.tile(mq_norm_g[l].astype(F32), MEM_HEADS) * (MEM_HEAD_DIM ** -0.5)).reshape(1, -1)
        gmk = jnp.tile(mk_norm_g[l].astype(F32), MEM_HEADS).reshape(1, -1)
        w_r = jnp.concatenate([w_router_expert[l], w_router_group[l],
                               jnp.zeros((d, LANES - N_EXPERTS - N_GROUPS), F32)], axis=1).astype(BF16)
        b_r = jnp.concatenate([b_router_expert[l], b_router_group[l],
                               jnp.zeros((LANES - N_EXPERTS - N_GROUPS,), F32)]).reshape(1, -1).astype(F32)
        wg, wu, wd = w_exp_gate[l], w_exp_up[l], w_exp_down[l]

        mix_consts = (row(norm_mix_g[l]), win, bdq, bdk, gqlo, gqhi, gk)
        conv_consts = (conv_dw_w[l].astype(F32), row(conv_dw_b[l]), row(conv_ln_g[l]), row(conv_ln_b[l]), wout)
        tail_consts = (wmo, row(norm_ffn_g[l]), w_r, b_r)

        x1p, kp, vp, cp = _mixer_prompt(xp, sinks, mix_consts + (bias_p,) + conv_consts, TM_PROMPT)
        mk, mv = _memory_kv(mem_prompt.reshape(bp * mlen, d), row(norm_mem_g[l]),
                            w_mk[l].astype(BF16), w_mv[l].astype(BF16), gmk, min(256, bp * mlen))
        mk = mk.reshape(bp, mlen, d)
        mv = mv.reshape(bp, mlen, d)
        x2p, hnp, routep, cntp = _cross_prompt(x1p, mk, mv, (row(norm_xa_g[l]), wmq, gmq) + tail_consts
                                               + (_strict_lower(TM_PROMPT),), TM_PROMPT)
        xp = _moe(x2p.reshape(bp * sp, d), hnp.reshape(bp * sp, d // 2), routep.reshape(bp * sp, LANES), cntp,
                  wg, wu, wd, TM_ROWS_PROMPT).reshape(bp, sp, d)
        kp_l.append(kp.reshape(bp, BLOCK, N_KV_HEADS, HEAD_DIM))
        vp_l.append(vp.reshape(bp, BLOCK, N_KV_HEADS, HEAD_DIM))
        cp_l.append(cp)
        mkp_l.append(mk.reshape(bp, mlen, MEM_HEADS, MEM_HEAD_DIM))
        mvp_l.append(mv.reshape(bp, mlen, MEM_HEADS, MEM_HEAD_DIM))

        x1s, qm, ksn, vsn, csn = _mixer_sample(
            xs, cache_swa_k[l].reshape(nb, wb, KV_DIM), cache_swa_v[l].reshape(nb, wb, KV_DIM),
            jnp.transpose(state_conv[l], (1, 0, 2)),
            mix_consts + (bias_s, sinkrow) + conv_consts + (row(norm_xa_g[l]), wmq, gmq), SEQ_PER_STEP, ts)
        o_s = _mem_attend_sample(qm, _mem_cache_rows(cache_mem_k[l]), _mem_cache_rows(cache_mem_v[l]),
                                 MEM_SEQ_PER_STEP, ts)
        tms = min(256, nb * ts)
        x2s, hns, routes, cnts = _cross_tail_call(o_s, x1s, tail_consts + (_strict_lower(tms),), tms)
        xs = _moe(x2s, hns, routes, cnts, wg, wu, wd, TM_ROWS_SAMPLE)
        ks_l.append(ksn.reshape(nb, wb, N_KV_HEADS, HEAD_DIM))
        vs_l.append(vsn.reshape(nb, wb, N_KV_HEADS, HEAD_DIM))
        cs_l.append(jnp.transpose(csn, (1, 0, 2)))

    st = lambda xs_: jnp.stack(xs_, axis=0)
    return (xp, xs.reshape(nb, ts, d), st(kp_l), st(vp_l), st(cp_l), st(mkp_l), st(mvp_l),
            st(ks_l), st(vs_l), st(cs_l))
```

```python
import functools

import numpy as np
import jax
import jax.numpy as jnp
from jax import lax
from jax.experimental import pallas as pl
from jax.experimental.pallas import tpu as pltpu
from jax.experimental.pallas import tpu_sc as plsc

F32 = jnp.float32
BF16 = jnp.bfloat16

D_MODEL = 1024
D_ATTN = 512
D_CONV = 512
HEAD_DIM = 64
N_HEADS = 8
N_KV_HEADS = 2
KV_DIM = N_KV_HEADS * HEAD_DIM
HEADS_PER_KV = N_HEADS // N_KV_HEADS
WINDOW = 128
BLOCK = 128
CONV_WIDTH = 31
CONV_PAST = CONV_WIDTH - 1
MEM_HEADS = 4
MEM_HEAD_DIM = 256
N_GROUPS = 4
EXPERTS_PER_GROUP = 8
N_EXPERTS = 32
D_EXPERT = 256
D_IN = D_ATTN + 2 * KV_DIM + 2 * D_CONV
EPS = 1e-6
NEG_INF = -1e30

LANES = 128
HALF = LANES // 2
SUBLANES = 8
CONV_PAD = 32
CONV_ROWS = 64
VMEM_LIMIT = 56 * 1024 * 1024

TM_PROMPT = 512
SEQ_PER_STEP = 16
MEM_SEQ_PER_STEP = 4
TOP_K = 2
SC_CORES = 2
SC_WORKERS = 32
SC_ROWS_PER_STEP = 64
TM_ROWS_PROMPT = 512
TM_ROWS_SAMPLE = 128
TM_COMBINE = 1024


def _rms(x, g):
    ms = jnp.mean(x * x, axis=-1, keepdims=True)
    return x * lax.rsqrt(ms + EPS) * g


def _group_mean_sq(x, bd_ref, width):
    x2 = x * x
    hi = x2.astype(BF16)
    lo = (x2 - hi.astype(F32)).astype(BF16)
    bd = bd_ref[...]
    s = jnp.dot(hi, bd, preferred_element_type=F32) + jnp.dot(lo, bd, preferred_element_type=F32)
    return s * (1.0 / width)


def _mixer_proj(x, ng_ref, win_ref, bdq_ref, bdk_ref, gqlo_ref, gqhi_ref, gk_ref):
    h = _rms(x, ng_ref[...]).astype(BF16)
    p = jnp.dot(h, win_ref[...], preferred_element_type=F32)
    q = p[:, :D_ATTN]
    k = p[:, D_ATTN:D_ATTN + KV_DIM]
    v = p[:, D_ATTN + KV_DIM:D_ATTN + 2 * KV_DIM]
    ua = p[:, D_ATTN + 2 * KV_DIM:D_ATTN + 2 * KV_DIM + D_CONV]
    ub = p[:, D_ATTN + 2 * KV_DIM + D_CONV:]
    qn = q * lax.rsqrt(_group_mean_sq(q, bdq_ref, HEAD_DIM) + EPS)
    q_lo = qn * gqlo_ref[...]
    q_hi = qn * gqhi_ref[...]
    kn = k * lax.rsqrt(_group_mean_sq(k, bdk_ref, HEAD_DIM) + EPS) * gk_ref[...]
    u = ua * jax.nn.sigmoid(ub)
    return q_lo, q_hi, kn, v, u


def _dup_halves(x):
    lo = lax.broadcasted_iota(jnp.int32, x.shape, 1) < HALF
    xr = pltpu.roll(x, HALF, axis=1)
    return jnp.where(lo, x, xr), jnp.where(lo, xr, x)


def _sink_softmax(s, sink):
    m = jnp.maximum(jnp.max(s, axis=-1, keepdims=True), sink)
    p = jnp.exp(s - m)
    denom = jnp.sum(p, axis=-1, keepdims=True) + jnp.exp(sink - m)
    return p * (1.0 / denom)


def _conv_ln_silu(y, cb_ref, lg_ref, lb_ref):
    y = y + cb_ref[...]
    mu = jnp.mean(y, axis=-1, keepdims=True)
    yc = y - mu
    yn = yc * lax.rsqrt(jnp.mean(yc * yc, axis=-1, keepdims=True) + EPS)
    z = yn * lg_ref[...] + lb_ref[...]
    return z * jax.nn.sigmoid(z)


def _mem_query(x1, xag_ref, wmq_ref, gmq_ref):
    h = _rms(x1, xag_ref[...]).astype(BF16)
    q = jnp.dot(h, wmq_ref[...], preferred_element_type=F32)
    parts = []
    for hd in range(MEM_HEADS):
        qh = q[:, hd * MEM_HEAD_DIM:(hd + 1) * MEM_HEAD_DIM]
        parts.append(qh * lax.rsqrt(jnp.mean(qh * qh, axis=-1, keepdims=True) + EPS))
    return jnp.concatenate(parts, axis=1) * gmq_ref[...]


def _mem_attend(q, k_heads, v_heads):
    outs = []
    for hd in range(MEM_HEADS):
        sl = slice(hd * MEM_HEAD_DIM, (hd + 1) * MEM_HEAD_DIM)
        s = lax.dot_general(q[:, sl].astype(BF16), k_heads[hd], (((1,), (1,)), ((), ())),
                            preferred_element_type=F32)
        m = jnp.max(s, axis=-1, keepdims=True)
        p = jnp.exp(s - m)
        p = p * (1.0 / jnp.sum(p, axis=-1, keepdims=True))
        outs.append(jnp.dot(p.astype(BF16), v_heads[hd], preferred_element_type=F32))
    return jnp.concatenate(outs, axis=1)


def _split_heads(x):
    return [x[:, hd * MEM_HEAD_DIM:(hd + 1) * MEM_HEAD_DIM] for hd in range(MEM_HEADS)]


def _pack_bf16_pairs(y):
    n = y.shape[1] // 2
    bits = pltpu.bitcast(y.astype(BF16).astype(F32), jnp.int32)
    return (bits[:, :n] & jnp.int32(-65536)) | lax.shift_right_logical(bits[:, n:], jnp.int32(16))


def _unpack_bf16_pairs(w):
    hi = pltpu.bitcast(w & jnp.int32(-65536), F32)
    lo = pltpu.bitcast(lax.shift_left(w, jnp.int32(16)), F32)
    return jnp.concatenate([hi, lo], axis=1)


def _cross_tail(o, x1, wmo_ref, fg_ref, wr_ref, br_ref, tri_ref, base):
    x2 = x1 + jnp.dot(o.astype(BF16), wmo_ref[...], preferred_element_type=F32)
    hn_f = _rms(x2, fg_ref[...])
    hn = hn_f.astype(BF16)
    logits = jnp.dot(hn, wr_ref[...], preferred_element_type=F32) + br_ref[...]
    lane = lax.broadcasted_iota(jnp.int32, logits.shape, 1)
    is_g = (lane >= N_EXPERTS) & (lane < N_EXPERTS + N_GROUPS)
    lg = jnp.where(is_g, logits, NEG_INF)
    gmax = jnp.max(lg, axis=-1, keepdims=True)
    gsel = jnp.min(jnp.where(lg == gmax, lane, 2 * LANES), axis=-1, keepdims=True) - N_EXPERTS
    pg_sel = 1.0 / jnp.sum(jnp.exp(lg - gmax), axis=-1, keepdims=True)
    in_grp = (lane >= gsel * EXPERTS_PER_GROUP) & (lane < (gsel + 1) * EXPERTS_PER_GROUP)
    le = jnp.where(in_grp, logits, NEG_INF)
    top1 = jnp.max(le, axis=-1, keepdims=True)
    idx1 = jnp.min(jnp.where(le == top1, lane, 2 * LANES), axis=-1, keepdims=True)
    le2 = jnp.where(lane == idx1, NEG_INF, le)
    top2 = jnp.max(le2, axis=-1, keepdims=True)
    idx2 = jnp.min(jnp.where(le2 == top2, lane, 2 * LANES), axis=-1, keepdims=True)
    e2 = jnp.exp(top2 - top1)
    inv = 1.0 / (1.0 + e2)
    gate1 = pg_sel * inv
    gate2 = pg_sel * (e2 * inv)
    used = jnp.where((lane == idx1) | (lane == idx2), 1.0, 0.0)
    before = jnp.dot(tri_ref[...], used.astype(BF16), preferred_element_type=F32) + base
    rank1 = jnp.sum(jnp.where(lane == idx1, before, 0.0), axis=-1, keepdims=True)
    rank2 = jnp.sum(jnp.where(lane == idx2, before, 0.0), axis=-1, keepdims=True)
    route = jnp.zeros_like(logits)
    for pos, val in enumerate((idx1.astype(F32), idx2.astype(F32), gate1, gate2, rank1, rank2)):
        route = jnp.where(lane == pos, val, route)
    return x2, _pack_bf16_pairs(hn_f), route, base + jnp.sum(used, axis=0, keepdims=True)


def _mixer_prompt_kernel(sink_ref, x_ref, ng_ref, win_ref, bdq_ref, bdk_ref, gqlo_ref, gqhi_ref, gk_ref,
                         bias_ref, cw_ref, cb_ref, lg_ref, lb_ref, wout_ref,
                         x1_ref, ko_ref, vo_ref, co_ref,
                         kband, vband, uext, ushift, *, tm, nt):
    t = pl.program_id(1)

    @pl.when(t == 0)
    def _():
        kband[0:BLOCK, :] = jnp.zeros((BLOCK, KV_DIM), F32)
        vband[0:BLOCK, :] = jnp.zeros((BLOCK, KV_DIM), F32)
        uext[0:CONV_PAD, :] = jnp.zeros((CONV_PAD, D_CONV), F32)

    x = x_ref[...]
    q_lo, q_hi, kn, v, u = _mixer_proj(x, ng_ref, win_ref, bdq_ref, bdk_ref, gqlo_ref, gqhi_ref, gk_ref)
    kband[BLOCK:BLOCK + tm, :] = kn
    vband[BLOCK:BLOCK + tm, :] = v
    uext[CONV_PAD:CONV_PAD + tm, :] = u

    @pl.when(t == nt - 1)
    def _():
        ko_ref[...] = kn[tm - BLOCK:, :]
        vo_ref[...] = v[tm - BLOCK:, :]
        co_ref[...] = uext[pl.ds(CONV_PAD + tm - CONV_PAST, CONV_PAST), :]

    kd = [a.astype(BF16) for a in _dup_halves(kband[...])]
    vd = _dup_halves(vband[...])
    lo = lax.broadcasted_iota(jnp.int32, vd[0].shape, 1) < HALF
    v_lo = [jnp.where(lo, a, 0.0).astype(BF16) for a in vd]
    v_hi = [jnp.where(lo, 0.0, a).astype(BF16) for a in vd]
    q_lo = q_lo.astype(BF16)
    q_hi = q_hi.astype(BF16)
    attn_blocks = []
    for j in range(tm // BLOCK):
        rows = slice(j * BLOCK, (j + 1) * BLOCK)
        keys = slice(j * BLOCK, j * BLOCK + 2 * BLOCK)
        bias_base = jnp.where(t == 0, N_HEADS, 0) if j == 0 else 0
        tiles = []
        for c in range(N_KV_HEADS):
            q4 = jnp.concatenate(
                [(q_lo if (HEADS_PER_KV * c + a) % 2 == 0 else q_hi)[rows,
                  ((HEADS_PER_KV * c + a) // 2) * LANES:((HEADS_PER_KV * c + a) // 2 + 1) * LANES]
                 for a in range(HEADS_PER_KV)], axis=0)
            s_all = lax.dot_general(q4, kd[c][keys], (((1,), (1,)), ((), ())), preferred_element_type=F32)
            ps = []
            for a in range(HEADS_PER_KV):
                hd = HEADS_PER_KV * c + a
                s = s_all[a * BLOCK:(a + 1) * BLOCK] + bias_ref[bias_base + hd]
                ps.append(_sink_softmax(s, sink_ref[hd]).astype(BF16))
            vstack = jnp.concatenate([v_lo[c][keys], v_hi[c][keys]], axis=0)
            for i2 in range(HEADS_PER_KV // 2):
                pp = jnp.concatenate([ps[2 * i2], ps[2 * i2 + 1]], axis=1)
                tiles.append(jnp.dot(pp, vstack, preferred_element_type=F32))
        attn_blocks.append(jnp.concatenate(tiles, axis=1))
    attn = jnp.concatenate(attn_blocks, axis=0)

    off = CONV_PAD - CONV_PAST
    span = tm + CONV_PAD - SUBLANES
    for s in range(1, SUBLANES):
        ushift[s - 1] = uext[pl.ds(s, span), :]
    chunks = []
    for lc in range(D_CONV // LANES):
        ls = slice(lc * LANES, (lc + 1) * LANES)
        accs = [jnp.zeros((CONV_ROWS, LANES), F32) for _ in range(tm // CONV_ROWS)]
        for j in range(CONV_WIDTH):
            s, a = (off + j) % SUBLANES, (off + j) // SUBLANES
            wj = cw_ref[j:j + 1, ls]
            for rc in range(tm // CONV_ROWS):
                r0 = rc * CONV_ROWS + a * SUBLANES
                tap = uext[r0:r0 + CONV_ROWS, ls] if s == 0 else ushift[s - 1, r0:r0 + CONV_ROWS, ls]
                accs[rc] = accs[rc] + wj * tap
        chunks.append(jnp.concatenate(accs, axis=0))
    y = _conv_ln_silu(jnp.concatenate(chunks, axis=1), cb_ref, lg_ref, lb_ref)

    x1_ref[...] = (x + jnp.dot(attn.astype(BF16), wout_ref[0:D_ATTN, :], preferred_element_type=F32)
                   + jnp.dot(y.astype(BF16), wout_ref[D_ATTN:, :], preferred_element_type=F32))

    kband[0:BLOCK, :] = kband[tm:tm + BLOCK, :]
    vband[0:BLOCK, :] = vband[tm:tm + BLOCK, :]
    uext[0:CONV_PAD, :] = uext[tm:tm + CONV_PAD, :]


def _const_spec(shape):
    nd = len(shape)
    return pl.BlockSpec(shape, lambda *_: (0,) * nd)


def _mixer_prompt(x, sinks, consts, tm):
    b, s, d = x.shape
    nt = s // tm
    (ng, win, bdq, bdk, gqlo, gqhi, gk, bias_p, cw, cb, lg, lb, wout) = consts
    kern = functools.partial(_mixer_prompt_kernel, tm=tm, nt=nt)
    cspecs = [_const_spec(a.shape) for a in consts]
    return pl.pallas_call(
        kern,
        grid=(b, nt),
        in_specs=[pl.BlockSpec(memory_space=pltpu.SMEM),
                  pl.BlockSpec((None, tm, d), lambda i, j: (i, j, 0))] + cspecs,
        out_specs=[pl.BlockSpec((None, tm, d), lambda i, j: (i, j, 0)),
                   pl.BlockSpec((None, BLOCK, KV_DIM), lambda i, j: (i, 0, 0)),
                   pl.BlockSpec((None, BLOCK, KV_DIM), lambda i, j: (i, 0, 0)),
                   pl.BlockSpec((None, CONV_PAST, D_CONV), lambda i, j: (i, 0, 0))],
        out_shape=[jax.ShapeDtypeStruct((b, s, d), F32),
                   jax.ShapeDtypeStruct((b, BLOCK, KV_DIM), F32),
                   jax.ShapeDtypeStruct((b, BLOCK, KV_DIM), F32),
                   jax.ShapeDtypeStruct((b, CONV_PAST, D_CONV), F32)],
        scratch_shapes=[pltpu.VMEM((BLOCK + tm, KV_DIM), F32),
                        pltpu.VMEM((BLOCK + tm, KV_DIM), F32),
                        pltpu.VMEM((CONV_PAD + tm, D_CONV), F32),
                        pltpu.VMEM((SUBLANES - 1, tm + CONV_PAD - SUBLANES, D_CONV), F32)],
        compiler_params=pltpu.CompilerParams(dimension_semantics=("arbitrary", "arbitrary"),
                                             vmem_limit_bytes=VMEM_LIMIT),
        name="mixer_prompt",
    )(sinks, x, *consts)


def _mixer_sample_kernel(x_ref, ck_ref, cv_ref, st_ref, ng_ref, win_ref, bdq_ref, bdk_ref, gqlo_ref, gqhi_ref,
                         gk_ref, bias_ref, sinkrow_ref, cw_ref, cb_ref, lg_ref, lb_ref, wout_ref,
                         xag_ref, wmq_ref, gmq_ref,
                         x1_ref, qm_ref, ko_ref, vo_ref, co_ref,
                         kall, vall, qs, s_scr, r_scr, u_scr, y_scr, *, g, ts):
    wb = ck_ref.shape[1]
    x = x_ref[...]
    q_lo, q_hi, kn, v, u = _mixer_proj(x, ng_ref, win_ref, bdq_ref, bdk_ref, gqlo_ref, gqhi_ref, gk_ref)
    kall[:, 0:wb, :] = ck_ref[...]
    kall[:, wb:wb + ts, :] = kn.reshape(g, ts, KV_DIM)
    vall[:, 0:wb, :] = cv_ref[...]
    vall[:, wb:wb + ts, :] = v.reshape(g, ts, KV_DIM)
    ko_ref[...] = kall[:, ts:wb + ts, :]
    vo_ref[...] = vall[:, ts:wb + ts, :]

    for lc in range(D_CONV // LANES):
        ls = slice(lc * LANES, (lc + 1) * LANES)
        u_scr[lc] = u[:, ls]
        u_t = [u_scr[lc, pl.ds(t, g, stride=ts), :] for t in range(ts)]

        def frame(r):
            return st_ref[r, :, ls] if r < CONV_PAST else u_t[r - CONV_PAST]

        for t in range(ts):
            acc = jnp.zeros((g, LANES), F32)
            for j in range(CONV_WIDTH):
                acc = acc + cw_ref[j:j + 1, ls] * frame(t + j)
            y_scr[lc, pl.ds(t, g, stride=ts), :] = acc
        for r in range(CONV_PAST):
            co_ref[r, :, ls] = frame(r + ts)
    y = _conv_ln_silu(jnp.concatenate([y_scr[lc] for lc in range(D_CONV // LANES)], axis=1), cb_ref, lg_ref, lb_ref)

    for hd in range(N_HEADS):
        tile = (q_lo if hd % 2 == 0 else q_hi)[:, (hd // 2) * LANES:(hd // 2 + 1) * LANES]
        if hd % 2 != hd // HEADS_PER_KV:
            tile = pltpu.roll(tile, HALF, axis=1)
        qs[:, hd * ts:(hd + 1) * ts, :] = tile.reshape(g, ts, LANES)
    for n in range(g):
        s_scr[n] = lax.dot_general(qs[n].astype(BF16), kall[n].astype(BF16), (((1,), (1,)), ((), ())),
                                   preferred_element_type=F32)
    s_scr[...] = _sink_softmax(s_scr[...] + bias_ref[...], sinkrow_ref[...])
    for n in range(g):
        r_scr[n] = jnp.dot(s_scr[n].astype(BF16), vall[n].astype(BF16), preferred_element_type=F32)
    r = r_scr[...].reshape(g * N_HEADS * ts, LANES)
    r_sw = pltpu.roll(r, HALF, axis=1).reshape(g, N_HEADS * ts, LANES)
    r = r.reshape(g, N_HEADS * ts, LANES)
    lo = lax.broadcasted_iota(jnp.int32, (g * ts, LANES), 1) < HALF
    heads = [(r if hd // HEADS_PER_KV == hd % 2 else r_sw)[:, hd * ts:(hd + 1) * ts, :].reshape(g * ts, LANES)
             for hd in range(N_HEADS)]
    attn = jnp.concatenate([jnp.where(lo, heads[2 * i], heads[2 * i + 1]) for i in range(N_HEADS // 2)], axis=1)

    x1 = (x + jnp.dot(attn.astype(BF16), wout_ref[0:D_ATTN, :], preferred_element_type=F32)
          + jnp.dot(y.astype(BF16), wout_ref[D_ATTN:, :], preferred_element_type=F32))
    x1_ref[...] = x1
    qm_ref[...] = _mem_query(x1, xag_ref, wmq_ref, gmq_ref)


def _mixer_sample(xs2d, ck, cv, st, consts, g, ts):
    n, d = xs2d.shape
    nb, wb = ck.shape[0], ck.shape[1]
    rows = g * ts
    cspecs = [_const_spec(a.shape) for a in consts]
    kern = functools.partial(_mixer_sample_kernel, g=g, ts=ts)
    return pl.pallas_call(
        kern,
        grid=(nb // g,),
        in_specs=[pl.BlockSpec((rows, d), lambda i: (i, 0)),
                  pl.BlockSpec((g, wb, KV_DIM), lambda i: (i, 0, 0)),
                  pl.BlockSpec((g, wb, KV_DIM), lambda i: (i, 0, 0)),
                  pl.BlockSpec((CONV_PAST, g, D_CONV), lambda i: (0, i, 0))] + cspecs,
        out_specs=[pl.BlockSpec((rows, d), lambda i: (i, 0)),
                   pl.BlockSpec((rows, d), lambda i: (i, 0)),
                   pl.BlockSpec((g, wb, KV_DIM), lambda i: (i, 0, 0)),
                   pl.BlockSpec((g, wb, KV_DIM), lambda i: (i, 0, 0)),
                   pl.BlockSpec((CONV_PAST, g, D_CONV), lambda i: (0, i, 0))],
        out_shape=[jax.ShapeDtypeStruct((n, d), F32),
                   jax.ShapeDtypeStruct((n, d), F32),
                   jax.ShapeDtypeStruct((nb, wb, KV_DIM), F32),
                   jax.ShapeDtypeStruct((nb, wb, KV_DIM), F32),
                   jax.ShapeDtypeStruct((CONV_PAST, nb, D_CONV), F32)],
        scratch_shapes=[pltpu.VMEM((g, wb + ts, KV_DIM), F32),
                        pltpu.VMEM((g, wb + ts, KV_DIM), F32),
                        pltpu.VMEM((g, N_HEADS * ts, LANES), F32),
                        pltpu.VMEM((g, N_HEADS * ts, wb + ts), F32),
                        pltpu.VMEM((g, N_HEADS * ts, LANES), F32),
                        pltpu.VMEM((D_CONV // LANES, rows, LANES), F32),
                        pltpu.VMEM((D_CONV // LANES, rows, LANES), F32)],
        compiler_params=pltpu.CompilerParams(dimension_semantics=("arbitrary",),
                                             vmem_limit_bytes=VMEM_LIMIT),
        name="mixer_sample",
    )(xs2d, ck, cv, st, *consts)


def _memory_kv_kernel(mem_ref, g_ref, wmk_ref, wmv_ref, gk_ref, k_ref, v_ref):
    h = _rms(mem_ref[...], g_ref[...]).astype(BF16)
    k = jnp.dot(h, wmk_ref[...], preferred_element_type=F32)
    parts = []
    for hd in range(MEM_HEADS):
        kh = k[:, hd * MEM_HEAD_DIM:(hd + 1) * MEM_HEAD_DIM]
        parts.append(kh * lax.rsqrt(jnp.mean(kh * kh, axis=-1, keepdims=True) + EPS))
    k_ref[...] = jnp.concatenate(parts, axis=1) * gk_ref[...]
    v_ref[...] = jnp.dot(h, wmv_ref[...], preferred_element_type=F32)


def _memory_kv(mem2d, g, wmk, wmv, gk, tm):
    n, d = mem2d.shape
    consts = (g, wmk, wmv, gk)
    return pl.pallas_call(
        _memory_kv_kernel,
        grid=(n // tm,),
        in_specs=[pl.BlockSpec((tm, d), lambda i: (i, 0))] + [_const_spec(a.shape) for a in consts],
        out_specs=[pl.BlockSpec((tm, d), lambda i: (i, 0)), pl.BlockSpec((tm, d), lambda i: (i, 0))],
        out_shape=[jax.ShapeDtypeStruct((n, d), F32), jax.ShapeDtypeStruct((n, d), F32)],
        compiler_params=pltpu.CompilerParams(dimension_semantics=("arbitrary",),
                                             vmem_limit_bytes=VMEM_LIMIT),
        name="memory_kv",
    )(mem2d, *consts)


def _cross_prompt_kernel(x1_ref, mk_ref, mv_ref, xag_ref, wmq_ref, gmq_ref, wmo_ref, fg_ref, wr_ref, br_ref, tri_ref,
                         x2_ref, hn_ref, route_ref, cnt_ref, base):
    @pl.when((pl.program_id(0) == 0) & (pl.program_id(1) == 0))
    def _():
        base[...] = jnp.zeros_like(base)

    x1 = x1_ref[...]
    q = _mem_query(x1, xag_ref, wmq_ref, gmq_ref)
    o = _mem_attend(q, _split_heads(mk_ref[...].astype(BF16)), _split_heads(mv_ref[...].astype(BF16)))
    x2, hn, route, new_base = _cross_tail(o, x1, wmo_ref, fg_ref, wr_ref, br_ref, tri_ref, base[...])
    x2_ref[...] = x2
    hn_ref[...] = hn
    route_ref[...] = route
    base[...] = new_base
    cnt_ref[...] = new_base


def _cross_prompt(x1, mk, mv, consts, tm):
    b, s, d = x1.shape
    m = mk.shape[1]
    cspecs = [_const_spec(a.shape) for a in consts]
    return pl.pallas_call(
        _cross_prompt_kernel,
        grid=(b, s // tm),
        in_specs=[pl.BlockSpec((None, tm, d), lambda i, j: (i, j, 0)),
                  pl.BlockSpec((None, m, d), lambda i, j: (i, 0, 0)),
                  pl.BlockSpec((None, m, d), lambda i, j: (i, 0, 0))] + cspecs,
        out_specs=[pl.BlockSpec((None, tm, d), lambda i, j: (i, j, 0)),
                   pl.BlockSpec((None, tm, d // 2), lambda i, j: (i, j, 0)),
                   pl.BlockSpec((None, tm, LANES), lambda i, j: (i, j, 0)),
                   pl.BlockSpec((1, LANES), lambda i, j: (0, 0))],
        out_shape=[jax.ShapeDtypeStruct((b, s, d), F32),
                   jax.ShapeDtypeStruct((b, s, d // 2), jnp.int32),
                   jax.ShapeDtypeStruct((b, s, LANES), F32),
                   jax.ShapeDtypeStruct((1, LANES), F32)],
        scratch_shapes=[pltpu.VMEM((1, LANES), F32)],
        compiler_params=pltpu.CompilerParams(dimension_semantics=("arbitrary", "arbitrary"),
                                             vmem_limit_bytes=VMEM_LIMIT),
        name="cross_prompt",
    )(x1, mk, mv, *consts)


def _mem_attend_sample_kernel(qm_ref, *refs, g, ts):
    k_refs, v_refs, o_ref = refs[:g], refs[g:2 * g], refs[2 * g]
    halves = MEM_HEAD_DIM // LANES
    rows_per_pos = MEM_HEADS * halves
    mlen = k_refs[0].shape[0] // rows_per_pos

    def all_heads(ref):
        return jnp.concatenate([ref[pl.ds(c * MEM_HEADS + hd, mlen, stride=rows_per_pos), :]
                                for hd in range(MEM_HEADS) for c in range(halves)], axis=1).astype(BF16)

    rows = MEM_HEADS * ts
    own = (lax.broadcasted_iota(jnp.int32, (rows, MEM_HEADS * MEM_HEAD_DIM), 1) // MEM_HEAD_DIM
           == lax.broadcasted_iota(jnp.int32, (rows, MEM_HEADS * MEM_HEAD_DIM), 0) // ts)
    for n in range(g):
        q = qm_ref[n * ts:(n + 1) * ts, :]
        q_bd = jnp.where(own, jnp.concatenate([q] * MEM_HEADS, axis=0), 0.0).astype(BF16)
        s = lax.dot_general(q_bd, all_heads(k_refs[n]), (((1,), (1,)), ((), ())), preferred_element_type=F32)
        m = jnp.max(s, axis=-1, keepdims=True)
        p = jnp.exp(s - m)
        p = p * (1.0 / jnp.sum(p, axis=-1, keepdims=True))
        r = jnp.dot(p.astype(BF16), all_heads(v_refs[n]), preferred_element_type=F32)
        o_ref[n * ts:(n + 1) * ts, :] = jnp.concatenate(
            [r[hd * ts:(hd + 1) * ts, hd * MEM_HEAD_DIM:(hd + 1) * MEM_HEAD_DIM] for hd in range(MEM_HEADS)], axis=1)


def _mem_cache_rows(cache):
    nb, mlen = cache.shape[:2]
    halves = MEM_HEAD_DIM // LANES
    return (cache.reshape(nb, mlen, MEM_HEADS, halves, LANES).transpose(0, 1, 3, 2, 4)
            .reshape(nb, mlen * halves * MEM_HEADS, LANES))


def _mem_attend_sample(qm, mk, mv, g, ts):
    n, d = qm.shape
    nb, rows = mk.shape[0], mk.shape[1]
    kern = functools.partial(_mem_attend_sample_kernel, g=g, ts=ts)

    def seq_spec(j):
        return pl.BlockSpec((None, rows, LANES), lambda i: (g * i + j, 0, 0))

    return pl.pallas_call(
        kern,
        grid=(nb // g,),
        in_specs=[pl.BlockSpec((g * ts, d), lambda i: (i, 0))] + [seq_spec(j) for j in range(g)] * 2,
        out_specs=pl.BlockSpec((g * ts, d), lambda i: (i, 0)),
        out_shape=jax.ShapeDtypeStruct((n, d), F32),
        compiler_params=pltpu.CompilerParams(dimension_semantics=("arbitrary",),
                                             vmem_limit_bytes=VMEM_LIMIT),
        name="mem_attend_sample",
    )(qm, *([mk] * g), *([mv] * g))


def _cross_tail_kernel(o_ref, x1_ref, wmo_ref, fg_ref, wr_ref, br_ref, tri_ref,
                       x2_ref, hn_ref, route_ref, cnt_ref, base):
    @pl.when(pl.program_id(0) == 0)
    def _():
        base[...] = jnp.zeros_like(base)

    x2, hn, route, new_base = _cross_tail(o_ref[...], x1_ref[...], wmo_ref, fg_ref, wr_ref, br_ref, tri_ref,
                                          base[...])
    x2_ref[...] = x2
    hn_ref[...] = hn
    route_ref[...] = route
    base[...] = new_base
    cnt_ref[...] = new_base


def _cross_tail_call(o, x1, consts, tm):
    n, d = x1.shape
    cspecs = [_const_spec(a.shape) for a in consts]
    return pl.pallas_call(
        _cross_tail_kernel,
        grid=(n // tm,),
        in_specs=[pl.BlockSpec((tm, d), lambda i: (i, 0)), pl.BlockSpec((tm, d), lambda i: (i, 0))] + cspecs,
        out_specs=[pl.BlockSpec((tm, d), lambda i: (i, 0)),
                   pl.BlockSpec((tm, d // 2), lambda i: (i, 0)),
                   pl.BlockSpec((tm, LANES), lambda i: (i, 0)),
                   pl.BlockSpec((1, LANES), lambda i: (0, 0))],
        out_shape=[jax.ShapeDtypeStruct((n, d), F32),
                   jax.ShapeDtypeStruct((n, d // 2), jnp.int32),
                   jax.ShapeDtypeStruct((n, LANES), F32),
                   jax.ShapeDtypeStruct((1, LANES), F32)],
        scratch_shapes=[pltpu.VMEM((1, LANES), F32)],
        compiler_params=pltpu.CompilerParams(dimension_semantics=("arbitrary",),
                                             vmem_limit_bytes=VMEM_LIMIT),
        name="cross_tail",
    )(o, x1, *consts)


def _sc_rows_per_step(per_worker):
    step = min(SC_ROWS_PER_STEP, per_worker)
    assert per_worker % step == 0 and step % 8 == 0
    return step


def _sc_gather_rows(table, idx):
    nrows = idx.shape[0]
    _, width = table.shape
    assert nrows % (8 * SC_WORKERS) == 0
    per_worker = nrows // SC_WORKERS
    step = _sc_rows_per_step(per_worker)
    mesh = plsc.VectorSubcoreMesh(core_axis_name="c", subcore_axis_name="s")

    @functools.partial(
        pl.kernel, mesh=mesh, out_type=jax.ShapeDtypeStruct((nrows, width), table.dtype),
        scratch_types=[pltpu.VMEM((step,), jnp.int32),
                       pltpu.VMEM((step, width), table.dtype),
                       pltpu.SemaphoreType.DMA])
    def gather(table_hbm, idx_hbm, out_hbm, idx_v, rows_v, sem):
        wid = lax.axis_index("s") * SC_CORES + lax.axis_index("c")
        base = wid * per_worker

        @pl.loop(0, per_worker // step)
        def _(i):
            off = base + i * step
            pltpu.sync_copy(idx_hbm.at[pl.ds(off, step)], idx_v)
            pltpu.async_copy(table_hbm.at[idx_v], rows_v, sem).wait()
            pltpu.sync_copy(rows_v, out_hbm.at[pl.ds(off, step)])

    return gather(table, idx)


def _sc_scatter_rows(table, pos, nrows_out):
    n, width = table.shape
    assert n % (8 * SC_WORKERS) == 0 and pos.shape == (TOP_K * n,)
    per_worker = n // SC_WORKERS
    step = _sc_rows_per_step(per_worker)
    mesh = plsc.VectorSubcoreMesh(core_axis_name="c", subcore_axis_name="s")

    @functools.partial(
        pl.kernel, mesh=mesh, out_type=jax.ShapeDtypeStruct((nrows_out, width), table.dtype),
        scratch_types=[pltpu.VMEM((step,), jnp.int32)] * TOP_K
        + [pltpu.VMEM((step, width), table.dtype), pltpu.SemaphoreType.DMA])
    def scatter(table_hbm, pos_hbm, out_hbm, *scratch):
        idx_vs, rows_v, sem = scratch[:TOP_K], scratch[TOP_K], scratch[TOP_K + 1]
        wid = lax.axis_index("s") * SC_CORES + lax.axis_index("c")
        base = wid * per_worker

        @pl.loop(0, per_worker // step)
        def _(i):
            off = base + i * step
            for k in range(TOP_K):
                pltpu.sync_copy(pos_hbm.at[pl.ds(k * n + off, step)], idx_vs[k])
            pltpu.sync_copy(table_hbm.at[pl.ds(off, step)], rows_v)
            for k in range(TOP_K):
                pltpu.async_copy(rows_v, out_hbm.at[idx_vs[k]], sem).wait()

    return scatter(table, pos)


W_CHUNKS = 4


def _expert_weight_copies(e, slot, w_hbm, w_f32, sems):
    copies = []
    for m, (src, dst) in enumerate(zip(w_hbm, w_f32)):
        rows = dst.shape[1] // W_CHUNKS
        for c in range(W_CHUNKS):
            sl = pl.ds(c * rows, rows)
            copies.append(pltpu.make_async_copy(src.at[e, sl, :], dst.at[slot, sl, :], sems.at[m * W_CHUNKS + c]))
    return copies


def _moe_grouped_kernel(te_ref, nv_ref, rv_ref, nxt_ref, par_ref, xs_ref, wg_hbm, wu_hbm, wd_hbm, ys_ref,
                        wg_f, wu_f, wd_f, wg_b, wu_b, wd_b, sems):
    i = pl.program_id(0)
    w_hbm, w_f32 = (wg_hbm, wu_hbm, wd_hbm), (wg_f, wu_f, wd_f)

    @pl.when(i == 0)
    def _():
        for cp in _expert_weight_copies(te_ref[0], par_ref[0], w_hbm, w_f32, sems):
            cp.start()

    @pl.when(i < nv_ref[0])
    def _():
        @pl.when((i == 0) | (te_ref[i] != te_ref[jnp.maximum(i - 1, 0)]))
        def _():
            slot = par_ref[i]
            for cp in _expert_weight_copies(te_ref[i], slot, w_hbm, w_f32, sems):
                cp.wait()
            for src, dst in zip(w_f32, (wg_b, wu_b, wd_b)):
                dst[...] = src[slot].astype(BF16)

            @pl.when(nxt_ref[i] >= 0)
            def _():
                for cp in _expert_weight_copies(nxt_ref[i], 1 - slot, w_hbm, w_f32, sems):
                    cp.start()

        xs = xs_ref[...]
        row = lax.broadcasted_iota(jnp.int32, xs.shape, 0)
        x = _unpack_bf16_pairs(jnp.where(row < rv_ref[i], xs, 0)).astype(BF16)
        a = jnp.dot(x, wg_b[...], preferred_element_type=F32)
        b = jnp.dot(x, wu_b[...], preferred_element_type=F32)
        act = (a * jax.nn.sigmoid(a)) * b
        y = jnp.dot(act.astype(BF16), wd_b[...], preferred_element_type=F32)
        ys_ref[...] = _pack_bf16_pairs(y)


def _moe_grouped(tile_expert, n_valid, rows_valid, next_expert, run_parity, xs, wg, wu, wd, tm):
    p, half = xs.shape
    _, d, f = wg.shape
    grid_spec = pltpu.PrefetchScalarGridSpec(
        num_scalar_prefetch=5,
        grid=(p // tm,),
        in_specs=[pl.BlockSpec((tm, half), lambda i, *_: (i, 0))] + [pl.BlockSpec(memory_space=pl.ANY)] * 3,
        out_specs=pl.BlockSpec((tm, half), lambda i, *_: (i, 0)),
        scratch_shapes=[pltpu.VMEM((2, d, f), F32), pltpu.VMEM((2, d, f), F32), pltpu.VMEM((2, f, d), F32),
                        pltpu.VMEM((d, f), BF16), pltpu.VMEM((d, f), BF16), pltpu.VMEM((f, d), BF16),
                        pltpu.SemaphoreType.DMA((3 * W_CHUNKS,))])
    return pl.pallas_call(
        _moe_grouped_kernel,
        grid_spec=grid_spec,
        out_shape=jax.ShapeDtypeStruct((p, half), jnp.int32),
        compiler_params=pltpu.CompilerParams(dimension_semantics=("arbitrary",),
                                             vmem_limit_bytes=VMEM_LIMIT),
        name="moe_grouped",
    )(tile_expert, n_valid, rows_valid, next_expert, run_parity, xs, wg, wu, wd)


def _moe_combine_kernel(x2_ref, route_ref, y0_ref, y1_ref, out_ref):
    route = route_ref[...]
    out_ref[...] = (x2_ref[...] + route[:, 2:3] * _unpack_bf16_pairs(y0_ref[...])
                    + route[:, 3:4] * _unpack_bf16_pairs(y1_ref[...]))


def _moe_combine(x2, route, gathered, tm):
    n, d = x2.shape
    nt = n // tm
    return pl.pallas_call(
        _moe_combine_kernel,
        grid=(nt,),
        in_specs=[pl.BlockSpec((tm, d), lambda i: (i, 0)),
                  pl.BlockSpec((tm, LANES), lambda i: (i, 0)),
                  pl.BlockSpec((tm, d // 2), lambda i: (i, 0)),
                  pl.BlockSpec((tm, d // 2), lambda i: (i + nt, 0))],
        out_specs=pl.BlockSpec((tm, d), lambda i: (i, 0)),
        out_shape=jax.ShapeDtypeStruct((n, d), F32),
        compiler_params=pltpu.CompilerParams(dimension_semantics=("arbitrary",),
                                             vmem_limit_bytes=VMEM_LIMIT),
        name="moe_combine",
    )(x2, route, gathered, gathered)


def _moe(x2, hn_packed, route, counts, wg, wu, wd, tm_rows):
    n = x2.shape[0]
    p = (TOP_K * n // tm_rows + N_EXPERTS) * tm_rows
    n_tiles = p // tm_rows
    cnt = counts[0, :N_EXPERTS].astype(jnp.int32)
    tiles_per_e = (cnt + tm_rows - 1) // tm_rows
    e_ids = jnp.arange(N_EXPERTS, dtype=jnp.int32)
    tile_end = jnp.sum(jnp.where(e_ids[None, :] <= e_ids[:, None], tiles_per_e[None, :], 0), axis=1)
    tile_start = tile_end - tiles_per_e
    n_valid = tile_end[-1:]
    tile_ids = jnp.arange(n_tiles, dtype=jnp.int32)
    tile_expert = jnp.minimum(jnp.sum((tile_end[None, :] <= tile_ids[:, None]).astype(jnp.int32), axis=1),
                              N_EXPERTS - 1)
    mine = tile_expert[:, None] == e_ids[None, :]
    rows_left = jnp.sum(jnp.where(mine, cnt - (tile_ids[:, None] - tile_start) * tm_rows, 0), axis=1)
    rows_valid = jnp.clip(rows_left, 0, tm_rows).astype(jnp.int32)
    has_tiles = tiles_per_e > 0
    later = has_tiles[None, :] & (e_ids[None, :] > e_ids[:, None])
    next_of_e = jnp.min(jnp.where(later, e_ids[None, :], N_EXPERTS), axis=1)
    next_of_e = jnp.where(next_of_e < N_EXPERTS, next_of_e, -1)
    runs_before_e = jnp.sum((has_tiles[None, :] & (e_ids[None, :] < e_ids[:, None])).astype(jnp.int32), axis=1)
    next_expert = jnp.sum(jnp.where(mine, next_of_e, 0), axis=1).astype(jnp.int32)
    run_parity = jnp.sum(jnp.where(mine, runs_before_e % 2, 0), axis=1).astype(jnp.int32)
    eidx = route[:, 0:TOP_K].astype(jnp.int32)
    row_start = jnp.sum(jnp.where(eidx[:, :, None] == e_ids, tile_start * tm_rows, 0), axis=2)
    pos = (row_start + route[:, 4:4 + TOP_K].astype(jnp.int32)).T.reshape(-1)

    xs = _sc_scatter_rows(hn_packed, pos, p)
    ys = _moe_grouped(tile_expert, n_valid, rows_valid, next_expert, run_parity, xs, wg, wu, wd, tm_rows)
    back = _sc_gather_rows(ys, pos)
    return _moe_combine(x2, route, back, min(TM_COMBINE, n))


def _block_diag(n, width):
    idx = np.arange(n) // width
    return jnp.asarray((idx[:, None] == idx[None, :]).astype(np.float32), dtype=BF16)


def _strict_lower(n):
    r = np.arange(n)
    return jnp.asarray((r[:, None] > r[None, :]).astype(np.float32), dtype=BF16)


def _alibi_slopes():
    return np.exp2(-8.0 * np.arange(1, N_HEADS + 1, dtype=np.float32) / N_HEADS).astype(np.float32)


def _prompt_bias():
    i = np.arange(BLOCK)[:, None]
    s = np.arange(2 * BLOCK)[None, :]
    dist = (i + BLOCK - s).astype(np.float32)
    mask = (dist >= 0) & (dist < WINDOW)
    first = mask & (s >= BLOCK)
    slopes = _alibi_slopes()[:, None, None]
    reg = np.where(mask[None], -slopes * dist[None], np.float32(NEG_INF))
    fst = np.where(first[None], -slopes * dist[None], np.float32(NEG_INF))
    return jnp.asarray(np.concatenate([reg, fst], axis=0).astype(np.float32))


def _sample_bias(ts, wb):
    i = np.arange(ts)[:, None]
    s = np.arange(wb + ts)[None, :]
    dist = (i + wb - s).astype(np.float32)
    mask = (dist >= 0) & (dist < WINDOW)
    slopes = _alibi_slopes()[:, None, None]
    b = np.where(mask[None], -slopes * dist[None], np.float32(NEG_INF)).astype(np.float32)
    return jnp.asarray(b.reshape(N_HEADS * ts, wb + ts))


def kernel(x_prompt, x_sample, cache_swa_k, cache_swa_v, state_conv, cache_mem_k, cache_mem_v, mem_prompt, norm_mix_g, w_in, q_norm_g, k_norm_g, attn_sinks, conv_dw_w, conv_dw_b, conv_ln_g, conv_ln_b, w_out, norm_xa_g, norm_mem_g, w_mq, w_mk, w_mv, mq_norm_g, mk_norm_g, w_mo, norm_ffn_g, w_router_group, b_router_group, w_router_expert, b_router_expert, w_exp_gate, w_exp_up, w_exp_down):
    depth = w_in.shape[0]
    bp, sp, d = x_prompt.shape
    nb, ts, _ = x_sample.shape
    wb = cache_swa_k.shape[2]
    mlen = mem_prompt.shape[1]
    assert d == D_MODEL and wb == WINDOW and sp % TM_PROMPT == 0 and nb % SEQ_PER_STEP == 0

    bdq = _block_diag(D_ATTN, HEAD_DIM)
    bdk = _block_diag(KV_DIM, HEAD_DIM)
    bias_p = _prompt_bias()
    bias_s = _sample_bias(ts, wb)
    lane_lo = (np.arange(D_ATTN) % LANES) < HALF
    row = lambda a: a.reshape(1, -1).astype(F32)

    xp = x_prompt
    xs = x_sample.reshape(nb * ts, d)
    kp_l, vp_l, cp_l, mkp_l, mvp_l, ks_l, vs_l, cs_l = [], [], [], [], [], [], [], []
    for l in range(depth):
        gq = jnp.tile(q_norm_g[l].astype(F32), N_HEADS) * (HEAD_DIM ** -0.5)
        gqlo = jnp.where(lane_lo, gq, 0.0).reshape(1, -1)
        gqhi = jnp.where(lane_lo, 0.0, gq).reshape(1, -1)
        gk = jnp.tile(k_norm_g[l].astype(F32), N_KV_HEADS).reshape(1, -1)
        sinks = attn_sinks[l].astype(F32)
        sinkrow = jnp.repeat(sinks, ts).reshape(N_HEADS * ts, 1)
        win = w_in[l].astype(BF16)
        wout = w_out[l].astype(BF16)
        wmq = w_mq[l].astype(BF16)
        wmo = w_mo[l].astype(BF16)
        gmq = (jnp.tile(mq_norm_g[l].astype(F32), MEM_HEADS) * (MEM_HEAD_DIM ** -0.5)).reshape(1, -1)
        gmk = jnp.tile(mk_norm_g[l].astype(F32), MEM_HEADS).reshape(1, -1)
        w_r = jnp.concatenate([w_router_expert[l], w_router_group[l],
                               jnp.zeros((d, LANES - N_EXPERTS - N_GROUPS), F32)], axis=1).astype(BF16)
        b_r = jnp.concatenate([b_router_expert[l], b_router_group[l],
                               jnp.zeros((LANES - N_EXPERTS - N_GROUPS,), F32)]).reshape(1, -1).astype(F32)
        wg, wu, wd = w_exp_gate[l], w_exp_up[l], w_exp_down[l]

        mix_consts = (row(norm_mix_g[l]), win, bdq, bdk, gqlo, gqhi, gk)
        conv_consts = (conv_dw_w[l].astype(F32), row(conv_dw_b[l]), row(conv_ln_g[l]), row(conv_ln_b[l]), wout)
        tail_consts = (wmo, row(norm_ffn_g[l]), w_r, b_r)

        x1p, kp, vp, cp = _mixer_prompt(xp, sinks, mix_consts + (bias_p,) + conv_consts, TM_PROMPT)
        mk, mv = _memory_kv(mem_prompt.reshape(bp * mlen, d), row(norm_mem_g[l]),
                            w_mk[l].astype(BF16), w_mv[l].astype(BF16), gmk, min(256, bp * mlen))
        mk = mk.reshape(bp, mlen, d)
        mv = mv.reshape(bp, mlen, d)
        x2p, hnp, routep, cntp = _cross_prompt(x1p, mk, mv, (row(norm_xa_g[l]), wmq, gmq) + tail_consts
                                               + (_strict_lower(TM_PROMPT),), TM_PROMPT)
        xp = _moe(x2p.reshape(bp * sp, d), hnp.reshape(bp * sp, d // 2), routep.reshape(bp * sp, LANES), cntp,
                  wg, wu, wd, TM_ROWS_PROMPT).reshape(bp, sp, d)
        kp_l.append(kp.reshape(bp, BLOCK, N_KV_HEADS, HEAD_DIM))
        vp_l.append(vp.reshape(bp, BLOCK, N_KV_HEADS, HEAD_DIM))
        cp_l.append(cp)
        mkp_l.append(mk.reshape(bp, mlen, MEM_HEADS, MEM_HEAD_DIM))
        mvp_l.append(mv.reshape(bp, mlen, MEM_HEADS, MEM_HEAD_DIM))

        x1s, qm, ksn, vsn, csn = _mixer_sample(
            xs, cache_swa_k[l].reshape(nb, wb, KV_DIM), cache_swa_v[l].reshape(nb, wb, KV_DIM),
            jnp.transpose(state_conv[l], (1, 0, 2)),
            mix_consts + (bias_s, sinkrow) + conv_consts + (row(norm_xa_g[l]), wmq, gmq), SEQ_PER_STEP, ts)
        o_s = _mem_attend_sample(qm, _mem_cache_rows(cache_mem_k[l]), _mem_cache_rows(cache_mem_v[l]),
                                 MEM_SEQ_PER_STEP, ts)
        tms = min(256, nb * ts)
        x2s, hns, routes, cnts = _cross_tail_call(o_s, x1s, tail_consts + (_strict_lower(tms),), tms)
        xs = _moe(x2s, hns, routes, cnts, wg, wu, wd, TM_ROWS_SAMPLE)
        ks_l.append(ksn.reshape(nb, wb, N_KV_HEADS, HEAD_DIM))
        vs_l.append(vsn.reshape(nb, wb, N_KV_HEADS, HEAD_DIM))
        cs_l.append(jnp.transpose(csn, (1, 0, 2)))

    st = lambda xs_: jnp.stack(xs_, axis=0)
    return (xp, xs.reshape(nb, ts, d), st(kp_l), st(vp_l), st(cp_l), st(mkp_l), st(mvp_l),
            st(ks_l), st(vs_l), st(cs_l))
```

```python
import functools

import numpy as np
import jax
import jax.numpy as jnp
from jax import lax
from jax.experimental import pallas as pl
from jax.experimental.pallas import tpu as pltpu
from jax.experimental.pallas import tpu_sc as plsc

F32 = jnp.float32
BF16 = jnp.bfloat16

D_MODEL = 1024
D_ATTN = 512
D_CONV = 512
HEAD_DIM = 64
N_HEADS = 8
N_KV_HEADS = 2
KV_DIM = N_KV_HEADS * HEAD_DIM
HEADS_PER_KV = N_HEADS // N_KV_HEADS
WINDOW = 128
BLOCK = 128
CONV_WIDTH = 31
CONV_PAST = CONV_WIDTH - 1
MEM_HEADS = 4
MEM_HEAD_DIM = 256
N_GROUPS = 4
EXPERTS_PER_GROUP = 8
N_EXPERTS = 32
D_EXPERT = 256
D_IN = D_ATTN + 2 * KV_DIM + 2 * D_CONV
EPS = 1e-6
NEG_INF = -1e30

LANES = 128
HALF = LANES // 2
MXU_TILE = 256
SUBLANES = 8
CONV_PAD = 32
CONV_ROWS = 64
VMEM_LIMIT = 56 * 1024 * 1024

TM_PROMPT = 512
SEQ_PER_STEP = 16
MEM_SEQ_PER_STEP = 8
TOP_K = 2
ROUTE_ROWS = 8
SC_CORES = 2
SC_WORKERS = 32
SC_ROWS_PER_STEP = 64
TM_ROWS_PROMPT = 512
TM_ROWS_SAMPLE = 128
TM_COMBINE = 1024


def _rms(x, g):
    ms = jnp.mean(x * x, axis=-1, keepdims=True)
    return x * lax.rsqrt(ms + EPS) * g


def _group_mean_sq(x, bd_ref, width):
    x2 = x * x
    hi = x2.astype(BF16)
    lo = (x2 - hi.astype(F32)).astype(BF16)
    bd = bd_ref[...]
    w = bd.shape[0]
    parts = [jnp.dot(hi[:, c:c + w], bd, preferred_element_type=F32)
             + jnp.dot(lo[:, c:c + w], bd, preferred_element_type=F32) for c in range(0, x.shape[1], w)]
    return jnp.concatenate(parts, axis=1) * (1.0 / width)


def _mixer_proj(x, ng_ref, win_ref, bdq_ref, bdk_ref, gqlo_ref, gqhi_ref, gk_ref):
    h = _rms(x, ng_ref[...]).astype(BF16)
    p = jnp.dot(h, win_ref[...], preferred_element_type=F32)
    q = p[:, :D_ATTN]
    k = p[:, D_ATTN:D_ATTN + KV_DIM]
    v = p[:, D_ATTN + KV_DIM:D_ATTN + 2 * KV_DIM]
    ua = p[:, D_ATTN + 2 * KV_DIM:D_ATTN + 2 * KV_DIM + D_CONV]
    ub = p[:, D_ATTN + 2 * KV_DIM + D_CONV:]
    qn = q * lax.rsqrt(_group_mean_sq(q, bdq_ref, HEAD_DIM) + EPS)
    q_lo = qn * gqlo_ref[...]
    q_hi = qn * gqhi_ref[...]
    kn = k * lax.rsqrt(_group_mean_sq(k, bdk_ref, HEAD_DIM) + EPS) * gk_ref[...]
    u = ua * jax.nn.sigmoid(ub)
    return q_lo, q_hi, kn, v, u


def _dup_halves(x):
    lo = lax.broadcasted_iota(jnp.int32, x.shape, 1) < HALF
    xr = pltpu.roll(x, HALF, axis=1)
    return jnp.where(lo, x, xr), jnp.where(lo, xr, x)


def _sink_softmax(s, sink):
    m = jnp.maximum(jnp.max(s, axis=-1, keepdims=True), sink)
    p = jnp.exp(s - m)
    denom = jnp.sum(p, axis=-1, keepdims=True) + jnp.exp(sink - m)
    return p * (1.0 / denom)


def _conv_ln_silu(y, cb_ref, lg_ref, lb_ref):
    y = y + cb_ref[...]
    mu = jnp.mean(y, axis=-1, keepdims=True)
    yc = y - mu
    yn = yc * lax.rsqrt(jnp.mean(yc * yc, axis=-1, keepdims=True) + EPS)
    z = yn * lg_ref[...] + lb_ref[...]
    return z * jax.nn.sigmoid(z)


def _mem_query(x1, xag_ref, wmq_ref, gmq_ref):
    h = _rms(x1, xag_ref[...]).astype(BF16)
    q = jnp.dot(h, wmq_ref[...], preferred_element_type=F32)
    parts = []
    for hd in range(MEM_HEADS):
        qh = q[:, hd * MEM_HEAD_DIM:(hd + 1) * MEM_HEAD_DIM]
        parts.append(qh * lax.rsqrt(jnp.mean(qh * qh, axis=-1, keepdims=True) + EPS))
    return jnp.concatenate(parts, axis=1) * gmq_ref[...]


def _mem_attend(q, k_heads, v_heads):
    outs = []
    for hd in range(MEM_HEADS):
        sl = slice(hd * MEM_HEAD_DIM, (hd + 1) * MEM_HEAD_DIM)
        s = lax.dot_general(q[:, sl].astype(BF16), k_heads[hd], (((1,), (1,)), ((), ())),
                            preferred_element_type=F32)
        m = jnp.max(s, axis=-1, keepdims=True)
        p = jnp.exp(s - m)
        p = p * (1.0 / jnp.sum(p, axis=-1, keepdims=True))
        outs.append(jnp.dot(p.astype(BF16), v_heads[hd], preferred_element_type=F32))
    return jnp.concatenate(outs, axis=1)


def _split_heads(x):
    return [x[:, hd * MEM_HEAD_DIM:(hd + 1) * MEM_HEAD_DIM] for hd in range(MEM_HEADS)]


def _pack_bf16_pairs(y):
    n = y.shape[1] // 2
    bits = pltpu.bitcast(y.astype(BF16).astype(F32), jnp.int32)
    return (bits[:, :n] & jnp.int32(-65536)) | lax.shift_right_logical(bits[:, n:], jnp.int32(16))


def _unpack_bf16_pairs(w):
    hi = pltpu.bitcast(w & jnp.int32(-65536), F32)
    lo = pltpu.bitcast(lax.shift_left(w, jnp.int32(16)), F32)
    return jnp.concatenate([hi, lo], axis=1)


def _cross_tail(o, x1, wmo_ref, fg_ref, wr_ref, br_ref, tri_ref, base):
    x2 = x1 + jnp.dot(o.astype(BF16), wmo_ref[...], preferred_element_type=F32)
    hn_f = _rms(x2, fg_ref[...])
    hn = hn_f.astype(BF16)
    logits = jnp.dot(hn, wr_ref[...], preferred_element_type=F32) + br_ref[...]
    lane = lax.broadcasted_iota(jnp.int32, logits.shape, 1)
    is_g = (lane >= N_EXPERTS) & (lane < N_EXPERTS + N_GROUPS)
    lg = jnp.where(is_g, logits, NEG_INF)
    gmax = jnp.max(lg, axis=-1, keepdims=True)
    gsel = jnp.min(jnp.where(lg == gmax, lane, 2 * LANES), axis=-1, keepdims=True) - N_EXPERTS
    pg_sel = 1.0 / jnp.sum(jnp.exp(lg - gmax), axis=-1, keepdims=True)
    in_grp = (lane >= gsel * EXPERTS_PER_GROUP) & (lane < (gsel + 1) * EXPERTS_PER_GROUP)
    le = jnp.where(in_grp, logits, NEG_INF)
    top1 = jnp.max(le, axis=-1, keepdims=True)
    idx1 = jnp.min(jnp.where(le == top1, lane, 2 * LANES), axis=-1, keepdims=True)
    le2 = jnp.where(lane == idx1, NEG_INF, le)
    top2 = jnp.max(le2, axis=-1, keepdims=True)
    idx2 = jnp.min(jnp.where(le2 == top2, lane, 2 * LANES), axis=-1, keepdims=True)
    e2 = jnp.exp(top2 - top1)
    inv = 1.0 / (1.0 + e2)
    gate1 = pg_sel * inv
    gate2 = pg_sel * (e2 * inv)
    used = jnp.where((lane == idx1) | (lane == idx2), 1.0, 0.0)
    before = jnp.dot(tri_ref[...], used.astype(BF16), preferred_element_type=F32) + base
    rank1 = jnp.sum(jnp.where(lane == idx1, before, 0.0), axis=-1, keepdims=True)
    rank2 = jnp.sum(jnp.where(lane == idx2, before, 0.0), axis=-1, keepdims=True)
    route = jnp.zeros_like(logits)
    for pos, val in enumerate((idx1.astype(F32), idx2.astype(F32), gate1, gate2, rank1, rank2)):
        route = jnp.where(lane == pos, val, route)
    info = jnp.transpose(route)[0:ROUTE_ROWS, :]
    return x2, _pack_bf16_pairs(hn_f), route, info, base + jnp.sum(used, axis=0, keepdims=True)


def _mixer_prompt_kernel(sink_ref, x_ref, ng_ref, win_ref, bdq_ref, bdk_ref, gqlo_ref, gqhi_ref, gk_ref,
                         bias_ref, cw_ref, cb_ref, lg_ref, lb_ref, wout_ref,
                         x1_ref, ko_ref, vo_ref, co_ref,
                         kband, vband, uext, ushift, *, tm, nt):
    t = pl.program_id(1)

    @pl.when(t == 0)
    def _():
        kband[0:BLOCK, :] = jnp.zeros((BLOCK, KV_DIM), F32)
        vband[0:BLOCK, :] = jnp.zeros((BLOCK, KV_DIM), F32)
        uext[0:CONV_PAD, :] = jnp.zeros((CONV_PAD, D_CONV), F32)

    x = x_ref[...]
    q_lo, q_hi, kn, v, u = _mixer_proj(x, ng_ref, win_ref, bdq_ref, bdk_ref, gqlo_ref, gqhi_ref, gk_ref)
    kband[BLOCK:BLOCK + tm, :] = kn
    vband[BLOCK:BLOCK + tm, :] = v
    uext[CONV_PAD:CONV_PAD + tm, :] = u

    @pl.when(t == nt - 1)
    def _():
        ko_ref[...] = kn[tm - BLOCK:, :]
        vo_ref[...] = v[tm - BLOCK:, :]
        co_ref[...] = uext[pl.ds(CONV_PAD + tm - CONV_PAST, CONV_PAST), :]

    kd = [a.astype(BF16) for a in _dup_halves(kband[...])]
    vd = _dup_halves(vband[...])
    lo = lax.broadcasted_iota(jnp.int32, vd[0].shape, 1) < HALF
    v_lo = [jnp.where(lo, a, 0.0).astype(BF16) for a in vd]
    v_hi = [jnp.where(lo, 0.0, a).astype(BF16) for a in vd]
    q_lo = q_lo.astype(BF16)
    q_hi = q_hi.astype(BF16)
    attn_blocks = []
    for j in range(tm // BLOCK):
        rows = slice(j * BLOCK, (j + 1) * BLOCK)
        keys = slice(j * BLOCK, j * BLOCK + 2 * BLOCK)
        bias_base = jnp.where(t == 0, N_HEADS, 0) if j == 0 else 0
        tiles = []
        for c in range(N_KV_HEADS):
            q4 = jnp.concatenate(
                [(q_lo if (HEADS_PER_KV * c + a) % 2 == 0 else q_hi)[rows,
                  ((HEADS_PER_KV * c + a) // 2) * LANES:((HEADS_PER_KV * c + a) // 2 + 1) * LANES]
                 for a in range(HEADS_PER_KV)], axis=0)
            s_all = lax.dot_general(q4, kd[c][keys], (((1,), (1,)), ((), ())), preferred_element_type=F32)
            ps = []
            for a in range(HEADS_PER_KV):
                hd = HEADS_PER_KV * c + a
                s = s_all[a * BLOCK:(a + 1) * BLOCK] + bias_ref[bias_base + hd]
                ps.append(_sink_softmax(s, sink_ref[hd]).astype(BF16))
            vstack = jnp.concatenate([v_lo[c][keys], v_hi[c][keys]], axis=0)
            for i2 in range(HEADS_PER_KV // 2):
                pp = jnp.concatenate([ps[2 * i2], ps[2 * i2 + 1]], axis=1)
                tiles.append(jnp.dot(pp, vstack, preferred_element_type=F32))
        attn_blocks.append(jnp.concatenate(tiles, axis=1))
    attn = jnp.concatenate(attn_blocks, axis=0)

    off = CONV_PAD - CONV_PAST
    span = tm + CONV_PAD - SUBLANES
    for s in range(1, SUBLANES):
        ushift[s - 1] = uext[pl.ds(s, span), :]
    chunks = []
    for lc in range(D_CONV // LANES):
        ls = slice(lc * LANES, (lc + 1) * LANES)
        accs = [jnp.zeros((CONV_ROWS, LANES), F32) for _ in range(tm // CONV_ROWS)]
        for j in range(CONV_WIDTH):
            s, a = (off + j) % SUBLANES, (off + j) // SUBLANES
            wj = cw_ref[j:j + 1, ls]
            for rc in range(tm // CONV_ROWS):
                r0 = rc * CONV_ROWS + a * SUBLANES
                tap = uext[r0:r0 + CONV_ROWS, ls] if s == 0 else ushift[s - 1, r0:r0 + CONV_ROWS, ls]
                accs[rc] = accs[rc] + wj * tap
        chunks.append(jnp.concatenate(accs, axis=0))
    y = _conv_ln_silu(jnp.concatenate(chunks, axis=1), cb_ref, lg_ref, lb_ref)

    x1_ref[...] = (x + jnp.dot(attn.astype(BF16), wout_ref[0:D_ATTN, :], preferred_element_type=F32)
                   + jnp.dot(y.astype(BF16), wout_ref[D_ATTN:, :], preferred_element_type=F32))

    kband[0:BLOCK, :] = kband[tm:tm + BLOCK, :]
    vband[0:BLOCK, :] = vband[tm:tm + BLOCK, :]
    uext[0:CONV_PAD, :] = uext[tm:tm + CONV_PAD, :]


def _const_spec(shape):
    nd = len(shape)
    return pl.BlockSpec(shape, lambda *_: (0,) * nd)


def _mixer_prompt(x, sinks, consts, tm):
    b, s, d = x.shape
    nt = s // tm
    (ng, win, bdq, bdk, gqlo, gqhi, gk, bias_p, cw, cb, lg, lb, wout) = consts
    kern = functools.partial(_mixer_prompt_kernel, tm=tm, nt=nt)
    cspecs = [_const_spec(a.shape) for a in consts]
    return pl.pallas_call(
        kern,
        grid=(b, nt),
        in_specs=[pl.BlockSpec(memory_space=pltpu.SMEM),
                  pl.BlockSpec((None, tm, d), lambda i, j: (i, j, 0))] + cspecs,
        out_specs=[pl.BlockSpec((None, tm, d), lambda i, j: (i, j, 0)),
                   pl.BlockSpec((None, BLOCK, KV_DIM), lambda i, j: (i, 0, 0)),
                   pl.BlockSpec((None, BLOCK, KV_DIM), lambda i, j: (i, 0, 0)),
                   pl.BlockSpec((None, CONV_PAST, D_CONV), lambda i, j: (i, 0, 0))],
        out_shape=[jax.ShapeDtypeStruct((b, s, d), F32),
                   jax.ShapeDtypeStruct((b, BLOCK, KV_DIM), F32),
                   jax.ShapeDtypeStruct((b, BLOCK, KV_DIM), F32),
                   jax.ShapeDtypeStruct((b, CONV_PAST, D_CONV), F32)],
        scratch_shapes=[pltpu.VMEM((BLOCK + tm, KV_DIM), F32),
                        pltpu.VMEM((BLOCK + tm, KV_DIM), F32),
                        pltpu.VMEM((CONV_PAD + tm, D_CONV), F32),
                        pltpu.VMEM((SUBLANES - 1, tm + CONV_PAD - SUBLANES, D_CONV), F32)],
        compiler_params=pltpu.CompilerParams(dimension_semantics=("arbitrary", "arbitrary"),
                                             vmem_limit_bytes=VMEM_LIMIT),
        name="mixer_prompt",
    )(sinks, x, *consts)


def _mixer_sample_kernel(x_ref, ck_ref, cv_ref, st_ref, ng_ref, win_ref, bdq_ref, bdk_ref, gqlo_ref, gqhi_ref,
                         gk_ref, bias_ref, sinkrow_ref, cw_ref, cb_ref, lg_ref, lb_ref, wout_ref,
                         xag_ref, wmq_ref, gmq_ref,
                         x1_ref, qm_ref, ko_ref, vo_ref, co_ref,
                         kall, vall, qs, s_scr, r_scr, u_scr, y_scr, *, g, ts):
    wb = ck_ref.shape[1]
    x = x_ref[...]
    q_lo, q_hi, kn, v, u = _mixer_proj(x, ng_ref, win_ref, bdq_ref, bdk_ref, gqlo_ref, gqhi_ref, gk_ref)
    kall[:, 0:wb, :] = ck_ref[...]
    kall[:, wb:wb + ts, :] = kn.reshape(g, ts, KV_DIM)
    vall[:, 0:wb, :] = cv_ref[...]
    vall[:, wb:wb + ts, :] = v.reshape(g, ts, KV_DIM)
    ko_ref[...] = kall[:, ts:wb + ts, :]
    vo_ref[...] = vall[:, ts:wb + ts, :]

    for lc in range(D_CONV // LANES):
        ls = slice(lc * LANES, (lc + 1) * LANES)
        u_scr[lc] = u[:, ls]
        u_t = [u_scr[lc, pl.ds(t, g, stride=ts), :] for t in range(ts)]

        def frame(r):
            return st_ref[r, :, ls] if r < CONV_PAST else u_t[r - CONV_PAST]

        for t in range(ts):
            acc = jnp.zeros((g, LANES), F32)
            for j in range(CONV_WIDTH):
                acc = acc + cw_ref[j:j + 1, ls] * frame(t + j)
            y_scr[lc, pl.ds(t, g, stride=ts), :] = acc
        for r in range(CONV_PAST):
            co_ref[r, :, ls] = frame(r + ts)
    y = _conv_ln_silu(jnp.concatenate([y_scr[lc] for lc in range(D_CONV // LANES)], axis=1), cb_ref, lg_ref, lb_ref)

    for hd in range(N_HEADS):
        tile = (q_lo if hd % 2 == 0 else q_hi)[:, (hd // 2) * LANES:(hd // 2 + 1) * LANES]
        if hd % 2 != hd // HEADS_PER_KV:
            tile = pltpu.roll(tile, HALF, axis=1)
        qs[:, hd * ts:(hd + 1) * ts, :] = tile.reshape(g, ts, LANES)
    for n in range(g):
        s_scr[n] = lax.dot_general(qs[n].astype(BF16), kall[n].astype(BF16), (((1,), (1,)), ((), ())),
                                   preferred_element_type=F32)
    s_scr[...] = _sink_softmax(s_scr[...] + bias_ref[...], sinkrow_ref[...])
    for n in range(g):
        r_scr[n] = jnp.dot(s_scr[n].astype(BF16), vall[n].astype(BF16), preferred_element_type=F32)
    r = r_scr[...].reshape(g * N_HEADS * ts, LANES)
    r_sw = pltpu.roll(r, HALF, axis=1).reshape(g, N_HEADS * ts, LANES)
    r = r.reshape(g, N_HEADS * ts, LANES)
    lo = lax.broadcasted_iota(jnp.int32, (g * ts, LANES), 1) < HALF
    heads = [(r if hd // HEADS_PER_KV == hd % 2 else r_sw)[:, hd * ts:(hd + 1) * ts, :].reshape(g * ts, LANES)
             for hd in range(N_HEADS)]
    attn = jnp.concatenate([jnp.where(lo, heads[2 * i], heads[2 * i + 1]) for i in range(N_HEADS // 2)], axis=1)

    x1 = (x + jnp.dot(attn.astype(BF16), wout_ref[0:D_ATTN, :], preferred_element_type=F32)
          + jnp.dot(y.astype(BF16), wout_ref[D_ATTN:, :], preferred_element_type=F32))
    x1_ref[...] = x1
    qm_ref[...] = _mem_query(x1, xag_ref, wmq_ref, gmq_ref)


def _mixer_sample(xs2d, ck, cv, st, consts, g, ts):
    n, d = xs2d.shape
    nb, wb = ck.shape[0], ck.shape[1]
    rows = g * ts
    cspecs = [_const_spec(a.shape) for a in consts]
    kern = functools.partial(_mixer_sample_kernel, g=g, ts=ts)
    return pl.pallas_call(
        kern,
        grid=(nb // g,),
        in_specs=[pl.BlockSpec((rows, d), lambda i: (i, 0)),
                  pl.BlockSpec((g, wb, KV_DIM), lambda i: (i, 0, 0)),
                  pl.BlockSpec((g, wb, KV_DIM), lambda i: (i, 0, 0)),
                  pl.BlockSpec((CONV_PAST, g, D_CONV), lambda i: (0, i, 0))] + cspecs,
        out_specs=[pl.BlockSpec((rows, d), lambda i: (i, 0)),
                   pl.BlockSpec((rows, d), lambda i: (i, 0)),
                   pl.BlockSpec((g, wb, KV_DIM), lambda i: (i, 0, 0)),
                   pl.BlockSpec((g, wb, KV_DIM), lambda i: (i, 0, 0)),
                   pl.BlockSpec((CONV_PAST, g, D_CONV), lambda i: (0, i, 0))],
        out_shape=[jax.ShapeDtypeStruct((n, d), F32),
                   jax.ShapeDtypeStruct((n, d), F32),
                   jax.ShapeDtypeStruct((nb, wb, KV_DIM), F32),
                   jax.ShapeDtypeStruct((nb, wb, KV_DIM), F32),
                   jax.ShapeDtypeStruct((CONV_PAST, nb, D_CONV), F32)],
        scratch_shapes=[pltpu.VMEM((g, wb + ts, KV_DIM), F32),
                        pltpu.VMEM((g, wb + ts, KV_DIM), F32),
                        pltpu.VMEM((g, N_HEADS * ts, LANES), F32),
                        pltpu.VMEM((g, N_HEADS * ts, wb + ts), F32),
                        pltpu.VMEM((g, N_HEADS * ts, LANES), F32),
                        pltpu.VMEM((D_CONV // LANES, rows, LANES), F32),
                        pltpu.VMEM((D_CONV // LANES, rows, LANES), F32)],
        compiler_params=pltpu.CompilerParams(dimension_semantics=("arbitrary",),
                                             vmem_limit_bytes=VMEM_LIMIT),
        name="mixer_sample",
    )(xs2d, ck, cv, st, *consts)


def _memory_kv_kernel(mem_ref, g_ref, wmk_ref, wmv_ref, gk_ref, k_ref, v_ref):
    h = _rms(mem_ref[...], g_ref[...]).astype(BF16)
    k = jnp.dot(h, wmk_ref[...], preferred_element_type=F32)
    parts = []
    for hd in range(MEM_HEADS):
        kh = k[:, hd * MEM_HEAD_DIM:(hd + 1) * MEM_HEAD_DIM]
        parts.append(kh * lax.rsqrt(jnp.mean(kh * kh, axis=-1, keepdims=True) + EPS))
    k_ref[...] = jnp.concatenate(parts, axis=1) * gk_ref[...]
    v_ref[...] = jnp.dot(h, wmv_ref[...], preferred_element_type=F32)


def _memory_kv(mem2d, g, wmk, wmv, gk, tm):
    n, d = mem2d.shape
    consts = (g, wmk, wmv, gk)
    return pl.pallas_call(
        _memory_kv_kernel,
        grid=(n // tm,),
        in_specs=[pl.BlockSpec((tm, d), lambda i: (i, 0))] + [_const_spec(a.shape) for a in consts],
        out_specs=[pl.BlockSpec((tm, d), lambda i: (i, 0)), pl.BlockSpec((tm, d), lambda i: (i, 0))],
        out_shape=[jax.ShapeDtypeStruct((n, d), F32), jax.ShapeDtypeStruct((n, d), F32)],
        compiler_params=pltpu.CompilerParams(dimension_semantics=("arbitrary",),
                                             vmem_limit_bytes=VMEM_LIMIT),
        name="memory_kv",
    )(mem2d, *consts)


def _cross_prompt_kernel(x1_ref, mk_ref, mv_ref, xag_ref, wmq_ref, gmq_ref, wmo_ref, fg_ref, wr_ref, br_ref, tri_ref,
                         x2_ref, hn_ref, route_ref, info_ref, cnt_ref, base):
    @pl.when((pl.program_id(0) == 0) & (pl.program_id(1) == 0))
    def _():
        base[...] = jnp.zeros_like(base)

    x1 = x1_ref[...]
    q = _mem_query(x1, xag_ref, wmq_ref, gmq_ref)
    o = _mem_attend(q, _split_heads(mk_ref[...].astype(BF16)), _split_heads(mv_ref[...].astype(BF16)))
    x2, hn, route, info, new_base = _cross_tail(o, x1, wmo_ref, fg_ref, wr_ref, br_ref, tri_ref, base[...])
    x2_ref[...] = x2
    hn_ref[...] = hn
    route_ref[...] = route
    info_ref[...] = info
    base[...] = new_base
    cnt_ref[...] = new_base


def _cross_prompt(x1, mk, mv, consts, tm):
    b, s, d = x1.shape
    m = mk.shape[1]
    cspecs = [_const_spec(a.shape) for a in consts]
    return pl.pallas_call(
        _cross_prompt_kernel,
        grid=(b, s // tm),
        in_specs=[pl.BlockSpec((None, tm, d), lambda i, j: (i, j, 0)),
                  pl.BlockSpec((None, m, d), lambda i, j: (i, 0, 0)),
                  pl.BlockSpec((None, m, d), lambda i, j: (i, 0, 0))] + cspecs,
        out_specs=[pl.BlockSpec((None, tm, d), lambda i, j: (i, j, 0)),
                   pl.BlockSpec((None, tm, d // 2), lambda i, j: (i, j, 0)),
                   pl.BlockSpec((None, tm, LANES), lambda i, j: (i, j, 0)),
                   pl.BlockSpec((ROUTE_ROWS, tm), lambda i, j: (0, i * (s // tm) + j)),
                   pl.BlockSpec((1, LANES), lambda i, j: (0, 0))],
        out_shape=[jax.ShapeDtypeStruct((b, s, d), F32),
                   jax.ShapeDtypeStruct((b, s, d // 2), jnp.int32),
                   jax.ShapeDtypeStruct((b, s, LANES), F32),
                   jax.ShapeDtypeStruct((ROUTE_ROWS, b * s), F32),
                   jax.ShapeDtypeStruct((1, LANES), F32)],
        scratch_shapes=[pltpu.VMEM((1, LANES), F32)],
        compiler_params=pltpu.CompilerParams(dimension_semantics=("arbitrary", "arbitrary"),
                                             vmem_limit_bytes=VMEM_LIMIT),
        name="cross_prompt",
    )(x1, mk, mv, *consts)


def _mem_attend_sample_kernel(qm_ref, *refs, g, ts):
    k_refs, v_refs, o_ref = refs[:g], refs[g:2 * g], refs[2 * g]
    halves = MEM_HEAD_DIM // LANES
    rows_per_pos = MEM_HEADS * halves
    mlen = k_refs[0].shape[0] // rows_per_pos

    def all_heads(ref):
        return jnp.concatenate([ref[pl.ds(c * MEM_HEADS + hd, mlen, stride=rows_per_pos), :]
                                for hd in range(MEM_HEADS) for c in range(halves)], axis=1).astype(BF16)

    rows = MEM_HEADS * ts
    own = (lax.broadcasted_iota(jnp.int32, (rows, MEM_HEADS * MEM_HEAD_DIM), 1) // MEM_HEAD_DIM
           == lax.broadcasted_iota(jnp.int32, (rows, MEM_HEADS * MEM_HEAD_DIM), 0) // ts)
    for n in range(g):
        q = qm_ref[n * ts:(n + 1) * ts, :]
        q_bd = jnp.where(own, jnp.concatenate([q] * MEM_HEADS, axis=0), 0.0).astype(BF16)
        s = lax.dot_general(q_bd, all_heads(k_refs[n]), (((1,), (1,)), ((), ())), preferred_element_type=F32)
        m = jnp.max(s, axis=-1, keepdims=True)
        p = jnp.exp(s - m)
        p = p * (1.0 / jnp.sum(p, axis=-1, keepdims=True))
        r = jnp.dot(p.astype(BF16), all_heads(v_refs[n]), preferred_element_type=F32)
        o_ref[n * ts:(n + 1) * ts, :] = jnp.concatenate(
            [r[hd * ts:(hd + 1) * ts, hd * MEM_HEAD_DIM:(hd + 1) * MEM_HEAD_DIM] for hd in range(MEM_HEADS)], axis=1)


def _mem_cache_rows(cache):
    nb, mlen = cache.shape[:2]
    halves = MEM_HEAD_DIM // LANES
    return (cache.reshape(nb, mlen, MEM_HEADS, halves, LANES).transpose(0, 1, 3, 2, 4)
            .reshape(nb, mlen * halves * MEM_HEADS, LANES))


def _mem_attend_sample(qm, mk, mv, g, ts):
    n, d = qm.shape
    nb, rows = mk.shape[0], mk.shape[1]
    kern = functools.partial(_mem_attend_sample_kernel, g=g, ts=ts)

    def seq_spec(j):
        return pl.BlockSpec((None, rows, LANES), lambda i: (g * i + j, 0, 0))

    return pl.pallas_call(
        kern,
        grid=(nb // g,),
        in_specs=[pl.BlockSpec((g * ts, d), lambda i: (i, 0))] + [seq_spec(j) for j in range(g)] * 2,
        out_specs=pl.BlockSpec((g * ts, d), lambda i: (i, 0)),
        out_shape=jax.ShapeDtypeStruct((n, d), F32),
        compiler_params=pltpu.CompilerParams(dimension_semantics=("arbitrary",),
                                             vmem_limit_bytes=VMEM_LIMIT),
        name="mem_attend_sample",
    )(qm, *([mk] * g), *([mv] * g))


def _cross_tail_kernel(o_ref, x1_ref, wmo_ref, fg_ref, wr_ref, br_ref, tri_ref,
                       x2_ref, hn_ref, route_ref, info_ref, cnt_ref, base):
    @pl.when(pl.program_id(0) == 0)
    def _():
        base[...] = jnp.zeros_like(base)

    x2, hn, route, info, new_base = _cross_tail(o_ref[...], x1_ref[...], wmo_ref, fg_ref, wr_ref, br_ref, tri_ref,
                                                base[...])
    x2_ref[...] = x2
    hn_ref[...] = hn
    route_ref[...] = route
    info_ref[...] = info
    base[...] = new_base
    cnt_ref[...] = new_base


def _cross_tail_call(o, x1, consts, tm):
    n, d = x1.shape
    cspecs = [_const_spec(a.shape) for a in consts]
    return pl.pallas_call(
        _cross_tail_kernel,
        grid=(n // tm,),
        in_specs=[pl.BlockSpec((tm, d), lambda i: (i, 0)), pl.BlockSpec((tm, d), lambda i: (i, 0))] + cspecs,
        out_specs=[pl.BlockSpec((tm, d), lambda i: (i, 0)),
                   pl.BlockSpec((tm, d // 2), lambda i: (i, 0)),
                   pl.BlockSpec((tm, LANES), lambda i: (i, 0)),
                   pl.BlockSpec((ROUTE_ROWS, tm), lambda i: (0, i)),
                   pl.BlockSpec((1, LANES), lambda i: (0, 0))],
        out_shape=[jax.ShapeDtypeStruct((n, d), F32),
                   jax.ShapeDtypeStruct((n, d // 2), jnp.int32),
                   jax.ShapeDtypeStruct((n, LANES), F32),
                   jax.ShapeDtypeStruct((ROUTE_ROWS, n), F32),
                   jax.ShapeDtypeStruct((1, LANES), F32)],
        scratch_shapes=[pltpu.VMEM((1, LANES), F32)],
        compiler_params=pltpu.CompilerParams(dimension_semantics=("arbitrary",),
                                             vmem_limit_bytes=VMEM_LIMIT),
        name="cross_tail",
    )(o, x1, *consts)


def _sc_rows_per_step(per_worker):
    step = min(SC_ROWS_PER_STEP, per_worker)
    assert per_worker % step == 0 and step % 8 == 0
    return step


def _sc_gather_rows(table, idx):
    nrows = idx.shape[0]
    _, width = table.shape
    assert nrows % (8 * SC_WORKERS) == 0
    per_worker = nrows // SC_WORKERS
    step = _sc_rows_per_step(per_worker)
    mesh = plsc.VectorSubcoreMesh(core_axis_name="c", subcore_axis_name="s")

    @functools.partial(
        pl.kernel, mesh=mesh, out_type=jax.ShapeDtypeStruct((nrows, width), table.dtype),
        scratch_types=[pltpu.VMEM((step,), jnp.int32),
                       pltpu.VMEM((step, width), table.dtype),
                       pltpu.SemaphoreType.DMA])
    def gather(table_hbm, idx_hbm, out_hbm, idx_v, rows_v, sem):
        wid = lax.axis_index("s") * SC_CORES + lax.axis_index("c")
        base = wid * per_worker

        @pl.loop(0, per_worker // step)
        def _(i):
            off = base + i * step
            pltpu.sync_copy(idx_hbm.at[pl.ds(off, step)], idx_v)
            pltpu.async_copy(table_hbm.at[idx_v], rows_v, sem).wait()
            pltpu.sync_copy(rows_v, out_hbm.at[pl.ds(off, step)])

    return gather(table, idx)


def _sc_scatter_rows(table, pos, nrows_out):
    n, width = table.shape
    assert n % (8 * SC_WORKERS) == 0 and pos.shape == (TOP_K * n,)
    per_worker = n // SC_WORKERS
    step = _sc_rows_per_step(per_worker)
    mesh = plsc.VectorSubcoreMesh(core_axis_name="c", subcore_axis_name="s")

    @functools.partial(
        pl.kernel, mesh=mesh, out_type=jax.ShapeDtypeStruct((nrows_out, width), table.dtype),
        scratch_types=[pltpu.VMEM((step,), jnp.int32)] * TOP_K
        + [pltpu.VMEM((step, width), table.dtype), pltpu.SemaphoreType.DMA])
    def scatter(table_hbm, pos_hbm, out_hbm, *scratch):
        idx_vs, rows_v, sem = scratch[:TOP_K], scratch[TOP_K], scratch[TOP_K + 1]
        wid = lax.axis_index("s") * SC_CORES + lax.axis_index("c")
        base = wid * per_worker

        @pl.loop(0, per_worker // step)
        def _(i):
            off = base + i * step
            for k in range(TOP_K):
                pltpu.sync_copy(pos_hbm.at[pl.ds(k * n + off, step)], idx_vs[k])
            pltpu.sync_copy(table_hbm.at[pl.ds(off, step)], rows_v)
            for k in range(TOP_K):
                pltpu.async_copy(rows_v, out_hbm.at[idx_vs[k]], sem).wait()

    return scatter(table, pos)


W_CHUNKS = 4


def _expert_weight_copies(e, slot, w_hbm, w_f32, sems):
    copies = []
    for m, (src, dst) in enumerate(zip(w_hbm, w_f32)):
        rows = dst.shape[1] // W_CHUNKS
        for c in range(W_CHUNKS):
            sl = pl.ds(c * rows, rows)
            copies.append(pltpu.make_async_copy(src.at[e, sl, :], dst.at[slot, sl, :], sems.at[m * W_CHUNKS + c]))
    return copies


def _moe_grouped_kernel(te_ref, nv_ref, rv_ref, nxt_ref, par_ref, xs_ref, wg_hbm, wu_hbm, wd_hbm, ys_ref,
                        wg_f, wu_f, wd_f, wg_b, wu_b, wd_b, sems):
    i = pl.program_id(0)
    w_hbm, w_f32 = (wg_hbm, wu_hbm, wd_hbm), (wg_f, wu_f, wd_f)

    @pl.when(i == 0)
    def _():
        for cp in _expert_weight_copies(te_ref[0], par_ref[0], w_hbm, w_f32, sems):
            cp.start()

    @pl.when(i < nv_ref[0])
    def _():
        @pl.when((i == 0) | (te_ref[i] != te_ref[jnp.maximum(i - 1, 0)]))
        def _():
            slot = par_ref[i]
            for cp in _expert_weight_copies(te_ref[i], slot, w_hbm, w_f32, sems):
                cp.wait()
            for src, dst in zip(w_f32, (wg_b, wu_b, wd_b)):
                dst[...] = src[slot].astype(BF16)

            @pl.when(nxt_ref[i] >= 0)
            def _():
                for cp in _expert_weight_copies(nxt_ref[i], 1 - slot, w_hbm, w_f32, sems):
                    cp.start()

        xs = xs_ref[...]
        row = lax.broadcasted_iota(jnp.int32, xs.shape, 0)
        x = _unpack_bf16_pairs(jnp.where(row < rv_ref[i], xs, 0)).astype(BF16)
        a = jnp.dot(x, wg_b[...], preferred_element_type=F32)
        b = jnp.dot(x, wu_b[...], preferred_element_type=F32)
        act = (a * jax.nn.sigmoid(a)) * b
        y = jnp.dot(act.astype(BF16), wd_b[...], preferred_element_type=F32)
        ys_ref[...] = _pack_bf16_pairs(y)


def _moe_grouped(tile_expert, n_valid, rows_valid, next_expert, run_parity, xs, wg, wu, wd, tm):
    p, half = xs.shape
    _, d, f = wg.shape
    grid_spec = pltpu.PrefetchScalarGridSpec(
        num_scalar_prefetch=5,
        grid=(p // tm,),
        in_specs=[pl.BlockSpec((tm, half), lambda i, *_: (i, 0))] + [pl.BlockSpec(memory_space=pl.ANY)] * 3,
        out_specs=pl.BlockSpec((tm, half), lambda i, *_: (i, 0)),
        scratch_shapes=[pltpu.VMEM((2, d, f), F32), pltpu.VMEM((2, d, f), F32), pltpu.VMEM((2, f, d), F32),
                        pltpu.VMEM((d, f), BF16), pltpu.VMEM((d, f), BF16), pltpu.VMEM((f, d), BF16),
                        pltpu.SemaphoreType.DMA((3 * W_CHUNKS,))])
    return pl.pallas_call(
        _moe_grouped_kernel,
        grid_spec=grid_spec,
        out_shape=jax.ShapeDtypeStruct((p, half), jnp.int32),
        compiler_params=pltpu.CompilerParams(dimension_semantics=("arbitrary",),
                                             vmem_limit_bytes=VMEM_LIMIT),
        name="moe_grouped",
    )(tile_expert, n_valid, rows_valid, next_expert, run_parity, xs, wg, wu, wd)


def _moe_combine_kernel(x2_ref, route_ref, y0_ref, y1_ref, out_ref):
    route = route_ref[...]
    out_ref[...] = (x2_ref[...] + route[:, 2:3] * _unpack_bf16_pairs(y0_ref[...])
                    + route[:, 3:4] * _unpack_bf16_pairs(y1_ref[...]))


def _moe_combine(x2, route, gathered, tm):
    n, d = x2.shape
    nt = n // tm
    return pl.pallas_call(
        _moe_combine_kernel,
        grid=(nt,),
        in_specs=[pl.BlockSpec((tm, d), lambda i: (i, 0)),
                  pl.BlockSpec((tm, LANES), lambda i: (i, 0)),
                  pl.BlockSpec((tm, d // 2), lambda i: (i, 0)),
                  pl.BlockSpec((tm, d // 2), lambda i: (i + nt, 0))],
        out_specs=pl.BlockSpec((tm, d), lambda i: (i, 0)),
        out_shape=jax.ShapeDtypeStruct((n, d), F32),
        compiler_params=pltpu.CompilerParams(dimension_semantics=("arbitrary",),
                                             vmem_limit_bytes=VMEM_LIMIT),
        name="moe_combine",
    )(x2, route, gathered, gathered)


def _moe(x2, hn_packed, route, info, counts, wg, wu, wd, tm_rows):
    n = x2.shape[0]
    p = (TOP_K * n // tm_rows + N_EXPERTS) * tm_rows
    n_tiles = p // tm_rows
    cnt = counts[0, :N_EXPERTS].astype(jnp.int32)
    tiles_per_e = (cnt + tm_rows - 1) // tm_rows
    e_ids = jnp.arange(N_EXPERTS, dtype=jnp.int32)
    tile_end = jnp.sum(jnp.where(e_ids[None, :] <= e_ids[:, None], tiles_per_e[None, :], 0), axis=1)
    tile_start = tile_end - tiles_per_e
    n_valid = tile_end[-1:]
    tile_ids = jnp.arange(n_tiles, dtype=jnp.int32)
    tile_expert = jnp.minimum(jnp.sum((tile_end[None, :] <= tile_ids[:, None]).astype(jnp.int32), axis=1),
                              N_EXPERTS - 1)
    mine = tile_expert[:, None] == e_ids[None, :]
    rows_left = jnp.sum(jnp.where(mine, cnt - (tile_ids[:, None] - tile_start) * tm_rows, 0), axis=1)
    rows_valid = jnp.clip(rows_left, 0, tm_rows).astype(jnp.int32)
    has_tiles = tiles_per_e > 0
    later = has_tiles[None, :] & (e_ids[None, :] > e_ids[:, None])
    next_of_e = jnp.min(jnp.where(later, e_ids[None, :], N_EXPERTS), axis=1)
    next_of_e = jnp.where(next_of_e < N_EXPERTS, next_of_e, -1)
    runs_before_e = jnp.sum((has_tiles[None, :] & (e_ids[None, :] < e_ids[:, None])).astype(jnp.int32), axis=1)
    next_expert = jnp.sum(jnp.where(mine, next_of_e, 0), axis=1).astype(jnp.int32)
    run_parity = jnp.sum(jnp.where(mine, runs_before_e % 2, 0), axis=1).astype(jnp.int32)
    eidx = info[0:TOP_K].astype(jnp.int32)
    row_start = jnp.sum(jnp.where(eidx[None] == e_ids[:, None, None], (tile_start * tm_rows)[:, None, None], 0),
                        axis=0)
    pos = (row_start + info[4:4 + TOP_K].astype(jnp.int32)).reshape(-1)

    xs = _sc_scatter_rows(hn_packed, pos, p)
    ys = _moe_grouped(tile_expert, n_valid, rows_valid, next_expert, run_parity, xs, wg, wu, wd, tm_rows)
    back = _sc_gather_rows(ys, pos)
    return _moe_combine(x2, route, back, min(TM_COMBINE, n))


def _block_diag(n, width):
    idx = np.arange(n) // width
    return jnp.asarray((idx[:, None] == idx[None, :]).astype(np.float32), dtype=BF16)


def _strict_lower(n):
    r = np.arange(n)
    return jnp.asarray((r[:, None] > r[None, :]).astype(np.float32), dtype=BF16)


def _alibi_slopes():
    return np.exp2(-8.0 * np.arange(1, N_HEADS + 1, dtype=np.float32) / N_HEADS).astype(np.float32)


def _prompt_bias():
    i = np.arange(BLOCK)[:, None]
    s = np.arange(2 * BLOCK)[None, :]
    dist = (i + BLOCK - s).astype(np.float32)
    mask = (dist >= 0) & (dist < WINDOW)
    first = mask & (s >= BLOCK)
    slopes = _alibi_slopes()[:, None, None]
    reg = np.where(mask[None], -slopes * dist[None], np.float32(NEG_INF))
    fst = np.where(first[None], -slopes * dist[None], np.float32(NEG_INF))
    return jnp.asarray(np.concatenate([reg, fst], axis=0).astype(np.float32))


def _sample_bias(ts, wb):
    i = np.arange(ts)[:, None]
    s = np.arange(wb + ts)[None, :]
    dist = (i + wb - s).astype(np.float32)
    mask = (dist >= 0) & (dist < WINDOW)
    slopes = _alibi_slopes()[:, None, None]
    b = np.where(mask[None], -slopes * dist[None], np.float32(NEG_INF)).astype(np.float32)
    return jnp.asarray(b.reshape(N_HEADS * ts, wb + ts))


def kernel(x_prompt, x_sample, cache_swa_k, cache_swa_v, state_conv, cache_mem_k, cache_mem_v, mem_prompt, norm_mix_g, w_in, q_norm_g, k_norm_g, attn_sinks, conv_dw_w, conv_dw_b, conv_ln_g, conv_ln_b, w_out, norm_xa_g, norm_mem_g, w_mq, w_mk, w_mv, mq_norm_g, mk_norm_g, w_mo, norm_ffn_g, w_router_group, b_router_group, w_router_expert, b_router_expert, w_exp_gate, w_exp_up, w_exp_down):
    depth = w_in.shape[0]
    bp, sp, d = x_prompt.shape
    nb, ts, _ = x_sample.shape
    wb = cache_swa_k.shape[2]
    mlen = mem_prompt.shape[1]
    assert d == D_MODEL and wb == WINDOW and sp % TM_PROMPT == 0 and nb % SEQ_PER_STEP == 0

    bdq = _block_diag(MXU_TILE, HEAD_DIM)
    bdk = _block_diag(KV_DIM, HEAD_DIM)
    bias_p = _prompt_bias()
    bias_s = _sample_bias(ts, wb)
    lane_lo = (np.arange(D_ATTN) % LANES) < HALF
    row = lambda a: a.reshape(1, -1).astype(F32)

    xp = x_prompt
    xs = x_sample.reshape(nb * ts, d)
    kp_l, vp_l, cp_l, mkp_l, mvp_l, ks_l, vs_l, cs_l = [], [], [], [], [], [], [], []
    for l in range(depth):
        gq = jnp.tile(q_norm_g[l].astype(F32), N_HEADS) * (HEAD_DIM ** -0.5)
        gqlo = jnp.where(lane_lo, gq, 0.0).reshape(1, -1)
        gqhi = jnp.where(lane_lo, 0.0, gq).reshape(1, -1)
        gk = jnp.tile(k_norm_g[l].astype(F32), N_KV_HEADS).reshape(1, -1)
        sinks = attn_sinks[l].astype(F32)
        sinkrow = jnp.repeat(sinks, ts).reshape(N_HEADS * ts, 1)
        win = w_in[l].astype(BF16)
        wout = w_out[l].astype(BF16)
        wmq = w_mq[l].astype(BF16)
        wmo = w_mo[l].astype(BF16)
        gmq = (jnp.tile(mq_norm_g[l].astype(F32), MEM_HEADS) * (MEM_HEAD_DIM ** -0.5)).reshape(1, -1)
        gmk = jnp.tile(mk_norm_g[l].astype(F32), MEM_HEADS).reshape(1, -1)
        w_r = jnp.concatenate([w_router_expert[l], w_router_group[l],
                               jnp.zeros((d, LANES - N_EXPERTS - N_GROUPS), F32)], axis=1).astype(BF16)
        b_r = jnp.concatenate([b_router_expert[l], b_router_group[l],
                               jnp.zeros((LANES - N_EXPERTS - N_GROUPS,), F32)]).reshape(1, -1).astype(F32)
        wg, wu, wd = w_exp_gate[l], w_exp_up[l], w_exp_down[l]

        mix_consts = (row(norm_mix_g[l]), win, bdq, bdk, gqlo, gqhi, gk)
        conv_consts = (conv_dw_w[l].astype(F32), row(conv_dw_b[l]), row(conv_ln_g[l]), row(conv_ln_b[l]), wout)
        tail_consts = (wmo, row(norm_ffn_g[l]), w_r, b_r)

        x1p, kp, vp, cp = _mixer_prompt(xp, sinks, mix_consts + (bias_p,) + conv_consts, TM_PROMPT)
        mk, mv = _memory_kv(mem_prompt.reshape(bp * mlen, d), row(norm_mem_g[l]),
                            w_mk[l].astype(BF16), w_mv[l].astype(BF16), gmk, min(256, bp * mlen))
        mk = mk.reshape(bp, mlen, d)
        mv = mv.reshape(bp, mlen, d)
        x2p, hnp, routep, infop, cntp = _cross_prompt(x1p, mk, mv, (row(norm_xa_g[l]), wmq, gmq) + tail_consts
                                               + (_strict_lower(TM_PROMPT),), TM_PROMPT)
        xp = _moe(x2p.reshape(bp * sp, d), hnp.reshape(bp * sp, d // 2), routep.reshape(bp * sp, LANES), infop, cntp,
                  wg, wu, wd, TM_ROWS_PROMPT).reshape(bp, sp, d)
        kp_l.append(kp.reshape(bp, BLOCK, N_KV_HEADS, HEAD_DIM))
        vp_l.append(vp.reshape(bp, BLOCK, N_KV_HEADS, HEAD_DIM))
        cp_l.append(cp)
        mkp_l.append(mk.reshape(bp, mlen, MEM_HEADS, MEM_HEAD_DIM))
        mvp_l.append(mv.reshape(bp, mlen, MEM_HEADS, MEM_HEAD_DIM))

        x1s, qm, ksn, vsn, csn = _mixer_sample(
            xs, cache_swa_k[l].reshape(nb, wb, KV_DIM), cache_swa_v[l].reshape(nb, wb, KV_DIM),
            jnp.transpose(state_conv[l], (1, 0, 2)),
            mix_consts + (bias_s, sinkrow) + conv_consts + (row(norm_xa_g[l]), wmq, gmq), SEQ_PER_STEP, ts)
        o_s = _mem_attend_sample(qm, _mem_cache_rows(cache_mem_k[l]), _mem_cache_rows(cache_mem_v[l]),
                                 MEM_SEQ_PER_STEP, ts)
        tms = min(256, nb * ts)
        x2s, hns, routes, infos, cnts = _cross_tail_call(o_s, x1s, tail_consts + (_strict_lower(tms),), tms)
        xs = _moe(x2s, hns, routes, infos, cnts, wg, wu, wd, TM_ROWS_SAMPLE)
        ks_l.append(ksn.reshape(nb, wb, N_KV_HEADS, HEAD_DIM))
        vs_l.append(vsn.reshape(nb, wb, N_KV_HEADS, HEAD_DIM))
        cs_l.append(jnp.transpose(csn, (1, 0, 2)))

    st = lambda xs_: jnp.stack(xs_, axis=0)
    return (xp, xs.reshape(nb, ts, d), st(kp_l), st(vp_l), st(cp_l), st(mkp_l), st(mvp_l),
            st(ks_l), st(vs_l), st(cs_l))
```

```python
import functools

import numpy as np
import jax
import jax.numpy as jnp
from jax import lax
from jax.experimental import pallas as pl
from jax.experimental.pallas import tpu as pltpu
from jax.experimental.pallas import tpu_sc as plsc

F32 = jnp.float32
BF16 = jnp.bfloat16

D_MODEL = 1024
D_ATTN = 512
D_CONV = 512
HEAD_DIM = 64
N_HEADS = 8
N_KV_HEADS = 2
KV_DIM = N_KV_HEADS * HEAD_DIM
HEADS_PER_KV = N_HEADS // N_KV_HEADS
WINDOW = 128
BLOCK = 128
CONV_WIDTH = 31
CONV_PAST = CONV_WIDTH - 1
MEM_HEADS = 4
MEM_HEAD_DIM = 256
N_GROUPS = 4
EXPERTS_PER_GROUP = 8
N_EXPERTS = 32
D_EXPERT = 256
D_IN = D_ATTN + 2 * KV_DIM + 2 * D_CONV
EPS = 1e-6
NEG_INF = -1e30

LANES = 128
HALF = LANES // 2
MXU_TILE = 256
SUBLANES = 8
CONV_PAD = 32
CONV_ROWS = 64
VMEM_LIMIT = 56 * 1024 * 1024

TM_PROMPT = 512
SEQ_PER_STEP = 32
MEM_SEQ_PER_STEP = 8
TOP_K = 2
ROUTE_ROWS = 8
SC_CORES = 2
SC_WORKERS = 32
SC_ROWS_PER_STEP = 64
TM_ROWS_PROMPT = 512
TM_ROWS_SAMPLE = 128
TM_COMBINE = 1024


def _rms(x, g):
    ms = jnp.mean(x * x, axis=-1, keepdims=True)
    return x * lax.rsqrt(ms + EPS) * g


def _group_mean_sq(x, bd_ref, width):
    x2 = x * x
    hi = x2.astype(BF16)
    lo = (x2 - hi.astype(F32)).astype(BF16)
    bd = bd_ref[...]
    w = bd.shape[0]
    parts = [jnp.dot(hi[:, c:c + w], bd, preferred_element_type=F32)
             + jnp.dot(lo[:, c:c + w], bd, preferred_element_type=F32) for c in range(0, x.shape[1], w)]
    return jnp.concatenate(parts, axis=1) * (1.0 / width)


def _mixer_proj(x, ng_ref, win_ref, bdq_ref, bdk_ref, gqlo_ref, gqhi_ref, gk_ref):
    h = _rms(x, ng_ref[...]).astype(BF16)
    p = jnp.dot(h, win_ref[...], preferred_element_type=F32)
    q = p[:, :D_ATTN]
    k = p[:, D_ATTN:D_ATTN + KV_DIM]
    v = p[:, D_ATTN + KV_DIM:D_ATTN + 2 * KV_DIM]
    ua = p[:, D_ATTN + 2 * KV_DIM:D_ATTN + 2 * KV_DIM + D_CONV]
    ub = p[:, D_ATTN + 2 * KV_DIM + D_CONV:]
    qn = q * lax.rsqrt(_group_mean_sq(q, bdq_ref, HEAD_DIM) + EPS)
    q_lo = qn * gqlo_ref[...]
    q_hi = qn * gqhi_ref[...]
    kn = k * lax.rsqrt(_group_mean_sq(k, bdk_ref, HEAD_DIM) + EPS) * gk_ref[...]
    u = ua * jax.nn.sigmoid(ub)
    return q_lo, q_hi, kn, v, u


def _dup_halves(x):
    lo = lax.broadcasted_iota(jnp.int32, x.shape, 1) < HALF
    xr = pltpu.roll(x, HALF, axis=1)
    return jnp.where(lo, x, xr), jnp.where(lo, xr, x)


def _sink_softmax(s, sink):
    m = jnp.maximum(jnp.max(s, axis=-1, keepdims=True), sink)
    p = jnp.exp(s - m)
    denom = jnp.sum(p, axis=-1, keepdims=True) + jnp.exp(sink - m)
    return p * (1.0 / denom)


def _conv_ln_silu(y, cb_ref, lg_ref, lb_ref):
    y = y + cb_ref[...]
    mu = jnp.mean(y, axis=-1, keepdims=True)
    yc = y - mu
    yn = yc * lax.rsqrt(jnp.mean(yc * yc, axis=-1, keepdims=True) + EPS)
    z = yn * lg_ref[...] + lb_ref[...]
    return z * jax.nn.sigmoid(z)


def _mem_query(x1, xag_ref, wmq_ref, gmq_ref):
    h = _rms(x1, xag_ref[...]).astype(BF16)
    q = jnp.dot(h, wmq_ref[...], preferred_element_type=F32)
    parts = []
    for hd in range(MEM_HEADS):
        qh = q[:, hd * MEM_HEAD_DIM:(hd + 1) * MEM_HEAD_DIM]
        parts.append(qh * lax.rsqrt(jnp.mean(qh * qh, axis=-1, keepdims=True) + EPS))
    return jnp.concatenate(parts, axis=1) * gmq_ref[...]


def _mem_attend(q, k_heads, v_heads):
    outs = []
    for hd in range(MEM_HEADS):
        sl = slice(hd * MEM_HEAD_DIM, (hd + 1) * MEM_HEAD_DIM)
        s = lax.dot_general(q[:, sl].astype(BF16), k_heads[hd], (((1,), (1,)), ((), ())),
                            preferred_element_type=F32)
        m = jnp.max(s, axis=-1, keepdims=True)
        p = jnp.exp(s - m)
        p = p * (1.0 / jnp.sum(p, axis=-1, keepdims=True))
        outs.append(jnp.dot(p.astype(BF16), v_heads[hd], preferred_element_type=F32))
    return jnp.concatenate(outs, axis=1)


def _split_heads(x):
    return [x[:, hd * MEM_HEAD_DIM:(hd + 1) * MEM_HEAD_DIM] for hd in range(MEM_HEADS)]


def _pack_bf16_pairs(y):
    n = y.shape[1] // 2
    bits = pltpu.bitcast(y.astype(BF16).astype(F32), jnp.int32)
    return (bits[:, :n] & jnp.int32(-65536)) | lax.shift_right_logical(bits[:, n:], jnp.int32(16))


def _unpack_bf16_pairs(w):
    hi = pltpu.bitcast(w & jnp.int32(-65536), F32)
    lo = pltpu.bitcast(lax.shift_left(w, jnp.int32(16)), F32)
    return jnp.concatenate([hi, lo], axis=1)


def _cross_tail(o, x1, wmo_ref, fg_ref, wr_ref, br_ref, tri_ref, base):
    x2 = x1 + jnp.dot(o.astype(BF16), wmo_ref[...], preferred_element_type=F32)
    hn_f = _rms(x2, fg_ref[...])
    hn = hn_f.astype(BF16)
    logits = jnp.dot(hn, wr_ref[...], preferred_element_type=F32) + br_ref[...]
    lane = lax.broadcasted_iota(jnp.int32, logits.shape, 1)
    is_g = (lane >= N_EXPERTS) & (lane < N_EXPERTS + N_GROUPS)
    lg = jnp.where(is_g, logits, NEG_INF)
    gmax = jnp.max(lg, axis=-1, keepdims=True)
    gsel = jnp.min(jnp.where(lg == gmax, lane, 2 * LANES), axis=-1, keepdims=True) - N_EXPERTS
    pg_sel = 1.0 / jnp.sum(jnp.exp(lg - gmax), axis=-1, keepdims=True)
    in_grp = (lane >= gsel * EXPERTS_PER_GROUP) & (lane < (gsel + 1) * EXPERTS_PER_GROUP)
    le = jnp.where(in_grp, logits, NEG_INF)
    top1 = jnp.max(le, axis=-1, keepdims=True)
    idx1 = jnp.min(jnp.where(le == top1, lane, 2 * LANES), axis=-1, keepdims=True)
    le2 = jnp.where(lane == idx1, NEG_INF, le)
    top2 = jnp.max(le2, axis=-1, keepdims=True)
    idx2 = jnp.min(jnp.where(le2 == top2, lane, 2 * LANES), axis=-1, keepdims=True)
    e2 = jnp.exp(top2 - top1)
    inv = 1.0 / (1.0 + e2)
    gate1 = pg_sel * inv
    gate2 = pg_sel * (e2 * inv)
    used = jnp.where((lane == idx1) | (lane == idx2), 1.0, 0.0)
    before = jnp.dot(tri_ref[...], used.astype(BF16), preferred_element_type=F32) + base
    rank1 = jnp.sum(jnp.where(lane == idx1, before, 0.0), axis=-1, keepdims=True)
    rank2 = jnp.sum(jnp.where(lane == idx2, before, 0.0), axis=-1, keepdims=True)
    route = jnp.zeros_like(logits)
    for pos, val in enumerate((idx1.astype(F32), idx2.astype(F32), gate1, gate2, rank1, rank2)):
        route = jnp.where(lane == pos, val, route)
    info = jnp.transpose(route)[0:ROUTE_ROWS, :]
    return x2, _pack_bf16_pairs(hn_f), route, info, base + jnp.sum(used, axis=0, keepdims=True)


def _mixer_prompt_kernel(sink_ref, x_ref, ng_ref, win_ref, bdq_ref, bdk_ref, gqlo_ref, gqhi_ref, gk_ref,
                         bias_ref, cw_ref, cb_ref, lg_ref, lb_ref, wout_ref,
                         x1_ref, ko_ref, vo_ref, co_ref,
                         kband, vband, uext, ushift, *, tm, nt):
    t = pl.program_id(1)

    @pl.when(t == 0)
    def _():
        kband[0:BLOCK, :] = jnp.zeros((BLOCK, KV_DIM), F32)
        vband[0:BLOCK, :] = jnp.zeros((BLOCK, KV_DIM), F32)
        uext[0:CONV_PAD, :] = jnp.zeros((CONV_PAD, D_CONV), F32)

    x = x_ref[...]
    q_lo, q_hi, kn, v, u = _mixer_proj(x, ng_ref, win_ref, bdq_ref, bdk_ref, gqlo_ref, gqhi_ref, gk_ref)
    kband[BLOCK:BLOCK + tm, :] = kn
    vband[BLOCK:BLOCK + tm, :] = v
    uext[CONV_PAD:CONV_PAD + tm, :] = u

    @pl.when(t == nt - 1)
    def _():
        ko_ref[...] = kn[tm - BLOCK:, :]
        vo_ref[...] = v[tm - BLOCK:, :]
        co_ref[...] = uext[pl.ds(CONV_PAD + tm - CONV_PAST, CONV_PAST), :]

    kd = [a.astype(BF16) for a in _dup_halves(kband[...])]
    vd = _dup_halves(vband[...])
    lo = lax.broadcasted_iota(jnp.int32, vd[0].shape, 1) < HALF
    v_lo = [jnp.where(lo, a, 0.0).astype(BF16) for a in vd]
    v_hi = [jnp.where(lo, 0.0, a).astype(BF16) for a in vd]
    q_lo = q_lo.astype(BF16)
    q_hi = q_hi.astype(BF16)
    attn_blocks = []
    for j in range(tm // BLOCK):
        rows = slice(j * BLOCK, (j + 1) * BLOCK)
        keys = slice(j * BLOCK, j * BLOCK + 2 * BLOCK)
        bias_base = jnp.where(t == 0, N_HEADS, 0) if j == 0 else 0
        tiles = []
        for c in range(N_KV_HEADS):
            q4 = jnp.concatenate(
                [(q_lo if (HEADS_PER_KV * c + a) % 2 == 0 else q_hi)[rows,
                  ((HEADS_PER_KV * c + a) // 2) * LANES:((HEADS_PER_KV * c + a) // 2 + 1) * LANES]
                 for a in range(HEADS_PER_KV)], axis=0)
            s_all = lax.dot_general(q4, kd[c][keys], (((1,), (1,)), ((), ())), preferred_element_type=F32)
            ps = []
            for a in range(HEADS_PER_KV):
                hd = HEADS_PER_KV * c + a
                s = s_all[a * BLOCK:(a + 1) * BLOCK] + bias_ref[bias_base + hd]
                ps.append(_sink_softmax(s, sink_ref[hd]).astype(BF16))
            vstack = jnp.concatenate([v_lo[c][keys], v_hi[c][keys]], axis=0)
            for i2 in range(HEADS_PER_KV // 2):
                pp = jnp.concatenate([ps[2 * i2], ps[2 * i2 + 1]], axis=1)
                tiles.append(jnp.dot(pp, vstack, preferred_element_type=F32))
        attn_blocks.append(jnp.concatenate(tiles, axis=1))
    attn = jnp.concatenate(attn_blocks, axis=0)

    off = CONV_PAD - CONV_PAST
    span = tm + CONV_PAD - SUBLANES
    for s in range(1, SUBLANES):
        ushift[s - 1] = uext[pl.ds(s, span), :]
    chunks = []
    for lc in range(D_CONV // LANES):
        ls = slice(lc * LANES, (lc + 1) * LANES)
        accs = [jnp.zeros((CONV_ROWS, LANES), F32) for _ in range(tm // CONV_ROWS)]
        for j in range(CONV_WIDTH):
            s, a = (off + j) % SUBLANES, (off + j) // SUBLANES
            wj = cw_ref[j:j + 1, ls]
            for rc in range(tm // CONV_ROWS):
                r0 = rc * CONV_ROWS + a * SUBLANES
                tap = uext[r0:r0 + CONV_ROWS, ls] if s == 0 else ushift[s - 1, r0:r0 + CONV_ROWS, ls]
                accs[rc] = accs[rc] + wj * tap
        chunks.append(jnp.concatenate(accs, axis=0))
    y = _conv_ln_silu(jnp.concatenate(chunks, axis=1), cb_ref, lg_ref, lb_ref)

    x1_ref[...] = (x + jnp.dot(attn.astype(BF16), wout_ref[0:D_ATTN, :], preferred_element_type=F32)
                   + jnp.dot(y.astype(BF16), wout_ref[D_ATTN:, :], preferred_element_type=F32))

    kband[0:BLOCK, :] = kband[tm:tm + BLOCK, :]
    vband[0:BLOCK, :] = vband[tm:tm + BLOCK, :]
    uext[0:CONV_PAD, :] = uext[tm:tm + CONV_PAD, :]


def _const_spec(shape):
    nd = len(shape)
    return pl.BlockSpec(shape, lambda *_: (0,) * nd)


def _mixer_prompt(x, sinks, consts, tm):
    b, s, d = x.shape
    nt = s // tm
    (ng, win, bdq, bdk, gqlo, gqhi, gk, bias_p, cw, cb, lg, lb, wout) = consts
    kern = functools.partial(_mixer_prompt_kernel, tm=tm, nt=nt)
    cspecs = [_const_spec(a.shape) for a in consts]
    return pl.pallas_call(
        kern,
        grid=(b, nt),
        in_specs=[pl.BlockSpec(memory_space=pltpu.SMEM),
                  pl.BlockSpec((None, tm, d), lambda i, j: (i, j, 0))] + cspecs,
        out_specs=[pl.BlockSpec((None, tm, d), lambda i, j: (i, j, 0)),
                   pl.BlockSpec((None, BLOCK, KV_DIM), lambda i, j: (i, 0, 0)),
                   pl.BlockSpec((None, BLOCK, KV_DIM), lambda i, j: (i, 0, 0)),
                   pl.BlockSpec((None, CONV_PAST, D_CONV), lambda i, j: (i, 0, 0))],
        out_shape=[jax.ShapeDtypeStruct((b, s, d), F32),
                   jax.ShapeDtypeStruct((b, BLOCK, KV_DIM), F32),
                   jax.ShapeDtypeStruct((b, BLOCK, KV_DIM), F32),
                   jax.ShapeDtypeStruct((b, CONV_PAST, D_CONV), F32)],
        scratch_shapes=[pltpu.VMEM((BLOCK + tm, KV_DIM), F32),
                        pltpu.VMEM((BLOCK + tm, KV_DIM), F32),
                        pltpu.VMEM((CONV_PAD + tm, D_CONV), F32),
                        pltpu.VMEM((SUBLANES - 1, tm + CONV_PAD - SUBLANES, D_CONV), F32)],
        compiler_params=pltpu.CompilerParams(dimension_semantics=("arbitrary", "arbitrary"),
                                             vmem_limit_bytes=VMEM_LIMIT),
        name="mixer_prompt",
    )(sinks, x, *consts)


def _mixer_sample_kernel(x_ref, ck_ref, cv_ref, st_ref, ng_ref, win_ref, bdq_ref, bdk_ref, gqlo_ref, gqhi_ref,
                         gk_ref, bias_ref, sinkrow_ref, cw_ref, cb_ref, lg_ref, lb_ref, wout_ref,
                         xag_ref, wmq_ref, gmq_ref,
                         x1_ref, qm_ref, ko_ref, vo_ref, co_ref,
                         kall, vall, qs, s_scr, r_scr, u_scr, y_scr, *, g, ts):
    wb = ck_ref.shape[2]
    x = x_ref[...]
    q_lo, q_hi, kn, v, u = _mixer_proj(x, ng_ref, win_ref, bdq_ref, bdk_ref, gqlo_ref, gqhi_ref, gk_ref)
    kall[:, wb:wb + ts, :] = kn.reshape(g, ts, KV_DIM)
    vall[:, wb:wb + ts, :] = v.reshape(g, ts, KV_DIM)
    for n in range(g):
        kall[n, 0:wb, :] = ck_ref[n].T
        vall[n, 0:wb, :] = cv_ref[n].T
        ko_ref[n] = kall[n, ts:wb + ts, :].T
        vo_ref[n] = vall[n, ts:wb + ts, :].T

    for lc in range(D_CONV // LANES):
        ls = slice(lc * LANES, (lc + 1) * LANES)
        u_scr[lc] = u[:, ls]
        u_t = [u_scr[lc, pl.ds(t, g, stride=ts), :] for t in range(ts)]

        def frame(r):
            return st_ref[r, :, ls] if r < CONV_PAST else u_t[r - CONV_PAST]

        for t in range(ts):
            acc = jnp.zeros((g, LANES), F32)
            for j in range(CONV_WIDTH):
                acc = acc + cw_ref[j:j + 1, ls] * frame(t + j)
            y_scr[lc, pl.ds(t, g, stride=ts), :] = acc
        for r in range(CONV_PAST):
            co_ref[r, :, ls] = frame(r + ts)
    y = _conv_ln_silu(jnp.concatenate([y_scr[lc] for lc in range(D_CONV // LANES)], axis=1), cb_ref, lg_ref, lb_ref)

    for hd in range(N_HEADS):
        tile = (q_lo if hd % 2 == 0 else q_hi)[:, (hd // 2) * LANES:(hd // 2 + 1) * LANES]
        if hd % 2 != hd // HEADS_PER_KV:
            tile = pltpu.roll(tile, HALF, axis=1)
        qs[:, hd * ts:(hd + 1) * ts, :] = tile.reshape(g, ts, LANES)
    for n in range(g):
        s_scr[n] = lax.dot_general(qs[n].astype(BF16), kall[n].astype(BF16), (((1,), (1,)), ((), ())),
                                   preferred_element_type=F32)
    s_scr[...] = _sink_softmax(s_scr[...] + bias_ref[...], sinkrow_ref[...])
    for n in range(g):
        r_scr[n] = jnp.dot(s_scr[n].astype(BF16), vall[n].astype(BF16), preferred_element_type=F32)
    r = r_scr[...].reshape(g * N_HEADS * ts, LANES)
    r_sw = pltpu.roll(r, HALF, axis=1).reshape(g, N_HEADS * ts, LANES)
    r = r.reshape(g, N_HEADS * ts, LANES)
    lo = lax.broadcasted_iota(jnp.int32, (g * ts, LANES), 1) < HALF
    heads = [(r if hd // HEADS_PER_KV == hd % 2 else r_sw)[:, hd * ts:(hd + 1) * ts, :].reshape(g * ts, LANES)
             for hd in range(N_HEADS)]
    attn = jnp.concatenate([jnp.where(lo, heads[2 * i], heads[2 * i + 1]) for i in range(N_HEADS // 2)], axis=1)

    x1 = (x + jnp.dot(attn.astype(BF16), wout_ref[0:D_ATTN, :], preferred_element_type=F32)
          + jnp.dot(y.astype(BF16), wout_ref[D_ATTN:, :], preferred_element_type=F32))
    x1_ref[...] = x1
    qm_ref[...] = _mem_query(x1, xag_ref, wmq_ref, gmq_ref)


def _mixer_sample(xs2d, ck, cv, st, consts, g, ts):
    n, d = xs2d.shape
    nb, wb = ck.shape[0], ck.shape[2]
    rows = g * ts
    cspecs = [_const_spec(a.shape) for a in consts]
    kern = functools.partial(_mixer_sample_kernel, g=g, ts=ts)
    return pl.pallas_call(
        kern,
        grid=(nb // g,),
        in_specs=[pl.BlockSpec((rows, d), lambda i: (i, 0)),
                  pl.BlockSpec((g, KV_DIM, wb), lambda i: (i, 0, 0)),
                  pl.BlockSpec((g, KV_DIM, wb), lambda i: (i, 0, 0)),
                  pl.BlockSpec((CONV_PAST, g, D_CONV), lambda i: (0, i, 0))] + cspecs,
        out_specs=[pl.BlockSpec((rows, d), lambda i: (i, 0)),
                   pl.BlockSpec((rows, d), lambda i: (i, 0)),
                   pl.BlockSpec((g, KV_DIM, wb), lambda i: (i, 0, 0)),
                   pl.BlockSpec((g, KV_DIM, wb), lambda i: (i, 0, 0)),
                   pl.BlockSpec((CONV_PAST, g, D_CONV), lambda i: (0, i, 0))],
        out_shape=[jax.ShapeDtypeStruct((n, d), F32),
                   jax.ShapeDtypeStruct((n, d), F32),
                   jax.ShapeDtypeStruct((nb, KV_DIM, wb), F32),
                   jax.ShapeDtypeStruct((nb, KV_DIM, wb), F32),
                   jax.ShapeDtypeStruct((CONV_PAST, nb, D_CONV), F32)],
        scratch_shapes=[pltpu.VMEM((g, wb + ts, KV_DIM), F32),
                        pltpu.VMEM((g, wb + ts, KV_DIM), F32),
                        pltpu.VMEM((g, N_HEADS * ts, LANES), F32),
                        pltpu.VMEM((g, N_HEADS * ts, wb + ts), F32),
                        pltpu.VMEM((g, N_HEADS * ts, LANES), F32),
                        pltpu.VMEM((D_CONV // LANES, rows, LANES), F32),
                        pltpu.VMEM((D_CONV // LANES, rows, LANES), F32)],
        compiler_params=pltpu.CompilerParams(dimension_semantics=("arbitrary",),
                                             vmem_limit_bytes=VMEM_LIMIT),
        name="mixer_sample",
    )(xs2d, ck, cv, st, *consts)


def _memory_kv_kernel(mem_ref, g_ref, wmk_ref, wmv_ref, gk_ref, k_ref, v_ref):
    h = _rms(mem_ref[...], g_ref[...]).astype(BF16)
    k = jnp.dot(h, wmk_ref[...], preferred_element_type=F32)
    parts = []
    for hd in range(MEM_HEADS):
        kh = k[:, hd * MEM_HEAD_DIM:(hd + 1) * MEM_HEAD_DIM]
        parts.append(kh * lax.rsqrt(jnp.mean(kh * kh, axis=-1, keepdims=True) + EPS))
    k_ref[...] = jnp.concatenate(parts, axis=1) * gk_ref[...]
    v_ref[...] = jnp.dot(h, wmv_ref[...], preferred_element_type=F32)


def _memory_kv(mem2d, g, wmk, wmv, gk, tm):
    n, d = mem2d.shape
    consts = (g, wmk, wmv, gk)
    return pl.pallas_call(
        _memory_kv_kernel,
        grid=(n // tm,),
        in_specs=[pl.BlockSpec((tm, d), lambda i: (i, 0))] + [_const_spec(a.shape) for a in consts],
        out_specs=[pl.BlockSpec((tm, d), lambda i: (i, 0)), pl.BlockSpec((tm, d), lambda i: (i, 0))],
        out_shape=[jax.ShapeDtypeStruct((n, d), F32), jax.ShapeDtypeStruct((n, d), F32)],
        compiler_params=pltpu.CompilerParams(dimension_semantics=("arbitrary",),
                                             vmem_limit_bytes=VMEM_LIMIT),
        name="memory_kv",
    )(mem2d, *consts)


def _cross_prompt_kernel(x1_ref, mk_ref, mv_ref, xag_ref, wmq_ref, gmq_ref, wmo_ref, fg_ref, wr_ref, br_ref, tri_ref,
                         x2_ref, hn_ref, route_ref, info_ref, cnt_ref, base):
    @pl.when((pl.program_id(0) == 0) & (pl.program_id(1) == 0))
    def _():
        base[...] = jnp.zeros_like(base)

    x1 = x1_ref[...]
    q = _mem_query(x1, xag_ref, wmq_ref, gmq_ref)
    o = _mem_attend(q, _split_heads(mk_ref[...].astype(BF16)), _split_heads(mv_ref[...].astype(BF16)))
    x2, hn, route, info, new_base = _cross_tail(o, x1, wmo_ref, fg_ref, wr_ref, br_ref, tri_ref, base[...])
    x2_ref[...] = x2
    hn_ref[...] = hn
    route_ref[...] = route
    info_ref[...] = info
    base[...] = new_base
    cnt_ref[...] = new_base


def _cross_prompt(x1, mk, mv, consts, tm):
    b, s, d = x1.shape
    m = mk.shape[1]
    cspecs = [_const_spec(a.shape) for a in consts]
    return pl.pallas_call(
        _cross_prompt_kernel,
        grid=(b, s // tm),
        in_specs=[pl.BlockSpec((None, tm, d), lambda i, j: (i, j, 0)),
                  pl.BlockSpec((None, m, d), lambda i, j: (i, 0, 0)),
                  pl.BlockSpec((None, m, d), lambda i, j: (i, 0, 0))] + cspecs,
        out_specs=[pl.BlockSpec((None, tm, d), lambda i, j: (i, j, 0)),
                   pl.BlockSpec((None, tm, d // 2), lambda i, j: (i, j, 0)),
                   pl.BlockSpec((None, tm, LANES), lambda i, j: (i, j, 0)),
                   pl.BlockSpec((ROUTE_ROWS, tm), lambda i, j: (0, i * (s // tm) + j)),
                   pl.BlockSpec((1, LANES), lambda i, j: (0, 0))],
        out_shape=[jax.ShapeDtypeStruct((b, s, d), F32),
                   jax.ShapeDtypeStruct((b, s, d // 2), jnp.int32),
                   jax.ShapeDtypeStruct((b, s, LANES), F32),
                   jax.ShapeDtypeStruct((ROUTE_ROWS, b * s), F32),
                   jax.ShapeDtypeStruct((1, LANES), F32)],
        scratch_shapes=[pltpu.VMEM((1, LANES), F32)],
        compiler_params=pltpu.CompilerParams(dimension_semantics=("arbitrary", "arbitrary"),
                                             vmem_limit_bytes=VMEM_LIMIT),
        name="cross_prompt",
    )(x1, mk, mv, *consts)


def _mem_attend_sample_kernel(qm_ref, *refs, g, ts):
    k_refs, v_refs, o_ref = refs[:g], refs[g:2 * g], refs[2 * g]
    halves = MEM_HEAD_DIM // LANES
    rows_per_pos = MEM_HEADS * halves
    mlen = k_refs[0].shape[0] // rows_per_pos

    def all_heads(ref):
        return jnp.concatenate([ref[pl.ds(c * MEM_HEADS + hd, mlen, stride=rows_per_pos), :]
                                for hd in range(MEM_HEADS) for c in range(halves)], axis=1).astype(BF16)

    rows = MEM_HEADS * ts
    own = (lax.broadcasted_iota(jnp.int32, (rows, MEM_HEADS * MEM_HEAD_DIM), 1) // MEM_HEAD_DIM
           == lax.broadcasted_iota(jnp.int32, (rows, MEM_HEADS * MEM_HEAD_DIM), 0) // ts)
    for n in range(g):
        q = qm_ref[n * ts:(n + 1) * ts, :]
        q_bd = jnp.where(own, jnp.concatenate([q] * MEM_HEADS, axis=0), 0.0).astype(BF16)
        s = lax.dot_general(q_bd, all_heads(k_refs[n]), (((1,), (1,)), ((), ())), preferred_element_type=F32)
        m = jnp.max(s, axis=-1, keepdims=True)
        p = jnp.exp(s - m)
        p = p * (1.0 / jnp.sum(p, axis=-1, keepdims=True))
        r = jnp.dot(p.astype(BF16), all_heads(v_refs[n]), preferred_element_type=F32)
        o_ref[n * ts:(n + 1) * ts, :] = jnp.concatenate(
            [r[hd * ts:(hd + 1) * ts, hd * MEM_HEAD_DIM:(hd + 1) * MEM_HEAD_DIM] for hd in range(MEM_HEADS)], axis=1)


def _mem_cache_rows(cache):
    nb, mlen = cache.shape[:2]
    halves = MEM_HEAD_DIM // LANES
    return (cache.reshape(nb, mlen, MEM_HEADS, halves, LANES).transpose(0, 1, 3, 2, 4)
            .reshape(nb, mlen * halves * MEM_HEADS, LANES))


def _mem_attend_sample(qm, mk, mv, g, ts):
    n, d = qm.shape
    nb, rows = mk.shape[0], mk.shape[1]
    kern = functools.partial(_mem_attend_sample_kernel, g=g, ts=ts)

    def seq_spec(j):
        return pl.BlockSpec((None, rows, LANES), lambda i: (g * i + j, 0, 0))

    return pl.pallas_call(
        kern,
        grid=(nb // g,),
        in_specs=[pl.BlockSpec((g * ts, d), lambda i: (i, 0))] + [seq_spec(j) for j in range(g)] * 2,
        out_specs=pl.BlockSpec((g * ts, d), lambda i: (i, 0)),
        out_shape=jax.ShapeDtypeStruct((n, d), F32),
        compiler_params=pltpu.CompilerParams(dimension_semantics=("arbitrary",),
                                             vmem_limit_bytes=VMEM_LIMIT),
        name="mem_attend_sample",
    )(qm, *([mk] * g), *([mv] * g))


def _cross_tail_kernel(o_ref, x1_ref, wmo_ref, fg_ref, wr_ref, br_ref, tri_ref,
                       x2_ref, hn_ref, route_ref, info_ref, cnt_ref, base):
    @pl.when(pl.program_id(0) == 0)
    def _():
        base[...] = jnp.zeros_like(base)

    x2, hn, route, info, new_base = _cross_tail(o_ref[...], x1_ref[...], wmo_ref, fg_ref, wr_ref, br_ref, tri_ref,
                                                base[...])
    x2_ref[...] = x2
    hn_ref[...] = hn
    route_ref[...] = route
    info_ref[...] = info
    base[...] = new_base
    cnt_ref[...] = new_base


def _cross_tail_call(o, x1, consts, tm):
    n, d = x1.shape
    cspecs = [_const_spec(a.shape) for a in consts]
    return pl.pallas_call(
        _cross_tail_kernel,
        grid=(n // tm,),
        in_specs=[pl.BlockSpec((tm, d), lambda i: (i, 0)), pl.BlockSpec((tm, d), lambda i: (i, 0))] + cspecs,
        out_specs=[pl.BlockSpec((tm, d), lambda i: (i, 0)),
                   pl.BlockSpec((tm, d // 2), lambda i: (i, 0)),
                   pl.BlockSpec((tm, LANES), lambda i: (i, 0)),
                   pl.BlockSpec((ROUTE_ROWS, tm), lambda i: (0, i)),
                   pl.BlockSpec((1, LANES), lambda i: (0, 0))],
        out_shape=[jax.ShapeDtypeStruct((n, d), F32),
                   jax.ShapeDtypeStruct((n, d // 2), jnp.int32),
                   jax.ShapeDtypeStruct((n, LANES), F32),
                   jax.ShapeDtypeStruct((ROUTE_ROWS, n), F32),
                   jax.ShapeDtypeStruct((1, LANES), F32)],
        scratch_shapes=[pltpu.VMEM((1, LANES), F32)],
        compiler_params=pltpu.CompilerParams(dimension_semantics=("arbitrary",),
                                             vmem_limit_bytes=VMEM_LIMIT),
        name="cross_tail",
    )(o, x1, *consts)


def _sc_rows_per_step(per_worker):
    step = min(SC_ROWS_PER_STEP, per_worker)
    assert per_worker % step == 0 and step % 8 == 0
    return step


def _sc_gather_rows(table, idx):
    nrows = idx.shape[0]
    _, width = table.shape
    assert nrows % (8 * SC_WORKERS) == 0
    per_worker = nrows // SC_WORKERS
    step = _sc_rows_per_step(per_worker)
    mesh = plsc.VectorSubcoreMesh(core_axis_name="c", subcore_axis_name="s")

    @functools.partial(
        pl.kernel, mesh=mesh, out_type=jax.ShapeDtypeStruct((nrows, width), table.dtype),
        scratch_types=[pltpu.VMEM((step,), jnp.int32),
                       pltpu.VMEM((step, width), table.dtype),
                       pltpu.SemaphoreType.DMA])
    def gather(table_hbm, idx_hbm, out_hbm, idx_v, rows_v, sem):
        wid = lax.axis_index("s") * SC_CORES + lax.axis_index("c")
        base = wid * per_worker

        @pl.loop(0, per_worker // step)
        def _(i):
            off = base + i * step
            pltpu.sync_copy(idx_hbm.at[pl.ds(off, step)], idx_v)
            pltpu.async_copy(table_hbm.at[idx_v], rows_v, sem).wait()
            pltpu.sync_copy(rows_v, out_hbm.at[pl.ds(off, step)])

    return gather(table, idx)


def _sc_scatter_rows(table, pos, nrows_out):
    n, width = table.shape
    assert n % (8 * SC_WORKERS) == 0 and pos.shape == (TOP_K * n,)
    per_worker = n // SC_WORKERS
    step = _sc_rows_per_step(per_worker)
    mesh = plsc.VectorSubcoreMesh(core_axis_name="c", subcore_axis_name="s")

    @functools.partial(
        pl.kernel, mesh=mesh, out_type=jax.ShapeDtypeStruct((nrows_out, width), table.dtype),
        scratch_types=[pltpu.VMEM((step,), jnp.int32)] * TOP_K
        + [pltpu.VMEM((step, width), table.dtype), pltpu.SemaphoreType.DMA])
    def scatter(table_hbm, pos_hbm, out_hbm, *scratch):
        idx_vs, rows_v, sem = scratch[:TOP_K], scratch[TOP_K], scratch[TOP_K + 1]
        wid = lax.axis_index("s") * SC_CORES + lax.axis_index("c")
        base = wid * per_worker

        @pl.loop(0, per_worker // step)
        def _(i):
            off = base + i * step
            for k in range(TOP_K):
                pltpu.sync_copy(pos_hbm.at[pl.ds(k * n + off, step)], idx_vs[k])
            pltpu.sync_copy(table_hbm.at[pl.ds(off, step)], rows_v)
            for k in range(TOP_K):
                pltpu.async_copy(rows_v, out_hbm.at[idx_vs[k]], sem).wait()

    return scatter(table, pos)


W_CHUNKS = 4


def _expert_weight_copies(e, slot, w_hbm, w_f32, sems):
    copies = []
    for m, (src, dst) in enumerate(zip(w_hbm, w_f32)):
        rows = dst.shape[1] // W_CHUNKS
        for c in range(W_CHUNKS):
            sl = pl.ds(c * rows, rows)
            copies.append(pltpu.make_async_copy(src.at[e, sl, :], dst.at[slot, sl, :], sems.at[m * W_CHUNKS + c]))
    return copies


def _moe_grouped_kernel(te_ref, nv_ref, rv_ref, nxt_ref, par_ref, xs_ref, wg_hbm, wu_hbm, wd_hbm, ys_ref,
                        wg_f, wu_f, wd_f, wg_b, wu_b, wd_b, sems):
    i = pl.program_id(0)
    w_hbm, w_f32 = (wg_hbm, wu_hbm, wd_hbm), (wg_f, wu_f, wd_f)

    @pl.when(i == 0)
    def _():
        for cp in _expert_weight_copies(te_ref[0], par_ref[0], w_hbm, w_f32, sems):
            cp.start()

    @pl.when(i < nv_ref[0])
    def _():
        @pl.when((i == 0) | (te_ref[i] != te_ref[jnp.maximum(i - 1, 0)]))
        def _():
            slot = par_ref[i]
            for cp in _expert_weight_copies(te_ref[i], slot, w_hbm, w_f32, sems):
                cp.wait()
            for src, dst in zip(w_f32, (wg_b, wu_b, wd_b)):
                dst[...] = src[slot].astype(BF16)

            @pl.when(nxt_ref[i] >= 0)
            def _():
                for cp in _expert_weight_copies(nxt_ref[i], 1 - slot, w_hbm, w_f32, sems):
                    cp.start()

        xs = xs_ref[...]
        row = lax.broadcasted_iota(jnp.int32, xs.shape, 0)
        x = _unpack_bf16_pairs(jnp.where(row < rv_ref[i], xs, 0)).astype(BF16)
        a = jnp.dot(x, wg_b[...], preferred_element_type=F32)
        b = jnp.dot(x, wu_b[...], preferred_element_type=F32)
        act = (a * jax.nn.sigmoid(a)) * b
        y = jnp.dot(act.astype(BF16), wd_b[...], preferred_element_type=F32)
        ys_ref[...] = _pack_bf16_pairs(y)


def _moe_grouped(tile_expert, n_valid, rows_valid, next_expert, run_parity, xs, wg, wu, wd, tm):
    p, half = xs.shape
    _, d, f = wg.shape

    def live_tile(i, te, nv, *_):
        return (jnp.minimum(i, nv[0] - 1), 0)

    grid_spec = pltpu.PrefetchScalarGridSpec(
        num_scalar_prefetch=5,
        grid=(p // tm,),
        in_specs=[pl.BlockSpec((tm, half), live_tile)] + [pl.BlockSpec(memory_space=pl.ANY)] * 3,
        out_specs=pl.BlockSpec((tm, half), live_tile),
        scratch_shapes=[pltpu.VMEM((2, d, f), F32), pltpu.VMEM((2, d, f), F32), pltpu.VMEM((2, f, d), F32),
                        pltpu.VMEM((d, f), BF16), pltpu.VMEM((d, f), BF16), pltpu.VMEM((f, d), BF16),
                        pltpu.SemaphoreType.DMA((3 * W_CHUNKS,))])
    return pl.pallas_call(
        _moe_grouped_kernel,
        grid_spec=grid_spec,
        out_shape=jax.ShapeDtypeStruct((p, half), jnp.int32),
        compiler_params=pltpu.CompilerParams(dimension_semantics=("arbitrary",),
                                             vmem_limit_bytes=VMEM_LIMIT),
        name="moe_grouped",
    )(tile_expert, n_valid, rows_valid, next_expert, run_parity, xs, wg, wu, wd)


def _moe_combine_kernel(x2_ref, route_ref, y0_ref, y1_ref, out_ref):
    route = route_ref[...]
    out_ref[...] = (x2_ref[...] + route[:, 2:3] * _unpack_bf16_pairs(y0_ref[...])
                    + route[:, 3:4] * _unpack_bf16_pairs(y1_ref[...]))


def _moe_combine(x2, route, gathered, tm):
    n, d = x2.shape
    nt = n // tm
    return pl.pallas_call(
        _moe_combine_kernel,
        grid=(nt,),
        in_specs=[pl.BlockSpec((tm, d), lambda i: (i, 0)),
                  pl.BlockSpec((tm, LANES), lambda i: (i, 0)),
                  pl.BlockSpec((tm, d // 2), lambda i: (i, 0)),
                  pl.BlockSpec((tm, d // 2), lambda i: (i + nt, 0))],
        out_specs=pl.BlockSpec((tm, d), lambda i: (i, 0)),
        out_shape=jax.ShapeDtypeStruct((n, d), F32),
        compiler_params=pltpu.CompilerParams(dimension_semantics=("arbitrary",),
                                             vmem_limit_bytes=VMEM_LIMIT),
        name="moe_combine",
    )(x2, route, gathered, gathered)


def _moe(x2, hn_packed, route, info, counts, wg, wu, wd, tm_rows):
    n = x2.shape[0]
    p = (TOP_K * n // tm_rows + N_EXPERTS) * tm_rows
    n_tiles = p // tm_rows
    cnt = counts[0, :N_EXPERTS].astype(jnp.int32)
    tiles_per_e = (cnt + tm_rows - 1) // tm_rows
    e_ids = jnp.arange(N_EXPERTS, dtype=jnp.int32)
    tile_end = jnp.sum(jnp.where(e_ids[None, :] <= e_ids[:, None], tiles_per_e[None, :], 0), axis=1)
    tile_start = tile_end - tiles_per_e
    n_valid = tile_end[-1:]
    tile_ids = jnp.arange(n_tiles, dtype=jnp.int32)
    tile_expert = jnp.minimum(jnp.sum((tile_end[None, :] <= tile_ids[:, None]).astype(jnp.int32), axis=1),
                              N_EXPERTS - 1)
    mine = tile_expert[:, None] == e_ids[None, :]
    rows_left = jnp.sum(jnp.where(mine, cnt - (tile_ids[:, None] - tile_start) * tm_rows, 0), axis=1)
    rows_valid = jnp.clip(rows_left, 0, tm_rows).astype(jnp.int32)
    has_tiles = tiles_per_e > 0
    later = has_tiles[None, :] & (e_ids[None, :] > e_ids[:, None])
    next_of_e = jnp.min(jnp.where(later, e_ids[None, :], N_EXPERTS), axis=1)
    next_of_e = jnp.where(next_of_e < N_EXPERTS, next_of_e, -1)
    runs_before_e = jnp.sum((has_tiles[None, :] & (e_ids[None, :] < e_ids[:, None])).astype(jnp.int32), axis=1)
    next_expert = jnp.sum(jnp.where(mine, next_of_e, 0), axis=1).astype(jnp.int32)
    run_parity = jnp.sum(jnp.where(mine, runs_before_e % 2, 0), axis=1).astype(jnp.int32)
    eidx = info[0:TOP_K].astype(jnp.int32)
    row_start = jnp.sum(jnp.where(eidx[None] == e_ids[:, None, None], (tile_start * tm_rows)[:, None, None], 0),
                        axis=0)
    pos = (row_start + info[4:4 + TOP_K].astype(jnp.int32)).reshape(-1)

    xs = _sc_scatter_rows(hn_packed, pos, p)
    ys = _moe_grouped(tile_expert, n_valid, rows_valid, next_expert, run_parity, xs, wg, wu, wd, tm_rows)
    back = _sc_gather_rows(ys, pos)
    return _moe_combine(x2, route, back, min(TM_COMBINE, n))


def _block_diag(n, width):
    idx = np.arange(n) // width
    return jnp.asarray((idx[:, None] == idx[None, :]).astype(np.float32), dtype=BF16)


def _kv_feature_major(cache):
    nb, wb = cache.shape[:2]
    return jnp.transpose(cache, (0, 2, 3, 1)).reshape(nb, KV_DIM, wb)


def _strict_lower(n):
    r = np.arange(n)
    return jnp.asarray((r[:, None] > r[None, :]).astype(np.float32), dtype=BF16)


def _alibi_slopes():
    return np.exp2(-8.0 * np.arange(1, N_HEADS + 1, dtype=np.float32) / N_HEADS).astype(np.float32)


def _prompt_bias():
    i = np.arange(BLOCK)[:, None]
    s = np.arange(2 * BLOCK)[None, :]
    dist = (i + BLOCK - s).astype(np.float32)
    mask = (dist >= 0) & (dist < WINDOW)
    first = mask & (s >= BLOCK)
    slopes = _alibi_slopes()[:, None, None]
    reg = np.where(mask[None], -slopes * dist[None], np.float32(NEG_INF))
    fst = np.where(first[None], -slopes * dist[None], np.float32(NEG_INF))
    return jnp.asarray(np.concatenate([reg, fst], axis=0).astype(np.float32))


def _sample_bias(ts, wb):
    i = np.arange(ts)[:, None]
    s = np.arange(wb + ts)[None, :]
    dist = (i + wb - s).astype(np.float32)
    mask = (dist >= 0) & (dist < WINDOW)
    slopes = _alibi_slopes()[:, None, None]
    b = np.where(mask[None], -slopes * dist[None], np.float32(NEG_INF)).astype(np.float32)
    return jnp.asarray(b.reshape(N_HEADS * ts, wb + ts))


def kernel(x_prompt, x_sample, cache_swa_k, cache_swa_v, state_conv, cache_mem_k, cache_mem_v, mem_prompt, norm_mix_g, w_in, q_norm_g, k_norm_g, attn_sinks, conv_dw_w, conv_dw_b, conv_ln_g, conv_ln_b, w_out, norm_xa_g, norm_mem_g, w_mq, w_mk, w_mv, mq_norm_g, mk_norm_g, w_mo, norm_ffn_g, w_router_group, b_router_group, w_router_expert, b_router_expert, w_exp_gate, w_exp_up, w_exp_down):
    depth = w_in.shape[0]
    bp, sp, d = x_prompt.shape
    nb, ts, _ = x_sample.shape
    wb = cache_swa_k.shape[2]
    mlen = mem_prompt.shape[1]
    assert d == D_MODEL and wb == WINDOW and sp % TM_PROMPT == 0 and nb % SEQ_PER_STEP == 0

    bdq = _block_diag(MXU_TILE, HEAD_DIM)
    bdk = _block_diag(KV_DIM, HEAD_DIM)
    bias_p = _prompt_bias()
    bias_s = _sample_bias(ts, wb)
    lane_lo = (np.arange(D_ATTN) % LANES) < HALF
    row = lambda a: a.reshape(1, -1).astype(F32)

    xp = x_prompt
    xs = x_sample.reshape(nb * ts, d)
    kp_l, vp_l, cp_l, mkp_l, mvp_l, ks_l, vs_l, cs_l = [], [], [], [], [], [], [], []
    for l in range(depth):
        gq = jnp.tile(q_norm_g[l].astype(F32), N_HEADS) * (HEAD_DIM ** -0.5)
        gqlo = jnp.where(lane_lo, gq, 0.0).reshape(1, -1)
        gqhi = jnp.where(lane_lo, 0.0, gq).reshape(1, -1)
        gk = jnp.tile(k_norm_g[l].astype(F32), N_KV_HEADS).reshape(1, -1)
        sinks = attn_sinks[l].astype(F32)
        sinkrow = jnp.repeat(sinks, ts).reshape(N_HEADS * ts, 1)
        win = w_in[l].astype(BF16)
        wout = w_out[l].astype(BF16)
        wmq = w_mq[l].astype(BF16)
        wmo = w_mo[l].astype(BF16)
        gmq = (jnp.tile(mq_norm_g[l].astype(F32), MEM_HEADS) * (MEM_HEAD_DIM ** -0.5)).reshape(1, -1)
        gmk = jnp.tile(mk_norm_g[l].astype(F32), MEM_HEADS).reshape(1, -1)
        w_r = jnp.concatenate([w_router_expert[l], w_router_group[l],
                               jnp.zeros((d, LANES - N_EXPERTS - N_GROUPS), F32)], axis=1).astype(BF16)
        b_r = jnp.concatenate([b_router_expert[l], b_router_group[l],
                               jnp.zeros((LANES - N_EXPERTS - N_GROUPS,), F32)]).reshape(1, -1).astype(F32)
        wg, wu, wd = w_exp_gate[l], w_exp_up[l], w_exp_down[l]

        mix_consts = (row(norm_mix_g[l]), win, bdq, bdk, gqlo, gqhi, gk)
        conv_consts = (conv_dw_w[l].astype(F32), row(conv_dw_b[l]), row(conv_ln_g[l]), row(conv_ln_b[l]), wout)
        tail_consts = (wmo, row(norm_ffn_g[l]), w_r, b_r)

        x1p, kp, vp, cp = _mixer_prompt(xp, sinks, mix_consts + (bias_p,) + conv_consts, TM_PROMPT)
        mk, mv = _memory_kv(mem_prompt.reshape(bp * mlen, d), row(norm_mem_g[l]),
                            w_mk[l].astype(BF16), w_mv[l].astype(BF16), gmk, min(256, bp * mlen))
        mk = mk.reshape(bp, mlen, d)
        mv = mv.reshape(bp, mlen, d)
        x2p, hnp, routep, infop, cntp = _cross_prompt(x1p, mk, mv, (row(norm_xa_g[l]), wmq, gmq) + tail_consts
                                               + (_strict_lower(TM_PROMPT),), TM_PROMPT)
        xp = _moe(x2p.reshape(bp * sp, d), hnp.reshape(bp * sp, d // 2), routep.reshape(bp * sp, LANES), infop, cntp,
                  wg, wu, wd, TM_ROWS_PROMPT).reshape(bp, sp, d)
        kp_l.append(kp.reshape(bp, BLOCK, N_KV_HEADS, HEAD_DIM))
        vp_l.append(vp.reshape(bp, BLOCK, N_KV_HEADS, HEAD_DIM))
        cp_l.append(cp)
        mkp_l.append(mk.reshape(bp, mlen, MEM_HEADS, MEM_HEAD_DIM))
        mvp_l.append(mv.reshape(bp, mlen, MEM_HEADS, MEM_HEAD_DIM))

        x1s, qm, ksn, vsn, csn = _mixer_sample(
            xs, _kv_feature_major(cache_swa_k[l]), _kv_feature_major(cache_swa_v[l]),
            jnp.transpose(state_conv[l], (1, 0, 2)),
            mix_consts + (bias_s, sinkrow) + conv_consts + (row(norm_xa_g[l]), wmq, gmq), SEQ_PER_STEP, ts)
        o_s = _mem_attend_sample(qm, _mem_cache_rows(cache_mem_k[l]), _mem_cache_rows(cache_mem_v[l]),
                                 MEM_SEQ_PER_STEP, ts)
        tms = min(256, nb * ts)
        x2s, hns, routes, infos, cnts = _cross_tail_call(o_s, x1s, tail_consts + (_strict_lower(tms),), tms)
        xs = _moe(x2s, hns, routes, infos, cnts, wg, wu, wd, TM_ROWS_SAMPLE)
        ks_l.append(jnp.transpose(ksn.reshape(nb, N_KV_HEADS, HEAD_DIM, wb), (0, 3, 1, 2)))
        vs_l.append(jnp.transpose(vsn.reshape(nb, N_KV_HEADS, HEAD_DIM, wb), (0, 3, 1, 2)))
        cs_l.append(jnp.transpose(csn, (1, 0, 2)))

    st = lambda xs_: jnp.stack(xs_, axis=0)
    return (xp, xs.reshape(nb, ts, d), st(kp_l), st(vp_l), st(cp_l), st(mkp_l), st(mvp_l),
            st(ks_l), st(vs_l), st(cs_l))
```

```python
import functools

import numpy as np
import jax
import jax.numpy as jnp
from jax import lax
from jax.experimental import pallas as pl
from jax.experimental.pallas import tpu as pltpu
from jax.experimental.pallas import tpu_sc as plsc

F32 = jnp.float32
BF16 = jnp.bfloat16

D_MODEL = 1024
D_ATTN = 512
D_CONV = 512
HEAD_DIM = 64
N_HEADS = 8
N_KV_HEADS = 2
KV_DIM = N_KV_HEADS * HEAD_DIM
HEADS_PER_KV = N_HEADS // N_KV_HEADS
WINDOW = 128
BLOCK = 128
CONV_WIDTH = 31
CONV_PAST = CONV_WIDTH - 1
MEM_HEADS = 4
MEM_HEAD_DIM = 256
N_GROUPS = 4
EXPERTS_PER_GROUP = 8
N_EXPERTS = 32
D_EXPERT = 256
D_IN = D_ATTN + 2 * KV_DIM + 2 * D_CONV
EPS = 1e-6
NEG_INF = -1e30

LANES = 128
HALF = LANES // 2
MXU_TILE = 256
SUBLANES = 8
CONV_PAD = 32
CONV_ROWS = 64
VMEM_LIMIT = 56 * 1024 * 1024

TM_PROMPT = 512
TM_CROSS = 1024
SEQ_PER_STEP = 32
MEM_SEQ_PER_STEP = 8
TOP_K = 2
ROUTE_ROWS = 8
SC_CORES = 2
SC_WORKERS = 32
SC_ROWS_PER_STEP = 64
TM_ROWS_PROMPT = 512
TM_ROWS_SAMPLE = 128
TM_COMBINE = 1024


def _rms(x, g):
    ms = jnp.mean(x * x, axis=-1, keepdims=True)
    return x * lax.rsqrt(ms + EPS) * g


def _group_mean_sq(x, bd_ref, width):
    x2 = x * x
    hi = x2.astype(BF16)
    lo = (x2 - hi.astype(F32)).astype(BF16)
    bd = bd_ref[...]
    w = bd.shape[0]
    parts = [jnp.dot(hi[:, c:c + w], bd, preferred_element_type=F32)
             + jnp.dot(lo[:, c:c + w], bd, preferred_element_type=F32) for c in range(0, x.shape[1], w)]
    return jnp.concatenate(parts, axis=1) * (1.0 / width)


def _mixer_proj(x, ng_ref, win_ref, bdq_ref, bdk_ref, gqlo_ref, gqhi_ref, gk_ref):
    h = _rms(x, ng_ref[...]).astype(BF16)
    p = jnp.dot(h, win_ref[...], preferred_element_type=F32)
    q = p[:, :D_ATTN]
    k = p[:, D_ATTN:D_ATTN + KV_DIM]
    v = p[:, D_ATTN + KV_DIM:D_ATTN + 2 * KV_DIM]
    ua = p[:, D_ATTN + 2 * KV_DIM:D_ATTN + 2 * KV_DIM + D_CONV]
    ub = p[:, D_ATTN + 2 * KV_DIM + D_CONV:]
    qn = q * lax.rsqrt(_group_mean_sq(q, bdq_ref, HEAD_DIM) + EPS)
    q_lo = qn * gqlo_ref[...]
    q_hi = qn * gqhi_ref[...]
    kn = k * lax.rsqrt(_group_mean_sq(k, bdk_ref, HEAD_DIM) + EPS) * gk_ref[...]
    u = ua * jax.nn.sigmoid(ub)
    return q_lo, q_hi, kn, v, u


def _dup_halves(x):
    lo = lax.broadcasted_iota(jnp.int32, x.shape, 1) < HALF
    xr = pltpu.roll(x, HALF, axis=1)
    return jnp.where(lo, x, xr), jnp.where(lo, xr, x)


def _sink_softmax(s, sink):
    m = jnp.maximum(jnp.max(s, axis=-1, keepdims=True), sink)
    p = jnp.exp(s - m)
    denom = jnp.sum(p, axis=-1, keepdims=True) + jnp.exp(sink - m)
    return p * (1.0 / denom)


def _conv_ln_silu(y, cb_ref, lg_ref, lb_ref):
    y = y + cb_ref[...]
    mu = jnp.mean(y, axis=-1, keepdims=True)
    yc = y - mu
    yn = yc * lax.rsqrt(jnp.mean(yc * yc, axis=-1, keepdims=True) + EPS)
    z = yn * lg_ref[...] + lb_ref[...]
    return z * jax.nn.sigmoid(z)


def _mem_query(x1, xag_ref, wmq_ref, gmq_ref):
    h = _rms(x1, xag_ref[...]).astype(BF16)
    q = jnp.dot(h, wmq_ref[...], preferred_element_type=F32)
    parts = []
    for hd in range(MEM_HEADS):
        qh = q[:, hd * MEM_HEAD_DIM:(hd + 1) * MEM_HEAD_DIM]
        parts.append(qh * lax.rsqrt(jnp.mean(qh * qh, axis=-1, keepdims=True) + EPS))
    return jnp.concatenate(parts, axis=1) * gmq_ref[...]


def _mem_attend(q, k_heads, v_heads):
    outs = []
    for hd in range(MEM_HEADS):
        sl = slice(hd * MEM_HEAD_DIM, (hd + 1) * MEM_HEAD_DIM)
        s = lax.dot_general(q[:, sl].astype(BF16), k_heads[hd], (((1,), (1,)), ((), ())),
                            preferred_element_type=F32)
        m = jnp.max(s, axis=-1, keepdims=True)
        p = jnp.exp(s - m)
        p = p * (1.0 / jnp.sum(p, axis=-1, keepdims=True))
        outs.append(jnp.dot(p.astype(BF16), v_heads[hd], preferred_element_type=F32))
    return jnp.concatenate(outs, axis=1)


def _split_heads(x):
    return [x[:, hd * MEM_HEAD_DIM:(hd + 1) * MEM_HEAD_DIM] for hd in range(MEM_HEADS)]


def _pack_bf16_pairs(y):
    n = y.shape[1] // 2
    bits = pltpu.bitcast(y.astype(BF16).astype(F32), jnp.int32)
    return (bits[:, :n] & jnp.int32(-65536)) | lax.shift_right_logical(bits[:, n:], jnp.int32(16))


def _unpack_bf16_pairs(w):
    hi = pltpu.bitcast(w & jnp.int32(-65536), F32)
    lo = pltpu.bitcast(lax.shift_left(w, jnp.int32(16)), F32)
    return jnp.concatenate([hi, lo], axis=1)


def _cross_tail(o, x1, wmo_ref, fg_ref, wr_ref, br_ref, tri_ref, base):
    x2 = x1 + jnp.dot(o.astype(BF16), wmo_ref[...], preferred_element_type=F32)
    hn_f = _rms(x2, fg_ref[...])
    hn = hn_f.astype(BF16)
    logits = jnp.dot(hn, wr_ref[...], preferred_element_type=F32) + br_ref[...]
    lane = lax.broadcasted_iota(jnp.int32, logits.shape, 1)
    is_g = (lane >= N_EXPERTS) & (lane < N_EXPERTS + N_GROUPS)
    lg = jnp.where(is_g, logits, NEG_INF)
    gmax = jnp.max(lg, axis=-1, keepdims=True)
    gsel = jnp.min(jnp.where(lg == gmax, lane, 2 * LANES), axis=-1, keepdims=True) - N_EXPERTS
    pg_sel = 1.0 / jnp.sum(jnp.exp(lg - gmax), axis=-1, keepdims=True)
    in_grp = (lane >= gsel * EXPERTS_PER_GROUP) & (lane < (gsel + 1) * EXPERTS_PER_GROUP)
    le = jnp.where(in_grp, logits, NEG_INF)
    top1 = jnp.max(le, axis=-1, keepdims=True)
    idx1 = jnp.min(jnp.where(le == top1, lane, 2 * LANES), axis=-1, keepdims=True)
    le2 = jnp.where(lane == idx1, NEG_INF, le)
    top2 = jnp.max(le2, axis=-1, keepdims=True)
    idx2 = jnp.min(jnp.where(le2 == top2, lane, 2 * LANES), axis=-1, keepdims=True)
    e2 = jnp.exp(top2 - top1)
    inv = 1.0 / (1.0 + e2)
    gate1 = pg_sel * inv
    gate2 = pg_sel * (e2 * inv)
    used = jnp.where((lane == idx1) | (lane == idx2), 1.0, 0.0)
    before = jnp.dot(tri_ref[...], used.astype(BF16), preferred_element_type=F32) + base
    rank1 = jnp.sum(jnp.where(lane == idx1, before, 0.0), axis=-1, keepdims=True)
    rank2 = jnp.sum(jnp.where(lane == idx2, before, 0.0), axis=-1, keepdims=True)
    route = jnp.zeros_like(logits)
    for pos, val in enumerate((idx1.astype(F32), idx2.astype(F32), gate1, gate2, rank1, rank2)):
        route = jnp.where(lane == pos, val, route)
    info = jnp.transpose(route)[0:ROUTE_ROWS, :]
    return x2, _pack_bf16_pairs(hn_f), route, info, base + jnp.sum(used, axis=0, keepdims=True)


def _mixer_prompt_kernel(sink_ref, x_ref, ng_ref, win_ref, bdq_ref, bdk_ref, gqlo_ref, gqhi_ref, gk_ref,
                         bias_ref, cw_ref, cb_ref, lg_ref, lb_ref, wout_ref,
                         x1_ref, ko_ref, vo_ref, co_ref,
                         kband, vband, uext, ushift, *, tm, nt):
    t = pl.program_id(1)

    @pl.when(t == 0)
    def _():
        kband[0:BLOCK, :] = jnp.zeros((BLOCK, KV_DIM), F32)
        vband[0:BLOCK, :] = jnp.zeros((BLOCK, KV_DIM), F32)
        uext[0:CONV_PAD, :] = jnp.zeros((CONV_PAD, D_CONV), F32)

    x = x_ref[...]
    q_lo, q_hi, kn, v, u = _mixer_proj(x, ng_ref, win_ref, bdq_ref, bdk_ref, gqlo_ref, gqhi_ref, gk_ref)
    kband[BLOCK:BLOCK + tm, :] = kn
    vband[BLOCK:BLOCK + tm, :] = v
    uext[CONV_PAD:CONV_PAD + tm, :] = u

    @pl.when(t == nt - 1)
    def _():
        ko_ref[...] = kn[tm - BLOCK:, :]
        vo_ref[...] = v[tm - BLOCK:, :]
        co_ref[...] = uext[pl.ds(CONV_PAD + tm - CONV_PAST, CONV_PAST), :]

    kd = [a.astype(BF16) for a in _dup_halves(kband[...])]
    vd = _dup_halves(vband[...])
    lo = lax.broadcasted_iota(jnp.int32, vd[0].shape, 1) < HALF
    v_lo = [jnp.where(lo, a, 0.0).astype(BF16) for a in vd]
    v_hi = [jnp.where(lo, 0.0, a).astype(BF16) for a in vd]
    q_lo = q_lo.astype(BF16)
    q_hi = q_hi.astype(BF16)
    attn_blocks = []
    for j in range(tm // BLOCK):
        rows = slice(j * BLOCK, (j + 1) * BLOCK)
        keys = slice(j * BLOCK, j * BLOCK + 2 * BLOCK)
        bias_base = jnp.where(t == 0, N_HEADS, 0) if j == 0 else 0
        tiles = []
        for c in range(N_KV_HEADS):
            q4 = jnp.concatenate(
                [(q_lo if (HEADS_PER_KV * c + a) % 2 == 0 else q_hi)[rows,
                  ((HEADS_PER_KV * c + a) // 2) * LANES:((HEADS_PER_KV * c + a) // 2 + 1) * LANES]
                 for a in range(HEADS_PER_KV)], axis=0)
            s_all = lax.dot_general(q4, kd[c][keys], (((1,), (1,)), ((), ())), preferred_element_type=F32)
            ps = []
            for a in range(HEADS_PER_KV):
                hd = HEADS_PER_KV * c + a
                s = s_all[a * BLOCK:(a + 1) * BLOCK] + bias_ref[bias_base + hd]
                ps.append(_sink_softmax(s, sink_ref[hd]).astype(BF16))
            vstack = jnp.concatenate([v_lo[c][keys], v_hi[c][keys]], axis=0)
            for i2 in range(HEADS_PER_KV // 2):
                pp = jnp.concatenate([ps[2 * i2], ps[2 * i2 + 1]], axis=1)
                tiles.append(jnp.dot(pp, vstack, preferred_element_type=F32))
        attn_blocks.append(jnp.concatenate(tiles, axis=1))
    attn = jnp.concatenate(attn_blocks, axis=0)

    off = CONV_PAD - CONV_PAST
    span = tm + CONV_PAD - SUBLANES
    for s in range(1, SUBLANES):
        ushift[s - 1] = uext[pl.ds(s, span), :]
    chunks = []
    for lc in range(D_CONV // LANES):
        ls = slice(lc * LANES, (lc + 1) * LANES)
        accs = [jnp.zeros((CONV_ROWS, LANES), F32) for _ in range(tm // CONV_ROWS)]
        for j in range(CONV_WIDTH):
            s, a = (off + j) % SUBLANES, (off + j) // SUBLANES
            wj = cw_ref[j:j + 1, ls]
            for rc in range(tm // CONV_ROWS):
                r0 = rc * CONV_ROWS + a * SUBLANES
                tap = uext[r0:r0 + CONV_ROWS, ls] if s == 0 else ushift[s - 1, r0:r0 + CONV_ROWS, ls]
                accs[rc] = accs[rc] + wj * tap
        chunks.append(jnp.concatenate(accs, axis=0))
    y = _conv_ln_silu(jnp.concatenate(chunks, axis=1), cb_ref, lg_ref, lb_ref)

    x1_ref[...] = (x + jnp.dot(attn.astype(BF16), wout_ref[0:D_ATTN, :], preferred_element_type=F32)
                   + jnp.dot(y.astype(BF16), wout_ref[D_ATTN:, :], preferred_element_type=F32))

    kband[0:BLOCK, :] = kband[tm:tm + BLOCK, :]
    vband[0:BLOCK, :] = vband[tm:tm + BLOCK, :]
    uext[0:CONV_PAD, :] = uext[tm:tm + CONV_PAD, :]


def _const_spec(shape):
    nd = len(shape)
    return pl.BlockSpec(shape, lambda *_: (0,) * nd)


def _mixer_prompt(x, sinks, consts, tm):
    b, s, d = x.shape
    nt = s // tm
    (ng, win, bdq, bdk, gqlo, gqhi, gk, bias_p, cw, cb, lg, lb, wout) = consts
    kern = functools.partial(_mixer_prompt_kernel, tm=tm, nt=nt)
    cspecs = [_const_spec(a.shape) for a in consts]
    return pl.pallas_call(
        kern,
        grid=(b, nt),
        in_specs=[pl.BlockSpec(memory_space=pltpu.SMEM),
                  pl.BlockSpec((None, tm, d), lambda i, j: (i, j, 0))] + cspecs,
        out_specs=[pl.BlockSpec((None, tm, d), lambda i, j: (i, j, 0)),
                   pl.BlockSpec((None, BLOCK, KV_DIM), lambda i, j: (i, 0, 0)),
                   pl.BlockSpec((None, BLOCK, KV_DIM), lambda i, j: (i, 0, 0)),
                   pl.BlockSpec((None, CONV_PAST, D_CONV), lambda i, j: (i, 0, 0))],
        out_shape=[jax.ShapeDtypeStruct((b, s, d), F32),
                   jax.ShapeDtypeStruct((b, BLOCK, KV_DIM), F32),
                   jax.ShapeDtypeStruct((b, BLOCK, KV_DIM), F32),
                   jax.ShapeDtypeStruct((b, CONV_PAST, D_CONV), F32)],
        scratch_shapes=[pltpu.VMEM((BLOCK + tm, KV_DIM), F32),
                        pltpu.VMEM((BLOCK + tm, KV_DIM), F32),
                        pltpu.VMEM((CONV_PAD + tm, D_CONV), F32),
                        pltpu.VMEM((SUBLANES - 1, tm + CONV_PAD - SUBLANES, D_CONV), F32)],
        compiler_params=pltpu.CompilerParams(dimension_semantics=("arbitrary", "arbitrary"),
                                             vmem_limit_bytes=VMEM_LIMIT),
        name="mixer_prompt",
    )(sinks, x, *consts)


def _mixer_sample_kernel(x_ref, ck_ref, cv_ref, st_ref, ng_ref, win_ref, bdq_ref, bdk_ref, gqlo_ref, gqhi_ref,
                         gk_ref, bias_ref, sinkrow_ref, cw_ref, cb_ref, lg_ref, lb_ref, wout_ref,
                         xag_ref, wmq_ref, gmq_ref,
                         x1_ref, qm_ref, ko_ref, vo_ref, co_ref,
                         kall, vall, qs, s_scr, r_scr, u_scr, y_scr, *, g, ts):
    wb = ck_ref.shape[2]
    x = x_ref[...]
    q_lo, q_hi, kn, v, u = _mixer_proj(x, ng_ref, win_ref, bdq_ref, bdk_ref, gqlo_ref, gqhi_ref, gk_ref)
    kall[:, wb:wb + ts, :] = kn.reshape(g, ts, KV_DIM)
    vall[:, wb:wb + ts, :] = v.reshape(g, ts, KV_DIM)
    for n in range(g):
        kall[n, 0:wb, :] = ck_ref[n].T
        vall[n, 0:wb, :] = cv_ref[n].T
        ko_ref[n] = kall[n, ts:wb + ts, :].T
        vo_ref[n] = vall[n, ts:wb + ts, :].T

    for lc in range(D_CONV // LANES):
        ls = slice(lc * LANES, (lc + 1) * LANES)
        u_scr[lc] = u[:, ls]
        u_t = [u_scr[lc, pl.ds(t, g, stride=ts), :] for t in range(ts)]

        def frame(r):
            return st_ref[r, :, ls] if r < CONV_PAST else u_t[r - CONV_PAST]

        for t in range(ts):
            acc = jnp.zeros((g, LANES), F32)
            for j in range(CONV_WIDTH):
                acc = acc + cw_ref[j:j + 1, ls] * frame(t + j)
            y_scr[lc, pl.ds(t, g, stride=ts), :] = acc
        for r in range(CONV_PAST):
            co_ref[r, :, ls] = frame(r + ts)
    y = _conv_ln_silu(jnp.concatenate([y_scr[lc] for lc in range(D_CONV // LANES)], axis=1), cb_ref, lg_ref, lb_ref)

    for hd in range(N_HEADS):
        tile = (q_lo if hd % 2 == 0 else q_hi)[:, (hd // 2) * LANES:(hd // 2 + 1) * LANES]
        if hd % 2 != hd // HEADS_PER_KV:
            tile = pltpu.roll(tile, HALF, axis=1)
        qs[:, hd * ts:(hd + 1) * ts, :] = tile.reshape(g, ts, LANES)
    for n in range(g):
        s_scr[n] = lax.dot_general(qs[n].astype(BF16), kall[n].astype(BF16), (((1,), (1,)), ((), ())),
                                   preferred_element_type=F32)
    s_scr[...] = _sink_softmax(s_scr[...] + bias_ref[...], sinkrow_ref[...])
    for n in range(g):
        r_scr[n] = jnp.dot(s_scr[n].astype(BF16), vall[n].astype(BF16), preferred_element_type=F32)
    r = r_scr[...].reshape(g * N_HEADS * ts, LANES)
    r_sw = pltpu.roll(r, HALF, axis=1).reshape(g, N_HEADS * ts, LANES)
    r = r.reshape(g, N_HEADS * ts, LANES)
    lo = lax.broadcasted_iota(jnp.int32, (g * ts, LANES), 1) < HALF
    heads = [(r if hd // HEADS_PER_KV == hd % 2 else r_sw)[:, hd * ts:(hd + 1) * ts, :].reshape(g * ts, LANES)
             for hd in range(N_HEADS)]
    attn = jnp.concatenate([jnp.where(lo, heads[2 * i], heads[2 * i + 1]) for i in range(N_HEADS // 2)], axis=1)

    x1 = (x + jnp.dot(attn.astype(BF16), wout_ref[0:D_ATTN, :], preferred_element_type=F32)
          + jnp.dot(y.astype(BF16), wout_ref[D_ATTN:, :], preferred_element_type=F32))
    x1_ref[...] = x1
    qm_ref[...] = _mem_query(x1, xag_ref, wmq_ref, gmq_ref)


def _mixer_sample(xs2d, ck, cv, st, consts, g, ts):
    n, d = xs2d.shape
    nb, wb = ck.shape[0], ck.shape[2]
    rows = g * ts
    cspecs = [_const_spec(a.shape) for a in consts]
    kern = functools.partial(_mixer_sample_kernel, g=g, ts=ts)
    return pl.pallas_call(
        kern,
        grid=(nb // g,),
        in_specs=[pl.BlockSpec((rows, d), lambda i: (i, 0)),
                  pl.BlockSpec((g, KV_DIM, wb), lambda i: (i, 0, 0)),
                  pl.BlockSpec((g, KV_DIM, wb), lambda i: (i, 0, 0)),
                  pl.BlockSpec((CONV_PAST, g, D_CONV), lambda i: (0, i, 0))] + cspecs,
        out_specs=[pl.BlockSpec((rows, d), lambda i: (i, 0)),
                   pl.BlockSpec((rows, d), lambda i: (i, 0)),
                   pl.BlockSpec((g, KV_DIM, wb), lambda i: (i, 0, 0)),
                   pl.BlockSpec((g, KV_DIM, wb), lambda i: (i, 0, 0)),
                   pl.BlockSpec((CONV_PAST, g, D_CONV), lambda i: (0, i, 0))],
        out_shape=[jax.ShapeDtypeStruct((n, d), F32),
                   jax.ShapeDtypeStruct((n, d), F32),
                   jax.ShapeDtypeStruct((nb, KV_DIM, wb), F32),
                   jax.ShapeDtypeStruct((nb, KV_DIM, wb), F32),
                   jax.ShapeDtypeStruct((CONV_PAST, nb, D_CONV), F32)],
        scratch_shapes=[pltpu.VMEM((g, wb + ts, KV_DIM), F32),
                        pltpu.VMEM((g, wb + ts, KV_DIM), F32),
                        pltpu.VMEM((g, N_HEADS * ts, LANES), F32),
                        pltpu.VMEM((g, N_HEADS * ts, wb + ts), F32),
                        pltpu.VMEM((g, N_HEADS * ts, LANES), F32),
                        pltpu.VMEM((D_CONV // LANES, rows, LANES), F32),
                        pltpu.VMEM((D_CONV // LANES, rows, LANES), F32)],
        compiler_params=pltpu.CompilerParams(dimension_semantics=("arbitrary",),
                                             vmem_limit_bytes=VMEM_LIMIT),
        name="mixer_sample",
    )(xs2d, ck, cv, st, *consts)


def _memory_kv_kernel(mem_ref, g_ref, wmk_ref, wmv_ref, gk_ref, k_ref, v_ref):
    h = _rms(mem_ref[...], g_ref[...]).astype(BF16)
    k = jnp.dot(h, wmk_ref[...], preferred_element_type=F32)
    parts = []
    for hd in range(MEM_HEADS):
        kh = k[:, hd * MEM_HEAD_DIM:(hd + 1) * MEM_HEAD_DIM]
        parts.append(kh * lax.rsqrt(jnp.mean(kh * kh, axis=-1, keepdims=True) + EPS))
    k_ref[...] = jnp.concatenate(parts, axis=1) * gk_ref[...]
    v_ref[...] = jnp.dot(h, wmv_ref[...], preferred_element_type=F32)


def _memory_kv(mem2d, g, wmk, wmv, gk, tm):
    n, d = mem2d.shape
    consts = (g, wmk, wmv, gk)
    return pl.pallas_call(
        _memory_kv_kernel,
        grid=(n // tm,),
        in_specs=[pl.BlockSpec((tm, d), lambda i: (i, 0))] + [_const_spec(a.shape) for a in consts],
        out_specs=[pl.BlockSpec((tm, d), lambda i: (i, 0)), pl.BlockSpec((tm, d), lambda i: (i, 0))],
        out_shape=[jax.ShapeDtypeStruct((n, d), F32), jax.ShapeDtypeStruct((n, d), F32)],
        compiler_params=pltpu.CompilerParams(dimension_semantics=("arbitrary",),
                                             vmem_limit_bytes=VMEM_LIMIT),
        name="memory_kv",
    )(mem2d, *consts)


def _cross_prompt_kernel(x1_ref, mk_ref, mv_ref, xag_ref, wmq_ref, gmq_ref, wmo_ref, fg_ref, wr_ref, br_ref, tri_ref,
                         x2_ref, hn_ref, route_ref, info_ref, cnt_ref, base):
    @pl.when((pl.program_id(0) == 0) & (pl.program_id(1) == 0))
    def _():
        base[...] = jnp.zeros_like(base)

    x1 = x1_ref[...]
    q = _mem_query(x1, xag_ref, wmq_ref, gmq_ref)
    o = _mem_attend(q, _split_heads(mk_ref[...].astype(BF16)), _split_heads(mv_ref[...].astype(BF16)))
    x2, hn, route, info, new_base = _cross_tail(o, x1, wmo_ref, fg_ref, wr_ref, br_ref, tri_ref, base[...])
    x2_ref[...] = x2
    hn_ref[...] = hn
    route_ref[...] = route
    info_ref[...] = info
    base[...] = new_base
    cnt_ref[...] = new_base


def _cross_prompt(x1, mk, mv, consts, tm):
    b, s, d = x1.shape
    m = mk.shape[1]
    cspecs = [_const_spec(a.shape) for a in consts]
    return pl.pallas_call(
        _cross_prompt_kernel,
        grid=(b, s // tm),
        in_specs=[pl.BlockSpec((None, tm, d), lambda i, j: (i, j, 0)),
                  pl.BlockSpec((None, m, d), lambda i, j: (i, 0, 0)),
                  pl.BlockSpec((None, m, d), lambda i, j: (i, 0, 0))] + cspecs,
        out_specs=[pl.BlockSpec((None, tm, d), lambda i, j: (i, j, 0)),
                   pl.BlockSpec((None, tm, d // 2), lambda i, j: (i, j, 0)),
                   pl.BlockSpec((None, tm, LANES), lambda i, j: (i, j, 0)),
                   pl.BlockSpec((ROUTE_ROWS, tm), lambda i, j: (0, i * (s // tm) + j)),
                   pl.BlockSpec((1, LANES), lambda i, j: (0, 0))],
        out_shape=[jax.ShapeDtypeStruct((b, s, d), F32),
                   jax.ShapeDtypeStruct((b, s, d // 2), jnp.int32),
                   jax.ShapeDtypeStruct((b, s, LANES), F32),
                   jax.ShapeDtypeStruct((ROUTE_ROWS, b * s), F32),
                   jax.ShapeDtypeStruct((1, LANES), F32)],
        scratch_shapes=[pltpu.VMEM((1, LANES), F32)],
        compiler_params=pltpu.CompilerParams(dimension_semantics=("arbitrary", "arbitrary"),
                                             vmem_limit_bytes=VMEM_LIMIT),
        name="cross_prompt",
    )(x1, mk, mv, *consts)


def _mem_attend_sample_kernel(qm_ref, *refs, g, ts):
    k_refs, v_refs, o_ref = refs[:g], refs[g:2 * g], refs[2 * g]
    halves = MEM_HEAD_DIM // LANES
    rows_per_pos = MEM_HEADS * halves
    mlen = k_refs[0].shape[0] // rows_per_pos

    def all_heads(ref):
        return jnp.concatenate([ref[pl.ds(c * MEM_HEADS + hd, mlen, stride=rows_per_pos), :]
                                for hd in range(MEM_HEADS) for c in range(halves)], axis=1).astype(BF16)

    rows = MEM_HEADS * ts
    own = (lax.broadcasted_iota(jnp.int32, (rows, MEM_HEADS * MEM_HEAD_DIM), 1) // MEM_HEAD_DIM
           == lax.broadcasted_iota(jnp.int32, (rows, MEM_HEADS * MEM_HEAD_DIM), 0) // ts)
    for n in range(g):
        q = qm_ref[n * ts:(n + 1) * ts, :]
        q_bd = jnp.where(own, jnp.concatenate([q] * MEM_HEADS, axis=0), 0.0).astype(BF16)
        s = lax.dot_general(q_bd, all_heads(k_refs[n]), (((1,), (1,)), ((), ())), preferred_element_type=F32)
        m = jnp.max(s, axis=-1, keepdims=True)
        p = jnp.exp(s - m)
        p = p * (1.0 / jnp.sum(p, axis=-1, keepdims=True))
        r = jnp.dot(p.astype(BF16), all_heads(v_refs[n]), preferred_element_type=F32)
        o_ref[n * ts:(n + 1) * ts, :] = jnp.concatenate(
            [r[hd * ts:(hd + 1) * ts, hd * MEM_HEAD_DIM:(hd + 1) * MEM_HEAD_DIM] for hd in range(MEM_HEADS)], axis=1)


def _mem_cache_rows(cache):
    nb, mlen = cache.shape[:2]
    halves = MEM_HEAD_DIM // LANES
    return (cache.reshape(nb, mlen, MEM_HEADS, halves, LANES).transpose(0, 1, 3, 2, 4)
            .reshape(nb, mlen * halves * MEM_HEADS, LANES))


def _mem_attend_sample(qm, mk, mv, g, ts):
    n, d = qm.shape
    nb, rows = mk.shape[0], mk.shape[1]
    kern = functools.partial(_mem_attend_sample_kernel, g=g, ts=ts)

    def seq_spec(j):
        return pl.BlockSpec((None, rows, LANES), lambda i: (g * i + j, 0, 0))

    return pl.pallas_call(
        kern,
        grid=(nb // g,),
        in_specs=[pl.BlockSpec((g * ts, d), lambda i: (i, 0))] + [seq_spec(j) for j in range(g)] * 2,
        out_specs=pl.BlockSpec((g * ts, d), lambda i: (i, 0)),
        out_shape=jax.ShapeDtypeStruct((n, d), F32),
        compiler_params=pltpu.CompilerParams(dimension_semantics=("arbitrary",),
                                             vmem_limit_bytes=VMEM_LIMIT),
        name="mem_attend_sample",
    )(qm, *([mk] * g), *([mv] * g))


def _cross_tail_kernel(o_ref, x1_ref, wmo_ref, fg_ref, wr_ref, br_ref, tri_ref,
                       x2_ref, hn_ref, route_ref, info_ref, cnt_ref, base):
    @pl.when(pl.program_id(0) == 0)
    def _():
        base[...] = jnp.zeros_like(base)

    x2, hn, route, info, new_base = _cross_tail(o_ref[...], x1_ref[...], wmo_ref, fg_ref, wr_ref, br_ref, tri_ref,
                                                base[...])
    x2_ref[...] = x2
    hn_ref[...] = hn
    route_ref[...] = route
    info_ref[...] = info
    base[...] = new_base
    cnt_ref[...] = new_base


def _cross_tail_call(o, x1, consts, tm):
    n, d = x1.shape
    cspecs = [_const_spec(a.shape) for a in consts]
    return pl.pallas_call(
        _cross_tail_kernel,
        grid=(n // tm,),
        in_specs=[pl.BlockSpec((tm, d), lambda i: (i, 0)), pl.BlockSpec((tm, d), lambda i: (i, 0))] + cspecs,
        out_specs=[pl.BlockSpec((tm, d), lambda i: (i, 0)),
                   pl.BlockSpec((tm, d // 2), lambda i: (i, 0)),
                   pl.BlockSpec((tm, LANES), lambda i: (i, 0)),
                   pl.BlockSpec((ROUTE_ROWS, tm), lambda i: (0, i)),
                   pl.BlockSpec((1, LANES), lambda i: (0, 0))],
        out_shape=[jax.ShapeDtypeStruct((n, d), F32),
                   jax.ShapeDtypeStruct((n, d // 2), jnp.int32),
                   jax.ShapeDtypeStruct((n, LANES), F32),
                   jax.ShapeDtypeStruct((ROUTE_ROWS, n), F32),
                   jax.ShapeDtypeStruct((1, LANES), F32)],
        scratch_shapes=[pltpu.VMEM((1, LANES), F32)],
        compiler_params=pltpu.CompilerParams(dimension_semantics=("arbitrary",),
                                             vmem_limit_bytes=VMEM_LIMIT),
        name="cross_tail",
    )(o, x1, *consts)


def _sc_rows_per_step(per_worker):
    step = min(SC_ROWS_PER_STEP, per_worker)
    assert per_worker % step == 0 and step % 8 == 0
    return step


def _sc_gather_rows(table, idx):
    nrows = idx.shape[0]
    _, width = table.shape
    assert nrows % (8 * SC_WORKERS) == 0
    per_worker = nrows // SC_WORKERS
    step = _sc_rows_per_step(per_worker)
    mesh = plsc.VectorSubcoreMesh(core_axis_name="c", subcore_axis_name="s")

    @functools.partial(
        pl.kernel, mesh=mesh, out_type=jax.ShapeDtypeStruct((nrows, width), table.dtype),
        scratch_types=[pltpu.VMEM((step,), jnp.int32),
                       pltpu.VMEM((step, width), table.dtype),
                       pltpu.SemaphoreType.DMA])
    def gather(table_hbm, idx_hbm, out_hbm, idx_v, rows_v, sem):
        wid = lax.axis_index("s") * SC_CORES + lax.axis_index("c")
        base = wid * per_worker

        @pl.loop(0, per_worker // step)
        def _(i):
            off = base + i * step
            pltpu.sync_copy(idx_hbm.at[pl.ds(off, step)], idx_v)
            pltpu.async_copy(table_hbm.at[idx_v], rows_v, sem).wait()
            pltpu.sync_copy(rows_v, out_hbm.at[pl.ds(off, step)])

    return gather(table, idx)


def _sc_scatter_rows(table, pos, nrows_out):
    n, width = table.shape
    assert n % (8 * SC_WORKERS) == 0 and pos.shape == (TOP_K * n,)
    per_worker = n // SC_WORKERS
    step = _sc_rows_per_step(per_worker)
    mesh = plsc.VectorSubcoreMesh(core_axis_name="c", subcore_axis_name="s")

    @functools.partial(
        pl.kernel, mesh=mesh, out_type=jax.ShapeDtypeStruct((nrows_out, width), table.dtype),
        scratch_types=[pltpu.VMEM((step,), jnp.int32)] * TOP_K
        + [pltpu.VMEM((step, width), table.dtype), pltpu.SemaphoreType.DMA])
    def scatter(table_hbm, pos_hbm, out_hbm, *scratch):
        idx_vs, rows_v, sem = scratch[:TOP_K], scratch[TOP_K], scratch[TOP_K + 1]
        wid = lax.axis_index("s") * SC_CORES + lax.axis_index("c")
        base = wid * per_worker

        @pl.loop(0, per_worker // step)
        def _(i):
            off = base + i * step
            for k in range(TOP_K):
                pltpu.sync_copy(pos_hbm.at[pl.ds(k * n + off, step)], idx_vs[k])
            pltpu.sync_copy(table_hbm.at[pl.ds(off, step)], rows_v)
            for k in range(TOP_K):
                pltpu.async_copy(rows_v, out_hbm.at[idx_vs[k]], sem).wait()

    return scatter(table, pos)


W_CHUNKS = 4


def _expert_weight_copies(e, slot, w_hbm, w_slots, sems):
    copies = []
    for m, (src, dst) in enumerate(zip(w_hbm, w_slots)):
        rows = dst.shape[1] // W_CHUNKS
        for c in range(W_CHUNKS):
            sl = pl.ds(c * rows, rows)
            copies.append(pltpu.make_async_copy(src.at[e, sl, :], dst.at[slot, sl, :], sems.at[m * W_CHUNKS + c]))
    return copies


def _moe_grouped_kernel(te_ref, nv_ref, rv_ref, nxt_ref, par_ref, xs_ref, wg_hbm, wu_hbm, wd_hbm, ys_ref,
                        *rest, publish):
    i = pl.program_id(0)
    w_hbm = (wg_hbm, wu_hbm, wd_hbm)
    if publish:
        out_hbm, w_slots, w_bf16, sems, out_sems = rest[0:3], rest[3:6], rest[6:9], rest[9], rest[10]
    else:
        w_slots, sems = rest[0:3], rest[3]

    def publish_copies(e):
        return [pltpu.make_async_copy(src, dst.at[e], out_sems.at[m])
                for m, (src, dst) in enumerate(zip(w_bf16, out_hbm))]

    @pl.when(i == 0)
    def _():
        for cp in _expert_weight_copies(te_ref[0], par_ref[0], w_hbm, w_slots, sems):
            cp.start()

    @pl.when(i < nv_ref[0])
    def _():
        slot = par_ref[i]

        @pl.when((i == 0) | (te_ref[i] != te_ref[jnp.maximum(i - 1, 0)]))
        def _():
            for cp in _expert_weight_copies(te_ref[i], slot, w_hbm, w_slots, sems):
                cp.wait()
            if publish:
                @pl.when(i > 0)
                def _():
                    for cp in publish_copies(te_ref[jnp.maximum(i - 1, 0)]):
                        cp.wait()

                for src, dst in zip(w_slots, w_bf16):
                    dst[...] = src[slot].astype(BF16)
                for cp in publish_copies(te_ref[i]):
                    cp.start()

            @pl.when(nxt_ref[i] >= 0)
            def _():
                for cp in _expert_weight_copies(nxt_ref[i], 1 - slot, w_hbm, w_slots, sems):
                    cp.start()

        wg, wu, wd = [r[...] for r in w_bf16] if publish else [r[slot] for r in w_slots]
        xs = xs_ref[...]
        row = lax.broadcasted_iota(jnp.int32, xs.shape, 0)
        x = _unpack_bf16_pairs(jnp.where(row < rv_ref[i], xs, 0)).astype(BF16)
        a = jnp.dot(x, wg, preferred_element_type=F32)
        b = jnp.dot(x, wu, preferred_element_type=F32)
        act = (a * jax.nn.sigmoid(a)) * b
        y = jnp.dot(act.astype(BF16), wd, preferred_element_type=F32)
        ys_ref[...] = _pack_bf16_pairs(y)

    if publish:
        @pl.when(i == pl.num_programs(0) - 1)
        def _():
            for cp in publish_copies(te_ref[nv_ref[0] - 1]):
                cp.wait()


def _moe_grouped(tile_expert, n_valid, rows_valid, next_expert, run_parity, xs, wg, wu, wd, tm, publish):
    p, half = xs.shape
    ne, d, f = wg.shape

    def live_tile(i, te, nv, *_):
        return (jnp.minimum(i, nv[0] - 1), 0)

    slot_dtype = F32 if publish else BF16
    scratch = [pltpu.VMEM((2, d, f), slot_dtype), pltpu.VMEM((2, d, f), slot_dtype), pltpu.VMEM((2, f, d), slot_dtype)]
    out_specs = [pl.BlockSpec((tm, half), live_tile)]
    out_shape = [jax.ShapeDtypeStruct((p, half), jnp.int32)]
    if publish:
        scratch += [pltpu.VMEM((d, f), BF16), pltpu.VMEM((d, f), BF16), pltpu.VMEM((f, d), BF16)]
        out_specs += [pl.BlockSpec(memory_space=pl.ANY)] * 3
        out_shape += [jax.ShapeDtypeStruct(w.shape, BF16) for w in (wg, wu, wd)]
    scratch += [pltpu.SemaphoreType.DMA((3 * W_CHUNKS,))]
    if publish:
        scratch += [pltpu.SemaphoreType.DMA((3,))]
    grid_spec = pltpu.PrefetchScalarGridSpec(
        num_scalar_prefetch=5,
        grid=(p // tm,),
        in_specs=[pl.BlockSpec((tm, half), live_tile)] + [pl.BlockSpec(memory_space=pl.ANY)] * 3,
        out_specs=out_specs,
        scratch_shapes=scratch)
    res = pl.pallas_call(
        functools.partial(_moe_grouped_kernel, publish=publish),
        grid_spec=grid_spec,
        out_shape=out_shape,
        compiler_params=pltpu.CompilerParams(dimension_semantics=("arbitrary",),
                                             vmem_limit_bytes=VMEM_LIMIT),
        name="moe_grouped",
    )(tile_expert, n_valid, rows_valid, next_expert, run_parity, xs, wg, wu, wd)
    return res[0], tuple(res[1:])


def _moe_combine_kernel(x2_ref, route_ref, y0_ref, y1_ref, out_ref):
    route = route_ref[...]
    out_ref[...] = (x2_ref[...] + route[:, 2:3] * _unpack_bf16_pairs(y0_ref[...])
                    + route[:, 3:4] * _unpack_bf16_pairs(y1_ref[...]))


def _moe_combine(x2, route, gathered, tm):
    n, d = x2.shape
    nt = n // tm
    return pl.pallas_call(
        _moe_combine_kernel,
        grid=(nt,),
        in_specs=[pl.BlockSpec((tm, d), lambda i: (i, 0)),
                  pl.BlockSpec((tm, LANES), lambda i: (i, 0)),
                  pl.BlockSpec((tm, d // 2), lambda i: (i, 0)),
                  pl.BlockSpec((tm, d // 2), lambda i: (i + nt, 0))],
        out_specs=pl.BlockSpec((tm, d), lambda i: (i, 0)),
        out_shape=jax.ShapeDtypeStruct((n, d), F32),
        compiler_params=pltpu.CompilerParams(dimension_semantics=("arbitrary",),
                                             vmem_limit_bytes=VMEM_LIMIT),
        name="moe_combine",
    )(x2, route, gathered, gathered)


def _moe(x2, hn_packed, route, info, counts, weights, tm_rows, publish):
    n = x2.shape[0]
    p = (TOP_K * n // tm_rows + N_EXPERTS) * tm_rows
    n_tiles = p // tm_rows
    cnt = counts[0, :N_EXPERTS].astype(jnp.int32)
    tiles_per_e = jnp.maximum((cnt + tm_rows - 1) // tm_rows, 1 if publish else 0)
    e_ids = jnp.arange(N_EXPERTS, dtype=jnp.int32)
    tile_end = jnp.sum(jnp.where(e_ids[None, :] <= e_ids[:, None], tiles_per_e[None, :], 0), axis=1)
    tile_start = tile_end - tiles_per_e
    n_valid = tile_end[-1:]
    tile_ids = jnp.arange(n_tiles, dtype=jnp.int32)
    tile_expert = jnp.minimum(jnp.sum((tile_end[None, :] <= tile_ids[:, None]).astype(jnp.int32), axis=1),
                              N_EXPERTS - 1)
    mine = tile_expert[:, None] == e_ids[None, :]
    rows_left = jnp.sum(jnp.where(mine, cnt - (tile_ids[:, None] - tile_start) * tm_rows, 0), axis=1)
    rows_valid = jnp.clip(rows_left, 0, tm_rows).astype(jnp.int32)
    has_tiles = tiles_per_e > 0
    later = has_tiles[None, :] & (e_ids[None, :] > e_ids[:, None])
    next_of_e = jnp.min(jnp.where(later, e_ids[None, :], N_EXPERTS), axis=1)
    next_of_e = jnp.where(next_of_e < N_EXPERTS, next_of_e, -1)
    runs_before_e = jnp.sum((has_tiles[None, :] & (e_ids[None, :] < e_ids[:, None])).astype(jnp.int32), axis=1)
    next_expert = jnp.sum(jnp.where(mine, next_of_e, 0), axis=1).astype(jnp.int32)
    run_parity = jnp.sum(jnp.where(mine, runs_before_e % 2, 0), axis=1).astype(jnp.int32)
    eidx = info[0:TOP_K].astype(jnp.int32)
    row_start = jnp.sum(jnp.where(eidx[None] == e_ids[:, None, None], (tile_start * tm_rows)[:, None, None], 0),
                        axis=0)
    pos = (row_start + info[4:4 + TOP_K].astype(jnp.int32)).reshape(-1)

    xs = _sc_scatter_rows(hn_packed, pos, p)
    ys, w_bf16 = _moe_grouped(tile_expert, n_valid, rows_valid, next_expert, run_parity, xs, *weights, tm_rows,
                              publish)
    back = _sc_gather_rows(ys, pos)
    return _moe_combine(x2, route, back, min(TM_COMBINE, n)), w_bf16


def _block_diag(n, width):
    idx = np.arange(n) // width
    return jnp.asarray((idx[:, None] == idx[None, :]).astype(np.float32), dtype=BF16)


def _kv_feature_major(cache):
    nb, wb = cache.shape[:2]
    return jnp.transpose(cache, (0, 2, 3, 1)).reshape(nb, KV_DIM, wb)


def _strict_lower(n):
    r = np.arange(n)
    return jnp.asarray((r[:, None] > r[None, :]).astype(np.float32), dtype=BF16)


def _alibi_slopes():
    return np.exp2(-8.0 * np.arange(1, N_HEADS + 1, dtype=np.float32) / N_HEADS).astype(np.float32)


def _prompt_bias():
    i = np.arange(BLOCK)[:, None]
    s = np.arange(2 * BLOCK)[None, :]
    dist = (i + BLOCK - s).astype(np.float32)
    mask = (dist >= 0) & (dist < WINDOW)
    first = mask & (s >= BLOCK)
    slopes = _alibi_slopes()[:, None, None]
    reg = np.where(mask[None], -slopes * dist[None], np.float32(NEG_INF))
    fst = np.where(first[None], -slopes * dist[None], np.float32(NEG_INF))
    return jnp.asarray(np.concatenate([reg, fst], axis=0).astype(np.float32))


def _sample_bias(ts, wb):
    i = np.arange(ts)[:, None]
    s = np.arange(wb + ts)[None, :]
    dist = (i + wb - s).astype(np.float32)
    mask = (dist >= 0) & (dist < WINDOW)
    slopes = _alibi_slopes()[:, None, None]
    b = np.where(mask[None], -slopes * dist[None], np.float32(NEG_INF)).astype(np.float32)
    return jnp.asarray(b.reshape(N_HEADS * ts, wb + ts))


def kernel(x_prompt, x_sample, cache_swa_k, cache_swa_v, state_conv, cache_mem_k, cache_mem_v, mem_prompt, norm_mix_g, w_in, q_norm_g, k_norm_g, attn_sinks, conv_dw_w, conv_dw_b, conv_ln_g, conv_ln_b, w_out, norm_xa_g, norm_mem_g, w_mq, w_mk, w_mv, mq_norm_g, mk_norm_g, w_mo, norm_ffn_g, w_router_group, b_router_group, w_router_expert, b_router_expert, w_exp_gate, w_exp_up, w_exp_down):
    depth = w_in.shape[0]
    bp, sp, d = x_prompt.shape
    nb, ts, _ = x_sample.shape
    wb = cache_swa_k.shape[2]
    mlen = mem_prompt.shape[1]
    assert d == D_MODEL and wb == WINDOW and sp % TM_PROMPT == 0 and sp % TM_CROSS == 0 and nb % SEQ_PER_STEP == 0

    bdq = _block_diag(MXU_TILE, HEAD_DIM)
    bdk = _block_diag(KV_DIM, HEAD_DIM)
    bias_p = _prompt_bias()
    bias_s = _sample_bias(ts, wb)
    lane_lo = (np.arange(D_ATTN) % LANES) < HALF
    row = lambda a: a.reshape(1, -1).astype(F32)

    xp = x_prompt
    xs = x_sample.reshape(nb * ts, d)
    kp_l, vp_l, cp_l, mkp_l, mvp_l, ks_l, vs_l, cs_l = [], [], [], [], [], [], [], []
    for l in range(depth):
        gq = jnp.tile(q_norm_g[l].astype(F32), N_HEADS) * (HEAD_DIM ** -0.5)
        gqlo = jnp.where(lane_lo, gq, 0.0).reshape(1, -1)
        gqhi = jnp.where(lane_lo, 0.0, gq).reshape(1, -1)
        gk = jnp.tile(k_norm_g[l].astype(F32), N_KV_HEADS).reshape(1, -1)
        sinks = attn_sinks[l].astype(F32)
        sinkrow = jnp.repeat(sinks, ts).reshape(N_HEADS * ts, 1)
        win = w_in[l].astype(BF16)
        wout = w_out[l].astype(BF16)
        wmq = w_mq[l].astype(BF16)
        wmo = w_mo[l].astype(BF16)
        gmq = (jnp.tile(mq_norm_g[l].astype(F32), MEM_HEADS) * (MEM_HEAD_DIM ** -0.5)).reshape(1, -1)
        gmk = jnp.tile(mk_norm_g[l].astype(F32), MEM_HEADS).reshape(1, -1)
        w_r = jnp.concatenate([w_router_expert[l], w_router_group[l],
                               jnp.zeros((d, LANES - N_EXPERTS - N_GROUPS), F32)], axis=1).astype(BF16)
        b_r = jnp.concatenate([b_router_expert[l], b_router_group[l],
                               jnp.zeros((LANES - N_EXPERTS - N_GROUPS,), F32)]).reshape(1, -1).astype(F32)
        wg, wu, wd = w_exp_gate[l], w_exp_up[l], w_exp_down[l]

        mix_consts = (row(norm_mix_g[l]), win, bdq, bdk, gqlo, gqhi, gk)
        conv_consts = (conv_dw_w[l].astype(F32), row(conv_dw_b[l]), row(conv_ln_g[l]), row(conv_ln_b[l]), wout)
        tail_consts = (wmo, row(norm_ffn_g[l]), w_r, b_r)

        x1p, kp, vp, cp = _mixer_prompt(xp, sinks, mix_consts + (bias_p,) + conv_consts, TM_PROMPT)
        mk, mv = _memory_kv(mem_prompt.reshape(bp * mlen, d), row(norm_mem_g[l]),
                            w_mk[l].astype(BF16), w_mv[l].astype(BF16), gmk, min(256, bp * mlen))
        mk = mk.reshape(bp, mlen, d)
        mv = mv.reshape(bp, mlen, d)
        x2p, hnp, routep, infop, cntp = _cross_prompt(x1p, mk, mv, (row(norm_xa_g[l]), wmq, gmq) + tail_consts
                                               + (_strict_lower(TM_CROSS),), TM_CROSS)
        xp, w_bf16 = _moe(x2p.reshape(bp * sp, d), hnp.reshape(bp * sp, d // 2), routep.reshape(bp * sp, LANES),
                          infop, cntp, (wg, wu, wd), TM_ROWS_PROMPT, True)
        xp = xp.reshape(bp, sp, d)
        kp_l.append(kp.reshape(bp, BLOCK, N_KV_HEADS, HEAD_DIM))
        vp_l.append(vp.reshape(bp, BLOCK, N_KV_HEADS, HEAD_DIM))
        cp_l.append(cp)
        mkp_l.append(mk.reshape(bp, mlen, MEM_HEADS, MEM_HEAD_DIM))
        mvp_l.append(mv.reshape(bp, mlen, MEM_HEADS, MEM_HEAD_DIM))

        x1s, qm, ksn, vsn, csn = _mixer_sample(
            xs, _kv_feature_major(cache_swa_k[l]), _kv_feature_major(cache_swa_v[l]),
            jnp.transpose(state_conv[l], (1, 0, 2)),
            mix_consts + (bias_s, sinkrow) + conv_consts + (row(norm_xa_g[l]), wmq, gmq), SEQ_PER_STEP, ts)
        o_s = _mem_attend_sample(qm, _mem_cache_rows(cache_mem_k[l]), _mem_cache_rows(cache_mem_v[l]),
                                 MEM_SEQ_PER_STEP, ts)
        tms = min(256, nb * ts)
        x2s, hns, routes, infos, cnts = _cross_tail_call(o_s, x1s, tail_consts + (_strict_lower(tms),), tms)
        xs, _ = _moe(x2s, hns, routes, infos, cnts, w_bf16, TM_ROWS_SAMPLE, False)
        ks_l.append(jnp.transpose(ksn.reshape(nb, N_KV_HEADS, HEAD_DIM, wb), (0, 3, 1, 2)))
        vs_l.append(jnp.transpose(vsn.reshape(nb, N_KV_HEADS, HEAD_DIM, wb), (0, 3, 1, 2)))
        cs_l.append(jnp.transpose(csn, (1, 0, 2)))

    st = lambda xs_: jnp.stack(xs_, axis=0)
    return (xp, xs.reshape(nb, ts, d), st(kp_l), st(vp_l), st(cp_l), st(mkp_l), st(mvp_l),
            st(ks_l), st(vs_l), st(cs_l))
```

```python
import functools

import numpy as np
import jax
import jax.numpy as jnp
from jax import lax
from jax.experimental import pallas as pl
from jax.experimental.pallas import tpu as pltpu
from jax.experimental.pallas import tpu_sc as plsc

F32 = jnp.float32
BF16 = jnp.bfloat16

D_MODEL = 1024
D_ATTN = 512
D_CONV = 512
HEAD_DIM = 64
N_HEADS = 8
N_KV_HEADS = 2
KV_DIM = N_KV_HEADS * HEAD_DIM
HEADS_PER_KV = N_HEADS // N_KV_HEADS
WINDOW = 128
BLOCK = 128
CONV_WIDTH = 31
CONV_PAST = CONV_WIDTH - 1
MEM_HEADS = 4
MEM_HEAD_DIM = 256
N_GROUPS = 4
EXPERTS_PER_GROUP = 8
N_EXPERTS = 32
D_EXPERT = 256
D_IN = D_ATTN + 2 * KV_DIM + 2 * D_CONV
EPS = 1e-6
NEG_INF = -1e30

LANES = 128
HALF = LANES // 2
MXU_TILE = 256
SUBLANES = 8
CONV_PAD = 32
CONV_ROWS = 64
VMEM_LIMIT = 56 * 1024 * 1024

TM_PROMPT = 512
TM_CROSS = 1024
SEQ_PER_STEP = 32
MEM_SEQ_PER_STEP = 8
TOP_K = 2
ROUTE_ROWS = 8
SC_CORES = 2
SC_WORKERS = 32
SC_ROWS_PER_STEP = 64
TM_ROWS_PROMPT = 512
TM_ROWS_SAMPLE = 128
TM_COMBINE = 1024


def _rms(x, g):
    ms = jnp.mean(x * x, axis=-1, keepdims=True)
    return x * lax.rsqrt(ms + EPS) * g


def _group_mean_sq(x, bd_ref, width):
    x2 = x * x
    hi = x2.astype(BF16)
    lo = (x2 - hi.astype(F32)).astype(BF16)
    bd = bd_ref[...]
    w = bd.shape[0]
    parts = [jnp.dot(hi[:, c:c + w], bd, preferred_element_type=F32)
             + jnp.dot(lo[:, c:c + w], bd, preferred_element_type=F32) for c in range(0, x.shape[1], w)]
    return jnp.concatenate(parts, axis=1) * (1.0 / width)


def _mixer_proj(x, ng_ref, win_ref, bdq_ref, bdk_ref, gqlo_ref, gqhi_ref, gk_ref):
    h = _rms(x, ng_ref[...]).astype(BF16)
    p = jnp.dot(h, win_ref[...], preferred_element_type=F32)
    q = p[:, :D_ATTN]
    k = p[:, D_ATTN:D_ATTN + KV_DIM]
    v = p[:, D_ATTN + KV_DIM:D_ATTN + 2 * KV_DIM]
    ua = p[:, D_ATTN + 2 * KV_DIM:D_ATTN + 2 * KV_DIM + D_CONV]
    ub = p[:, D_ATTN + 2 * KV_DIM + D_CONV:]
    qn = q * lax.rsqrt(_group_mean_sq(q, bdq_ref, HEAD_DIM) + EPS)
    q_lo = qn * gqlo_ref[...]
    q_hi = qn * gqhi_ref[...]
    kn = k * lax.rsqrt(_group_mean_sq(k, bdk_ref, HEAD_DIM) + EPS) * gk_ref[...]
    u = ua * jax.nn.sigmoid(ub)
    return q_lo, q_hi, kn, v, u


def _dup_halves(x):
    lo = lax.broadcasted_iota(jnp.int32, x.shape, 1) < HALF
    xr = pltpu.roll(x, HALF, axis=1)
    return jnp.where(lo, x, xr), jnp.where(lo, xr, x)


def _sink_softmax(s, sink):
    m = jnp.maximum(jnp.max(s, axis=-1, keepdims=True), sink)
    p = jnp.exp(s - m)
    denom = jnp.sum(p, axis=-1, keepdims=True) + jnp.exp(sink - m)
    return p * (1.0 / denom)


def _conv_ln_silu(y, cb_ref, lg_ref, lb_ref):
    y = y + cb_ref[...]
    mu = jnp.mean(y, axis=-1, keepdims=True)
    yc = y - mu
    yn = yc * lax.rsqrt(jnp.mean(yc * yc, axis=-1, keepdims=True) + EPS)
    z = yn * lg_ref[...] + lb_ref[...]
    return z * jax.nn.sigmoid(z)


def _mem_query(x1, xag_ref, wmq_ref, gmq_ref):
    h = _rms(x1, xag_ref[...]).astype(BF16)
    q = jnp.dot(h, wmq_ref[...], preferred_element_type=F32)
    parts = []
    for hd in range(MEM_HEADS):
        qh = q[:, hd * MEM_HEAD_DIM:(hd + 1) * MEM_HEAD_DIM]
        parts.append(qh * lax.rsqrt(jnp.mean(qh * qh, axis=-1, keepdims=True) + EPS))
    return jnp.concatenate(parts, axis=1) * gmq_ref[...]


def _mem_attend(q, k_heads, v_heads):
    outs = []
    for hd in range(MEM_HEADS):
        sl = slice(hd * MEM_HEAD_DIM, (hd + 1) * MEM_HEAD_DIM)
        s = lax.dot_general(q[:, sl].astype(BF16), k_heads[hd], (((1,), (1,)), ((), ())),
                            preferred_element_type=F32)
        m = jnp.max(s, axis=-1, keepdims=True)
        p = jnp.exp(s - m)
        p = p * (1.0 / jnp.sum(p, axis=-1, keepdims=True))
        outs.append(jnp.dot(p.astype(BF16), v_heads[hd], preferred_element_type=F32))
    return jnp.concatenate(outs, axis=1)


def _split_heads(x):
    return [x[:, hd * MEM_HEAD_DIM:(hd + 1) * MEM_HEAD_DIM] for hd in range(MEM_HEADS)]


def _pack_bf16_pairs(y):
    n = y.shape[1] // 2
    bits = pltpu.bitcast(y.astype(BF16).astype(F32), jnp.int32)
    return (bits[:, :n] & jnp.int32(-65536)) | lax.shift_right_logical(bits[:, n:], jnp.int32(16))


def _unpack_bf16_pairs(w):
    hi = pltpu.bitcast(w & jnp.int32(-65536), F32)
    lo = pltpu.bitcast(lax.shift_left(w, jnp.int32(16)), F32)
    return jnp.concatenate([hi, lo], axis=1)


def _cross_tail(o, x1, wmo_ref, fg_ref, wr_ref, br_ref, tri_ref, base):
    x2 = x1 + jnp.dot(o.astype(BF16), wmo_ref[...], preferred_element_type=F32)
    hn_f = _rms(x2, fg_ref[...])
    hn = hn_f.astype(BF16)
    logits = jnp.dot(hn, wr_ref[...], preferred_element_type=F32) + br_ref[...]
    lane = lax.broadcasted_iota(jnp.int32, logits.shape, 1)
    is_g = (lane >= N_EXPERTS) & (lane < N_EXPERTS + N_GROUPS)
    lg = jnp.where(is_g, logits, NEG_INF)
    gmax = jnp.max(lg, axis=-1, keepdims=True)
    gsel = jnp.min(jnp.where(lg == gmax, lane, 2 * LANES), axis=-1, keepdims=True) - N_EXPERTS
    pg_sel = 1.0 / jnp.sum(jnp.exp(lg - gmax), axis=-1, keepdims=True)
    in_grp = (lane >= gsel * EXPERTS_PER_GROUP) & (lane < (gsel + 1) * EXPERTS_PER_GROUP)
    le = jnp.where(in_grp, logits, NEG_INF)
    top1 = jnp.max(le, axis=-1, keepdims=True)
    idx1 = jnp.min(jnp.where(le == top1, lane, 2 * LANES), axis=-1, keepdims=True)
    le2 = jnp.where(lane == idx1, NEG_INF, le)
    top2 = jnp.max(le2, axis=-1, keepdims=True)
    idx2 = jnp.min(jnp.where(le2 == top2, lane, 2 * LANES), axis=-1, keepdims=True)
    e2 = jnp.exp(top2 - top1)
    inv = 1.0 / (1.0 + e2)
    gate1 = pg_sel * inv
    gate2 = pg_sel * (e2 * inv)
    used = jnp.where((lane == idx1) | (lane == idx2), 1.0, 0.0)
    before = jnp.dot(tri_ref[...], used.astype(BF16), preferred_element_type=F32) + base
    rank1 = jnp.sum(jnp.where(lane == idx1, before, 0.0), axis=-1, keepdims=True)
    rank2 = jnp.sum(jnp.where(lane == idx2, before, 0.0), axis=-1, keepdims=True)
    route = jnp.zeros_like(logits)
    for pos, val in enumerate((idx1.astype(F32), idx2.astype(F32), gate1, gate2, rank1, rank2)):
        route = jnp.where(lane == pos, val, route)
    info = jnp.transpose(route)[0:ROUTE_ROWS, :]
    return x2, _pack_bf16_pairs(hn_f), route, info, base + jnp.sum(used, axis=0, keepdims=True)


def _mixer_prompt_kernel(sink_ref, x_ref, ng_ref, win_ref, bdq_ref, bdk_ref, gqlo_ref, gqhi_ref, gk_ref,
                         bias_ref, cw_ref, cb_ref, lg_ref, lb_ref, wout_ref,
                         x1_ref, ko_ref, vo_ref, co_ref,
                         kband, vband, uext, ushift, *, tm, nt):
    t = pl.program_id(1)

    @pl.when(t == 0)
    def _():
        kband[0:BLOCK, :] = jnp.zeros((BLOCK, KV_DIM), F32)
        vband[0:BLOCK, :] = jnp.zeros((BLOCK, KV_DIM), F32)
        uext[0:CONV_PAD, :] = jnp.zeros((CONV_PAD, D_CONV), F32)

    x = x_ref[...]
    q_lo, q_hi, kn, v, u = _mixer_proj(x, ng_ref, win_ref, bdq_ref, bdk_ref, gqlo_ref, gqhi_ref, gk_ref)
    kband[BLOCK:BLOCK + tm, :] = kn
    vband[BLOCK:BLOCK + tm, :] = v
    uext[CONV_PAD:CONV_PAD + tm, :] = u

    @pl.when(t == nt - 1)
    def _():
        ko_ref[...] = kn[tm - BLOCK:, :]
        vo_ref[...] = v[tm - BLOCK:, :]
        co_ref[...] = uext[pl.ds(CONV_PAD + tm - CONV_PAST, CONV_PAST), :]

    kd = [a.astype(BF16) for a in _dup_halves(kband[...])]
    vd = _dup_halves(vband[...])
    lo = lax.broadcasted_iota(jnp.int32, vd[0].shape, 1) < HALF
    v_lo = [jnp.where(lo, a, 0.0).astype(BF16) for a in vd]
    v_hi = [jnp.where(lo, 0.0, a).astype(BF16) for a in vd]
    q_lo = q_lo.astype(BF16)
    q_hi = q_hi.astype(BF16)
    attn_blocks = []
    for j in range(tm // BLOCK):
        rows = slice(j * BLOCK, (j + 1) * BLOCK)
        keys = slice(j * BLOCK, j * BLOCK + 2 * BLOCK)
        bias_base = jnp.where(t == 0, N_HEADS, 0) if j == 0 else 0
        tiles = []
        for c in range(N_KV_HEADS):
            q4 = jnp.concatenate(
                [(q_lo if (HEADS_PER_KV * c + a) % 2 == 0 else q_hi)[rows,
                  ((HEADS_PER_KV * c + a) // 2) * LANES:((HEADS_PER_KV * c + a) // 2 + 1) * LANES]
                 for a in range(HEADS_PER_KV)], axis=0)
            s_all = lax.dot_general(q4, kd[c][keys], (((1,), (1,)), ((), ())), preferred_element_type=F32)
            ps = []
            for a in range(HEADS_PER_KV):
                hd = HEADS_PER_KV * c + a
                s = s_all[a * BLOCK:(a + 1) * BLOCK] + bias_ref[bias_base + hd]
                ps.append(_sink_softmax(s, sink_ref[hd]).astype(BF16))
            vstack = jnp.concatenate([v_lo[c][keys], v_hi[c][keys]], axis=0)
            for i2 in range(HEADS_PER_KV // 2):
                pp = jnp.concatenate([ps[2 * i2], ps[2 * i2 + 1]], axis=1)
                tiles.append(jnp.dot(pp, vstack, preferred_element_type=F32))
        attn_blocks.append(jnp.concatenate(tiles, axis=1))
    attn = jnp.concatenate(attn_blocks, axis=0)

    off = CONV_PAD - CONV_PAST
    span = tm + CONV_PAD - SUBLANES
    for s in range(1, SUBLANES):
        ushift[s - 1] = uext[pl.ds(s, span), :]
    chunks = []
    for lc in range(D_CONV // LANES):
        ls = slice(lc * LANES, (lc + 1) * LANES)
        accs = [jnp.zeros((CONV_ROWS, LANES), F32) for _ in range(tm // CONV_ROWS)]
        for j in range(CONV_WIDTH):
            s, a = (off + j) % SUBLANES, (off + j) // SUBLANES
            wj = cw_ref[j:j + 1, ls]
            for rc in range(tm // CONV_ROWS):
                r0 = rc * CONV_ROWS + a * SUBLANES
                tap = uext[r0:r0 + CONV_ROWS, ls] if s == 0 else ushift[s - 1, r0:r0 + CONV_ROWS, ls]
                accs[rc] = accs[rc] + wj * tap
        chunks.append(jnp.concatenate(accs, axis=0))
    y = _conv_ln_silu(jnp.concatenate(chunks, axis=1), cb_ref, lg_ref, lb_ref)

    x1_ref[...] = (x + jnp.dot(attn.astype(BF16), wout_ref[0:D_ATTN, :], preferred_element_type=F32)
                   + jnp.dot(y.astype(BF16), wout_ref[D_ATTN:, :], preferred_element_type=F32))

    kband[0:BLOCK, :] = kband[tm:tm + BLOCK, :]
    vband[0:BLOCK, :] = vband[tm:tm + BLOCK, :]
    uext[0:CONV_PAD, :] = uext[tm:tm + CONV_PAD, :]


def _const_spec(shape):
    nd = len(shape)
    return pl.BlockSpec(shape, lambda *_: (0,) * nd)


def _mixer_prompt(x, sinks, consts, tm):
    b, s, d = x.shape
    nt = s // tm
    (ng, win, bdq, bdk, gqlo, gqhi, gk, bias_p, cw, cb, lg, lb, wout) = consts
    kern = functools.partial(_mixer_prompt_kernel, tm=tm, nt=nt)
    cspecs = [_const_spec(a.shape) for a in consts]
    return pl.pallas_call(
        kern,
        grid=(b, nt),
        in_specs=[pl.BlockSpec(memory_space=pltpu.SMEM),
                  pl.BlockSpec((None, tm, d), lambda i, j: (i, j, 0))] + cspecs,
        out_specs=[pl.BlockSpec((None, tm, d), lambda i, j: (i, j, 0)),
                   pl.BlockSpec((None, BLOCK, KV_DIM), lambda i, j: (i, 0, 0)),
                   pl.BlockSpec((None, BLOCK, KV_DIM), lambda i, j: (i, 0, 0)),
                   pl.BlockSpec((None, CONV_PAST, D_CONV), lambda i, j: (i, 0, 0))],
        out_shape=[jax.ShapeDtypeStruct((b, s, d), F32),
                   jax.ShapeDtypeStruct((b, BLOCK, KV_DIM), F32),
                   jax.ShapeDtypeStruct((b, BLOCK, KV_DIM), F32),
                   jax.ShapeDtypeStruct((b, CONV_PAST, D_CONV), F32)],
        scratch_shapes=[pltpu.VMEM((BLOCK + tm, KV_DIM), F32),
                        pltpu.VMEM((BLOCK + tm, KV_DIM), F32),
                        pltpu.VMEM((CONV_PAD + tm, D_CONV), F32),
                        pltpu.VMEM((SUBLANES - 1, tm + CONV_PAD - SUBLANES, D_CONV), F32)],
        compiler_params=pltpu.CompilerParams(dimension_semantics=("arbitrary", "arbitrary"),
                                             vmem_limit_bytes=VMEM_LIMIT),
        name="mixer_prompt",
    )(sinks, x, *consts)


def _mixer_sample_kernel(x_ref, ck_ref, cv_ref, st_ref, ng_ref, win_ref, bdq_ref, bdk_ref, gqlo_ref, gqhi_ref,
                         gk_ref, bias_ref, sinkrow_ref, cw_ref, cb_ref, lg_ref, lb_ref, wout_ref,
                         xag_ref, wmq_ref, gmq_ref,
                         x1_ref, qm_ref, ko_ref, vo_ref, co_ref,
                         kall, vall, qs, s_scr, r_scr, u_scr, y_scr, *, g, ts):
    wb = ck_ref.shape[2]
    x = x_ref[...]
    q_lo, q_hi, kn, v, u = _mixer_proj(x, ng_ref, win_ref, bdq_ref, bdk_ref, gqlo_ref, gqhi_ref, gk_ref)
    kall[:, wb:wb + ts, :] = kn.reshape(g, ts, KV_DIM)
    vall[:, wb:wb + ts, :] = v.reshape(g, ts, KV_DIM)
    for n in range(g):
        kall[n, 0:wb, :] = ck_ref[n].T
        vall[n, 0:wb, :] = cv_ref[n].T
        ko_ref[n] = kall[n, ts:wb + ts, :].T
        vo_ref[n] = vall[n, ts:wb + ts, :].T

    for lc in range(D_CONV // LANES):
        ls = slice(lc * LANES, (lc + 1) * LANES)
        u_scr[lc] = u[:, ls]
        u_t = [u_scr[lc, pl.ds(t, g, stride=ts), :] for t in range(ts)]

        def frame(r):
            return st_ref[r, :, ls] if r < CONV_PAST else u_t[r - CONV_PAST]

        for t in range(ts):
            acc = jnp.zeros((g, LANES), F32)
            for j in range(CONV_WIDTH):
                acc = acc + cw_ref[j:j + 1, ls] * frame(t + j)
            y_scr[lc, pl.ds(t, g, stride=ts), :] = acc
        for r in range(CONV_PAST):
            co_ref[r, :, ls] = frame(r + ts)
    y = _conv_ln_silu(jnp.concatenate([y_scr[lc] for lc in range(D_CONV // LANES)], axis=1), cb_ref, lg_ref, lb_ref)

    for hd in range(N_HEADS):
        tile = (q_lo if hd % 2 == 0 else q_hi)[:, (hd // 2) * LANES:(hd // 2 + 1) * LANES]
        if hd % 2 != hd // HEADS_PER_KV:
            tile = pltpu.roll(tile, HALF, axis=1)
        qs[:, hd * ts:(hd + 1) * ts, :] = tile.reshape(g, ts, LANES)
    for n in range(g):
        s_scr[n] = lax.dot_general(qs[n].astype(BF16), kall[n].astype(BF16), (((1,), (1,)), ((), ())),
                                   preferred_element_type=F32)
    s_scr[...] = _sink_softmax(s_scr[...] + bias_ref[...], sinkrow_ref[...])
    for n in range(g):
        r_scr[n] = jnp.dot(s_scr[n].astype(BF16), vall[n].astype(BF16), preferred_element_type=F32)
    r = r_scr[...].reshape(g * N_HEADS * ts, LANES)
    r_sw = pltpu.roll(r, HALF, axis=1).reshape(g, N_HEADS * ts, LANES)
    r = r.reshape(g, N_HEADS * ts, LANES)
    lo = lax.broadcasted_iota(jnp.int32, (g * ts, LANES), 1) < HALF
    heads = [(r if hd // HEADS_PER_KV == hd % 2 else r_sw)[:, hd * ts:(hd + 1) * ts, :].reshape(g * ts, LANES)
             for hd in range(N_HEADS)]
    attn = jnp.concatenate([jnp.where(lo, heads[2 * i], heads[2 * i + 1]) for i in range(N_HEADS // 2)], axis=1)

    x1 = (x + jnp.dot(attn.astype(BF16), wout_ref[0:D_ATTN, :], preferred_element_type=F32)
          + jnp.dot(y.astype(BF16), wout_ref[D_ATTN:, :], preferred_element_type=F32))
    x1_ref[...] = x1
    qm_ref[...] = _mem_query(x1, xag_ref, wmq_ref, gmq_ref)


def _mixer_sample(xs2d, ck, cv, st, consts, g, ts):
    n, d = xs2d.shape
    nb, wb = ck.shape[0], ck.shape[2]
    rows = g * ts
    cspecs = [_const_spec(a.shape) for a in consts]
    kern = functools.partial(_mixer_sample_kernel, g=g, ts=ts)
    return pl.pallas_call(
        kern,
        grid=(nb // g,),
        in_specs=[pl.BlockSpec((rows, d), lambda i: (i, 0)),
                  pl.BlockSpec((g, KV_DIM, wb), lambda i: (i, 0, 0)),
                  pl.BlockSpec((g, KV_DIM, wb), lambda i: (i, 0, 0)),
                  pl.BlockSpec((CONV_PAST, g, D_CONV), lambda i: (0, i, 0))] + cspecs,
        out_specs=[pl.BlockSpec((rows, d), lambda i: (i, 0)),
                   pl.BlockSpec((rows, d), lambda i: (i, 0)),
                   pl.BlockSpec((g, KV_DIM, wb), lambda i: (i, 0, 0)),
                   pl.BlockSpec((g, KV_DIM, wb), lambda i: (i, 0, 0)),
                   pl.BlockSpec((CONV_PAST, g, D_CONV), lambda i: (0, i, 0))],
        out_shape=[jax.ShapeDtypeStruct((n, d), F32),
                   jax.ShapeDtypeStruct((n, d), F32),
                   jax.ShapeDtypeStruct((nb, KV_DIM, wb), F32),
                   jax.ShapeDtypeStruct((nb, KV_DIM, wb), F32),
                   jax.ShapeDtypeStruct((CONV_PAST, nb, D_CONV), F32)],
        scratch_shapes=[pltpu.VMEM((g, wb + ts, KV_DIM), F32),
                        pltpu.VMEM((g, wb + ts, KV_DIM), F32),
                        pltpu.VMEM((g, N_HEADS * ts, LANES), F32),
                        pltpu.VMEM((g, N_HEADS * ts, wb + ts), F32),
                        pltpu.VMEM((g, N_HEADS * ts, LANES), F32),
                        pltpu.VMEM((D_CONV // LANES, rows, LANES), F32),
                        pltpu.VMEM((D_CONV // LANES, rows, LANES), F32)],
        compiler_params=pltpu.CompilerParams(dimension_semantics=("arbitrary",),
                                             vmem_limit_bytes=VMEM_LIMIT),
        name="mixer_sample",
    )(xs2d, ck, cv, st, *consts)


def _memory_kv_kernel(mem_ref, g_ref, wmk_ref, wmv_ref, gk_ref, k_ref, v_ref):
    h = _rms(mem_ref[...], g_ref[...]).astype(BF16)
    k = jnp.dot(h, wmk_ref[...], preferred_element_type=F32)
    parts = []
    for hd in range(MEM_HEADS):
        kh = k[:, hd * MEM_HEAD_DIM:(hd + 1) * MEM_HEAD_DIM]
        parts.append(kh * lax.rsqrt(jnp.mean(kh * kh, axis=-1, keepdims=True) + EPS))
    k_ref[...] = jnp.concatenate(parts, axis=1) * gk_ref[...]
    v_ref[...] = jnp.dot(h, wmv_ref[...], preferred_element_type=F32)


def _memory_kv(mem2d, g, wmk, wmv, gk, tm):
    n, d = mem2d.shape
    consts = (g, wmk, wmv, gk)
    return pl.pallas_call(
        _memory_kv_kernel,
        grid=(n // tm,),
        in_specs=[pl.BlockSpec((tm, d), lambda i: (i, 0))] + [_const_spec(a.shape) for a in consts],
        out_specs=[pl.BlockSpec((tm, d), lambda i: (i, 0)), pl.BlockSpec((tm, d), lambda i: (i, 0))],
        out_shape=[jax.ShapeDtypeStruct((n, d), F32), jax.ShapeDtypeStruct((n, d), F32)],
        compiler_params=pltpu.CompilerParams(dimension_semantics=("arbitrary",),
                                             vmem_limit_bytes=VMEM_LIMIT),
        name="memory_kv",
    )(mem2d, *consts)


def _cross_prompt_kernel(x1_ref, mk_ref, mv_ref, xag_ref, wmq_ref, gmq_ref, wmo_ref, fg_ref, wr_ref, br_ref, tri_ref,
                         x2_ref, hn_ref, route_ref, info_ref, cnt_ref, base):
    @pl.when((pl.program_id(0) == 0) & (pl.program_id(1) == 0))
    def _():
        base[...] = jnp.zeros_like(base)

    x1 = x1_ref[...]
    q = _mem_query(x1, xag_ref, wmq_ref, gmq_ref)
    o = _mem_attend(q, _split_heads(mk_ref[...].astype(BF16)), _split_heads(mv_ref[...].astype(BF16)))
    x2, hn, route, info, new_base = _cross_tail(o, x1, wmo_ref, fg_ref, wr_ref, br_ref, tri_ref, base[...])
    x2_ref[...] = x2
    hn_ref[...] = hn
    route_ref[...] = route
    info_ref[...] = info
    base[...] = new_base
    cnt_ref[...] = new_base


def _cross_prompt(x1, mk, mv, consts, tm):
    b, s, d = x1.shape
    m = mk.shape[1]
    cspecs = [_const_spec(a.shape) for a in consts]
    return pl.pallas_call(
        _cross_prompt_kernel,
        grid=(b, s // tm),
        in_specs=[pl.BlockSpec((None, tm, d), lambda i, j: (i, j, 0)),
                  pl.BlockSpec((None, m, d), lambda i, j: (i, 0, 0)),
                  pl.BlockSpec((None, m, d), lambda i, j: (i, 0, 0))] + cspecs,
        out_specs=[pl.BlockSpec((None, tm, d), lambda i, j: (i, j, 0)),
                   pl.BlockSpec((None, tm, d // 2), lambda i, j: (i, j, 0)),
                   pl.BlockSpec((None, tm, LANES), lambda i, j: (i, j, 0)),
                   pl.BlockSpec((ROUTE_ROWS, tm), lambda i, j: (0, i * (s // tm) + j)),
                   pl.BlockSpec((1, LANES), lambda i, j: (0, 0))],
        out_shape=[jax.ShapeDtypeStruct((b, s, d), F32),
                   jax.ShapeDtypeStruct((b, s, d // 2), jnp.int32),
                   jax.ShapeDtypeStruct((b, s, LANES), F32),
                   jax.ShapeDtypeStruct((ROUTE_ROWS, b * s), F32),
                   jax.ShapeDtypeStruct((1, LANES), F32)],
        scratch_shapes=[pltpu.VMEM((1, LANES), F32)],
        compiler_params=pltpu.CompilerParams(dimension_semantics=("arbitrary", "arbitrary"),
                                             vmem_limit_bytes=VMEM_LIMIT),
        name="cross_prompt",
    )(x1, mk, mv, *consts)


def _mem_attend_sample_kernel(qm_ref, *refs, g, ts):
    k_refs, v_refs, o_ref, s_scr = refs[:g], refs[g:2 * g], refs[2 * g], refs[2 * g + 1]
    halves = MEM_HEAD_DIM // LANES
    rows_per_pos = MEM_HEADS * halves
    mlen = k_refs[0].shape[0] // rows_per_pos

    def all_heads(ref):
        return jnp.concatenate([ref[pl.ds(c * MEM_HEADS + hd, mlen, stride=rows_per_pos), :]
                                for hd in range(MEM_HEADS) for c in range(halves)], axis=1).astype(BF16)

    rows = MEM_HEADS * ts
    own = (lax.broadcasted_iota(jnp.int32, (rows, MEM_HEADS * MEM_HEAD_DIM), 1) // MEM_HEAD_DIM
           == lax.broadcasted_iota(jnp.int32, (rows, MEM_HEADS * MEM_HEAD_DIM), 0) // ts)
    for n in range(g):
        q = qm_ref[n * ts:(n + 1) * ts, :]
        q_bd = jnp.where(own, jnp.concatenate([q] * MEM_HEADS, axis=0), 0.0).astype(BF16)
        s_scr[n] = lax.dot_general(q_bd, all_heads(k_refs[n]), (((1,), (1,)), ((), ())),
                                   preferred_element_type=F32)
    s = s_scr[...]
    p = jnp.exp(s - jnp.max(s, axis=-1, keepdims=True))
    s_scr[...] = p * (1.0 / jnp.sum(p, axis=-1, keepdims=True))
    for n in range(g):
        r = jnp.dot(s_scr[n].astype(BF16), all_heads(v_refs[n]), preferred_element_type=F32)
        o_ref[n * ts:(n + 1) * ts, :] = jnp.concatenate(
            [r[hd * ts:(hd + 1) * ts, hd * MEM_HEAD_DIM:(hd + 1) * MEM_HEAD_DIM] for hd in range(MEM_HEADS)], axis=1)


def _mem_cache_rows(cache):
    nb, mlen = cache.shape[:2]
    halves = MEM_HEAD_DIM // LANES
    return (cache.reshape(nb, mlen, MEM_HEADS, halves, LANES).transpose(0, 1, 3, 2, 4)
            .reshape(nb, mlen * halves * MEM_HEADS, LANES))


def _mem_attend_sample(qm, mk, mv, g, ts):
    n, d = qm.shape
    nb, rows = mk.shape[0], mk.shape[1]
    kern = functools.partial(_mem_attend_sample_kernel, g=g, ts=ts)

    def seq_spec(j):
        return pl.BlockSpec((None, rows, LANES), lambda i: (g * i + j, 0, 0))

    return pl.pallas_call(
        kern,
        grid=(nb // g,),
        in_specs=[pl.BlockSpec((g * ts, d), lambda i: (i, 0))] + [seq_spec(j) for j in range(g)] * 2,
        out_specs=pl.BlockSpec((g * ts, d), lambda i: (i, 0)),
        out_shape=jax.ShapeDtypeStruct((n, d), F32),
        scratch_shapes=[pltpu.VMEM((g, MEM_HEADS * ts, rows // (MEM_HEADS * (MEM_HEAD_DIM // LANES))), F32)],
        compiler_params=pltpu.CompilerParams(dimension_semantics=("arbitrary",),
                                             vmem_limit_bytes=VMEM_LIMIT),
        name="mem_attend_sample",
    )(qm, *([mk] * g), *([mv] * g))


def _cross_tail_kernel(o_ref, x1_ref, wmo_ref, fg_ref, wr_ref, br_ref, tri_ref,
                       x2_ref, hn_ref, route_ref, info_ref, cnt_ref, base):
    @pl.when(pl.program_id(0) == 0)
    def _():
        base[...] = jnp.zeros_like(base)

    x2, hn, route, info, new_base = _cross_tail(o_ref[...], x1_ref[...], wmo_ref, fg_ref, wr_ref, br_ref, tri_ref,
                                                base[...])
    x2_ref[...] = x2
    hn_ref[...] = hn
    route_ref[...] = route
    info_ref[...] = info
    base[...] = new_base
    cnt_ref[...] = new_base


def _cross_tail_call(o, x1, consts, tm):
    n, d = x1.shape
    cspecs = [_const_spec(a.shape) for a in consts]
    return pl.pallas_call(
        _cross_tail_kernel,
        grid=(n // tm,),
        in_specs=[pl.BlockSpec((tm, d), lambda i: (i, 0)), pl.BlockSpec((tm, d), lambda i: (i, 0))] + cspecs,
        out_specs=[pl.BlockSpec((tm, d), lambda i: (i, 0)),
                   pl.BlockSpec((tm, d // 2), lambda i: (i, 0)),
                   pl.BlockSpec((tm, LANES), lambda i: (i, 0)),
                   pl.BlockSpec((ROUTE_ROWS, tm), lambda i: (0, i)),
                   pl.BlockSpec((1, LANES), lambda i: (0, 0))],
        out_shape=[jax.ShapeDtypeStruct((n, d), F32),
                   jax.ShapeDtypeStruct((n, d // 2), jnp.int32),
                   jax.ShapeDtypeStruct((n, LANES), F32),
                   jax.ShapeDtypeStruct((ROUTE_ROWS, n), F32),
                   jax.ShapeDtypeStruct((1, LANES), F32)],
        scratch_shapes=[pltpu.VMEM((1, LANES), F32)],
        compiler_params=pltpu.CompilerParams(dimension_semantics=("arbitrary",),
                                             vmem_limit_bytes=VMEM_LIMIT),
        name="cross_tail",
    )(o, x1, *consts)


def _sc_rows_per_step(per_worker):
    step = min(SC_ROWS_PER_STEP, per_worker)
    assert per_worker % step == 0 and step % 8 == 0
    return step


def _sc_gather_rows(table, idx):
    nrows = idx.shape[0]
    _, width = table.shape
    assert nrows % (8 * SC_WORKERS) == 0
    per_worker = nrows // SC_WORKERS
    step = _sc_rows_per_step(per_worker // 2)
    mesh = plsc.VectorSubcoreMesh(core_axis_name="c", subcore_axis_name="s")

    @functools.partial(
        pl.kernel, mesh=mesh, out_type=jax.ShapeDtypeStruct((nrows, width), table.dtype),
        scratch_types=[pltpu.VMEM((step,), jnp.int32)] * 2 + [pltpu.VMEM((step, width), table.dtype)] * 2
        + [pltpu.SemaphoreType.DMA] * 2)
    def gather(table_hbm, idx_hbm, out_hbm, idx_a, idx_b, rows_a, rows_b, sem_a, sem_b):
        wid = lax.axis_index("s") * SC_CORES + lax.axis_index("c")
        base = wid * per_worker

        @pl.loop(0, per_worker // (2 * step))
        def _(i):
            off = base + i * (2 * step)
            pltpu.sync_copy(idx_hbm.at[pl.ds(off, step)], idx_a)
            pltpu.sync_copy(idx_hbm.at[pl.ds(off + step, step)], idx_b)
            gather_a = pltpu.async_copy(table_hbm.at[idx_a], rows_a, sem_a)
            gather_b = pltpu.async_copy(table_hbm.at[idx_b], rows_b, sem_b)
            gather_a.wait()
            write_a = pltpu.async_copy(rows_a, out_hbm.at[pl.ds(off, step)], sem_a)
            gather_b.wait()
            write_b = pltpu.async_copy(rows_b, out_hbm.at[pl.ds(off + step, step)], sem_b)
            write_a.wait()
            write_b.wait()

    return gather(table, idx)


def _sc_scatter_rows(table, pos, nrows_out):
    n, width = table.shape
    assert n % (8 * SC_WORKERS) == 0 and pos.shape == (TOP_K * n,)
    per_worker = n // SC_WORKERS
    step = _sc_rows_per_step(per_worker)
    mesh = plsc.VectorSubcoreMesh(core_axis_name="c", subcore_axis_name="s")

    @functools.partial(
        pl.kernel, mesh=mesh, out_type=jax.ShapeDtypeStruct((nrows_out, width), table.dtype),
        scratch_types=[pltpu.VMEM((step,), jnp.int32)] * TOP_K
        + [pltpu.VMEM((step, width), table.dtype), pltpu.SemaphoreType.DMA])
    def scatter(table_hbm, pos_hbm, out_hbm, *scratch):
        idx_vs, rows_v, sem = scratch[:TOP_K], scratch[TOP_K], scratch[TOP_K + 1]
        wid = lax.axis_index("s") * SC_CORES + lax.axis_index("c")
        base = wid * per_worker

        @pl.loop(0, per_worker // step)
        def _(i):
            off = base + i * step
            for k in range(TOP_K):
                pltpu.sync_copy(pos_hbm.at[pl.ds(k * n + off, step)], idx_vs[k])
            pltpu.sync_copy(table_hbm.at[pl.ds(off, step)], rows_v)
            for k in range(TOP_K):
                pltpu.async_copy(rows_v, out_hbm.at[idx_vs[k]], sem).wait()

    return scatter(table, pos)


W_CHUNKS = 4


def _expert_weight_copies(e, slot, w_hbm, w_slots, sems):
    copies = []
    for m, (src, dst) in enumerate(zip(w_hbm, w_slots)):
        rows = dst.shape[1] // W_CHUNKS
        for c in range(W_CHUNKS):
            sl = pl.ds(c * rows, rows)
            copies.append(pltpu.make_async_copy(src.at[e, sl, :], dst.at[slot, sl, :], sems.at[m * W_CHUNKS + c]))
    return copies


def _moe_grouped_kernel(te_ref, nv_ref, rv_ref, nxt_ref, par_ref, xs_ref, wg_hbm, wu_hbm, wd_hbm, ys_ref,
                        *rest, publish):
    i = pl.program_id(0)
    w_hbm = (wg_hbm, wu_hbm, wd_hbm)
    if publish:
        out_hbm, w_slots, w_bf16, sems, out_sems = rest[0:3], rest[3:6], rest[6:9], rest[9], rest[10]
    else:
        w_slots, sems = rest[0:3], rest[3]

    def publish_copies(e):
        return [pltpu.make_async_copy(src, dst.at[e], out_sems.at[m])
                for m, (src, dst) in enumerate(zip(w_bf16, out_hbm))]

    @pl.when(i == 0)
    def _():
        for cp in _expert_weight_copies(te_ref[0], par_ref[0], w_hbm, w_slots, sems):
            cp.start()

    @pl.when(i < nv_ref[0])
    def _():
        slot = par_ref[i]

        @pl.when((i == 0) | (te_ref[i] != te_ref[jnp.maximum(i - 1, 0)]))
        def _():
            for cp in _expert_weight_copies(te_ref[i], slot, w_hbm, w_slots, sems):
                cp.wait()
            if publish:
                @pl.when(i > 0)
                def _():
                    for cp in publish_copies(te_ref[jnp.maximum(i - 1, 0)]):
                        cp.wait()

                for src, dst in zip(w_slots, w_bf16):
                    dst[...] = src[slot].astype(BF16)
                for cp in publish_copies(te_ref[i]):
                    cp.start()

            @pl.when(nxt_ref[i] >= 0)
            def _():
                for cp in _expert_weight_copies(nxt_ref[i], 1 - slot, w_hbm, w_slots, sems):
                    cp.start()

        wg, wu, wd = [r[...] for r in w_bf16] if publish else [r[slot] for r in w_slots]
        xs = xs_ref[...]
        row = lax.broadcasted_iota(jnp.int32, xs.shape, 0)
        x = _unpack_bf16_pairs(jnp.where(row < rv_ref[i], xs, 0)).astype(BF16)
        a = jnp.dot(x, wg, preferred_element_type=F32)
        b = jnp.dot(x, wu, preferred_element_type=F32)
        act = (a * jax.nn.sigmoid(a)) * b
        y = jnp.dot(act.astype(BF16), wd, preferred_element_type=F32)
        ys_ref[...] = _pack_bf16_pairs(y)

    if publish:
        @pl.when(i == pl.num_programs(0) - 1)
        def _():
            for cp in publish_copies(te_ref[nv_ref[0] - 1]):
                cp.wait()


def _moe_grouped(tile_expert, n_valid, rows_valid, next_expert, run_parity, xs, wg, wu, wd, tm, publish):
    p, half = xs.shape
    ne, d, f = wg.shape

    def live_tile(i, te, nv, *_):
        return (jnp.minimum(i, nv[0] - 1), 0)

    slot_dtype = F32 if publish else BF16
    scratch = [pltpu.VMEM((2, d, f), slot_dtype), pltpu.VMEM((2, d, f), slot_dtype), pltpu.VMEM((2, f, d), slot_dtype)]
    out_specs = [pl.BlockSpec((tm, half), live_tile)]
    out_shape = [jax.ShapeDtypeStruct((p, half), jnp.int32)]
    if publish:
        scratch += [pltpu.VMEM((d, f), BF16), pltpu.VMEM((d, f), BF16), pltpu.VMEM((f, d), BF16)]
        out_specs += [pl.BlockSpec(memory_space=pl.ANY)] * 3
        out_shape += [jax.ShapeDtypeStruct(w.shape, BF16) for w in (wg, wu, wd)]
    scratch += [pltpu.SemaphoreType.DMA((3 * W_CHUNKS,))]
    if publish:
        scratch += [pltpu.SemaphoreType.DMA((3,))]
    grid_spec = pltpu.PrefetchScalarGridSpec(
        num_scalar_prefetch=5,
        grid=(p // tm,),
        in_specs=[pl.BlockSpec((tm, half), live_tile)] + [pl.BlockSpec(memory_space=pl.ANY)] * 3,
        out_specs=out_specs,
        scratch_shapes=scratch)
    res = pl.pallas_call(
        functools.partial(_moe_grouped_kernel, publish=publish),
        grid_spec=grid_spec,
        out_shape=out_shape,
        compiler_params=pltpu.CompilerParams(dimension_semantics=("arbitrary",),
                                             vmem_limit_bytes=VMEM_LIMIT),
        name="moe_grouped",
    )(tile_expert, n_valid, rows_valid, next_expert, run_parity, xs, wg, wu, wd)
    return res[0], tuple(res[1:])


def _moe_combine_kernel(x2_ref, route_ref, y0_ref, y1_ref, out_ref):
    route = route_ref[...]
    out_ref[...] = (x2_ref[...] + route[:, 2:3] * _unpack_bf16_pairs(y0_ref[...])
                    + route[:, 3:4] * _unpack_bf16_pairs(y1_ref[...]))


def _moe_combine(x2, route, gathered, tm):
    n, d = x2.shape
    nt = n // tm
    return pl.pallas_call(
        _moe_combine_kernel,
        grid=(nt,),
        in_specs=[pl.BlockSpec((tm, d), lambda i: (i, 0)),
                  pl.BlockSpec((tm, LANES), lambda i: (i, 0)),
                  pl.BlockSpec((tm, d // 2), lambda i: (i, 0)),
                  pl.BlockSpec((tm, d // 2), lambda i: (i + nt, 0))],
        out_specs=pl.BlockSpec((tm, d), lambda i: (i, 0)),
        out_shape=jax.ShapeDtypeStruct((n, d), F32),
        compiler_params=pltpu.CompilerParams(dimension_semantics=("arbitrary",),
                                             vmem_limit_bytes=VMEM_LIMIT),
        name="moe_combine",
    )(x2, route, gathered, gathered)


def _moe(x2, hn_packed, route, info, counts, weights, tm_rows, publish):
    n = x2.shape[0]
    p = (TOP_K * n // tm_rows + N_EXPERTS) * tm_rows
    n_tiles = p // tm_rows
    cnt = counts[0, :N_EXPERTS].astype(jnp.int32)
    tiles_per_e = jnp.maximum((cnt + tm_rows - 1) // tm_rows, 1 if publish else 0)
    e_ids = jnp.arange(N_EXPERTS, dtype=jnp.int32)
    tile_end = jnp.sum(jnp.where(e_ids[None, :] <= e_ids[:, None], tiles_per_e[None, :], 0), axis=1)
    tile_start = tile_end - tiles_per_e
    n_valid = tile_end[-1:]
    tile_ids = jnp.arange(n_tiles, dtype=jnp.int32)
    tile_expert = jnp.minimum(jnp.sum((tile_end[None, :] <= tile_ids[:, None]).astype(jnp.int32), axis=1),
                              N_EXPERTS - 1)
    mine = tile_expert[:, None] == e_ids[None, :]
    rows_left = jnp.sum(jnp.where(mine, cnt - (tile_ids[:, None] - tile_start) * tm_rows, 0), axis=1)
    rows_valid = jnp.clip(rows_left, 0, tm_rows).astype(jnp.int32)
    has_tiles = tiles_per_e > 0
    later = has_tiles[None, :] & (e_ids[None, :] > e_ids[:, None])
    next_of_e = jnp.min(jnp.where(later, e_ids[None, :], N_EXPERTS), axis=1)
    next_of_e = jnp.where(next_of_e < N_EXPERTS, next_of_e, -1)
    runs_before_e = jnp.sum((has_tiles[None, :] & (e_ids[None, :] < e_ids[:, None])).astype(jnp.int32), axis=1)
    next_expert = jnp.sum(jnp.where(mine, next_of_e, 0), axis=1).astype(jnp.int32)
    run_parity = jnp.sum(jnp.where(mine, runs_before_e % 2, 0), axis=1).astype(jnp.int32)
    eidx = info[0:TOP_K].astype(jnp.int32)
    row_start = jnp.sum(jnp.where(eidx[None] == e_ids[:, None, None], (tile_start * tm_rows)[:, None, None], 0),
                        axis=0)
    pos = (row_start + info[4:4 + TOP_K].astype(jnp.int32)).reshape(-1)

    xs = _sc_scatter_rows(hn_packed, pos, p)
    ys, w_bf16 = _moe_grouped(tile_expert, n_valid, rows_valid, next_expert, run_parity, xs, *weights, tm_rows,
                              publish)
    back = _sc_gather_rows(ys, pos)
    return _moe_combine(x2, route, back, min(TM_COMBINE, n)), w_bf16


def _block_diag(n, width):
    idx = np.arange(n) // width
    return jnp.asarray((idx[:, None] == idx[None, :]).astype(np.float32), dtype=BF16)


def _kv_feature_major(cache):
    nb, wb = cache.shape[:2]
    return jnp.transpose(cache, (0, 2, 3, 1)).reshape(nb, KV_DIM, wb)


def _strict_lower(n):
    r = np.arange(n)
    return jnp.asarray((r[:, None] > r[None, :]).astype(np.float32), dtype=BF16)


def _alibi_slopes():
    return np.exp2(-8.0 * np.arange(1, N_HEADS + 1, dtype=np.float32) / N_HEADS).astype(np.float32)


def _prompt_bias():
    i = np.arange(BLOCK)[:, None]
    s = np.arange(2 * BLOCK)[None, :]
    dist = (i + BLOCK - s).astype(np.float32)
    mask = (dist >= 0) & (dist < WINDOW)
    first = mask & (s >= BLOCK)
    slopes = _alibi_slopes()[:, None, None]
    reg = np.where(mask[None], -slopes * dist[None], np.float32(NEG_INF))
    fst = np.where(first[None], -slopes * dist[None], np.float32(NEG_INF))
    return jnp.asarray(np.concatenate([reg, fst], axis=0).astype(np.float32))


def _sample_bias(ts, wb):
    i = np.arange(ts)[:, None]
    s = np.arange(wb + ts)[None, :]
    dist = (i + wb - s).astype(np.float32)
    mask = (dist >= 0) & (dist < WINDOW)
    slopes = _alibi_slopes()[:, None, None]
    b = np.where(mask[None], -slopes * dist[None], np.float32(NEG_INF)).astype(np.float32)
    return jnp.asarray(b.reshape(N_HEADS * ts, wb + ts))


def kernel(x_prompt, x_sample, cache_swa_k, cache_swa_v, state_conv, cache_mem_k, cache_mem_v, mem_prompt, norm_mix_g, w_in, q_norm_g, k_norm_g, attn_sinks, conv_dw_w, conv_dw_b, conv_ln_g, conv_ln_b, w_out, norm_xa_g, norm_mem_g, w_mq, w_mk, w_mv, mq_norm_g, mk_norm_g, w_mo, norm_ffn_g, w_router_group, b_router_group, w_router_expert, b_router_expert, w_exp_gate, w_exp_up, w_exp_down):
    depth = w_in.shape[0]
    bp, sp, d = x_prompt.shape
    nb, ts, _ = x_sample.shape
    wb = cache_swa_k.shape[2]
    mlen = mem_prompt.shape[1]
    assert d == D_MODEL and wb == WINDOW and sp % TM_PROMPT == 0 and sp % TM_CROSS == 0 and nb % SEQ_PER_STEP == 0

    bdq = _block_diag(MXU_TILE, HEAD_DIM)
    bdk = _block_diag(KV_DIM, HEAD_DIM)
    bias_p = _prompt_bias()
    bias_s = _sample_bias(ts, wb)
    lane_lo = (np.arange(D_ATTN) % LANES) < HALF
    row = lambda a: a.reshape(1, -1).astype(F32)

    xp = x_prompt
    xs = x_sample.reshape(nb * ts, d)
    kp_l, vp_l, cp_l, mkp_l, mvp_l, ks_l, vs_l, cs_l = [], [], [], [], [], [], [], []
    for l in range(depth):
        gq = jnp.tile(q_norm_g[l].astype(F32), N_HEADS) * (HEAD_DIM ** -0.5)
        gqlo = jnp.where(lane_lo, gq, 0.0).reshape(1, -1)
        gqhi = jnp.where(lane_lo, 0.0, gq).reshape(1, -1)
        gk = jnp.tile(k_norm_g[l].astype(F32), N_KV_HEADS).reshape(1, -1)
        sinks = attn_sinks[l].astype(F32)
        sinkrow = jnp.repeat(sinks, ts).reshape(N_HEADS * ts, 1)
        win = w_in[l].astype(BF16)
        wout = w_out[l].astype(BF16)
        wmq = w_mq[l].astype(BF16)
        wmo = w_mo[l].astype(BF16)
        gmq = (jnp.tile(mq_norm_g[l].astype(F32), MEM_HEADS) * (MEM_HEAD_DIM ** -0.5)).reshape(1, -1)
        gmk = jnp.tile(mk_norm_g[l].astype(F32), MEM_HEADS).reshape(1, -1)
        w_r = jnp.concatenate([w_router_expert[l], w_router_group[l],
                               jnp.zeros((d, LANES - N_EXPERTS - N_GROUPS), F32)], axis=1).astype(BF16)
        b_r = jnp.concatenate([b_router_expert[l], b_router_group[l],
                               jnp.zeros((LANES - N_EXPERTS - N_GROUPS,), F32)]).reshape(1, -1).astype(F32)
        wg, wu, wd = w_exp_gate[l], w_exp_up[l], w_exp_down[l]

        mix_consts = (row(norm_mix_g[l]), win, bdq, bdk, gqlo, gqhi, gk)
        conv_consts = (conv_dw_w[l].astype(F32), row(conv_dw_b[l]), row(conv_ln_g[l]), row(conv_ln_b[l]), wout)
        tail_consts = (wmo, row(norm_ffn_g[l]), w_r, b_r)

        x1p, kp, vp, cp = _mixer_prompt(xp, sinks, mix_consts + (bias_p,) + conv_consts, TM_PROMPT)
        mk, mv = _memory_kv(mem_prompt.reshape(bp * mlen, d), row(norm_mem_g[l]),
                            w_mk[l].astype(BF16), w_mv[l].astype(BF16), gmk, min(256, bp * mlen))
        mk = mk.reshape(bp, mlen, d)
        mv = mv.reshape(bp, mlen, d)
        x2p, hnp, routep, infop, cntp = _cross_prompt(x1p, mk, mv, (row(norm_xa_g[l]), wmq, gmq) + tail_consts
                                               + (_strict_lower(TM_CROSS),), TM_CROSS)
        xp, w_bf16 = _moe(x2p.reshape(bp * sp, d), hnp.reshape(bp * sp, d // 2), routep.reshape(bp * sp, LANES),
                          infop, cntp, (wg, wu, wd), TM_ROWS_PROMPT, True)
        xp = xp.reshape(bp, sp, d)
        kp_l.append(kp.reshape(bp, BLOCK, N_KV_HEADS, HEAD_DIM))
        vp_l.append(vp.reshape(bp, BLOCK, N_KV_HEADS, HEAD_DIM))
        cp_l.append(cp)
        mkp_l.append(mk.reshape(bp, mlen, MEM_HEADS, MEM_HEAD_DIM))
        mvp_l.append(mv.reshape(bp, mlen, MEM_HEADS, MEM_HEAD_DIM))

        x1s, qm, ksn, vsn, csn = _mixer_sample(
            xs, _kv_feature_major(cache_swa_k[l]), _kv_feature_major(cache_swa_v[l]),
            jnp.transpose(state_conv[l], (1, 0, 2)),
            mix_consts + (bias_s, sinkrow) + conv_consts + (row(norm_xa_g[l]), wmq, gmq), SEQ_PER_STEP, ts)
        o_s = _mem_attend_sample(qm, _mem_cache_rows(cache_mem_k[l]), _mem_cache_rows(cache_mem_v[l]),
                                 MEM_SEQ_PER_STEP, ts)
        tms = min(256, nb * ts)
        x2s, hns, routes, infos, cnts = _cross_tail_call(o_s, x1s, tail_consts + (_strict_lower(tms),), tms)
        xs, _ = _moe(x2s, hns, routes, infos, cnts, w_bf16, TM_ROWS_SAMPLE, False)
        ks_l.append(jnp.transpose(ksn.reshape(nb, N_KV_HEADS, HEAD_DIM, wb), (0, 3, 1, 2)))
        vs_l.append(jnp.transpose(vsn.reshape(nb, N_KV_HEADS, HEAD_DIM, wb), (0, 3, 1, 2)))
        cs_l.append(jnp.transpose(csn, (1, 0, 2)))

    st = lambda xs_: jnp.stack(xs_, axis=0)
    return (xp, xs.reshape(nb, ts, d), st(kp_l), st(vp_l), st(cp_l), st(mkp_l), st(mvp_l),
            st(ks_l), st(vs_l), st(cs_l))
```

```python
import functools

import numpy as np
import jax
import jax.numpy as jnp
from jax import lax
from jax.experimental import pallas as pl
from jax.experimental.pallas import tpu as pltpu
from jax.experimental.pallas import tpu_sc as plsc

F32 = jnp.float32
BF16 = jnp.bfloat16

D_MODEL = 1024
D_ATTN = 512
D_CONV = 512
HEAD_DIM = 64
N_HEADS = 8
N_KV_HEADS = 2
KV_DIM = N_KV_HEADS * HEAD_DIM
HEADS_PER_KV = N_HEADS // N_KV_HEADS
WINDOW = 128
BLOCK = 128
CONV_WIDTH = 31
CONV_PAST = CONV_WIDTH - 1
MEM_HEADS = 4
MEM_HEAD_DIM = 256
N_GROUPS = 4
EXPERTS_PER_GROUP = 8
N_EXPERTS = 32
EPS = 1e-6
NEG_INF = -1e30

LANES = 128
HALF = LANES // 2
MXU_TILE = 256
SUBLANES = 8
CONV_PAD = 32
CONV_ROWS = 64
VMEM_LIMIT = 56 * 1024 * 1024

TM_PROMPT = 512
TM_CROSS = 1024
SEQ_PER_STEP = 32
MEM_SEQ_PER_STEP = 8
TOP_K = 2
RANK_BLOCK = 256
ROUTE_ROWS = 8
SC_CORES = 2
SC_WORKERS = 32
SC_ROWS_PER_STEP = 64
TM_ROWS_PROMPT = 512
TM_ROWS_SAMPLE = 128
TM_COMBINE = 1024


def _rms(x, g):
    ms = jnp.mean(x * x, axis=-1, keepdims=True)
    return x * lax.rsqrt(ms + EPS) * g


def _group_mean_sq(x, bd_ref, width):
    x2 = x * x
    hi = x2.astype(BF16)
    lo = (x2 - hi.astype(F32)).astype(BF16)
    bd = bd_ref[...]
    w = bd.shape[0]
    parts = [jnp.dot(hi[:, c:c + w], bd, preferred_element_type=F32)
             + jnp.dot(lo[:, c:c + w], bd, preferred_element_type=F32) for c in range(0, x.shape[1], w)]
    return jnp.concatenate(parts, axis=1) * (1.0 / width)


def _mixer_proj(x, ng_ref, win_ref, bdq_ref, bdk_ref, gqlo_ref, gqhi_ref, gk_ref):
    h = _rms(x, ng_ref[...]).astype(BF16)
    p = jnp.dot(h, win_ref[...], preferred_element_type=F32)
    q = p[:, :D_ATTN]
    k = p[:, D_ATTN:D_ATTN + KV_DIM]
    v = p[:, D_ATTN + KV_DIM:D_ATTN + 2 * KV_DIM]
    ua = p[:, D_ATTN + 2 * KV_DIM:D_ATTN + 2 * KV_DIM + D_CONV]
    ub = p[:, D_ATTN + 2 * KV_DIM + D_CONV:]
    qn = q * lax.rsqrt(_group_mean_sq(q, bdq_ref, HEAD_DIM) + EPS)
    q_lo = qn * gqlo_ref[...]
    q_hi = qn * gqhi_ref[...]
    kn = k * lax.rsqrt(_group_mean_sq(k, bdk_ref, HEAD_DIM) + EPS) * gk_ref[...]
    u = ua * jax.nn.sigmoid(ub)
    return q_lo, q_hi, kn, v, u


def _dup_halves(x):
    lo = lax.broadcasted_iota(jnp.int32, x.shape, 1) < HALF
    xr = pltpu.roll(x, HALF, axis=1)
    return jnp.where(lo, x, xr), jnp.where(lo, xr, x)


def _sink_softmax(s, sink):
    m = jnp.maximum(jnp.max(s, axis=-1, keepdims=True), sink)
    p = jnp.exp(s - m)
    denom = jnp.sum(p, axis=-1, keepdims=True) + jnp.exp(sink - m)
    return p * (1.0 / denom)


def _conv_ln_silu(y, cb_ref, lg_ref, lb_ref):
    y = y + cb_ref[...]
    mu = jnp.mean(y, axis=-1, keepdims=True)
    yc = y - mu
    yn = yc * lax.rsqrt(jnp.mean(yc * yc, axis=-1, keepdims=True) + EPS)
    z = yn * lg_ref[...] + lb_ref[...]
    return z * jax.nn.sigmoid(z)


def _mem_query(x1, xag_ref, wmq_ref, gmq_ref):
    h = _rms(x1, xag_ref[...]).astype(BF16)
    q = jnp.dot(h, wmq_ref[...], preferred_element_type=F32)
    parts = []
    for hd in range(MEM_HEADS):
        qh = q[:, hd * MEM_HEAD_DIM:(hd + 1) * MEM_HEAD_DIM]
        parts.append(qh * lax.rsqrt(jnp.mean(qh * qh, axis=-1, keepdims=True) + EPS))
    return jnp.concatenate(parts, axis=1) * gmq_ref[...]


def _mem_attend(q, k_heads, v_heads):
    outs = []
    for hd in range(MEM_HEADS):
        sl = slice(hd * MEM_HEAD_DIM, (hd + 1) * MEM_HEAD_DIM)
        s = lax.dot_general(q[:, sl].astype(BF16), k_heads[hd], (((1,), (1,)), ((), ())),
                            preferred_element_type=F32)
        m = jnp.max(s, axis=-1, keepdims=True)
        p = jnp.exp(s - m)
        p = p * (1.0 / jnp.sum(p, axis=-1, keepdims=True))
        outs.append(jnp.dot(p.astype(BF16), v_heads[hd], preferred_element_type=F32))
    return jnp.concatenate(outs, axis=1)


def _split_heads(x):
    return [x[:, hd * MEM_HEAD_DIM:(hd + 1) * MEM_HEAD_DIM] for hd in range(MEM_HEADS)]


def _pack_bf16_pairs(y):
    n = y.shape[1] // 2
    bits = pltpu.bitcast(y.astype(BF16).astype(F32), jnp.int32)
    return (bits[:, :n] & jnp.int32(-65536)) | lax.shift_right_logical(bits[:, n:], jnp.int32(16))


def _unpack_bf16_pairs(w):
    hi = pltpu.bitcast(w & jnp.int32(-65536), F32)
    lo = pltpu.bitcast(lax.shift_left(w, jnp.int32(16)), F32)
    return jnp.concatenate([hi, lo], axis=1)


def _cross_tail(o, x1, wmo_ref, fg_ref, wr_ref, br_ref, tri_ref, base):
    x2 = x1 + jnp.dot(o.astype(BF16), wmo_ref[...], preferred_element_type=F32)
    hn_f = _rms(x2, fg_ref[...])
    hn = hn_f.astype(BF16)
    logits = jnp.dot(hn, wr_ref[...], preferred_element_type=F32) + br_ref[...]
    lane = lax.broadcasted_iota(jnp.int32, logits.shape, 1)
    is_g = (lane >= N_EXPERTS) & (lane < N_EXPERTS + N_GROUPS)
    lg = jnp.where(is_g, logits, NEG_INF)
    gmax = jnp.max(lg, axis=-1, keepdims=True)
    gsel = jnp.min(jnp.where(lg == gmax, lane, 2 * LANES), axis=-1, keepdims=True) - N_EXPERTS
    pg_sel = 1.0 / jnp.sum(jnp.exp(lg - gmax), axis=-1, keepdims=True)
    in_grp = (lane >= gsel * EXPERTS_PER_GROUP) & (lane < (gsel + 1) * EXPERTS_PER_GROUP)
    le = jnp.where(in_grp, logits, NEG_INF)
    top1 = jnp.max(le, axis=-1, keepdims=True)
    idx1 = jnp.min(jnp.where(le == top1, lane, 2 * LANES), axis=-1, keepdims=True)
    le2 = jnp.where(lane == idx1, NEG_INF, le)
    top2 = jnp.max(le2, axis=-1, keepdims=True)
    idx2 = jnp.min(jnp.where(le2 == top2, lane, 2 * LANES), axis=-1, keepdims=True)
    e2 = jnp.exp(top2 - top1)
    inv = 1.0 / (1.0 + e2)
    gate1 = pg_sel * inv
    gate2 = pg_sel * (e2 * inv)
    used = jnp.where((lane == idx1) | (lane == idx2), 1.0, 0.0)
    blk = tri_ref.shape[0]
    used_b = used.astype(BF16)
    parts, run = [], base
    for r0 in range(0, used.shape[0], blk):
        parts.append(jnp.dot(tri_ref[...], used_b[r0:r0 + blk], preferred_element_type=F32) + run)
        run = run + jnp.sum(used[r0:r0 + blk], axis=0, keepdims=True)
    before = jnp.concatenate(parts, axis=0)
    rank1 = jnp.sum(jnp.where(lane == idx1, before, 0.0), axis=-1, keepdims=True)
    rank2 = jnp.sum(jnp.where(lane == idx2, before, 0.0), axis=-1, keepdims=True)
    route = jnp.zeros_like(logits)
    for pos, val in enumerate((idx1.astype(F32), idx2.astype(F32), gate1, gate2, rank1, rank2)):
        route = jnp.where(lane == pos, val, route)
    info = jnp.transpose(route)[0:ROUTE_ROWS, :]
    return x2, _pack_bf16_pairs(hn_f), route, info, run


def _mixer_prompt_kernel(sink_ref, x_ref, ng_ref, win_ref, bdq_ref, bdk_ref, gqlo_ref, gqhi_ref, gk_ref,
                         bias_ref, cw_ref, cb_ref, lg_ref, lb_ref, wout_ref,
                         x1_ref, ko_ref, vo_ref, co_ref,
                         kband, vband, uext, ushift, *, tm, nt):
    t = pl.program_id(1)

    @pl.when(t == 0)
    def _():
        kband[0:BLOCK, :] = jnp.zeros((BLOCK, KV_DIM), F32)
        vband[0:BLOCK, :] = jnp.zeros((BLOCK, KV_DIM), F32)
        uext[0:CONV_PAD, :] = jnp.zeros((CONV_PAD, D_CONV), F32)

    x = x_ref[...]
    q_lo, q_hi, kn, v, u = _mixer_proj(x, ng_ref, win_ref, bdq_ref, bdk_ref, gqlo_ref, gqhi_ref, gk_ref)
    kband[BLOCK:BLOCK + tm, :] = kn
    vband[BLOCK:BLOCK + tm, :] = v
    uext[CONV_PAD:CONV_PAD + tm, :] = u

    @pl.when(t == nt - 1)
    def _():
        ko_ref[...] = kn[tm - BLOCK:, :]
        vo_ref[...] = v[tm - BLOCK:, :]
        co_ref[...] = uext[pl.ds(CONV_PAD + tm - CONV_PAST, CONV_PAST), :]

    kd = [a.astype(BF16) for a in _dup_halves(kband[...])]
    vd = _dup_halves(vband[...])
    lo = lax.broadcasted_iota(jnp.int32, vd[0].shape, 1) < HALF
    v_lo = [jnp.where(lo, a, 0.0).astype(BF16) for a in vd]
    v_hi = [jnp.where(lo, 0.0, a).astype(BF16) for a in vd]
    q_lo = q_lo.astype(BF16)
    q_hi = q_hi.astype(BF16)
    attn_blocks = []
    for j in range(tm // BLOCK):
        rows = slice(j * BLOCK, (j + 1) * BLOCK)
        keys = slice(j * BLOCK, j * BLOCK + 2 * BLOCK)
        bias_base = jnp.where(t == 0, N_HEADS, 0) if j == 0 else 0
        tiles = []
        for c in range(N_KV_HEADS):
            q4 = jnp.concatenate(
                [(q_lo if (HEADS_PER_KV * c + a) % 2 == 0 else q_hi)[rows,
                  ((HEADS_PER_KV * c + a) // 2) * LANES:((HEADS_PER_KV * c + a) // 2 + 1) * LANES]
                 for a in range(HEADS_PER_KV)], axis=0)
            s_all = lax.dot_general(q4, kd[c][keys], (((1,), (1,)), ((), ())), preferred_element_type=F32)
            ps = []
            for a in range(HEADS_PER_KV):
                hd = HEADS_PER_KV * c + a
                s = s_all[a * BLOCK:(a + 1) * BLOCK] + bias_ref[bias_base + hd]
                ps.append(_sink_softmax(s, sink_ref[hd]).astype(BF16))
            vstack = jnp.concatenate([v_lo[c][keys], v_hi[c][keys]], axis=0)
            for i2 in range(HEADS_PER_KV // 2):
                pp = jnp.concatenate([ps[2 * i2], ps[2 * i2 + 1]], axis=1)
                tiles.append(jnp.dot(pp, vstack, preferred_element_type=F32))
        attn_blocks.append(jnp.concatenate(tiles, axis=1))
    attn = jnp.concatenate(attn_blocks, axis=0)

    off = CONV_PAD - CONV_PAST
    span = tm + CONV_PAD - SUBLANES
    for s in range(1, SUBLANES):
        ushift[s - 1] = uext[pl.ds(s, span), :]
    chunks = []
    for lc in range(D_CONV // LANES):
        ls = slice(lc * LANES, (lc + 1) * LANES)
        accs = [jnp.zeros((CONV_ROWS, LANES), F32) for _ in range(tm // CONV_ROWS)]
        for j in range(CONV_WIDTH):
            s, a = (off + j) % SUBLANES, (off + j) // SUBLANES
            wj = cw_ref[j:j + 1, ls]
            for rc in range(tm // CONV_ROWS):
                r0 = rc * CONV_ROWS + a * SUBLANES
                tap = uext[r0:r0 + CONV_ROWS, ls] if s == 0 else ushift[s - 1, r0:r0 + CONV_ROWS, ls]
                accs[rc] = accs[rc] + wj * tap
        chunks.append(jnp.concatenate(accs, axis=0))
    y = _conv_ln_silu(jnp.concatenate(chunks, axis=1), cb_ref, lg_ref, lb_ref)

    x1_ref[...] = (x + jnp.dot(attn.astype(BF16), wout_ref[0:D_ATTN, :], preferred_element_type=F32)
                   + jnp.dot(y.astype(BF16), wout_ref[D_ATTN:, :], preferred_element_type=F32))

    kband[0:BLOCK, :] = kband[tm:tm + BLOCK, :]
    vband[0:BLOCK, :] = vband[tm:tm + BLOCK, :]
    uext[0:CONV_PAD, :] = uext[tm:tm + CONV_PAD, :]


def _const_spec(shape):
    nd = len(shape)
    return pl.BlockSpec(shape, lambda *_: (0,) * nd)


def _mixer_prompt(x, sinks, consts, tm):
    b, s, d = x.shape
    nt = s // tm
    (ng, win, bdq, bdk, gqlo, gqhi, gk, bias_p, cw, cb, lg, lb, wout) = consts
    kern = functools.partial(_mixer_prompt_kernel, tm=tm, nt=nt)
    cspecs = [_const_spec(a.shape) for a in consts]
    return pl.pallas_call(
        kern,
        grid=(b, nt),
        in_specs=[pl.BlockSpec(memory_space=pltpu.SMEM),
                  pl.BlockSpec((None, tm, d), lambda i, j: (i, j, 0))] + cspecs,
        out_specs=[pl.BlockSpec((None, tm, d), lambda i, j: (i, j, 0)),
                   pl.BlockSpec((None, BLOCK, KV_DIM), lambda i, j: (i, 0, 0)),
                   pl.BlockSpec((None, BLOCK, KV_DIM), lambda i, j: (i, 0, 0)),
                   pl.BlockSpec((None, CONV_PAST, D_CONV), lambda i, j: (i, 0, 0))],
        out_shape=[jax.ShapeDtypeStruct((b, s, d), F32),
                   jax.ShapeDtypeStruct((b, BLOCK, KV_DIM), F32),
                   jax.ShapeDtypeStruct((b, BLOCK, KV_DIM), F32),
                   jax.ShapeDtypeStruct((b, CONV_PAST, D_CONV), F32)],
        scratch_shapes=[pltpu.VMEM((BLOCK + tm, KV_DIM), F32),
                        pltpu.VMEM((BLOCK + tm, KV_DIM), F32),
                        pltpu.VMEM((CONV_PAD + tm, D_CONV), F32),
                        pltpu.VMEM((SUBLANES - 1, tm + CONV_PAD - SUBLANES, D_CONV), F32)],
        compiler_params=pltpu.CompilerParams(dimension_semantics=("arbitrary", "arbitrary"),
                                             vmem_limit_bytes=VMEM_LIMIT),
        name="mixer_prompt",
    )(sinks, x, *consts)


def _mixer_sample_kernel(x_ref, ck_ref, cv_ref, st_ref, ng_ref, win_ref, bdq_ref, bdk_ref, gqlo_ref, gqhi_ref,
                         gk_ref, bias_ref, sinkrow_ref, cw_ref, cb_ref, lg_ref, lb_ref, wout_ref,
                         xag_ref, wmq_ref, gmq_ref,
                         x1_ref, qm_ref, ko_ref, vo_ref, co_ref,
                         kall, vall, qs, s_scr, r_scr, u_scr, y_scr, *, g, ts):
    wb = ck_ref.shape[2]
    x = x_ref[...]
    q_lo, q_hi, kn, v, u = _mixer_proj(x, ng_ref, win_ref, bdq_ref, bdk_ref, gqlo_ref, gqhi_ref, gk_ref)
    kall[:, wb:wb + ts, :] = kn.reshape(g, ts, KV_DIM)
    vall[:, wb:wb + ts, :] = v.reshape(g, ts, KV_DIM)
    for n in range(g):
        kall[n, 0:wb, :] = ck_ref[n].T
        vall[n, 0:wb, :] = cv_ref[n].T
        ko_ref[n] = kall[n, ts:wb + ts, :].T
        vo_ref[n] = vall[n, ts:wb + ts, :].T

    for lc in range(D_CONV // LANES):
        ls = slice(lc * LANES, (lc + 1) * LANES)
        u_scr[lc] = u[:, ls]
        u_t = [u_scr[lc, pl.ds(t, g, stride=ts), :] for t in range(ts)]

        def frame(r):
            return st_ref[r, :, ls] if r < CONV_PAST else u_t[r - CONV_PAST]

        for t in range(ts):
            acc = jnp.zeros((g, LANES), F32)
            for j in range(CONV_WIDTH):
                acc = acc + cw_ref[j:j + 1, ls] * frame(t + j)
            y_scr[lc, pl.ds(t, g, stride=ts), :] = acc
        for r in range(CONV_PAST):
            co_ref[r, :, ls] = frame(r + ts)
    y = _conv_ln_silu(jnp.concatenate([y_scr[lc] for lc in range(D_CONV // LANES)], axis=1), cb_ref, lg_ref, lb_ref)

    for hd in range(N_HEADS):
        tile = (q_lo if hd % 2 == 0 else q_hi)[:, (hd // 2) * LANES:(hd // 2 + 1) * LANES]
        if hd % 2 != hd // HEADS_PER_KV:
            tile = pltpu.roll(tile, HALF, axis=1)
        qs[:, hd * ts:(hd + 1) * ts, :] = tile.reshape(g, ts, LANES)
    for n in range(g):
        s_scr[n] = lax.dot_general(qs[n].astype(BF16), kall[n].astype(BF16), (((1,), (1,)), ((), ())),
                                   preferred_element_type=F32)
    s_scr[...] = _sink_softmax(s_scr[...] + bias_ref[...], sinkrow_ref[...])
    for n in range(g):
        r_scr[n] = jnp.dot(s_scr[n].astype(BF16), vall[n].astype(BF16), preferred_element_type=F32)
    r = r_scr[...].reshape(g * N_HEADS * ts, LANES)
    r_sw = pltpu.roll(r, HALF, axis=1).reshape(g, N_HEADS * ts, LANES)
    r = r.reshape(g, N_HEADS * ts, LANES)
    lo = lax.broadcasted_iota(jnp.int32, (g * ts, LANES), 1) < HALF
    heads = [(r if hd // HEADS_PER_KV == hd % 2 else r_sw)[:, hd * ts:(hd + 1) * ts, :].reshape(g * ts, LANES)
             for hd in range(N_HEADS)]
    attn = jnp.concatenate([jnp.where(lo, heads[2 * i], heads[2 * i + 1]) for i in range(N_HEADS // 2)], axis=1)

    x1 = (x + jnp.dot(attn.astype(BF16), wout_ref[0:D_ATTN, :], preferred_element_type=F32)
          + jnp.dot(y.astype(BF16), wout_ref[D_ATTN:, :], preferred_element_type=F32))
    x1_ref[...] = x1
    qm_ref[...] = _mem_query(x1, xag_ref, wmq_ref, gmq_ref)


def _mixer_sample(xs2d, ck, cv, st, consts, g, ts):
    n, d = xs2d.shape
    nb, wb = ck.shape[0], ck.shape[2]
    rows = g * ts
    cspecs = [_const_spec(a.shape) for a in consts]
    kern = functools.partial(_mixer_sample_kernel, g=g, ts=ts)
    return pl.pallas_call(
        kern,
        grid=(nb // g,),
        in_specs=[pl.BlockSpec((rows, d), lambda i: (i, 0)),
                  pl.BlockSpec((g, KV_DIM, wb), lambda i: (i, 0, 0)),
                  pl.BlockSpec((g, KV_DIM, wb), lambda i: (i, 0, 0)),
                  pl.BlockSpec((CONV_PAST, g, D_CONV), lambda i: (0, i, 0))] + cspecs,
        out_specs=[pl.BlockSpec((rows, d), lambda i: (i, 0)),
                   pl.BlockSpec((rows, d), lambda i: (i, 0)),
                   pl.BlockSpec((g, KV_DIM, wb), lambda i: (i, 0, 0)),
                   pl.BlockSpec((g, KV_DIM, wb), lambda i: (i, 0, 0)),
                   pl.BlockSpec((CONV_PAST, g, D_CONV), lambda i: (0, i, 0))],
        out_shape=[jax.ShapeDtypeStruct((n, d), F32),
                   jax.ShapeDtypeStruct((n, d), F32),
                   jax.ShapeDtypeStruct((nb, KV_DIM, wb), F32),
                   jax.ShapeDtypeStruct((nb, KV_DIM, wb), F32),
                   jax.ShapeDtypeStruct((CONV_PAST, nb, D_CONV), F32)],
        scratch_shapes=[pltpu.VMEM((g, wb + ts, KV_DIM), F32),
                        pltpu.VMEM((g, wb + ts, KV_DIM), F32),
                        pltpu.VMEM((g, N_HEADS * ts, LANES), F32),
                        pltpu.VMEM((g, N_HEADS * ts, wb + ts), F32),
                        pltpu.VMEM((g, N_HEADS * ts, LANES), F32),
                        pltpu.VMEM((D_CONV // LANES, rows, LANES), F32),
                        pltpu.VMEM((D_CONV // LANES, rows, LANES), F32)],
        compiler_params=pltpu.CompilerParams(dimension_semantics=("arbitrary",),
                                             vmem_limit_bytes=VMEM_LIMIT),
        name="mixer_sample",
    )(xs2d, ck, cv, st, *consts)


def _memory_kv_kernel(mem_ref, g_ref, wmk_ref, wmv_ref, gk_ref, k_ref, v_ref):
    h = _rms(mem_ref[...], g_ref[...]).astype(BF16)
    k = jnp.dot(h, wmk_ref[...], preferred_element_type=F32)
    parts = []
    for hd in range(MEM_HEADS):
        kh = k[:, hd * MEM_HEAD_DIM:(hd + 1) * MEM_HEAD_DIM]
        parts.append(kh * lax.rsqrt(jnp.mean(kh * kh, axis=-1, keepdims=True) + EPS))
    k_ref[...] = jnp.concatenate(parts, axis=1) * gk_ref[...]
    v_ref[...] = jnp.dot(h, wmv_ref[...], preferred_element_type=F32)


def _memory_kv(mem2d, g, wmk, wmv, gk, tm):
    n, d = mem2d.shape
    consts = (g, wmk, wmv, gk)
    return pl.pallas_call(
        _memory_kv_kernel,
        grid=(n // tm,),
        in_specs=[pl.BlockSpec((tm, d), lambda i: (i, 0))] + [_const_spec(a.shape) for a in consts],
        out_specs=[pl.BlockSpec((tm, d), lambda i: (i, 0)), pl.BlockSpec((tm, d), lambda i: (i, 0))],
        out_shape=[jax.ShapeDtypeStruct((n, d), F32), jax.ShapeDtypeStruct((n, d), F32)],
        compiler_params=pltpu.CompilerParams(dimension_semantics=("arbitrary",),
                                             vmem_limit_bytes=VMEM_LIMIT),
        name="memory_kv",
    )(mem2d, *consts)


def _cross_prompt_kernel(x1_ref, mk_ref, mv_ref, xag_ref, wmq_ref, gmq_ref, wmo_ref, fg_ref, wr_ref, br_ref, tri_ref,
                         x2_ref, hn_ref, route_ref, info_ref, cnt_ref, base):
    @pl.when((pl.program_id(0) == 0) & (pl.program_id(1) == 0))
    def _():
        base[...] = jnp.zeros_like(base)

    x1 = x1_ref[...]
    q = _mem_query(x1, xag_ref, wmq_ref, gmq_ref)
    o = _mem_attend(q, _split_heads(mk_ref[...].astype(BF16)), _split_heads(mv_ref[...].astype(BF16)))
    x2, hn, route, info, new_base = _cross_tail(o, x1, wmo_ref, fg_ref, wr_ref, br_ref, tri_ref, base[...])
    x2_ref[...] = x2
    hn_ref[...] = hn
    route_ref[...] = route
    info_ref[...] = info
    base[...] = new_base
    cnt_ref[...] = new_base


def _cross_prompt(x1, mk, mv, consts, tm):
    b, s, d = x1.shape
    m = mk.shape[1]
    cspecs = [_const_spec(a.shape) for a in consts]
    return pl.pallas_call(
        _cross_prompt_kernel,
        grid=(b, s // tm),
        in_specs=[pl.BlockSpec((None, tm, d), lambda i, j: (i, j, 0)),
                  pl.BlockSpec((None, m, d), lambda i, j: (i, 0, 0)),
                  pl.BlockSpec((None, m, d), lambda i, j: (i, 0, 0))] + cspecs,
        out_specs=[pl.BlockSpec((None, tm, d), lambda i, j: (i, j, 0)),
                   pl.BlockSpec((None, tm, d // 2), lambda i, j: (i, j, 0)),
                   pl.BlockSpec((None, tm, LANES), lambda i, j: (i, j, 0)),
                   pl.BlockSpec((ROUTE_ROWS, tm), lambda i, j: (0, i * (s // tm) + j)),
                   pl.BlockSpec((1, LANES), lambda i, j: (0, 0))],
        out_shape=[jax.ShapeDtypeStruct((b, s, d), F32),
                   jax.ShapeDtypeStruct((b, s, d // 2), jnp.int32),
                   jax.ShapeDtypeStruct((b, s, LANES), F32),
                   jax.ShapeDtypeStruct((ROUTE_ROWS, b * s), F32),
                   jax.ShapeDtypeStruct((1, LANES), F32)],
        scratch_shapes=[pltpu.VMEM((1, LANES), F32)],
        compiler_params=pltpu.CompilerParams(dimension_semantics=("arbitrary", "arbitrary"),
                                             vmem_limit_bytes=VMEM_LIMIT),
        name="cross_prompt",
    )(x1, mk, mv, *consts)


def _mem_attend_sample_kernel(qm_ref, *refs, g, ts):
    k_refs, v_refs, o_ref, s_scr = refs[:g], refs[g:2 * g], refs[2 * g], refs[2 * g + 1]
    halves = MEM_HEAD_DIM // LANES
    rows_per_pos = MEM_HEADS * halves
    mlen = k_refs[0].shape[0] // rows_per_pos

    def all_heads(ref):
        return jnp.concatenate([ref[pl.ds(c * MEM_HEADS + hd, mlen, stride=rows_per_pos), :]
                                for hd in range(MEM_HEADS) for c in range(halves)], axis=1).astype(BF16)

    rows = MEM_HEADS * ts
    own = (lax.broadcasted_iota(jnp.int32, (rows, MEM_HEADS * MEM_HEAD_DIM), 1) // MEM_HEAD_DIM
           == lax.broadcasted_iota(jnp.int32, (rows, MEM_HEADS * MEM_HEAD_DIM), 0) // ts)
    for n in range(g):
        q = qm_ref[n * ts:(n + 1) * ts, :]
        q_bd = jnp.where(own, jnp.concatenate([q] * MEM_HEADS, axis=0), 0.0).astype(BF16)
        s_scr[n] = lax.dot_general(q_bd, all_heads(k_refs[n]), (((1,), (1,)), ((), ())),
                                   preferred_element_type=F32)
    s = s_scr[...]
    p = jnp.exp(s - jnp.max(s, axis=-1, keepdims=True))
    s_scr[...] = p * (1.0 / jnp.sum(p, axis=-1, keepdims=True))
    for n in range(g):
        r = jnp.dot(s_scr[n].astype(BF16), all_heads(v_refs[n]), preferred_element_type=F32)
        o_ref[n * ts:(n + 1) * ts, :] = jnp.concatenate(
            [r[hd * ts:(hd + 1) * ts, hd * MEM_HEAD_DIM:(hd + 1) * MEM_HEAD_DIM] for hd in range(MEM_HEADS)], axis=1)


def _mem_cache_rows(cache):
    nb, mlen = cache.shape[:2]
    halves = MEM_HEAD_DIM // LANES
    return (cache.reshape(nb, mlen, MEM_HEADS, halves, LANES).transpose(0, 1, 3, 2, 4)
            .reshape(nb, mlen * halves * MEM_HEADS, LANES))


def _mem_attend_sample(qm, mk, mv, g, ts):
    n, d = qm.shape
    nb, rows = mk.shape[0], mk.shape[1]
    kern = functools.partial(_mem_attend_sample_kernel, g=g, ts=ts)

    def seq_spec(j):
        return pl.BlockSpec((None, rows, LANES), lambda i: (g * i + j, 0, 0))

    return pl.pallas_call(
        kern,
        grid=(nb // g,),
        in_specs=[pl.BlockSpec((g * ts, d), lambda i: (i, 0))] + [seq_spec(j) for j in range(g)] * 2,
        out_specs=pl.BlockSpec((g * ts, d), lambda i: (i, 0)),
        out_shape=jax.ShapeDtypeStruct((n, d), F32),
        scratch_shapes=[pltpu.VMEM((g, MEM_HEADS * ts, rows // (MEM_HEADS * (MEM_HEAD_DIM // LANES))), F32)],
        compiler_params=pltpu.CompilerParams(dimension_semantics=("arbitrary",),
                                             vmem_limit_bytes=VMEM_LIMIT),
        name="mem_attend_sample",
    )(qm, *([mk] * g), *([mv] * g))


def _cross_tail_kernel(o_ref, x1_ref, wmo_ref, fg_ref, wr_ref, br_ref, tri_ref,
                       x2_ref, hn_ref, route_ref, info_ref, cnt_ref, base):
    @pl.when(pl.program_id(0) == 0)
    def _():
        base[...] = jnp.zeros_like(base)

    x2, hn, route, info, new_base = _cross_tail(o_ref[...], x1_ref[...], wmo_ref, fg_ref, wr_ref, br_ref, tri_ref,
                                                base[...])
    x2_ref[...] = x2
    hn_ref[...] = hn
    route_ref[...] = route
    info_ref[...] = info
    base[...] = new_base
    cnt_ref[...] = new_base


def _cross_tail_call(o, x1, consts, tm):
    n, d = x1.shape
    cspecs = [_const_spec(a.shape) for a in consts]
    return pl.pallas_call(
        _cross_tail_kernel,
        grid=(n // tm,),
        in_specs=[pl.BlockSpec((tm, d), lambda i: (i, 0)), pl.BlockSpec((tm, d), lambda i: (i, 0))] + cspecs,
        out_specs=[pl.BlockSpec((tm, d), lambda i: (i, 0)),
                   pl.BlockSpec((tm, d // 2), lambda i: (i, 0)),
                   pl.BlockSpec((tm, LANES), lambda i: (i, 0)),
                   pl.BlockSpec((ROUTE_ROWS, tm), lambda i: (0, i)),
                   pl.BlockSpec((1, LANES), lambda i: (0, 0))],
        out_shape=[jax.ShapeDtypeStruct((n, d), F32),
                   jax.ShapeDtypeStruct((n, d // 2), jnp.int32),
                   jax.ShapeDtypeStruct((n, LANES), F32),
                   jax.ShapeDtypeStruct((ROUTE_ROWS, n), F32),
                   jax.ShapeDtypeStruct((1, LANES), F32)],
        scratch_shapes=[pltpu.VMEM((1, LANES), F32)],
        compiler_params=pltpu.CompilerParams(dimension_semantics=("arbitrary",),
                                             vmem_limit_bytes=VMEM_LIMIT),
        name="cross_tail",
    )(o, x1, *consts)


def _sc_rows_per_step(per_worker):
    step = min(SC_ROWS_PER_STEP, per_worker)
    assert per_worker % step == 0 and step % 8 == 0
    return step


def _sc_gather_rows(table, idx):
    nrows = idx.shape[0]
    _, width = table.shape
    assert nrows % (8 * SC_WORKERS) == 0
    per_worker = nrows // SC_WORKERS
    step = _sc_rows_per_step(per_worker // 2)
    mesh = plsc.VectorSubcoreMesh(core_axis_name="c", subcore_axis_name="s")

    @functools.partial(
        pl.kernel, mesh=mesh, out_type=jax.ShapeDtypeStruct((nrows, width), table.dtype),
        scratch_types=[pltpu.VMEM((step,), jnp.int32)] * 2 + [pltpu.VMEM((step, width), table.dtype)] * 2
        + [pltpu.SemaphoreType.DMA] * 2)
    def gather(table_hbm, idx_hbm, out_hbm, idx_a, idx_b, rows_a, rows_b, sem_a, sem_b):
        wid = lax.axis_index("s") * SC_CORES + lax.axis_index("c")
        base = wid * per_worker

        @pl.loop(0, per_worker // (2 * step))
        def _(i):
            off = base + i * (2 * step)
            pltpu.sync_copy(idx_hbm.at[pl.ds(off, step)], idx_a)
            pltpu.sync_copy(idx_hbm.at[pl.ds(off + step, step)], idx_b)
            gather_a = pltpu.async_copy(table_hbm.at[idx_a], rows_a, sem_a)
            gather_b = pltpu.async_copy(table_hbm.at[idx_b], rows_b, sem_b)
            gather_a.wait()
            write_a = pltpu.async_copy(rows_a, out_hbm.at[pl.ds(off, step)], sem_a)
            gather_b.wait()
            write_b = pltpu.async_copy(rows_b, out_hbm.at[pl.ds(off + step, step)], sem_b)
            write_a.wait()
            write_b.wait()

    return gather(table, idx)


def _sc_scatter_rows(table, pos, nrows_out):
    n, width = table.shape
    assert n % (8 * SC_WORKERS) == 0 and pos.shape == (TOP_K * n,)
    per_worker = n // SC_WORKERS
    step = _sc_rows_per_step(per_worker)
    mesh = plsc.VectorSubcoreMesh(core_axis_name="c", subcore_axis_name="s")

    @functools.partial(
        pl.kernel, mesh=mesh, out_type=jax.ShapeDtypeStruct((nrows_out, width), table.dtype),
        scratch_types=[pltpu.VMEM((step,), jnp.int32)] * TOP_K
        + [pltpu.VMEM((step, width), table.dtype), pltpu.SemaphoreType.DMA])
    def scatter(table_hbm, pos_hbm, out_hbm, *scratch):
        idx_vs, rows_v, sem = scratch[:TOP_K], scratch[TOP_K], scratch[TOP_K + 1]
        wid = lax.axis_index("s") * SC_CORES + lax.axis_index("c")
        base = wid * per_worker

        @pl.loop(0, per_worker // step)
        def _(i):
            off = base + i * step
            for k in range(TOP_K):
                pltpu.sync_copy(pos_hbm.at[pl.ds(k * n + off, step)], idx_vs[k])
            pltpu.sync_copy(table_hbm.at[pl.ds(off, step)], rows_v)
            for k in range(TOP_K):
                pltpu.async_copy(rows_v, out_hbm.at[idx_vs[k]], sem).wait()

    return scatter(table, pos)


W_CHUNKS = 4
FETCH_AHEAD = 3
FETCH_AHEAD_F32 = 2


def _expert_weight_copies(e, slot, w_hbm, w_slots, sems, chunks):
    copies = []
    for m, (src, dst) in enumerate(zip(w_hbm, w_slots)):
        rows = dst.shape[1] // chunks
        for c in range(chunks):
            sl = pl.ds(c * rows, rows)
            sem = sems.at[(slot * len(w_hbm) + m) * chunks + c]
            copies.append(pltpu.make_async_copy(src.at[e, sl, :], dst.at[slot, sl, :], sem))
    return copies


def _moe_grouped_kernel(te_ref, nv_ref, rv_ref, slot_ref, *refs, publish, ahead, chunks):
    later_refs, (xs_ref, wg_hbm, wu_hbm, wd_hbm, ys_ref), rest = refs[:ahead], refs[ahead:ahead + 5], refs[ahead + 5:]
    i = pl.program_id(0)
    n_slots = ahead + 1
    w_hbm = (wg_hbm, wu_hbm, wd_hbm)
    if publish:
        out_hbm, w_slots, w_bf16, sems, out_sems = rest[0:3], rest[3:6], rest[6:9], rest[9], rest[10]
    else:
        w_slots, sems = rest[0:3], rest[3]

    def publish_copies(e):
        return [pltpu.make_async_copy(src, dst.at[e], out_sems.at[m])
                for m, (src, dst) in enumerate(zip(w_bf16, out_hbm))]

    def fetch(e, slot):
        return _expert_weight_copies(e, slot, w_hbm, w_slots, sems, chunks)

    @pl.when(i == 0)
    def _():
        for cp in fetch(te_ref[0], slot_ref[0]):
            cp.start()
        for k in range(ahead - 1):
            @pl.when(later_refs[k][0] >= 0)
            def _():
                for cp in fetch(later_refs[k][0], (slot_ref[0] + k + 1) % n_slots):
                    cp.start()

    @pl.when(i < nv_ref[0])
    def _():
        slot = slot_ref[i]

        @pl.when((i == 0) | (te_ref[i] != te_ref[jnp.maximum(i - 1, 0)]))
        def _():
            for cp in fetch(te_ref[i], slot):
                cp.wait()
            if publish:
                @pl.when(i > 0)
                def _():
                    for cp in publish_copies(te_ref[jnp.maximum(i - 1, 0)]):
                        cp.wait()

                for src, dst in zip(w_slots, w_bf16):
                    dst[...] = src[slot].astype(BF16)
                for cp in publish_copies(te_ref[i]):
                    cp.start()

            @pl.when(later_refs[ahead - 1][i] >= 0)
            def _():
                for cp in fetch(later_refs[ahead - 1][i], (slot + ahead) % n_slots):
                    cp.start()

        wg, wu, wd = [r[...] for r in w_bf16] if publish else [r[slot] for r in w_slots]
        xs = xs_ref[...]
        row = lax.broadcasted_iota(jnp.int32, xs.shape, 0)
        x = _unpack_bf16_pairs(jnp.where(row < rv_ref[i], xs, 0)).astype(BF16)
        a = jnp.dot(x, wg, preferred_element_type=F32)
        b = jnp.dot(x, wu, preferred_element_type=F32)
        act = (a * jax.nn.sigmoid(a)) * b
        y = jnp.dot(act.astype(BF16), wd, preferred_element_type=F32)
        ys_ref[...] = _pack_bf16_pairs(y)

    if publish:
        @pl.when(i == pl.num_programs(0) - 1)
        def _():
            for cp in publish_copies(te_ref[nv_ref[0] - 1]):
                cp.wait()


def _moe_grouped(tile_expert, n_valid, rows_valid, run_slot, later_experts, xs, wg, wu, wd, tm, publish):
    p, half = xs.shape
    ne, d, f = wg.shape
    ahead = len(later_experts)
    chunks = W_CHUNKS if publish else 1

    def live_tile(i, te, nv, *_):
        return (jnp.minimum(i, nv[0] - 1), 0)

    slot_dtype = F32 if publish else BF16
    n_slots = ahead + 1
    scratch = [pltpu.VMEM((n_slots, d, f), slot_dtype), pltpu.VMEM((n_slots, d, f), slot_dtype),
               pltpu.VMEM((n_slots, f, d), slot_dtype)]
    out_specs = [pl.BlockSpec((tm, half), live_tile)]
    out_shape = [jax.ShapeDtypeStruct((p, half), jnp.int32)]
    if publish:
        scratch += [pltpu.VMEM((d, f), BF16), pltpu.VMEM((d, f), BF16), pltpu.VMEM((f, d), BF16)]
        out_specs += [pl.BlockSpec(memory_space=pl.ANY)] * 3
        out_shape += [jax.ShapeDtypeStruct(w.shape, BF16) for w in (wg, wu, wd)]
    scratch += [pltpu.SemaphoreType.DMA((n_slots * 3 * chunks,))]
    if publish:
        scratch += [pltpu.SemaphoreType.DMA((3,))]
    grid_spec = pltpu.PrefetchScalarGridSpec(
        num_scalar_prefetch=4 + ahead,
        grid=(p // tm,),
        in_specs=[pl.BlockSpec((tm, half), live_tile)] + [pl.BlockSpec(memory_space=pl.ANY)] * 3,
        out_specs=out_specs,
        scratch_shapes=scratch)
    res = pl.pallas_call(
        functools.partial(_moe_grouped_kernel, publish=publish, ahead=ahead, chunks=chunks),
        grid_spec=grid_spec,
        out_shape=out_shape,
        compiler_params=pltpu.CompilerParams(dimension_semantics=("arbitrary",),
                                             vmem_limit_bytes=VMEM_LIMIT),
        name="moe_grouped",
    )(tile_expert, n_valid, rows_valid, run_slot, *later_experts, xs, wg, wu, wd)
    return res[0], tuple(res[1:])


def _moe_combine_kernel(x2_ref, route_ref, y0_ref, y1_ref, out_ref):
    route = route_ref[...]
    out_ref[...] = (x2_ref[...] + route[:, 2:3] * _unpack_bf16_pairs(y0_ref[...])
                    + route[:, 3:4] * _unpack_bf16_pairs(y1_ref[...]))


def _moe_combine(x2, route, gathered, tm):
    n, d = x2.shape
    nt = n // tm
    return pl.pallas_call(
        _moe_combine_kernel,
        grid=(nt,),
        in_specs=[pl.BlockSpec((tm, d), lambda i: (i, 0)),
                  pl.BlockSpec((tm, LANES), lambda i: (i, 0)),
                  pl.BlockSpec((tm, d // 2), lambda i: (i, 0)),
                  pl.BlockSpec((tm, d // 2), lambda i: (i + nt, 0))],
        out_specs=pl.BlockSpec((tm, d), lambda i: (i, 0)),
        out_shape=jax.ShapeDtypeStruct((n, d), F32),
        compiler_params=pltpu.CompilerParams(dimension_semantics=("arbitrary",),
                                             vmem_limit_bytes=VMEM_LIMIT),
        name="moe_combine",
    )(x2, route, gathered, gathered)


def _moe(x2, hn_packed, route, info, counts, weights, tm_rows, publish):
    n = x2.shape[0]
    p = (TOP_K * n // tm_rows + N_EXPERTS) * tm_rows
    n_tiles = p // tm_rows
    cnt = counts[0, :N_EXPERTS].astype(jnp.int32)
    tiles_per_e = jnp.maximum((cnt + tm_rows - 1) // tm_rows, 1 if publish else 0)
    e_ids = jnp.arange(N_EXPERTS, dtype=jnp.int32)
    tile_end = jnp.sum(jnp.where(e_ids[None, :] <= e_ids[:, None], tiles_per_e[None, :], 0), axis=1)
    tile_start = tile_end - tiles_per_e
    n_valid = tile_end[-1:]
    tile_ids = jnp.arange(n_tiles, dtype=jnp.int32)
    tile_expert = jnp.minimum(jnp.sum((tile_end[None, :] <= tile_ids[:, None]).astype(jnp.int32), axis=1),
                              N_EXPERTS - 1)
    mine = tile_expert[:, None] == e_ids[None, :]
    rows_left = jnp.sum(jnp.where(mine, cnt - (tile_ids[:, None] - tile_start) * tm_rows, 0), axis=1)
    rows_valid = jnp.clip(rows_left, 0, tm_rows).astype(jnp.int32)
    ahead = FETCH_AHEAD_F32 if publish else FETCH_AHEAD
    has_tiles = tiles_per_e > 0
    run_of_e = jnp.sum((has_tiles[None, :] & (e_ids[None, :] < e_ids[:, None])).astype(jnp.int32), axis=1)
    run_of_tile = jnp.sum(jnp.where(mine, run_of_e, 0), axis=1)
    run_slot = (run_of_tile % (ahead + 1)).astype(jnp.int32)
    later_experts = []
    for k in range(1, ahead + 1):
        is_run = has_tiles[None, :] & (run_of_e[None, :] == run_of_tile[:, None] + k)
        later_experts.append(jnp.sum(jnp.where(is_run, e_ids[None, :] + 1, 0), axis=1).astype(jnp.int32) - 1)
    eidx = info[0:TOP_K].astype(jnp.int32)
    row_start = jnp.sum(jnp.where(eidx[None] == e_ids[:, None, None], (tile_start * tm_rows)[:, None, None], 0),
                        axis=0)
    pos = (row_start + info[4:4 + TOP_K].astype(jnp.int32)).reshape(-1)

    xs = _sc_scatter_rows(hn_packed, pos, p)
    ys, w_bf16 = _moe_grouped(tile_expert, n_valid, rows_valid, run_slot, later_experts, xs, *weights, tm_rows,
                              publish)
    back = _sc_gather_rows(ys, pos)
    return _moe_combine(x2, route, back, min(TM_COMBINE, n)), w_bf16


def _block_diag(n, width):
    idx = np.arange(n) // width
    return jnp.asarray((idx[:, None] == idx[None, :]).astype(np.float32), dtype=BF16)


def _kv_feature_major(cache):
    nb, wb = cache.shape[:2]
    return jnp.transpose(cache, (0, 2, 3, 1)).reshape(nb, KV_DIM, wb)


def _strict_lower(n):
    r = np.arange(n)
    return jnp.asarray((r[:, None] > r[None, :]).astype(np.float32), dtype=BF16)


def _alibi_slopes():
    return np.exp2(-8.0 * np.arange(1, N_HEADS + 1, dtype=np.float32) / N_HEADS).astype(np.float32)


def _prompt_bias():
    i = np.arange(BLOCK)[:, None]
    s = np.arange(2 * BLOCK)[None, :]
    dist = (i + BLOCK - s).astype(np.float32)
    mask = (dist >= 0) & (dist < WINDOW)
    first = mask & (s >= BLOCK)
    slopes = _alibi_slopes()[:, None, None]
    reg = np.where(mask[None], -slopes * dist[None], np.float32(NEG_INF))
    fst = np.where(first[None], -slopes * dist[None], np.float32(NEG_INF))
    return jnp.asarray(np.concatenate([reg, fst], axis=0).astype(np.float32))


def _sample_bias(ts, wb):
    i = np.arange(ts)[:, None]
    s = np.arange(wb + ts)[None, :]
    dist = (i + wb - s).astype(np.float32)
    mask = (dist >= 0) & (dist < WINDOW)
    slopes = _alibi_slopes()[:, None, None]
    b = np.where(mask[None], -slopes * dist[None], np.float32(NEG_INF)).astype(np.float32)
    return jnp.asarray(b.reshape(N_HEADS * ts, wb + ts))


def kernel(x_prompt, x_sample, cache_swa_k, cache_swa_v, state_conv, cache_mem_k, cache_mem_v, mem_prompt, norm_mix_g, w_in, q_norm_g, k_norm_g, attn_sinks, conv_dw_w, conv_dw_b, conv_ln_g, conv_ln_b, w_out, norm_xa_g, norm_mem_g, w_mq, w_mk, w_mv, mq_norm_g, mk_norm_g, w_mo, norm_ffn_g, w_router_group, b_router_group, w_router_expert, b_router_expert, w_exp_gate, w_exp_up, w_exp_down):
    depth = w_in.shape[0]
    bp, sp, d = x_prompt.shape
    nb, ts, _ = x_sample.shape
    wb = cache_swa_k.shape[2]
    mlen = mem_prompt.shape[1]
    assert d == D_MODEL and wb == WINDOW and sp % TM_PROMPT == 0 and sp % TM_CROSS == 0 and nb % SEQ_PER_STEP == 0

    bdq = _block_diag(MXU_TILE, HEAD_DIM)
    bdk = _block_diag(KV_DIM, HEAD_DIM)
    bias_p = _prompt_bias()
    bias_s = _sample_bias(ts, wb)
    lane_lo = (np.arange(D_ATTN) % LANES) < HALF
    row = lambda a: a.reshape(1, -1).astype(F32)

    xp = x_prompt
    xs = x_sample.reshape(nb * ts, d)
    kp_l, vp_l, cp_l, mkp_l, mvp_l, ks_l, vs_l, cs_l = [], [], [], [], [], [], [], []
    for l in range(depth):
        gq = jnp.tile(q_norm_g[l].astype(F32), N_HEADS) * (HEAD_DIM ** -0.5)
        gqlo = jnp.where(lane_lo, gq, 0.0).reshape(1, -1)
        gqhi = jnp.where(lane_lo, 0.0, gq).reshape(1, -1)
        gk = jnp.tile(k_norm_g[l].astype(F32), N_KV_HEADS).reshape(1, -1)
        sinks = attn_sinks[l].astype(F32)
        sinkrow = jnp.repeat(sinks, ts).reshape(N_HEADS * ts, 1)
        win = w_in[l].astype(BF16)
        wout = w_out[l].astype(BF16)
        wmq = w_mq[l].astype(BF16)
        wmo = w_mo[l].astype(BF16)
        gmq = (jnp.tile(mq_norm_g[l].astype(F32), MEM_HEADS) * (MEM_HEAD_DIM ** -0.5)).reshape(1, -1)
        gmk = jnp.tile(mk_norm_g[l].astype(F32), MEM_HEADS).reshape(1, -1)
        w_r = jnp.concatenate([w_router_expert[l], w_router_group[l],
                               jnp.zeros((d, LANES - N_EXPERTS - N_GROUPS), F32)], axis=1).astype(BF16)
        b_r = jnp.concatenate([b_router_expert[l], b_router_group[l],
                               jnp.zeros((LANES - N_EXPERTS - N_GROUPS,), F32)]).reshape(1, -1).astype(F32)
        wg, wu, wd = w_exp_gate[l], w_exp_up[l], w_exp_down[l]

        mix_consts = (row(norm_mix_g[l]), win, bdq, bdk, gqlo, gqhi, gk)
        conv_consts = (conv_dw_w[l].astype(F32), row(conv_dw_b[l]), row(conv_ln_g[l]), row(conv_ln_b[l]), wout)
        tail_consts = (wmo, row(norm_ffn_g[l]), w_r, b_r)

        x1p, kp, vp, cp = _mixer_prompt(xp, sinks, mix_consts + (bias_p,) + conv_consts, TM_PROMPT)
        mk, mv = _memory_kv(mem_prompt.reshape(bp * mlen, d), row(norm_mem_g[l]),
                            w_mk[l].astype(BF16), w_mv[l].astype(BF16), gmk, min(256, bp * mlen))
        mk = mk.reshape(bp, mlen, d)
        mv = mv.reshape(bp, mlen, d)
        x2p, hnp, routep, infop, cntp = _cross_prompt(x1p, mk, mv, (row(norm_xa_g[l]), wmq, gmq) + tail_consts
                                               + (_strict_lower(RANK_BLOCK),), TM_CROSS)
        xp, w_bf16 = _moe(x2p.reshape(bp * sp, d), hnp.reshape(bp * sp, d // 2), routep.reshape(bp * sp, LANES),
                          infop, cntp, (wg, wu, wd), TM_ROWS_PROMPT, True)
        xp = xp.reshape(bp, sp, d)
        kp_l.append(kp.reshape(bp, BLOCK, N_KV_HEADS, HEAD_DIM))
        vp_l.append(vp.reshape(bp, BLOCK, N_KV_HEADS, HEAD_DIM))
        cp_l.append(cp)
        mkp_l.append(mk.reshape(bp, mlen, MEM_HEADS, MEM_HEAD_DIM))
        mvp_l.append(mv.reshape(bp, mlen, MEM_HEADS, MEM_HEAD_DIM))

        x1s, qm, ksn, vsn, csn = _mixer_sample(
            xs, _kv_feature_major(cache_swa_k[l]), _kv_feature_major(cache_swa_v[l]),
            jnp.transpose(state_conv[l], (1, 0, 2)),
            mix_consts + (bias_s, sinkrow) + conv_consts + (row(norm_xa_g[l]), wmq, gmq), SEQ_PER_STEP, ts)
        o_s = _mem_attend_sample(qm, _mem_cache_rows(cache_mem_k[l]), _mem_cache_rows(cache_mem_v[l]),
                                 MEM_SEQ_PER_STEP, ts)
        tms = min(256, nb * ts)
        x2s, hns, routes, infos, cnts = _cross_tail_call(o_s, x1s, tail_consts + (_strict_lower(RANK_BLOCK),), tms)
        xs, _ = _moe(x2s, hns, routes, infos, cnts, w_bf16, TM_ROWS_SAMPLE, False)
        ks_l.append(jnp.transpose(ksn.reshape(nb, N_KV_HEADS, HEAD_DIM, wb), (0, 3, 1, 2)))
        vs_l.append(jnp.transpose(vsn.reshape(nb, N_KV_HEADS, HEAD_DIM, wb), (0, 3, 1, 2)))
        cs_l.append(jnp.transpose(csn, (1, 0, 2)))

    st = lambda xs_: jnp.stack(xs_, axis=0)
    return (xp, xs.reshape(nb, ts, d), st(kp_l), st(vp_l), st(cp_l), st(mkp_l), st(mvp_l),
            st(ks_l), st(vs_l), st(cs_l))
```

```python
import functools

import numpy as np
import jax
import jax.numpy as jnp
from jax import lax
from jax.experimental import pallas as pl
from jax.experimental.pallas import tpu as pltpu
from jax.experimental.pallas import tpu_sc as plsc

F32 = jnp.float32
BF16 = jnp.bfloat16

D_MODEL = 1024
D_ATTN = 512
D_CONV = 512
HEAD_DIM = 64
N_HEADS = 8
N_KV_HEADS = 2
KV_DIM = N_KV_HEADS * HEAD_DIM
HEADS_PER_KV = N_HEADS // N_KV_HEADS
WINDOW = 128
BLOCK = 128
CONV_WIDTH = 31
CONV_PAST = CONV_WIDTH - 1
MEM_HEADS = 4
MEM_HEAD_DIM = 256
N_GROUPS = 4
EXPERTS_PER_GROUP = 8
N_EXPERTS = 32
EPS = 1e-6
NEG_INF = -1e30

LANES = 128
HALF = LANES // 2
MXU_TILE = 256
SUBLANES = 8
CONV_PAD = 32
CONV_ROWS = 64
VMEM_LIMIT = 56 * 1024 * 1024

TM_PROMPT = 512
TM_CROSS = 1024
SEQ_PER_STEP = 32
MEM_SEQ_PER_STEP = 8
TOP_K = 2
RANK_BLOCK = 256
ROUTE_ROWS = 8
SC_CORES = 2
SC_WORKERS = 32
SC_ROWS_PER_STEP = 64
TM_ROWS_PROMPT = 512
TM_ROWS_SAMPLE = 128
TM_COMBINE = 1024


def _rms(x, g):
    ms = jnp.mean(x * x, axis=-1, keepdims=True)
    return x * lax.rsqrt(ms + EPS) * g


def _group_mean_sq(x, bd_ref, width):
    x2 = x * x
    hi = x2.astype(BF16)
    lo = (x2 - hi.astype(F32)).astype(BF16)
    bd = bd_ref[...]
    w = bd.shape[0]
    parts = [jnp.dot(hi[:, c:c + w], bd, preferred_element_type=F32)
             + jnp.dot(lo[:, c:c + w], bd, preferred_element_type=F32) for c in range(0, x.shape[1], w)]
    return jnp.concatenate(parts, axis=1) * (1.0 / width)


def _mixer_proj(x, ng_ref, win_ref, bdq_ref, bdk_ref, gqlo_ref, gqhi_ref, gk_ref):
    h = _rms(x, ng_ref[...]).astype(BF16)
    p = jnp.dot(h, win_ref[...], preferred_element_type=F32)
    q = p[:, :D_ATTN]
    k = p[:, D_ATTN:D_ATTN + KV_DIM]
    v = p[:, D_ATTN + KV_DIM:D_ATTN + 2 * KV_DIM]
    ua = p[:, D_ATTN + 2 * KV_DIM:D_ATTN + 2 * KV_DIM + D_CONV]
    ub = p[:, D_ATTN + 2 * KV_DIM + D_CONV:]
    qn = q * lax.rsqrt(_group_mean_sq(q, bdq_ref, HEAD_DIM) + EPS)
    q_lo = qn * gqlo_ref[...]
    q_hi = qn * gqhi_ref[...]
    kn = k * lax.rsqrt(_group_mean_sq(k, bdk_ref, HEAD_DIM) + EPS) * gk_ref[...]
    u = ua * jax.nn.sigmoid(ub)
    return q_lo, q_hi, kn, v, u


def _dup_halves(x):
    lo = lax.broadcasted_iota(jnp.int32, x.shape, 1) < HALF
    xr = pltpu.roll(x, HALF, axis=1)
    return jnp.where(lo, x, xr), jnp.where(lo, xr, x)


def _sink_softmax(s, sink):
    m = jnp.maximum(jnp.max(s, axis=-1, keepdims=True), sink)
    p = jnp.exp(s - m)
    denom = jnp.sum(p, axis=-1, keepdims=True) + jnp.exp(sink - m)
    return p * (1.0 / denom)


def _conv_ln_silu(y, cb_ref, lg_ref, lb_ref):
    y = y + cb_ref[...]
    mu = jnp.mean(y, axis=-1, keepdims=True)
    yc = y - mu
    yn = yc * lax.rsqrt(jnp.mean(yc * yc, axis=-1, keepdims=True) + EPS)
    z = yn * lg_ref[...] + lb_ref[...]
    return z * jax.nn.sigmoid(z)


def _mem_query(x1, xag_ref, wmq_ref, gmq_ref):
    h = _rms(x1, xag_ref[...]).astype(BF16)
    q = jnp.dot(h, wmq_ref[...], preferred_element_type=F32)
    parts = []
    for hd in range(MEM_HEADS):
        qh = q[:, hd * MEM_HEAD_DIM:(hd + 1) * MEM_HEAD_DIM]
        parts.append(qh * lax.rsqrt(jnp.mean(qh * qh, axis=-1, keepdims=True) + EPS))
    return jnp.concatenate(parts, axis=1) * gmq_ref[...]


def _mem_attend(q, k_heads, v_heads):
    outs = []
    for hd in range(MEM_HEADS):
        sl = slice(hd * MEM_HEAD_DIM, (hd + 1) * MEM_HEAD_DIM)
        s = lax.dot_general(q[:, sl].astype(BF16), k_heads[hd], (((1,), (1,)), ((), ())),
                            preferred_element_type=F32)
        m = jnp.max(s, axis=-1, keepdims=True)
        p = jnp.exp(s - m)
        p = p * (1.0 / jnp.sum(p, axis=-1, keepdims=True))
        outs.append(jnp.dot(p.astype(BF16), v_heads[hd], preferred_element_type=F32))
    return jnp.concatenate(outs, axis=1)


def _split_heads(x):
    return [x[:, hd * MEM_HEAD_DIM:(hd + 1) * MEM_HEAD_DIM] for hd in range(MEM_HEADS)]


def _pack_bf16_pairs(y):
    n = y.shape[1] // 2
    bits = pltpu.bitcast(y.astype(BF16).astype(F32), jnp.int32)
    return (bits[:, :n] & jnp.int32(-65536)) | lax.shift_right_logical(bits[:, n:], jnp.int32(16))


def _unpack_bf16_pairs(w):
    hi = pltpu.bitcast(w & jnp.int32(-65536), F32)
    lo = pltpu.bitcast(lax.shift_left(w, jnp.int32(16)), F32)
    return jnp.concatenate([hi, lo], axis=1)


def _cross_tail(o, x1, wmo_ref, fg_ref, wr_ref, br_ref, tri_ref, base):
    x2 = x1 + jnp.dot(o.astype(BF16), wmo_ref[...], preferred_element_type=F32)
    hn_f = _rms(x2, fg_ref[...])
    hn = hn_f.astype(BF16)
    logits = jnp.dot(hn, wr_ref[...], preferred_element_type=F32) + br_ref[...]
    lane = lax.broadcasted_iota(jnp.int32, logits.shape, 1)
    is_g = (lane >= N_EXPERTS) & (lane < N_EXPERTS + N_GROUPS)
    lg = jnp.where(is_g, logits, NEG_INF)
    gmax = jnp.max(lg, axis=-1, keepdims=True)
    gsel = jnp.min(jnp.where(lg == gmax, lane, 2 * LANES), axis=-1, keepdims=True) - N_EXPERTS
    pg_sel = 1.0 / jnp.sum(jnp.exp(lg - gmax), axis=-1, keepdims=True)
    in_grp = (lane >= gsel * EXPERTS_PER_GROUP) & (lane < (gsel + 1) * EXPERTS_PER_GROUP)
    le = jnp.where(in_grp, logits, NEG_INF)
    top1 = jnp.max(le, axis=-1, keepdims=True)
    idx1 = jnp.min(jnp.where(le == top1, lane, 2 * LANES), axis=-1, keepdims=True)
    le2 = jnp.where(lane == idx1, NEG_INF, le)
    top2 = jnp.max(le2, axis=-1, keepdims=True)
    idx2 = jnp.min(jnp.where(le2 == top2, lane, 2 * LANES), axis=-1, keepdims=True)
    e2 = jnp.exp(top2 - top1)
    inv = 1.0 / (1.0 + e2)
    gate1 = pg_sel * inv
    gate2 = pg_sel * (e2 * inv)
    used = jnp.where((lane == idx1) | (lane == idx2), 1.0, 0.0)
    blk = tri_ref.shape[0]
    used_b = used.astype(BF16)
    parts, run = [], base
    for r0 in range(0, used.shape[0], blk):
        parts.append(jnp.dot(tri_ref[...], used_b[r0:r0 + blk], preferred_element_type=F32) + run)
        run = run + jnp.sum(used[r0:r0 + blk], axis=0, keepdims=True)
    before = jnp.concatenate(parts, axis=0)
    rank1 = jnp.sum(jnp.where(lane == idx1, before, 0.0), axis=-1, keepdims=True)
    rank2 = jnp.sum(jnp.where(lane == idx2, before, 0.0), axis=-1, keepdims=True)
    route = jnp.zeros_like(logits)
    for pos, val in enumerate((idx1.astype(F32), idx2.astype(F32), gate1, gate2, rank1, rank2)):
        route = jnp.where(lane == pos, val, route)
    info = jnp.transpose(route)[0:ROUTE_ROWS, :]
    return x2, _pack_bf16_pairs(hn_f), route, info, run


def _mixer_prompt_kernel(sink_ref, x_ref, ng_ref, win_ref, bdq_ref, bdk_ref, gqlo_ref, gqhi_ref, gk_ref,
                         bias_ref, cw_ref, cb_ref, lg_ref, lb_ref, wout_ref,
                         x1_ref, ko_ref, vo_ref, co_ref,
                         kband, vband, uext, ushift, *, tm, nt):
    t = pl.program_id(1)

    @pl.when(t == 0)
    def _():
        kband[0:BLOCK, :] = jnp.zeros((BLOCK, KV_DIM), F32)
        vband[0:BLOCK, :] = jnp.zeros((BLOCK, KV_DIM), F32)
        uext[0:CONV_PAD, :] = jnp.zeros((CONV_PAD, D_CONV), F32)

    x = x_ref[...]
    q_lo, q_hi, kn, v, u = _mixer_proj(x, ng_ref, win_ref, bdq_ref, bdk_ref, gqlo_ref, gqhi_ref, gk_ref)
    kband[BLOCK:BLOCK + tm, :] = kn
    vband[BLOCK:BLOCK + tm, :] = v
    uext[CONV_PAD:CONV_PAD + tm, :] = u

    @pl.when(t == nt - 1)
    def _():
        ko_ref[...] = kn[tm - BLOCK:, :]
        vo_ref[...] = v[tm - BLOCK:, :]
        co_ref[...] = uext[pl.ds(CONV_PAD + tm - CONV_PAST, CONV_PAST), :]

    kd = [a.astype(BF16) for a in _dup_halves(kband[...])]
    vd = _dup_halves(vband[...])
    lo = lax.broadcasted_iota(jnp.int32, vd[0].shape, 1) < HALF
    v_lo = [jnp.where(lo, a, 0.0).astype(BF16) for a in vd]
    v_hi = [jnp.where(lo, 0.0, a).astype(BF16) for a in vd]
    q_lo = q_lo.astype(BF16)
    q_hi = q_hi.astype(BF16)
    attn_blocks = []
    for j in range(tm // BLOCK):
        rows = slice(j * BLOCK, (j + 1) * BLOCK)
        keys = slice(j * BLOCK, j * BLOCK + 2 * BLOCK)
        bias_base = jnp.where(t == 0, N_HEADS, 0) if j == 0 else 0
        tiles = []
        for c in range(N_KV_HEADS):
            q4 = jnp.concatenate(
                [(q_lo if (HEADS_PER_KV * c + a) % 2 == 0 else q_hi)[rows,
                  ((HEADS_PER_KV * c + a) // 2) * LANES:((HEADS_PER_KV * c + a) // 2 + 1) * LANES]
                 for a in range(HEADS_PER_KV)], axis=0)
            s_all = lax.dot_general(q4, kd[c][keys], (((1,), (1,)), ((), ())), preferred_element_type=F32)
            ps = []
            for a in range(HEADS_PER_KV):
                hd = HEADS_PER_KV * c + a
                s = s_all[a * BLOCK:(a + 1) * BLOCK] + bias_ref[bias_base + hd]
                ps.append(_sink_softmax(s, sink_ref[hd]).astype(BF16))
            vstack = jnp.concatenate([v_lo[c][keys], v_hi[c][keys]], axis=0)
            for i2 in range(HEADS_PER_KV // 2):
                pp = jnp.concatenate([ps[2 * i2], ps[2 * i2 + 1]], axis=1)
                tiles.append(jnp.dot(pp, vstack, preferred_element_type=F32))
        attn_blocks.append(jnp.concatenate(tiles, axis=1))
    attn = jnp.concatenate(attn_blocks, axis=0)

    off = CONV_PAD - CONV_PAST
    span = tm + CONV_PAD - SUBLANES
    for s in range(1, SUBLANES):
        ushift[s - 1] = uext[pl.ds(s, span), :]
    chunks = []
    for lc in range(D_CONV // LANES):
        ls = slice(lc * LANES, (lc + 1) * LANES)
        accs = [jnp.zeros((CONV_ROWS, LANES), F32) for _ in range(tm // CONV_ROWS)]
        for j in range(CONV_WIDTH):
            s, a = (off + j) % SUBLANES, (off + j) // SUBLANES
            wj = cw_ref[j:j + 1, ls]
            for rc in range(tm // CONV_ROWS):
                r0 = rc * CONV_ROWS + a * SUBLANES
                tap = uext[r0:r0 + CONV_ROWS, ls] if s == 0 else ushift[s - 1, r0:r0 + CONV_ROWS, ls]
                accs[rc] = accs[rc] + wj * tap
        chunks.append(jnp.concatenate(accs, axis=0))
    y = _conv_ln_silu(jnp.concatenate(chunks, axis=1), cb_ref, lg_ref, lb_ref)

    x1_ref[...] = (x + jnp.dot(attn.astype(BF16), wout_ref[0:D_ATTN, :], preferred_element_type=F32)
                   + jnp.dot(y.astype(BF16), wout_ref[D_ATTN:, :], preferred_element_type=F32))

    kband[0:BLOCK, :] = kband[tm:tm + BLOCK, :]
    vband[0:BLOCK, :] = vband[tm:tm + BLOCK, :]
    uext[0:CONV_PAD, :] = uext[tm:tm + CONV_PAD, :]


def _const_spec(shape):
    nd = len(shape)
    return pl.BlockSpec(shape, lambda *_: (0,) * nd)


def _mixer_prompt(x, sinks, consts, tm):
    b, s, d = x.shape
    nt = s // tm
    (ng, win, bdq, bdk, gqlo, gqhi, gk, bias_p, cw, cb, lg, lb, wout) = consts
    kern = functools.partial(_mixer_prompt_kernel, tm=tm, nt=nt)
    cspecs = [_const_spec(a.shape) for a in consts]
    return pl.pallas_call(
        kern,
        grid=(b, nt),
        in_specs=[pl.BlockSpec(memory_space=pltpu.SMEM),
                  pl.BlockSpec((None, tm, d), lambda i, j: (i, j, 0))] + cspecs,
        out_specs=[pl.BlockSpec((None, tm, d), lambda i, j: (i, j, 0)),
                   pl.BlockSpec((None, BLOCK, KV_DIM), lambda i, j: (i, 0, 0)),
                   pl.BlockSpec((None, BLOCK, KV_DIM), lambda i, j: (i, 0, 0)),
                   pl.BlockSpec((None, CONV_PAST, D_CONV), lambda i, j: (i, 0, 0))],
        out_shape=[jax.ShapeDtypeStruct((b, s, d), F32),
                   jax.ShapeDtypeStruct((b, BLOCK, KV_DIM), F32),
                   jax.ShapeDtypeStruct((b, BLOCK, KV_DIM), F32),
                   jax.ShapeDtypeStruct((b, CONV_PAST, D_CONV), F32)],
        scratch_shapes=[pltpu.VMEM((BLOCK + tm, KV_DIM), F32),
                        pltpu.VMEM((BLOCK + tm, KV_DIM), F32),
                        pltpu.VMEM((CONV_PAD + tm, D_CONV), F32),
                        pltpu.VMEM((SUBLANES - 1, tm + CONV_PAD - SUBLANES, D_CONV), F32)],
        compiler_params=pltpu.CompilerParams(dimension_semantics=("arbitrary", "arbitrary"),
                                             vmem_limit_bytes=VMEM_LIMIT),
        name="mixer_prompt",
    )(sinks, x, *consts)


def _mixer_sample_kernel(x_ref, ck_ref, cv_ref, st_ref, ng_ref, win_ref, bdq_ref, bdk_ref, gqlo_ref, gqhi_ref,
                         gk_ref, bias_ref, sinkrow_ref, cw_ref, cb_ref, lg_ref, lb_ref, wout_ref,
                         xag_ref, wmq_ref, gmq_ref,
                         x1_ref, qm_ref, ko_ref, vo_ref, co_ref,
                         kall, vall, qs, s_scr, r_scr, u_scr, y_scr, *, g, ts):
    wb = ck_ref.shape[2]
    x = x_ref[...]
    q_lo, q_hi, kn, v, u = _mixer_proj(x, ng_ref, win_ref, bdq_ref, bdk_ref, gqlo_ref, gqhi_ref, gk_ref)
    kall[:, wb:wb + ts, :] = kn.reshape(g, ts, KV_DIM)
    vall[:, wb:wb + ts, :] = v.reshape(g, ts, KV_DIM)
    for n in range(g):
        kall[n, 0:wb, :] = ck_ref[n].T
        vall[n, 0:wb, :] = cv_ref[n].T
        ko_ref[n] = kall[n, ts:wb + ts, :].T
        vo_ref[n] = vall[n, ts:wb + ts, :].T

    for lc in range(D_CONV // LANES):
        ls = slice(lc * LANES, (lc + 1) * LANES)
        u_scr[lc] = u[:, ls]
        u_t = [u_scr[lc, pl.ds(t, g, stride=ts), :] for t in range(ts)]

        def frame(r):
            return st_ref[r, :, ls] if r < CONV_PAST else u_t[r - CONV_PAST]

        for t in range(ts):
            acc = jnp.zeros((g, LANES), F32)
            for j in range(CONV_WIDTH):
                acc = acc + cw_ref[j:j + 1, ls] * frame(t + j)
            y_scr[lc, pl.ds(t, g, stride=ts), :] = acc
        for r in range(CONV_PAST):
            co_ref[r, :, ls] = frame(r + ts)
    y = _conv_ln_silu(jnp.concatenate([y_scr[lc] for lc in range(D_CONV // LANES)], axis=1), cb_ref, lg_ref, lb_ref)

    for hd in range(N_HEADS):
        tile = (q_lo if hd % 2 == 0 else q_hi)[:, (hd // 2) * LANES:(hd // 2 + 1) * LANES]
        if hd % 2 != hd // HEADS_PER_KV:
            tile = pltpu.roll(tile, HALF, axis=1)
        qs[:, hd * ts:(hd + 1) * ts, :] = tile.reshape(g, ts, LANES)
    for n in range(g):
        s_scr[n] = lax.dot_general(qs[n].astype(BF16), kall[n].astype(BF16), (((1,), (1,)), ((), ())),
                                   preferred_element_type=F32)
    s_scr[...] = _sink_softmax(s_scr[...] + bias_ref[...], sinkrow_ref[...])
    for n in range(g):
        r_scr[n] = jnp.dot(s_scr[n].astype(BF16), vall[n].astype(BF16), preferred_element_type=F32)
    r = r_scr[...].reshape(g * N_HEADS * ts, LANES)
    r_sw = pltpu.roll(r, HALF, axis=1).reshape(g, N_HEADS * ts, LANES)
    r = r.reshape(g, N_HEADS * ts, LANES)
    lo = lax.broadcasted_iota(jnp.int32, (g * ts, LANES), 1) < HALF
    heads = [(r if hd // HEADS_PER_KV == hd % 2 else r_sw)[:, hd * ts:(hd + 1) * ts, :].reshape(g * ts, LANES)
             for hd in range(N_HEADS)]
    attn = jnp.concatenate([jnp.where(lo, heads[2 * i], heads[2 * i + 1]) for i in range(N_HEADS // 2)], axis=1)

    x1 = (x + jnp.dot(attn.astype(BF16), wout_ref[0:D_ATTN, :], preferred_element_type=F32)
          + jnp.dot(y.astype(BF16), wout_ref[D_ATTN:, :], preferred_element_type=F32))
    x1_ref[...] = x1
    qm_ref[...] = _mem_query(x1, xag_ref, wmq_ref, gmq_ref)


def _mixer_sample(xs2d, ck, cv, st, consts, g, ts):
    n, d = xs2d.shape
    nb, wb = ck.shape[0], ck.shape[2]
    rows = g * ts
    cspecs = [_const_spec(a.shape) for a in consts]
    kern = functools.partial(_mixer_sample_kernel, g=g, ts=ts)
    return pl.pallas_call(
        kern,
        grid=(nb // g,),
        in_specs=[pl.BlockSpec((rows, d), lambda i: (i, 0)),
                  pl.BlockSpec((g, KV_DIM, wb), lambda i: (i, 0, 0)),
                  pl.BlockSpec((g, KV_DIM, wb), lambda i: (i, 0, 0)),
                  pl.BlockSpec((CONV_PAST, g, D_CONV), lambda i: (0, i, 0))] + cspecs,
        out_specs=[pl.BlockSpec((rows, d), lambda i: (i, 0)),
                   pl.BlockSpec((rows, d), lambda i: (i, 0)),
                   pl.BlockSpec((g, KV_DIM, wb), lambda i: (i, 0, 0)),
                   pl.BlockSpec((g, KV_DIM, wb), lambda i: (i, 0, 0)),
                   pl.BlockSpec((CONV_PAST, g, D_CONV), lambda i: (0, i, 0))],
        out_shape=[jax.ShapeDtypeStruct((n, d), F32),
                   jax.ShapeDtypeStruct((n, d), F32),
                   jax.ShapeDtypeStruct((nb, KV_DIM, wb), F32),
                   jax.ShapeDtypeStruct((nb, KV_DIM, wb), F32),
                   jax.ShapeDtypeStruct((CONV_PAST, nb, D_CONV), F32)],
        scratch_shapes=[pltpu.VMEM((g, wb + ts, KV_DIM), F32),
                        pltpu.VMEM((g, wb + ts, KV_DIM), F32),
                        pltpu.VMEM((g, N_HEADS * ts, LANES), F32),
                        pltpu.VMEM((g, N_HEADS * ts, wb + ts), F32),
                        pltpu.VMEM((g, N_HEADS * ts, LANES), F32),
                        pltpu.VMEM((D_CONV // LANES, rows, LANES), F32),
                        pltpu.VMEM((D_CONV // LANES, rows, LANES), F32)],
        compiler_params=pltpu.CompilerParams(dimension_semantics=("arbitrary",),
                                             vmem_limit_bytes=VMEM_LIMIT),
        name="mixer_sample",
    )(xs2d, ck, cv, st, *consts)


def _memory_kv_kernel(mem_ref, g_ref, wmk_ref, wmv_ref, gk_ref, k_ref, v_ref):
    h = _rms(mem_ref[...], g_ref[...]).astype(BF16)
    k = jnp.dot(h, wmk_ref[...], preferred_element_type=F32)
    parts = []
    for hd in range(MEM_HEADS):
        kh = k[:, hd * MEM_HEAD_DIM:(hd + 1) * MEM_HEAD_DIM]
        parts.append(kh * lax.rsqrt(jnp.mean(kh * kh, axis=-1, keepdims=True) + EPS))
    k_ref[...] = jnp.concatenate(parts, axis=1) * gk_ref[...]
    v_ref[...] = jnp.dot(h, wmv_ref[...], preferred_element_type=F32)


def _memory_kv(mem2d, g, wmk, wmv, gk, tm):
    n, d = mem2d.shape
    consts = (g, wmk, wmv, gk)
    return pl.pallas_call(
        _memory_kv_kernel,
        grid=(n // tm,),
        in_specs=[pl.BlockSpec((tm, d), lambda i: (i, 0))] + [_const_spec(a.shape) for a in consts],
        out_specs=[pl.BlockSpec((tm, d), lambda i: (i, 0)), pl.BlockSpec((tm, d), lambda i: (i, 0))],
        out_shape=[jax.ShapeDtypeStruct((n, d), F32), jax.ShapeDtypeStruct((n, d), F32)],
        compiler_params=pltpu.CompilerParams(dimension_semantics=("arbitrary",),
                                             vmem_limit_bytes=VMEM_LIMIT),
        name="memory_kv",
    )(mem2d, *consts)


def _cross_prompt_kernel(x1_ref, mk_ref, mv_ref, xag_ref, wmq_ref, gmq_ref, wmo_ref, fg_ref, wr_ref, br_ref, tri_ref,
                         x2_ref, hn_ref, route_ref, info_ref, cnt_ref, base):
    @pl.when((pl.program_id(0) == 0) & (pl.program_id(1) == 0))
    def _():
        base[...] = jnp.zeros_like(base)

    x1 = x1_ref[...]
    q = _mem_query(x1, xag_ref, wmq_ref, gmq_ref)
    o = _mem_attend(q, _split_heads(mk_ref[...].astype(BF16)), _split_heads(mv_ref[...].astype(BF16)))
    x2, hn, route, info, new_base = _cross_tail(o, x1, wmo_ref, fg_ref, wr_ref, br_ref, tri_ref, base[...])
    x2_ref[...] = x2
    hn_ref[...] = hn
    route_ref[...] = route
    info_ref[...] = info
    base[...] = new_base
    cnt_ref[...] = new_base


def _cross_prompt(x1, mk, mv, consts, tm):
    b, s, d = x1.shape
    m = mk.shape[1]
    cspecs = [_const_spec(a.shape) for a in consts]
    return pl.pallas_call(
        _cross_prompt_kernel,
        grid=(b, s // tm),
        in_specs=[pl.BlockSpec((None, tm, d), lambda i, j: (i, j, 0)),
                  pl.BlockSpec((None, m, d), lambda i, j: (i, 0, 0)),
                  pl.BlockSpec((None, m, d), lambda i, j: (i, 0, 0))] + cspecs,
        out_specs=[pl.BlockSpec((None, tm, d), lambda i, j: (i, j, 0)),
                   pl.BlockSpec((None, tm, d // 2), lambda i, j: (i, j, 0)),
                   pl.BlockSpec((None, tm, LANES), lambda i, j: (i, j, 0)),
                   pl.BlockSpec((ROUTE_ROWS, tm), lambda i, j: (0, i * (s // tm) + j)),
                   pl.BlockSpec((1, LANES), lambda i, j: (0, 0))],
        out_shape=[jax.ShapeDtypeStruct((b, s, d), F32),
                   jax.ShapeDtypeStruct((b, s, d // 2), jnp.int32),
                   jax.ShapeDtypeStruct((b, s, LANES), F32),
                   jax.ShapeDtypeStruct((ROUTE_ROWS, b * s), F32),
                   jax.ShapeDtypeStruct((1, LANES), F32)],
        scratch_shapes=[pltpu.VMEM((1, LANES), F32)],
        compiler_params=pltpu.CompilerParams(dimension_semantics=("arbitrary", "arbitrary"),
                                             vmem_limit_bytes=VMEM_LIMIT),
        name="cross_prompt",
    )(x1, mk, mv, *consts)


def _mem_attend_sample_kernel(qm_ref, x1_ref, *refs, g, ts):
    k_refs, v_refs = refs[:g], refs[g:2 * g]
    wmo_ref, fg_ref, wr_ref, br_ref, tri_ref = refs[2 * g:2 * g + 5]
    x2_ref, hn_ref, route_ref, cnt_ref, s_scr, o_scr, base = refs[2 * g + 5:]

    @pl.when(pl.program_id(0) == 0)
    def _():
        base[...] = jnp.zeros_like(base)

    halves = MEM_HEAD_DIM // LANES
    rows_per_pos = MEM_HEADS * halves
    mlen = k_refs[0].shape[0] // rows_per_pos

    def all_heads(ref):
        return jnp.concatenate([ref[pl.ds(c * MEM_HEADS + hd, mlen, stride=rows_per_pos), :]
                                for hd in range(MEM_HEADS) for c in range(halves)], axis=1).astype(BF16)

    rows = MEM_HEADS * ts
    own = (lax.broadcasted_iota(jnp.int32, (rows, MEM_HEADS * MEM_HEAD_DIM), 1) // MEM_HEAD_DIM
           == lax.broadcasted_iota(jnp.int32, (rows, MEM_HEADS * MEM_HEAD_DIM), 0) // ts)
    for n in range(g):
        q = qm_ref[n * ts:(n + 1) * ts, :]
        q_bd = jnp.where(own, jnp.concatenate([q] * MEM_HEADS, axis=0), 0.0).astype(BF16)
        s_scr[n] = lax.dot_general(q_bd, all_heads(k_refs[n]), (((1,), (1,)), ((), ())),
                                   preferred_element_type=F32)
    s = s_scr[...]
    p = jnp.exp(s - jnp.max(s, axis=-1, keepdims=True))
    s_scr[...] = p * (1.0 / jnp.sum(p, axis=-1, keepdims=True))
    for n in range(g):
        r = jnp.dot(s_scr[n].astype(BF16), all_heads(v_refs[n]), preferred_element_type=F32)
        o_scr[n * ts:(n + 1) * ts, :] = jnp.concatenate(
            [r[hd * ts:(hd + 1) * ts, hd * MEM_HEAD_DIM:(hd + 1) * MEM_HEAD_DIM] for hd in range(MEM_HEADS)], axis=1)

    x2, hn, route, _, new_base = _cross_tail(o_scr[...], x1_ref[...], wmo_ref, fg_ref, wr_ref, br_ref, tri_ref,
                                             base[...])
    x2_ref[...] = x2
    hn_ref[...] = hn
    route_ref[...] = route
    base[...] = new_base
    cnt_ref[...] = new_base


def _mem_cache_rows(cache):
    nb, mlen = cache.shape[:2]
    halves = MEM_HEAD_DIM // LANES
    return (cache.reshape(nb, mlen, MEM_HEADS, halves, LANES).transpose(0, 1, 3, 2, 4)
            .reshape(nb, mlen * halves * MEM_HEADS, LANES))


def _mem_attend_sample(qm, x1, mk, mv, consts, g, ts):
    n, d = qm.shape
    nb, rows = mk.shape[0], mk.shape[1]
    kern = functools.partial(_mem_attend_sample_kernel, g=g, ts=ts)

    def seq_spec(j):
        return pl.BlockSpec((None, rows, LANES), lambda i: (g * i + j, 0, 0))

    return pl.pallas_call(
        kern,
        grid=(nb // g,),
        in_specs=[pl.BlockSpec((g * ts, d), lambda i: (i, 0)), pl.BlockSpec((g * ts, d), lambda i: (i, 0))]
        + [seq_spec(j) for j in range(g)] * 2 + [_const_spec(a.shape) for a in consts],
        out_specs=[pl.BlockSpec((g * ts, d), lambda i: (i, 0)),
                   pl.BlockSpec((g * ts, d // 2), lambda i: (i, 0)),
                   pl.BlockSpec((g * ts, LANES), lambda i: (i, 0)),
                   pl.BlockSpec((1, LANES), lambda i: (0, 0))],
        out_shape=[jax.ShapeDtypeStruct((n, d), F32),
                   jax.ShapeDtypeStruct((n, d // 2), jnp.int32),
                   jax.ShapeDtypeStruct((n, LANES), F32),
                   jax.ShapeDtypeStruct((1, LANES), F32)],
        scratch_shapes=[pltpu.VMEM((g, MEM_HEADS * ts, rows // (MEM_HEADS * (MEM_HEAD_DIM // LANES))), F32),
                        pltpu.VMEM((g * ts, d), F32),
                        pltpu.VMEM((1, LANES), F32)],
        compiler_params=pltpu.CompilerParams(dimension_semantics=("arbitrary",),
                                             vmem_limit_bytes=VMEM_LIMIT),
        name="mem_attend_sample",
    )(qm, x1, *([mk] * g), *([mv] * g), *consts)


def _sc_rows_per_step(per_worker):
    step = min(SC_ROWS_PER_STEP, per_worker)
    assert per_worker % step == 0 and step % 8 == 0
    return step


def _sc_gather_rows(table, idx):
    nrows = idx.shape[0]
    _, width = table.shape
    assert nrows % (8 * SC_WORKERS) == 0
    per_worker = nrows // SC_WORKERS
    step = _sc_rows_per_step(per_worker // 2)
    mesh = plsc.VectorSubcoreMesh(core_axis_name="c", subcore_axis_name="s")

    @functools.partial(
        pl.kernel, mesh=mesh, out_type=jax.ShapeDtypeStruct((nrows, width), table.dtype),
        scratch_types=[pltpu.VMEM((step,), jnp.int32)] * 2 + [pltpu.VMEM((step, width), table.dtype)] * 2
        + [pltpu.SemaphoreType.DMA] * 2)
    def gather(table_hbm, idx_hbm, out_hbm, idx_a, idx_b, rows_a, rows_b, sem_a, sem_b):
        wid = lax.axis_index("s") * SC_CORES + lax.axis_index("c")
        base = wid * per_worker

        @pl.loop(0, per_worker // (2 * step))
        def _(i):
            off = base + i * (2 * step)
            pltpu.sync_copy(idx_hbm.at[pl.ds(off, step)], idx_a)
            pltpu.sync_copy(idx_hbm.at[pl.ds(off + step, step)], idx_b)
            gather_a = pltpu.async_copy(table_hbm.at[idx_a], rows_a, sem_a)
            gather_b = pltpu.async_copy(table_hbm.at[idx_b], rows_b, sem_b)
            gather_a.wait()
            write_a = pltpu.async_copy(rows_a, out_hbm.at[pl.ds(off, step)], sem_a)
            gather_b.wait()
            write_b = pltpu.async_copy(rows_b, out_hbm.at[pl.ds(off + step, step)], sem_b)
            write_a.wait()
            write_b.wait()

    return gather(table, idx)


def _sc_scatter_rows(table, pos, nrows_out):
    n, width = table.shape
    assert n % (8 * SC_WORKERS) == 0 and pos.shape == (TOP_K * n,)
    per_worker = n // SC_WORKERS
    step = _sc_rows_per_step(per_worker)
    mesh = plsc.VectorSubcoreMesh(core_axis_name="c", subcore_axis_name="s")

    @functools.partial(
        pl.kernel, mesh=mesh, out_type=jax.ShapeDtypeStruct((nrows_out, width), table.dtype),
        scratch_types=[pltpu.VMEM((step,), jnp.int32)] * TOP_K
        + [pltpu.VMEM((step, width), table.dtype), pltpu.SemaphoreType.DMA])
    def scatter(table_hbm, pos_hbm, out_hbm, *scratch):
        idx_vs, rows_v, sem = scratch[:TOP_K], scratch[TOP_K], scratch[TOP_K + 1]
        wid = lax.axis_index("s") * SC_CORES + lax.axis_index("c")
        base = wid * per_worker

        @pl.loop(0, per_worker // step)
        def _(i):
            off = base + i * step
            for k in range(TOP_K):
                pltpu.sync_copy(pos_hbm.at[pl.ds(k * n + off, step)], idx_vs[k])
            pltpu.sync_copy(table_hbm.at[pl.ds(off, step)], rows_v)
            for k in range(TOP_K):
                pltpu.async_copy(rows_v, out_hbm.at[idx_vs[k]], sem).wait()

    return scatter(table, pos)


W_CHUNKS = 4
FETCH_AHEAD = 3
FETCH_AHEAD_F32 = 2


def _expert_weight_copies(e, slot, w_hbm, w_slots, sems, chunks):
    copies = []
    for m, (src, dst) in enumerate(zip(w_hbm, w_slots)):
        rows = dst.shape[1] // chunks
        for c in range(chunks):
            sl = pl.ds(c * rows, rows)
            sem = sems.at[(slot * len(w_hbm) + m) * chunks + c]
            copies.append(pltpu.make_async_copy(src.at[e, sl, :], dst.at[slot, sl, :], sem))
    return copies


def _moe_grouped_kernel(te_ref, nv_ref, rv_ref, slot_ref, *refs, publish, ahead, chunks):
    later_refs, (xs_ref, wg_hbm, wu_hbm, wd_hbm, ys_ref), rest = refs[:ahead], refs[ahead:ahead + 5], refs[ahead + 5:]
    i = pl.program_id(0)
    n_slots = ahead + 1
    w_hbm = (wg_hbm, wu_hbm, wd_hbm)
    if publish:
        out_hbm, w_slots, w_bf16, sems, out_sems = rest[0:3], rest[3:6], rest[6:9], rest[9], rest[10]
    else:
        w_slots, sems = rest[0:3], rest[3]

    def publish_copies(e):
        return [pltpu.make_async_copy(src, dst.at[e], out_sems.at[m])
                for m, (src, dst) in enumerate(zip(w_bf16, out_hbm))]

    def fetch(e, slot):
        return _expert_weight_copies(e, slot, w_hbm, w_slots, sems, chunks)

    @pl.when(i == 0)
    def _():
        for cp in fetch(te_ref[0], slot_ref[0]):
            cp.start()
        for k in range(ahead - 1):
            @pl.when(later_refs[k][0] >= 0)
            def _():
                for cp in fetch(later_refs[k][0], (slot_ref[0] + k + 1) % n_slots):
                    cp.start()

    @pl.when(i < nv_ref[0])
    def _():
        slot = slot_ref[i]

        @pl.when((i == 0) | (te_ref[i] != te_ref[jnp.maximum(i - 1, 0)]))
        def _():
            for cp in fetch(te_ref[i], slot):
                cp.wait()
            if publish:
                @pl.when(i > 0)
                def _():
                    for cp in publish_copies(te_ref[jnp.maximum(i - 1, 0)]):
                        cp.wait()

                for src, dst in zip(w_slots, w_bf16):
                    dst[...] = src[slot].astype(BF16)
                for cp in publish_copies(te_ref[i]):
                    cp.start()

            @pl.when(later_refs[ahead - 1][i] >= 0)
            def _():
                for cp in fetch(later_refs[ahead - 1][i], (slot + ahead) % n_slots):
                    cp.start()

        wg, wu, wd = [r[...] for r in w_bf16] if publish else [r[slot] for r in w_slots]
        xs = xs_ref[...]
        row = lax.broadcasted_iota(jnp.int32, xs.shape, 0)
        x = _unpack_bf16_pairs(jnp.where(row < rv_ref[i], xs, 0)).astype(BF16)
        a = jnp.dot(x, wg, preferred_element_type=F32)
        b = jnp.dot(x, wu, preferred_element_type=F32)
        act = (a * jax.nn.sigmoid(a)) * b
        y = jnp.dot(act.astype(BF16), wd, preferred_element_type=F32)
        ys_ref[...] = _pack_bf16_pairs(y)

    if publish:
        @pl.when(i == pl.num_programs(0) - 1)
        def _():
            for cp in publish_copies(te_ref[nv_ref[0] - 1]):
                cp.wait()


def _moe_grouped(tile_expert, n_valid, rows_valid, run_slot, later_experts, xs, wg, wu, wd, tm, publish):
    p, half = xs.shape
    ne, d, f = wg.shape
    ahead = len(later_experts)
    chunks = W_CHUNKS if publish else 1

    def live_tile(i, te, nv, *_):
        return (jnp.minimum(i, nv[0] - 1), 0)

    slot_dtype = F32 if publish else BF16
    n_slots = ahead + 1
    scratch = [pltpu.VMEM((n_slots, d, f), slot_dtype), pltpu.VMEM((n_slots, d, f), slot_dtype),
               pltpu.VMEM((n_slots, f, d), slot_dtype)]
    out_specs = [pl.BlockSpec((tm, half), live_tile)]
    out_shape = [jax.ShapeDtypeStruct((p, half), jnp.int32)]
    if publish:
        scratch += [pltpu.VMEM((d, f), BF16), pltpu.VMEM((d, f), BF16), pltpu.VMEM((f, d), BF16)]
        out_specs += [pl.BlockSpec(memory_space=pl.ANY)] * 3
        out_shape += [jax.ShapeDtypeStruct(w.shape, BF16) for w in (wg, wu, wd)]
    scratch += [pltpu.SemaphoreType.DMA((n_slots * 3 * chunks,))]
    if publish:
        scratch += [pltpu.SemaphoreType.DMA((3,))]
    grid_spec = pltpu.PrefetchScalarGridSpec(
        num_scalar_prefetch=4 + ahead,
        grid=(p // tm,),
        in_specs=[pl.BlockSpec((tm, half), live_tile)] + [pl.BlockSpec(memory_space=pl.ANY)] * 3,
        out_specs=out_specs,
        scratch_shapes=scratch)
    res = pl.pallas_call(
        functools.partial(_moe_grouped_kernel, publish=publish, ahead=ahead, chunks=chunks),
        grid_spec=grid_spec,
        out_shape=out_shape,
        compiler_params=pltpu.CompilerParams(dimension_semantics=("arbitrary",),
                                             vmem_limit_bytes=VMEM_LIMIT),
        name="moe_grouped",
    )(tile_expert, n_valid, rows_valid, run_slot, *later_experts, xs, wg, wu, wd)
    return res[0], tuple(res[1:])


def _moe_combine_kernel(x2_ref, route_ref, y0_ref, y1_ref, out_ref):
    route = route_ref[...]
    out_ref[...] = (x2_ref[...] + route[:, 2:3] * _unpack_bf16_pairs(y0_ref[...])
                    + route[:, 3:4] * _unpack_bf16_pairs(y1_ref[...]))


def _moe_combine(x2, route, gathered, tm):
    n, d = x2.shape
    nt = n // tm
    return pl.pallas_call(
        _moe_combine_kernel,
        grid=(nt,),
        in_specs=[pl.BlockSpec((tm, d), lambda i: (i, 0)),
                  pl.BlockSpec((tm, LANES), lambda i: (i, 0)),
                  pl.BlockSpec((tm, d // 2), lambda i: (i, 0)),
                  pl.BlockSpec((tm, d // 2), lambda i: (i + nt, 0))],
        out_specs=pl.BlockSpec((tm, d), lambda i: (i, 0)),
        out_shape=jax.ShapeDtypeStruct((n, d), F32),
        compiler_params=pltpu.CompilerParams(dimension_semantics=("arbitrary",),
                                             vmem_limit_bytes=VMEM_LIMIT),
        name="moe_combine",
    )(x2, route, gathered, gathered)


def _moe(x2, hn_packed, route, info, counts, weights, tm_rows, publish):
    n = x2.shape[0]
    p = (TOP_K * n // tm_rows + N_EXPERTS) * tm_rows
    n_tiles = p // tm_rows
    cnt = counts[0, :N_EXPERTS].astype(jnp.int32)
    tiles_per_e = jnp.maximum((cnt + tm_rows - 1) // tm_rows, 1 if publish else 0)
    e_ids = jnp.arange(N_EXPERTS, dtype=jnp.int32)
    tile_end = jnp.sum(jnp.where(e_ids[None, :] <= e_ids[:, None], tiles_per_e[None, :], 0), axis=1)
    tile_start = tile_end - tiles_per_e
    n_valid = tile_end[-1:]
    tile_ids = jnp.arange(n_tiles, dtype=jnp.int32)
    tile_expert = jnp.minimum(jnp.sum((tile_end[None, :] <= tile_ids[:, None]).astype(jnp.int32), axis=1),
                              N_EXPERTS - 1)
    mine = tile_expert[:, None] == e_ids[None, :]
    rows_left = jnp.sum(jnp.where(mine, cnt - (tile_ids[:, None] - tile_start) * tm_rows, 0), axis=1)
    rows_valid = jnp.clip(rows_left, 0, tm_rows).astype(jnp.int32)
    ahead = FETCH_AHEAD_F32 if publish else FETCH_AHEAD
    has_tiles = tiles_per_e > 0
    run_of_e = jnp.sum((has_tiles[None, :] & (e_ids[None, :] < e_ids[:, None])).astype(jnp.int32), axis=1)
    run_of_tile = jnp.sum(jnp.where(mine, run_of_e, 0), axis=1)
    run_slot = (run_of_tile % (ahead + 1)).astype(jnp.int32)
    later_experts = []
    for k in range(1, ahead + 1):
        is_run = has_tiles[None, :] & (run_of_e[None, :] == run_of_tile[:, None] + k)
        later_experts.append(jnp.sum(jnp.where(is_run, e_ids[None, :] + 1, 0), axis=1).astype(jnp.int32) - 1)
    eidx = info[0:TOP_K].astype(jnp.int32)
    row_start = jnp.sum(jnp.where(eidx[None] == e_ids[:, None, None], (tile_start * tm_rows)[:, None, None], 0),
                        axis=0)
    pos = (row_start + info[4:4 + TOP_K].astype(jnp.int32)).reshape(-1)

    xs = _sc_scatter_rows(hn_packed, pos, p)
    ys, w_bf16 = _moe_grouped(tile_expert, n_valid, rows_valid, run_slot, later_experts, xs, *weights, tm_rows,
                              publish)
    back = _sc_gather_rows(ys, pos)
    return _moe_combine(x2, route, back, min(TM_COMBINE, n)), w_bf16


def _block_diag(n, width):
    idx = np.arange(n) // width
    return jnp.asarray((idx[:, None] == idx[None, :]).astype(np.float32), dtype=BF16)


def _kv_feature_major(cache):
    nb, wb = cache.shape[:2]
    return jnp.transpose(cache, (0, 2, 3, 1)).reshape(nb, KV_DIM, wb)


def _strict_lower(n):
    r = np.arange(n)
    return jnp.asarray((r[:, None] > r[None, :]).astype(np.float32), dtype=BF16)


def _alibi_slopes():
    return np.exp2(-8.0 * np.arange(1, N_HEADS + 1, dtype=np.float32) / N_HEADS).astype(np.float32)


def _prompt_bias():
    i = np.arange(BLOCK)[:, None]
    s = np.arange(2 * BLOCK)[None, :]
    dist = (i + BLOCK - s).astype(np.float32)
    mask = (dist >= 0) & (dist < WINDOW)
    first = mask & (s >= BLOCK)
    slopes = _alibi_slopes()[:, None, None]
    reg = np.where(mask[None], -slopes * dist[None], np.float32(NEG_INF))
    fst = np.where(first[None], -slopes * dist[None], np.float32(NEG_INF))
    return jnp.asarray(np.concatenate([reg, fst], axis=0).astype(np.float32))


def _sample_bias(ts, wb):
    i = np.arange(ts)[:, None]
    s = np.arange(wb + ts)[None, :]
    dist = (i + wb - s).astype(np.float32)
    mask = (dist >= 0) & (dist < WINDOW)
    slopes = _alibi_slopes()[:, None, None]
    b = np.where(mask[None], -slopes * dist[None], np.float32(NEG_INF)).astype(np.float32)
    return jnp.asarray(b.reshape(N_HEADS * ts, wb + ts))


def kernel(x_prompt, x_sample, cache_swa_k, cache_swa_v, state_conv, cache_mem_k, cache_mem_v, mem_prompt, norm_mix_g, w_in, q_norm_g, k_norm_g, attn_sinks, conv_dw_w, conv_dw_b, conv_ln_g, conv_ln_b, w_out, norm_xa_g, norm_mem_g, w_mq, w_mk, w_mv, mq_norm_g, mk_norm_g, w_mo, norm_ffn_g, w_router_group, b_router_group, w_router_expert, b_router_expert, w_exp_gate, w_exp_up, w_exp_down):
    depth = w_in.shape[0]
    bp, sp, d = x_prompt.shape
    nb, ts, _ = x_sample.shape
    wb = cache_swa_k.shape[2]
    mlen = mem_prompt.shape[1]
    assert d == D_MODEL and wb == WINDOW and sp % TM_PROMPT == 0 and sp % TM_CROSS == 0 and nb % SEQ_PER_STEP == 0

    bdq = _block_diag(MXU_TILE, HEAD_DIM)
    bdk = _block_diag(KV_DIM, HEAD_DIM)
    bias_p = _prompt_bias()
    bias_s = _sample_bias(ts, wb)
    lane_lo = (np.arange(D_ATTN) % LANES) < HALF
    row = lambda a: a.reshape(1, -1).astype(F32)

    xp = x_prompt
    xs = x_sample.reshape(nb * ts, d)
    kp_l, vp_l, cp_l, mkp_l, mvp_l, ks_l, vs_l, cs_l = [], [], [], [], [], [], [], []
    for l in range(depth):
        gq = jnp.tile(q_norm_g[l].astype(F32), N_HEADS) * (HEAD_DIM ** -0.5)
        gqlo = jnp.where(lane_lo, gq, 0.0).reshape(1, -1)
        gqhi = jnp.where(lane_lo, 0.0, gq).reshape(1, -1)
        gk = jnp.tile(k_norm_g[l].astype(F32), N_KV_HEADS).reshape(1, -1)
        sinks = attn_sinks[l].astype(F32)
        sinkrow = jnp.repeat(sinks, ts).reshape(N_HEADS * ts, 1)
        win = w_in[l].astype(BF16)
        wout = w_out[l].astype(BF16)
        wmq = w_mq[l].astype(BF16)
        wmo = w_mo[l].astype(BF16)
        gmq = (jnp.tile(mq_norm_g[l].astype(F32), MEM_HEADS) * (MEM_HEAD_DIM ** -0.5)).reshape(1, -1)
        gmk = jnp.tile(mk_norm_g[l].astype(F32), MEM_HEADS).reshape(1, -1)
        w_r = jnp.concatenate([w_router_expert[l], w_router_group[l],
                               jnp.zeros((d, LANES - N_EXPERTS - N_GROUPS), F32)], axis=1).astype(BF16)
        b_r = jnp.concatenate([b_router_expert[l], b_router_group[l],
                               jnp.zeros((LANES - N_EXPERTS - N_GROUPS,), F32)]).reshape(1, -1).astype(F32)
        wg, wu, wd = w_exp_gate[l], w_exp_up[l], w_exp_down[l]

        mix_consts = (row(norm_mix_g[l]), win, bdq, bdk, gqlo, gqhi, gk)
        conv_consts = (conv_dw_w[l].astype(F32), row(conv_dw_b[l]), row(conv_ln_g[l]), row(conv_ln_b[l]), wout)
        tail_consts = (wmo, row(norm_ffn_g[l]), w_r, b_r)

        x1p, kp, vp, cp = _mixer_prompt(xp, sinks, mix_consts + (bias_p,) + conv_consts, TM_PROMPT)
        mk, mv = _memory_kv(mem_prompt.reshape(bp * mlen, d), row(norm_mem_g[l]),
                            w_mk[l].astype(BF16), w_mv[l].astype(BF16), gmk, min(256, bp * mlen))
        mk = mk.reshape(bp, mlen, d)
        mv = mv.reshape(bp, mlen, d)
        x2p, hnp, routep, infop, cntp = _cross_prompt(x1p, mk, mv, (row(norm_xa_g[l]), wmq, gmq) + tail_consts
                                               + (_strict_lower(RANK_BLOCK),), TM_CROSS)
        xp, w_bf16 = _moe(x2p.reshape(bp * sp, d), hnp.reshape(bp * sp, d // 2), routep.reshape(bp * sp, LANES),
                          infop, cntp, (wg, wu, wd), TM_ROWS_PROMPT, True)
        xp = xp.reshape(bp, sp, d)
        kp_l.append(kp.reshape(bp, BLOCK, N_KV_HEADS, HEAD_DIM))
        vp_l.append(vp.reshape(bp, BLOCK, N_KV_HEADS, HEAD_DIM))
        cp_l.append(cp)
        mkp_l.append(mk.reshape(bp, mlen, MEM_HEADS, MEM_HEAD_DIM))
        mvp_l.append(mv.reshape(bp, mlen, MEM_HEADS, MEM_HEAD_DIM))

        x1s, qm, ksn, vsn, csn = _mixer_sample(
            xs, _kv_feature_major(cache_swa_k[l]), _kv_feature_major(cache_swa_v[l]),
            jnp.transpose(state_conv[l], (1, 0, 2)),
            mix_consts + (bias_s, sinkrow) + conv_consts + (row(norm_xa_g[l]), wmq, gmq), SEQ_PER_STEP, ts)
        rows_s = MEM_SEQ_PER_STEP * ts
        x2s, hns, routes, cnts = _mem_attend_sample(
            qm, x1s, _mem_cache_rows(cache_mem_k[l]), _mem_cache_rows(cache_mem_v[l]),
            tail_consts + (_strict_lower(min(RANK_BLOCK, rows_s)),), MEM_SEQ_PER_STEP, ts)
        infos = routes[:, 0:ROUTE_ROWS].T
        xs, _ = _moe(x2s, hns, routes, infos, cnts, w_bf16, TM_ROWS_SAMPLE, False)
        ks_l.append(jnp.transpose(ksn.reshape(nb, N_KV_HEADS, HEAD_DIM, wb), (0, 3, 1, 2)))
        vs_l.append(jnp.transpose(vsn.reshape(nb, N_KV_HEADS, HEAD_DIM, wb), (0, 3, 1, 2)))
        cs_l.append(jnp.transpose(csn, (1, 0, 2)))

    st = lambda xs_: jnp.stack(xs_, axis=0)
    return (xp, xs.reshape(nb, ts, d), st(kp_l), st(vp_l), st(cp_l), st(mkp_l), st(mvp_l),
            st(ks_l), st(vs_l), st(cs_l))
```

```python
import functools

import numpy as np
import jax
import jax.numpy as jnp
from jax import lax
from jax.experimental import pallas as pl
from jax.experimental.pallas import tpu as pltpu
from jax.experimental.pallas import tpu_sc as plsc

F32 = jnp.float32
BF16 = jnp.bfloat16

D_MODEL = 1024
D_ATTN = 512
D_CONV = 512
HEAD_DIM = 64
N_HEADS = 8
N_KV_HEADS = 2
KV_DIM = N_KV_HEADS * HEAD_DIM
HEADS_PER_KV = N_HEADS // N_KV_HEADS
WINDOW = 128
BLOCK = 128
CONV_WIDTH = 31
CONV_PAST = CONV_WIDTH - 1
MEM_HEADS = 4
MEM_HEAD_DIM = 256
N_GROUPS = 4
EXPERTS_PER_GROUP = 8
N_EXPERTS = 32
EPS = 1e-6
NEG_INF = -1e30

LANES = 128
HALF = LANES // 2
MXU_TILE = 256
SUBLANES = 8
CONV_PAD = 32
CONV_ROWS = 64
VMEM_LIMIT = 56 * 1024 * 1024

TM_PROMPT = 512
TM_CROSS = 1024
SEQ_PER_STEP = 32
MEM_SEQ_PER_STEP = 8
TOP_K = 2
RANK_BLOCK = 256
ROUTE_ROWS = 8
SC_CORES = 2
SC_WORKERS = 32
SC_ROWS_PER_STEP = 64
TM_ROWS_PROMPT = 512
TM_ROWS_SAMPLE = 128
TM_COMBINE = 1024
COMBINE_PARTS = 2


def _rms(x, g):
    ms = jnp.mean(x * x, axis=-1, keepdims=True)
    return x * lax.rsqrt(ms + EPS) * g


def _group_mean_sq(x, bd_ref, width):
    x2 = x * x
    hi = x2.astype(BF16)
    lo = (x2 - hi.astype(F32)).astype(BF16)
    bd = bd_ref[...]
    w = bd.shape[0]
    parts = [jnp.dot(hi[:, c:c + w], bd, preferred_element_type=F32)
             + jnp.dot(lo[:, c:c + w], bd, preferred_element_type=F32) for c in range(0, x.shape[1], w)]
    return jnp.concatenate(parts, axis=1) * (1.0 / width)


def _mixer_proj(x, ng_ref, win_ref, bdq_ref, bdk_ref, gqlo_ref, gqhi_ref, gk_ref):
    h = _rms(x, ng_ref[...]).astype(BF16)
    p = jnp.dot(h, win_ref[...], preferred_element_type=F32)
    q = p[:, :D_ATTN]
    k = p[:, D_ATTN:D_ATTN + KV_DIM]
    v = p[:, D_ATTN + KV_DIM:D_ATTN + 2 * KV_DIM]
    ua = p[:, D_ATTN + 2 * KV_DIM:D_ATTN + 2 * KV_DIM + D_CONV]
    ub = p[:, D_ATTN + 2 * KV_DIM + D_CONV:]
    qn = q * lax.rsqrt(_group_mean_sq(q, bdq_ref, HEAD_DIM) + EPS)
    q_lo = qn * gqlo_ref[...]
    q_hi = qn * gqhi_ref[...]
    kn = k * lax.rsqrt(_group_mean_sq(k, bdk_ref, HEAD_DIM) + EPS) * gk_ref[...]
    u = ua * jax.nn.sigmoid(ub)
    return q_lo, q_hi, kn, v, u


def _dup_halves(x):
    lo = lax.broadcasted_iota(jnp.int32, x.shape, 1) < HALF
    xr = pltpu.roll(x, HALF, axis=1)
    return jnp.where(lo, x, xr), jnp.where(lo, xr, x)


def _sink_softmax(s, sink):
    m = jnp.maximum(jnp.max(s, axis=-1, keepdims=True), sink)
    p = jnp.exp(s - m)
    denom = jnp.sum(p, axis=-1, keepdims=True) + jnp.exp(sink - m)
    return p * (1.0 / denom)


def _conv_ln_silu(y, cb_ref, lg_ref, lb_ref):
    y = y + cb_ref[...]
    mu = jnp.mean(y, axis=-1, keepdims=True)
    yc = y - mu
    yn = yc * lax.rsqrt(jnp.mean(yc * yc, axis=-1, keepdims=True) + EPS)
    z = yn * lg_ref[...] + lb_ref[...]
    return z * jax.nn.sigmoid(z)


def _mem_query(x1, xag_ref, wmq_ref, gmq_ref):
    h = _rms(x1, xag_ref[...]).astype(BF16)
    q = jnp.dot(h, wmq_ref[...], preferred_element_type=F32)
    parts = []
    for hd in range(MEM_HEADS):
        qh = q[:, hd * MEM_HEAD_DIM:(hd + 1) * MEM_HEAD_DIM]
        parts.append(qh * lax.rsqrt(jnp.mean(qh * qh, axis=-1, keepdims=True) + EPS))
    return jnp.concatenate(parts, axis=1) * gmq_ref[...]


def _mem_attend(q, k_heads, v_heads):
    outs = []
    for hd in range(MEM_HEADS):
        sl = slice(hd * MEM_HEAD_DIM, (hd + 1) * MEM_HEAD_DIM)
        s = lax.dot_general(q[:, sl].astype(BF16), k_heads[hd], (((1,), (1,)), ((), ())),
                            preferred_element_type=F32)
        m = jnp.max(s, axis=-1, keepdims=True)
        p = jnp.exp(s - m)
        p = p * (1.0 / jnp.sum(p, axis=-1, keepdims=True))
        outs.append(jnp.dot(p.astype(BF16), v_heads[hd], preferred_element_type=F32))
    return jnp.concatenate(outs, axis=1)


def _split_heads(x):
    return [x[:, hd * MEM_HEAD_DIM:(hd + 1) * MEM_HEAD_DIM] for hd in range(MEM_HEADS)]


def _pack_bf16_pairs(y):
    n = y.shape[1] // 2
    bits = pltpu.bitcast(y.astype(BF16).astype(F32), jnp.int32)
    return (bits[:, :n] & jnp.int32(-65536)) | lax.shift_right_logical(bits[:, n:], jnp.int32(16))


def _unpack_bf16_pairs(w):
    hi = pltpu.bitcast(w & jnp.int32(-65536), F32)
    lo = pltpu.bitcast(lax.shift_left(w, jnp.int32(16)), F32)
    return jnp.concatenate([hi, lo], axis=1)


def _cross_tail(o, x1, wmo_ref, fg_ref, wr_ref, br_ref, tri_ref, base):
    x2 = x1 + jnp.dot(o.astype(BF16), wmo_ref[...], preferred_element_type=F32)
    hn_f = _rms(x2, fg_ref[...])
    hn = hn_f.astype(BF16)
    logits = jnp.dot(hn, wr_ref[...], preferred_element_type=F32) + br_ref[...]
    lane = lax.broadcasted_iota(jnp.int32, logits.shape, 1)
    is_g = (lane >= N_EXPERTS) & (lane < N_EXPERTS + N_GROUPS)
    lg = jnp.where(is_g, logits, NEG_INF)
    gmax = jnp.max(lg, axis=-1, keepdims=True)
    gsel = jnp.min(jnp.where(lg == gmax, lane, 2 * LANES), axis=-1, keepdims=True) - N_EXPERTS
    pg_sel = 1.0 / jnp.sum(jnp.exp(lg - gmax), axis=-1, keepdims=True)
    in_grp = (lane >= gsel * EXPERTS_PER_GROUP) & (lane < (gsel + 1) * EXPERTS_PER_GROUP)
    le = jnp.where(in_grp, logits, NEG_INF)
    top1 = jnp.max(le, axis=-1, keepdims=True)
    idx1 = jnp.min(jnp.where(le == top1, lane, 2 * LANES), axis=-1, keepdims=True)
    le2 = jnp.where(lane == idx1, NEG_INF, le)
    top2 = jnp.max(le2, axis=-1, keepdims=True)
    idx2 = jnp.min(jnp.where(le2 == top2, lane, 2 * LANES), axis=-1, keepdims=True)
    e2 = jnp.exp(top2 - top1)
    inv = 1.0 / (1.0 + e2)
    gate1 = pg_sel * inv
    gate2 = pg_sel * (e2 * inv)
    used = jnp.where((lane == idx1) | (lane == idx2), 1.0, 0.0)
    blk = tri_ref.shape[0]
    used_b = used.astype(BF16)
    parts, run = [], base
    for r0 in range(0, used.shape[0], blk):
        parts.append(jnp.dot(tri_ref[...], used_b[r0:r0 + blk], preferred_element_type=F32) + run)
        run = run + jnp.sum(used[r0:r0 + blk], axis=0, keepdims=True)
    before = jnp.concatenate(parts, axis=0)
    rank1 = jnp.sum(jnp.where(lane == idx1, before, 0.0), axis=-1, keepdims=True)
    rank2 = jnp.sum(jnp.where(lane == idx2, before, 0.0), axis=-1, keepdims=True)
    route = jnp.zeros_like(logits)
    for pos, val in enumerate((idx1.astype(F32), idx2.astype(F32), gate1, gate2, rank1, rank2)):
        route = jnp.where(lane == pos, val, route)
    info = jnp.transpose(route)[0:ROUTE_ROWS, :]
    return x2, _pack_bf16_pairs(hn_f), route, info, run


def _mixer_prompt_kernel(sink_ref, x_ref, ng_ref, win_ref, bdq_ref, bdk_ref, gqlo_ref, gqhi_ref, gk_ref,
                         bias_ref, cw_ref, cb_ref, lg_ref, lb_ref, wout_ref,
                         x1_ref, ko_ref, vo_ref, co_ref,
                         kband, vband, uext, ushift, *, tm, nt):
    t = pl.program_id(1)

    @pl.when(t == 0)
    def _():
        kband[0:BLOCK, :] = jnp.zeros((BLOCK, KV_DIM), F32)
        vband[0:BLOCK, :] = jnp.zeros((BLOCK, KV_DIM), F32)
        uext[0:CONV_PAD, :] = jnp.zeros((CONV_PAD, D_CONV), F32)

    x = x_ref[...]
    q_lo, q_hi, kn, v, u = _mixer_proj(x, ng_ref, win_ref, bdq_ref, bdk_ref, gqlo_ref, gqhi_ref, gk_ref)
    kband[BLOCK:BLOCK + tm, :] = kn
    vband[BLOCK:BLOCK + tm, :] = v
    uext[CONV_PAD:CONV_PAD + tm, :] = u

    @pl.when(t == nt - 1)
    def _():
        ko_ref[...] = kn[tm - BLOCK:, :]
        vo_ref[...] = v[tm - BLOCK:, :]
        co_ref[...] = uext[pl.ds(CONV_PAD + tm - CONV_PAST, CONV_PAST), :]

    kd = [a.astype(BF16) for a in _dup_halves(kband[...])]
    vd = _dup_halves(vband[...])
    lo = lax.broadcasted_iota(jnp.int32, vd[0].shape, 1) < HALF
    v_lo = [jnp.where(lo, a, 0.0).astype(BF16) for a in vd]
    v_hi = [jnp.where(lo, 0.0, a).astype(BF16) for a in vd]
    q_lo = q_lo.astype(BF16)
    q_hi = q_hi.astype(BF16)
    attn_blocks = []
    for j in range(tm // BLOCK):
        rows = slice(j * BLOCK, (j + 1) * BLOCK)
        keys = slice(j * BLOCK, j * BLOCK + 2 * BLOCK)
        bias_base = jnp.where(t == 0, N_HEADS, 0) if j == 0 else 0
        tiles = []
        for c in range(N_KV_HEADS):
            q4 = jnp.concatenate(
                [(q_lo if (HEADS_PER_KV * c + a) % 2 == 0 else q_hi)[rows,
                  ((HEADS_PER_KV * c + a) // 2) * LANES:((HEADS_PER_KV * c + a) // 2 + 1) * LANES]
                 for a in range(HEADS_PER_KV)], axis=0)
            s_all = lax.dot_general(q4, kd[c][keys], (((1,), (1,)), ((), ())), preferred_element_type=F32)
            ps = []
            for a in range(HEADS_PER_KV):
                hd = HEADS_PER_KV * c + a
                s = s_all[a * BLOCK:(a + 1) * BLOCK] + bias_ref[bias_base + hd]
                ps.append(_sink_softmax(s, sink_ref[hd]).astype(BF16))
            vstack = jnp.concatenate([v_lo[c][keys], v_hi[c][keys]], axis=0)
            for i2 in range(HEADS_PER_KV // 2):
                pp = jnp.concatenate([ps[2 * i2], ps[2 * i2 + 1]], axis=1)
                tiles.append(jnp.dot(pp, vstack, preferred_element_type=F32))
        attn_blocks.append(jnp.concatenate(tiles, axis=1))
    attn = jnp.concatenate(attn_blocks, axis=0)

    off = CONV_PAD - CONV_PAST
    span = tm + CONV_PAD - SUBLANES
    for s in range(1, SUBLANES):
        ushift[s - 1] = uext[pl.ds(s, span), :]
    chunks = []
    for lc in range(D_CONV // LANES):
        ls = slice(lc * LANES, (lc + 1) * LANES)
        accs = [jnp.zeros((CONV_ROWS, LANES), F32) for _ in range(tm // CONV_ROWS)]
        for j in range(CONV_WIDTH):
            s, a = (off + j) % SUBLANES, (off + j) // SUBLANES
            wj = cw_ref[j:j + 1, ls]
            for rc in range(tm // CONV_ROWS):
                r0 = rc * CONV_ROWS + a * SUBLANES
                tap = uext[r0:r0 + CONV_ROWS, ls] if s == 0 else ushift[s - 1, r0:r0 + CONV_ROWS, ls]
                accs[rc] = accs[rc] + wj * tap
        chunks.append(jnp.concatenate(accs, axis=0))
    y = _conv_ln_silu(jnp.concatenate(chunks, axis=1), cb_ref, lg_ref, lb_ref)

    x1_ref[...] = (x + jnp.dot(attn.astype(BF16), wout_ref[0:D_ATTN, :], preferred_element_type=F32)
                   + jnp.dot(y.astype(BF16), wout_ref[D_ATTN:, :], preferred_element_type=F32))

    kband[0:BLOCK, :] = kband[tm:tm + BLOCK, :]
    vband[0:BLOCK, :] = vband[tm:tm + BLOCK, :]
    uext[0:CONV_PAD, :] = uext[tm:tm + CONV_PAD, :]


def _const_spec(shape):
    nd = len(shape)
    return pl.BlockSpec(shape, lambda *_: (0,) * nd)


def _mixer_prompt(x, sinks, consts, tm):
    b, s, d = x.shape
    nt = s // tm
    (ng, win, bdq, bdk, gqlo, gqhi, gk, bias_p, cw, cb, lg, lb, wout) = consts
    kern = functools.partial(_mixer_prompt_kernel, tm=tm, nt=nt)
    cspecs = [_const_spec(a.shape) for a in consts]
    return pl.pallas_call(
        kern,
        grid=(b, nt),
        in_specs=[pl.BlockSpec(memory_space=pltpu.SMEM),
                  pl.BlockSpec((None, tm, d), lambda i, j: (i, j, 0))] + cspecs,
        out_specs=[pl.BlockSpec((None, tm, d), lambda i, j: (i, j, 0)),
                   pl.BlockSpec((None, BLOCK, KV_DIM), lambda i, j: (i, 0, 0)),
                   pl.BlockSpec((None, BLOCK, KV_DIM), lambda i, j: (i, 0, 0)),
                   pl.BlockSpec((None, CONV_PAST, D_CONV), lambda i, j: (i, 0, 0))],
        out_shape=[jax.ShapeDtypeStruct((b, s, d), F32),
                   jax.ShapeDtypeStruct((b, BLOCK, KV_DIM), F32),
                   jax.ShapeDtypeStruct((b, BLOCK, KV_DIM), F32),
                   jax.ShapeDtypeStruct((b, CONV_PAST, D_CONV), F32)],
        scratch_shapes=[pltpu.VMEM((BLOCK + tm, KV_DIM), F32),
                        pltpu.VMEM((BLOCK + tm, KV_DIM), F32),
                        pltpu.VMEM((CONV_PAD + tm, D_CONV), F32),
                        pltpu.VMEM((SUBLANES - 1, tm + CONV_PAD - SUBLANES, D_CONV), F32)],
        compiler_params=pltpu.CompilerParams(dimension_semantics=("arbitrary", "arbitrary"),
                                             vmem_limit_bytes=VMEM_LIMIT),
        name="mixer_prompt",
    )(sinks, x, *consts)


def _mixer_sample_kernel(x_ref, ck_ref, cv_ref, st_ref, ng_ref, win_ref, bdq_ref, bdk_ref, gqlo_ref, gqhi_ref,
                         gk_ref, bias_ref, sinkrow_ref, cw_ref, cb_ref, lg_ref, lb_ref, wout_ref,
                         xag_ref, wmq_ref, gmq_ref,
                         x1_ref, qm_ref, ko_ref, vo_ref, co_ref,
                         kall, vall, qs, s_scr, r_scr, u_scr, y_scr, *, g, ts):
    wb = ck_ref.shape[2]
    x = x_ref[...]
    q_lo, q_hi, kn, v, u = _mixer_proj(x, ng_ref, win_ref, bdq_ref, bdk_ref, gqlo_ref, gqhi_ref, gk_ref)
    kall[:, wb:wb + ts, :] = kn.reshape(g, ts, KV_DIM)
    vall[:, wb:wb + ts, :] = v.reshape(g, ts, KV_DIM)
    for n in range(g):
        kall[n, 0:wb, :] = ck_ref[n].T
        vall[n, 0:wb, :] = cv_ref[n].T
        ko_ref[n] = kall[n, ts:wb + ts, :].T
        vo_ref[n] = vall[n, ts:wb + ts, :].T

    for lc in range(D_CONV // LANES):
        ls = slice(lc * LANES, (lc + 1) * LANES)
        u_scr[lc] = u[:, ls]
        u_t = [u_scr[lc, pl.ds(t, g, stride=ts), :] for t in range(ts)]

        def frame(r):
            return st_ref[r, :, ls] if r < CONV_PAST else u_t[r - CONV_PAST]

        for t in range(ts):
            acc = jnp.zeros((g, LANES), F32)
            for j in range(CONV_WIDTH):
                acc = acc + cw_ref[j:j + 1, ls] * frame(t + j)
            y_scr[lc, pl.ds(t, g, stride=ts), :] = acc
        for r in range(CONV_PAST):
            co_ref[r, :, ls] = frame(r + ts)
    y = _conv_ln_silu(jnp.concatenate([y_scr[lc] for lc in range(D_CONV // LANES)], axis=1), cb_ref, lg_ref, lb_ref)

    for hd in range(N_HEADS):
        tile = (q_lo if hd % 2 == 0 else q_hi)[:, (hd // 2) * LANES:(hd // 2 + 1) * LANES]
        if hd % 2 != hd // HEADS_PER_KV:
            tile = pltpu.roll(tile, HALF, axis=1)
        qs[:, hd * ts:(hd + 1) * ts, :] = tile.reshape(g, ts, LANES)
    for n in range(g):
        s_scr[n] = lax.dot_general(qs[n].astype(BF16), kall[n].astype(BF16), (((1,), (1,)), ((), ())),
                                   preferred_element_type=F32)
    s_scr[...] = _sink_softmax(s_scr[...] + bias_ref[...], sinkrow_ref[...])
    for n in range(g):
        r_scr[n] = jnp.dot(s_scr[n].astype(BF16), vall[n].astype(BF16), preferred_element_type=F32)
    r = r_scr[...].reshape(g * N_HEADS * ts, LANES)
    r_sw = pltpu.roll(r, HALF, axis=1).reshape(g, N_HEADS * ts, LANES)
    r = r.reshape(g, N_HEADS * ts, LANES)
    lo = lax.broadcasted_iota(jnp.int32, (g * ts, LANES), 1) < HALF
    heads = [(r if hd // HEADS_PER_KV == hd % 2 else r_sw)[:, hd * ts:(hd + 1) * ts, :].reshape(g * ts, LANES)
             for hd in range(N_HEADS)]
    attn = jnp.concatenate([jnp.where(lo, heads[2 * i], heads[2 * i + 1]) for i in range(N_HEADS // 2)], axis=1)

    x1 = (x + jnp.dot(attn.astype(BF16), wout_ref[0:D_ATTN, :], preferred_element_type=F32)
          + jnp.dot(y.astype(BF16), wout_ref[D_ATTN:, :], preferred_element_type=F32))
    x1_ref[...] = x1
    qm_ref[...] = _mem_query(x1, xag_ref, wmq_ref, gmq_ref)


def _mixer_sample(xs2d, ck, cv, st, consts, g, ts):
    n, d = xs2d.shape
    nb, wb = ck.shape[0], ck.shape[2]
    rows = g * ts
    cspecs = [_const_spec(a.shape) for a in consts]
    kern = functools.partial(_mixer_sample_kernel, g=g, ts=ts)
    return pl.pallas_call(
        kern,
        grid=(nb // g,),
        in_specs=[pl.BlockSpec((rows, d), lambda i: (i, 0)),
                  pl.BlockSpec((g, KV_DIM, wb), lambda i: (i, 0, 0)),
                  pl.BlockSpec((g, KV_DIM, wb), lambda i: (i, 0, 0)),
                  pl.BlockSpec((CONV_PAST, g, D_CONV), lambda i: (0, i, 0))] + cspecs,
        out_specs=[pl.BlockSpec((rows, d), lambda i: (i, 0)),
                   pl.BlockSpec((rows, d), lambda i: (i, 0)),
                   pl.BlockSpec((g, KV_DIM, wb), lambda i: (i, 0, 0)),
                   pl.BlockSpec((g, KV_DIM, wb), lambda i: (i, 0, 0)),
                   pl.BlockSpec((CONV_PAST, g, D_CONV), lambda i: (0, i, 0))],
        out_shape=[jax.ShapeDtypeStruct((n, d), F32),
                   jax.ShapeDtypeStruct((n, d), F32),
                   jax.ShapeDtypeStruct((nb, KV_DIM, wb), F32),
                   jax.ShapeDtypeStruct((nb, KV_DIM, wb), F32),
                   jax.ShapeDtypeStruct((CONV_PAST, nb, D_CONV), F32)],
        scratch_shapes=[pltpu.VMEM((g, wb + ts, KV_DIM), F32),
                        pltpu.VMEM((g, wb + ts, KV_DIM), F32),
                        pltpu.VMEM((g, N_HEADS * ts, LANES), F32),
                        pltpu.VMEM((g, N_HEADS * ts, wb + ts), F32),
                        pltpu.VMEM((g, N_HEADS * ts, LANES), F32),
                        pltpu.VMEM((D_CONV // LANES, rows, LANES), F32),
                        pltpu.VMEM((D_CONV // LANES, rows, LANES), F32)],
        compiler_params=pltpu.CompilerParams(dimension_semantics=("arbitrary",),
                                             vmem_limit_bytes=VMEM_LIMIT),
        name="mixer_sample",
    )(xs2d, ck, cv, st, *consts)


def _memory_kv_kernel(mem_ref, g_ref, wmk_ref, wmv_ref, gk_ref, k_ref, v_ref):
    h = _rms(mem_ref[...], g_ref[...]).astype(BF16)
    k = jnp.dot(h, wmk_ref[...], preferred_element_type=F32)
    parts = []
    for hd in range(MEM_HEADS):
        kh = k[:, hd * MEM_HEAD_DIM:(hd + 1) * MEM_HEAD_DIM]
        parts.append(kh * lax.rsqrt(jnp.mean(kh * kh, axis=-1, keepdims=True) + EPS))
    k_ref[...] = jnp.concatenate(parts, axis=1) * gk_ref[...]
    v_ref[...] = jnp.dot(h, wmv_ref[...], preferred_element_type=F32)


def _memory_kv(mem2d, g, wmk, wmv, gk, tm):
    n, d = mem2d.shape
    consts = (g, wmk, wmv, gk)
    return pl.pallas_call(
        _memory_kv_kernel,
        grid=(n // tm,),
        in_specs=[pl.BlockSpec((tm, d), lambda i: (i, 0))] + [_const_spec(a.shape) for a in consts],
        out_specs=[pl.BlockSpec((tm, d), lambda i: (i, 0)), pl.BlockSpec((tm, d), lambda i: (i, 0))],
        out_shape=[jax.ShapeDtypeStruct((n, d), F32), jax.ShapeDtypeStruct((n, d), F32)],
        compiler_params=pltpu.CompilerParams(dimension_semantics=("arbitrary",),
                                             vmem_limit_bytes=VMEM_LIMIT),
        name="memory_kv",
    )(mem2d, *consts)


def _cross_prompt_kernel(x1_ref, mk_ref, mv_ref, xag_ref, wmq_ref, gmq_ref, wmo_ref, fg_ref, wr_ref, br_ref, tri_ref,
                         x2_ref, hn_ref, route_ref, info_ref, cnt_ref, base):
    @pl.when((pl.program_id(0) == 0) & (pl.program_id(1) == 0))
    def _():
        base[...] = jnp.zeros_like(base)

    x1 = x1_ref[...]
    q = _mem_query(x1, xag_ref, wmq_ref, gmq_ref)
    o = _mem_attend(q, _split_heads(mk_ref[...].astype(BF16)), _split_heads(mv_ref[...].astype(BF16)))
    x2, hn, route, info, new_base = _cross_tail(o, x1, wmo_ref, fg_ref, wr_ref, br_ref, tri_ref, base[...])
    x2_ref[...] = x2
    hn_ref[...] = hn
    route_ref[...] = route
    info_ref[...] = info
    base[...] = new_base
    cnt_ref[...] = new_base


def _cross_prompt(x1, mk, mv, consts, tm):
    b, s, d = x1.shape
    m = mk.shape[1]
    cspecs = [_const_spec(a.shape) for a in consts]
    return pl.pallas_call(
        _cross_prompt_kernel,
        grid=(b, s // tm),
        in_specs=[pl.BlockSpec((None, tm, d), lambda i, j: (i, j, 0)),
                  pl.BlockSpec((None, m, d), lambda i, j: (i, 0, 0)),
                  pl.BlockSpec((None, m, d), lambda i, j: (i, 0, 0))] + cspecs,
        out_specs=[pl.BlockSpec((None, tm, d), lambda i, j: (i, j, 0)),
                   pl.BlockSpec((None, tm, d // 2), lambda i, j: (i, j, 0)),
                   pl.BlockSpec((None, tm, LANES), lambda i, j: (i, j, 0)),
                   pl.BlockSpec((ROUTE_ROWS, tm), lambda i, j: (0, i * (s // tm) + j)),
                   pl.BlockSpec((1, LANES), lambda i, j: (0, 0))],
        out_shape=[jax.ShapeDtypeStruct((b, s, d), F32),
                   jax.ShapeDtypeStruct((b, s, d // 2), jnp.int32),
                   jax.ShapeDtypeStruct((b, s, LANES), F32),
                   jax.ShapeDtypeStruct((ROUTE_ROWS, b * s), F32),
                   jax.ShapeDtypeStruct((1, LANES), F32)],
        scratch_shapes=[pltpu.VMEM((1, LANES), F32)],
        compiler_params=pltpu.CompilerParams(dimension_semantics=("arbitrary", "arbitrary"),
                                             vmem_limit_bytes=VMEM_LIMIT),
        name="cross_prompt",
    )(x1, mk, mv, *consts)


def _mem_attend_sample_kernel(qm_ref, *refs, g, ts):
    k_refs, v_refs, o_ref, s_scr = refs[:g], refs[g:2 * g], refs[2 * g], refs[2 * g + 1]
    halves = MEM_HEAD_DIM // LANES
    rows_per_pos = MEM_HEADS * halves
    mlen = k_refs[0].shape[0] // rows_per_pos

    def all_heads(ref):
        return jnp.concatenate([ref[pl.ds(c * MEM_HEADS + hd, mlen, stride=rows_per_pos), :]
                                for hd in range(MEM_HEADS) for c in range(halves)], axis=1).astype(BF16)

    rows = MEM_HEADS * ts
    own = (lax.broadcasted_iota(jnp.int32, (rows, MEM_HEADS * MEM_HEAD_DIM), 1) // MEM_HEAD_DIM
           == lax.broadcasted_iota(jnp.int32, (rows, MEM_HEADS * MEM_HEAD_DIM), 0) // ts)
    for n in range(g):
        q = qm_ref[n * ts:(n + 1) * ts, :]
        q_bd = jnp.where(own, jnp.concatenate([q] * MEM_HEADS, axis=0), 0.0).astype(BF16)
        s_scr[n] = lax.dot_general(q_bd, all_heads(k_refs[n]), (((1,), (1,)), ((), ())),
                                   preferred_element_type=F32)
    s = s_scr[...]
    p = jnp.exp(s - jnp.max(s, axis=-1, keepdims=True))
    s_scr[...] = p * (1.0 / jnp.sum(p, axis=-1, keepdims=True))
    for n in range(g):
        r = jnp.dot(s_scr[n].astype(BF16), all_heads(v_refs[n]), preferred_element_type=F32)
        o_ref[n * ts:(n + 1) * ts, :] = jnp.concatenate(
            [r[hd * ts:(hd + 1) * ts, hd * MEM_HEAD_DIM:(hd + 1) * MEM_HEAD_DIM] for hd in range(MEM_HEADS)], axis=1)


def _mem_cache_rows(cache):
    nb, mlen = cache.shape[:2]
    halves = MEM_HEAD_DIM // LANES
    return (cache.reshape(nb, mlen, MEM_HEADS, halves, LANES).transpose(0, 1, 3, 2, 4)
            .reshape(nb, mlen * halves * MEM_HEADS, LANES))


def _mem_attend_sample(qm, mk, mv, g, ts):
    n, d = qm.shape
    nb, rows = mk.shape[0], mk.shape[1]
    kern = functools.partial(_mem_attend_sample_kernel, g=g, ts=ts)

    def seq_spec(j):
        return pl.BlockSpec((None, rows, LANES), lambda i: (g * i + j, 0, 0))

    return pl.pallas_call(
        kern,
        grid=(nb // g,),
        in_specs=[pl.BlockSpec((g * ts, d), lambda i: (i, 0))] + [seq_spec(j) for j in range(g)] * 2,
        out_specs=pl.BlockSpec((g * ts, d), lambda i: (i, 0)),
        out_shape=jax.ShapeDtypeStruct((n, d), F32),
        scratch_shapes=[pltpu.VMEM((g, MEM_HEADS * ts, rows // (MEM_HEADS * (MEM_HEAD_DIM // LANES))), F32)],
        compiler_params=pltpu.CompilerParams(dimension_semantics=("arbitrary",),
                                             vmem_limit_bytes=VMEM_LIMIT),
        name="mem_attend_sample",
    )(qm, *([mk] * g), *([mv] * g))


def _cross_tail_kernel(o_ref, x1_ref, wmo_ref, fg_ref, wr_ref, br_ref, tri_ref,
                       x2_ref, hn_ref, route_ref, info_ref, cnt_ref, base):
    @pl.when(pl.program_id(0) == 0)
    def _():
        base[...] = jnp.zeros_like(base)

    x2, hn, route, info, new_base = _cross_tail(o_ref[...], x1_ref[...], wmo_ref, fg_ref, wr_ref, br_ref, tri_ref,
                                                base[...])
    x2_ref[...] = x2
    hn_ref[...] = hn
    route_ref[...] = route
    info_ref[...] = info
    base[...] = new_base
    cnt_ref[...] = new_base


def _cross_tail_call(o, x1, consts, tm):
    n, d = x1.shape
    cspecs = [_const_spec(a.shape) for a in consts]
    return pl.pallas_call(
        _cross_tail_kernel,
        grid=(n // tm,),
        in_specs=[pl.BlockSpec((tm, d), lambda i: (i, 0)), pl.BlockSpec((tm, d), lambda i: (i, 0))] + cspecs,
        out_specs=[pl.BlockSpec((tm, d), lambda i: (i, 0)),
                   pl.BlockSpec((tm, d // 2), lambda i: (i, 0)),
                   pl.BlockSpec((tm, LANES), lambda i: (i, 0)),
                   pl.BlockSpec((ROUTE_ROWS, tm), lambda i: (0, i)),
                   pl.BlockSpec((1, LANES), lambda i: (0, 0))],
        out_shape=[jax.ShapeDtypeStruct((n, d), F32),
                   jax.ShapeDtypeStruct((n, d // 2), jnp.int32),
                   jax.ShapeDtypeStruct((n, LANES), F32),
                   jax.ShapeDtypeStruct((ROUTE_ROWS, n), F32),
                   jax.ShapeDtypeStruct((1, LANES), F32)],
        scratch_shapes=[pltpu.VMEM((1, LANES), F32)],
        compiler_params=pltpu.CompilerParams(dimension_semantics=("arbitrary",),
                                             vmem_limit_bytes=VMEM_LIMIT),
        name="cross_tail",
    )(o, x1, *consts)


def _sc_rows_per_step(per_worker):
    step = min(SC_ROWS_PER_STEP, per_worker)
    assert per_worker % step == 0 and step % 8 == 0
    return step


def _sc_gather_rows(table, idx):
    nrows = idx.shape[0]
    _, width = table.shape
    assert nrows % (8 * SC_WORKERS) == 0
    per_worker = nrows // SC_WORKERS
    step = _sc_rows_per_step(per_worker // 2)
    mesh = plsc.VectorSubcoreMesh(core_axis_name="c", subcore_axis_name="s")

    @functools.partial(
        pl.kernel, mesh=mesh, out_type=jax.ShapeDtypeStruct((nrows, width), table.dtype),
        scratch_types=[pltpu.VMEM((step,), jnp.int32)] * 2 + [pltpu.VMEM((step, width), table.dtype)] * 2
        + [pltpu.SemaphoreType.DMA] * 2)
    def gather(table_hbm, idx_hbm, out_hbm, idx_a, idx_b, rows_a, rows_b, sem_a, sem_b):
        wid = lax.axis_index("s") * SC_CORES + lax.axis_index("c")
        base = wid * per_worker

        @pl.loop(0, per_worker // (2 * step))
        def _(i):
            off = base + i * (2 * step)
            pltpu.sync_copy(idx_hbm.at[pl.ds(off, step)], idx_a)
            pltpu.sync_copy(idx_hbm.at[pl.ds(off + step, step)], idx_b)
            gather_a = pltpu.async_copy(table_hbm.at[idx_a], rows_a, sem_a)
            gather_b = pltpu.async_copy(table_hbm.at[idx_b], rows_b, sem_b)
            gather_a.wait()
            write_a = pltpu.async_copy(rows_a, out_hbm.at[pl.ds(off, step)], sem_a)
            gather_b.wait()
            write_b = pltpu.async_copy(rows_b, out_hbm.at[pl.ds(off + step, step)], sem_b)
            write_a.wait()
            write_b.wait()

    return gather(table, idx)


def _sc_scatter_rows(table, pos, nrows_out):
    n, width = table.shape
    assert n % (8 * SC_WORKERS) == 0 and pos.shape == (TOP_K * n,)
    per_worker = n // SC_WORKERS
    step = _sc_rows_per_step(per_worker)
    mesh = plsc.VectorSubcoreMesh(core_axis_name="c", subcore_axis_name="s")

    @functools.partial(
        pl.kernel, mesh=mesh, out_type=jax.ShapeDtypeStruct((nrows_out, width), table.dtype),
        scratch_types=[pltpu.VMEM((step,), jnp.int32)] * TOP_K
        + [pltpu.VMEM((step, width), table.dtype), pltpu.SemaphoreType.DMA])
    def scatter(table_hbm, pos_hbm, out_hbm, *scratch):
        idx_vs, rows_v, sem = scratch[:TOP_K], scratch[TOP_K], scratch[TOP_K + 1]
        wid = lax.axis_index("s") * SC_CORES + lax.axis_index("c")
        base = wid * per_worker

        @pl.loop(0, per_worker // step)
        def _(i):
            off = base + i * step
            for k in range(TOP_K):
                pltpu.sync_copy(pos_hbm.at[pl.ds(k * n + off, step)], idx_vs[k])
            pltpu.sync_copy(table_hbm.at[pl.ds(off, step)], rows_v)
            for k in range(TOP_K):
                pltpu.async_copy(rows_v, out_hbm.at[idx_vs[k]], sem).wait()

    return scatter(table, pos)


W_CHUNKS = 4
FETCH_AHEAD = 3
FETCH_AHEAD_F32 = 2


def _expert_weight_copies(e, slot, w_hbm, w_slots, sems, chunks):
    copies = []
    for m, (src, dst) in enumerate(zip(w_hbm, w_slots)):
        rows = dst.shape[1] // chunks
        for c in range(chunks):
            sl = pl.ds(c * rows, rows)
            sem = sems.at[(slot * len(w_hbm) + m) * chunks + c]
            copies.append(pltpu.make_async_copy(src.at[e, sl, :], dst.at[slot, sl, :], sem))
    return copies


def _moe_grouped_kernel(te_ref, nv_ref, rv_ref, slot_ref, *refs, publish, ahead, chunks):
    later_refs, (xs_ref, wg_hbm, wu_hbm, wd_hbm, ys_ref), rest = refs[:ahead], refs[ahead:ahead + 5], refs[ahead + 5:]
    i = pl.program_id(0)
    n_slots = ahead + 1
    w_hbm = (wg_hbm, wu_hbm, wd_hbm)
    if publish:
        out_hbm, w_slots, w_bf16, sems, out_sems = rest[0:3], rest[3:6], rest[6:9], rest[9], rest[10]
    else:
        w_slots, sems = rest[0:3], rest[3]

    def publish_copies(e):
        return [pltpu.make_async_copy(src, dst.at[e], out_sems.at[m])
                for m, (src, dst) in enumerate(zip(w_bf16, out_hbm))]

    def fetch(e, slot):
        return _expert_weight_copies(e, slot, w_hbm, w_slots, sems, chunks)

    @pl.when(i == 0)
    def _():
        for cp in fetch(te_ref[0], slot_ref[0]):
            cp.start()
        for k in range(ahead - 1):
            @pl.when(later_refs[k][0] >= 0)
            def _():
                for cp in fetch(later_refs[k][0], (slot_ref[0] + k + 1) % n_slots):
                    cp.start()

    @pl.when(i < nv_ref[0])
    def _():
        slot = slot_ref[i]

        @pl.when((i == 0) | (te_ref[i] != te_ref[jnp.maximum(i - 1, 0)]))
        def _():
            for cp in fetch(te_ref[i], slot):
                cp.wait()
            if publish:
                @pl.when(i > 0)
                def _():
                    for cp in publish_copies(te_ref[jnp.maximum(i - 1, 0)]):
                        cp.wait()

                for src, dst in zip(w_slots, w_bf16):
                    dst[...] = src[slot].astype(BF16)
                for cp in publish_copies(te_ref[i]):
                    cp.start()

            @pl.when(later_refs[ahead - 1][i] >= 0)
            def _():
                for cp in fetch(later_refs[ahead - 1][i], (slot + ahead) % n_slots):
                    cp.start()

        wg, wu, wd = [r[...] for r in w_bf16] if publish else [r[slot] for r in w_slots]
        xs = xs_ref[...]
        row = lax.broadcasted_iota(jnp.int32, xs.shape, 0)
        x = _unpack_bf16_pairs(jnp.where(row < rv_ref[i], xs, 0)).astype(BF16)
        a = jnp.dot(x, wg, preferred_element_type=F32)
        b = jnp.dot(x, wu, preferred_element_type=F32)
        act = (a * jax.nn.sigmoid(a)) * b
        y = jnp.dot(act.astype(BF16), wd, preferred_element_type=F32)
        ys_ref[...] = _pack_bf16_pairs(y)

    if publish:
        @pl.when(i == pl.num_programs(0) - 1)
        def _():
            for cp in publish_copies(te_ref[nv_ref[0] - 1]):
                cp.wait()


def _moe_grouped(tile_expert, n_valid, rows_valid, run_slot, later_experts, xs, wg, wu, wd, tm, publish):
    p, half = xs.shape
    ne, d, f = wg.shape
    ahead = len(later_experts)
    chunks = W_CHUNKS if publish else 1

    def live_tile(i, te, nv, *_):
        return (jnp.minimum(i, nv[0] - 1), 0)

    slot_dtype = F32 if publish else BF16
    n_slots = ahead + 1
    scratch = [pltpu.VMEM((n_slots, d, f), slot_dtype), pltpu.VMEM((n_slots, d, f), slot_dtype),
               pltpu.VMEM((n_slots, f, d), slot_dtype)]
    out_specs = [pl.BlockSpec((tm, half), live_tile)]
    out_shape = [jax.ShapeDtypeStruct((p, half), jnp.int32)]
    if publish:
        scratch += [pltpu.VMEM((d, f), BF16), pltpu.VMEM((d, f), BF16), pltpu.VMEM((f, d), BF16)]
        out_specs += [pl.BlockSpec(memory_space=pl.ANY)] * 3
        out_shape += [jax.ShapeDtypeStruct(w.shape, BF16) for w in (wg, wu, wd)]
    scratch += [pltpu.SemaphoreType.DMA((n_slots * 3 * chunks,))]
    if publish:
        scratch += [pltpu.SemaphoreType.DMA((3,))]
    grid_spec = pltpu.PrefetchScalarGridSpec(
        num_scalar_prefetch=4 + ahead,
        grid=(p // tm,),
        in_specs=[pl.BlockSpec((tm, half), live_tile)] + [pl.BlockSpec(memory_space=pl.ANY)] * 3,
        out_specs=out_specs,
        scratch_shapes=scratch)
    res = pl.pallas_call(
        functools.partial(_moe_grouped_kernel, publish=publish, ahead=ahead, chunks=chunks),
        grid_spec=grid_spec,
        out_shape=out_shape,
        compiler_params=pltpu.CompilerParams(dimension_semantics=("arbitrary",),
                                             vmem_limit_bytes=VMEM_LIMIT),
        name="moe_grouped",
    )(tile_expert, n_valid, rows_valid, run_slot, *later_experts, xs, wg, wu, wd)
    return res[0], tuple(res[1:])


def _moe_combine_kernel(x2_ref, route_ref, y0_ref, y1_ref, *rest):
    out_ref = rest[-1]
    route = route_ref[...]
    out_ref[...] = (x2_ref[...] + route[:, 2:3] * _unpack_bf16_pairs(y0_ref[...])
                    + route[:, 3:4] * _unpack_bf16_pairs(y1_ref[...]))


def _moe_combine(x2, route, gathered, tm, first_tile, partial):
    n, d = x2.shape
    tiles = gathered.shape[0] // (TOP_K * tm)
    in_specs = [pl.BlockSpec((tm, d), lambda i: (i + first_tile, 0)),
                pl.BlockSpec((tm, LANES), lambda i: (i + first_tile, 0)),
                pl.BlockSpec((tm, d // 2), lambda i: (i, 0)),
                pl.BlockSpec((tm, d // 2), lambda i: (i + tiles, 0))]
    args = [x2, route, gathered, gathered]
    aliases = {}
    if partial is not None:
        in_specs.append(pl.BlockSpec(memory_space=pl.ANY))
        args.append(partial)
        aliases = {len(args) - 1: 0}
    return pl.pallas_call(
        _moe_combine_kernel,
        grid=(tiles,),
        in_specs=in_specs,
        out_specs=pl.BlockSpec((tm, d), lambda i: (i + first_tile, 0)),
        out_shape=jax.ShapeDtypeStruct((n, d), F32),
        input_output_aliases=aliases,
        compiler_params=pltpu.CompilerParams(dimension_semantics=("arbitrary",),
                                             vmem_limit_bytes=VMEM_LIMIT),
        name="moe_combine",
    )(*args)


def _moe(x2, hn_packed, route, info, counts, weights, tm_rows, publish):
    n = x2.shape[0]
    p = (TOP_K * n // tm_rows + N_EXPERTS) * tm_rows
    n_tiles = p // tm_rows
    cnt = counts[0, :N_EXPERTS].astype(jnp.int32)
    tiles_per_e = jnp.maximum((cnt + tm_rows - 1) // tm_rows, 1 if publish else 0)
    e_ids = jnp.arange(N_EXPERTS, dtype=jnp.int32)
    tile_end = jnp.sum(jnp.where(e_ids[None, :] <= e_ids[:, None], tiles_per_e[None, :], 0), axis=1)
    tile_start = tile_end - tiles_per_e
    n_valid = tile_end[-1:]
    tile_ids = jnp.arange(n_tiles, dtype=jnp.int32)
    tile_expert = jnp.minimum(jnp.sum((tile_end[None, :] <= tile_ids[:, None]).astype(jnp.int32), axis=1),
                              N_EXPERTS - 1)
    mine = tile_expert[:, None] == e_ids[None, :]
    rows_left = jnp.sum(jnp.where(mine, cnt - (tile_ids[:, None] - tile_start) * tm_rows, 0), axis=1)
    rows_valid = jnp.clip(rows_left, 0, tm_rows).astype(jnp.int32)
    ahead = FETCH_AHEAD_F32 if publish else FETCH_AHEAD
    has_tiles = tiles_per_e > 0
    run_of_e = jnp.sum((has_tiles[None, :] & (e_ids[None, :] < e_ids[:, None])).astype(jnp.int32), axis=1)
    run_of_tile = jnp.sum(jnp.where(mine, run_of_e, 0), axis=1)
    run_slot = (run_of_tile % (ahead + 1)).astype(jnp.int32)
    later_experts = []
    for k in range(1, ahead + 1):
        is_run = has_tiles[None, :] & (run_of_e[None, :] == run_of_tile[:, None] + k)
        later_experts.append(jnp.sum(jnp.where(is_run, e_ids[None, :] + 1, 0), axis=1).astype(jnp.int32) - 1)
    eidx = info[0:TOP_K].astype(jnp.int32)
    row_start = jnp.sum(jnp.where(eidx[None] == e_ids[:, None, None], (tile_start * tm_rows)[:, None, None], 0),
                        axis=0)
    pos = (row_start + info[4:4 + TOP_K].astype(jnp.int32)).reshape(-1)

    xs = _sc_scatter_rows(hn_packed, pos, p)
    ys, w_bf16 = _moe_grouped(tile_expert, n_valid, rows_valid, run_slot, later_experts, xs, *weights, tm_rows,
                              publish)
    tm = min(TM_COMBINE, n)
    parts = COMBINE_PARTS if n % (COMBINE_PARTS * tm) == 0 else 1
    pos_by_choice = pos.reshape(TOP_K, n)
    out = None
    for part in range(parts):
        lo, hi = part * n // parts, (part + 1) * n // parts
        back = _sc_gather_rows(ys, pos_by_choice[:, lo:hi].reshape(-1))
        out = _moe_combine(x2, route, back, tm, lo // tm, out)
    return out, w_bf16


def _block_diag(n, width):
    idx = np.arange(n) // width
    return jnp.asarray((idx[:, None] == idx[None, :]).astype(np.float32), dtype=BF16)


def _kv_feature_major(cache):
    nb, wb = cache.shape[:2]
    return jnp.transpose(cache, (0, 2, 3, 1)).reshape(nb, KV_DIM, wb)


def _strict_lower(n):
    r = np.arange(n)
    return jnp.asarray((r[:, None] > r[None, :]).astype(np.float32), dtype=BF16)


def _alibi_slopes():
    return np.exp2(-8.0 * np.arange(1, N_HEADS + 1, dtype=np.float32) / N_HEADS).astype(np.float32)


def _prompt_bias():
    i = np.arange(BLOCK)[:, None]
    s = np.arange(2 * BLOCK)[None, :]
    dist = (i + BLOCK - s).astype(np.float32)
    mask = (dist >= 0) & (dist < WINDOW)
    first = mask & (s >= BLOCK)
    slopes = _alibi_slopes()[:, None, None]
    reg = np.where(mask[None], -slopes * dist[None], np.float32(NEG_INF))
    fst = np.where(first[None], -slopes * dist[None], np.float32(NEG_INF))
    return jnp.asarray(np.concatenate([reg, fst], axis=0).astype(np.float32))


def _sample_bias(ts, wb):
    i = np.arange(ts)[:, None]
    s = np.arange(wb + ts)[None, :]
    dist = (i + wb - s).astype(np.float32)
    mask = (dist >= 0) & (dist < WINDOW)
    slopes = _alibi_slopes()[:, None, None]
    b = np.where(mask[None], -slopes * dist[None], np.float32(NEG_INF)).astype(np.float32)
    return jnp.asarray(b.reshape(N_HEADS * ts, wb + ts))


def kernel(x_prompt, x_sample, cache_swa_k, cache_swa_v, state_conv, cache_mem_k, cache_mem_v, mem_prompt, norm_mix_g, w_in, q_norm_g, k_norm_g, attn_sinks, conv_dw_w, conv_dw_b, conv_ln_g, conv_ln_b, w_out, norm_xa_g, norm_mem_g, w_mq, w_mk, w_mv, mq_norm_g, mk_norm_g, w_mo, norm_ffn_g, w_router_group, b_router_group, w_router_expert, b_router_expert, w_exp_gate, w_exp_up, w_exp_down):
    depth = w_in.shape[0]
    bp, sp, d = x_prompt.shape
    nb, ts, _ = x_sample.shape
    wb = cache_swa_k.shape[2]
    mlen = mem_prompt.shape[1]
    assert d == D_MODEL and wb == WINDOW and sp % TM_PROMPT == 0 and sp % TM_CROSS == 0 and nb % SEQ_PER_STEP == 0

    bdq = _block_diag(MXU_TILE, HEAD_DIM)
    bdk = _block_diag(KV_DIM, HEAD_DIM)
    bias_p = _prompt_bias()
    bias_s = _sample_bias(ts, wb)
    lane_lo = (np.arange(D_ATTN) % LANES) < HALF
    row = lambda a: a.reshape(1, -1).astype(F32)

    xp = x_prompt
    xs = x_sample.reshape(nb * ts, d)
    kp_l, vp_l, cp_l, mkp_l, mvp_l, ks_l, vs_l, cs_l = [], [], [], [], [], [], [], []
    for l in range(depth):
        gq = jnp.tile(q_norm_g[l].astype(F32), N_HEADS) * (HEAD_DIM ** -0.5)
        gqlo = jnp.where(lane_lo, gq, 0.0).reshape(1, -1)
        gqhi = jnp.where(lane_lo, 0.0, gq).reshape(1, -1)
        gk = jnp.tile(k_norm_g[l].astype(F32), N_KV_HEADS).reshape(1, -1)
        sinks = attn_sinks[l].astype(F32)
        sinkrow = jnp.repeat(sinks, ts).reshape(N_HEADS * ts, 1)
        win = w_in[l].astype(BF16)
        wout = w_out[l].astype(BF16)
        wmq = w_mq[l].astype(BF16)
        wmo = w_mo[l].astype(BF16)
        gmq = (jnp.tile(mq_norm_g[l].astype(F32), MEM_HEADS) * (MEM_HEAD_DIM ** -0.5)).reshape(1, -1)
        gmk = jnp.tile(mk_norm_g[l].astype(F32), MEM_HEADS).reshape(1, -1)
        w_r = jnp.concatenate([w_router_expert[l], w_router_group[l],
                               jnp.zeros((d, LANES - N_EXPERTS - N_GROUPS), F32)], axis=1).astype(BF16)
        b_r = jnp.concatenate([b_router_expert[l], b_router_group[l],
                               jnp.zeros((LANES - N_EXPERTS - N_GROUPS,), F32)]).reshape(1, -1).astype(F32)
        wg, wu, wd = w_exp_gate[l], w_exp_up[l], w_exp_down[l]

        mix_consts = (row(norm_mix_g[l]), win, bdq, bdk, gqlo, gqhi, gk)
        conv_consts = (conv_dw_w[l].astype(F32), row(conv_dw_b[l]), row(conv_ln_g[l]), row(conv_ln_b[l]), wout)
        tail_consts = (wmo, row(norm_ffn_g[l]), w_r, b_r)

        x1p, kp, vp, cp = _mixer_prompt(xp, sinks, mix_consts + (bias_p,) + conv_consts, TM_PROMPT)
        mk, mv = _memory_kv(mem_prompt.reshape(bp * mlen, d), row(norm_mem_g[l]),
                            w_mk[l].astype(BF16), w_mv[l].astype(BF16), gmk, min(256, bp * mlen))
        mk = mk.reshape(bp, mlen, d)
        mv = mv.reshape(bp, mlen, d)
        x2p, hnp, routep, infop, cntp = _cross_prompt(x1p, mk, mv, (row(norm_xa_g[l]), wmq, gmq) + tail_consts
                                               + (_strict_lower(RANK_BLOCK),), TM_CROSS)
        xp, w_bf16 = _moe(x2p.reshape(bp * sp, d), hnp.reshape(bp * sp, d // 2), routep.reshape(bp * sp, LANES),
                          infop, cntp, (wg, wu, wd), TM_ROWS_PROMPT, True)
        xp = xp.reshape(bp, sp, d)
        kp_l.append(kp.reshape(bp, BLOCK, N_KV_HEADS, HEAD_DIM))
        vp_l.append(vp.reshape(bp, BLOCK, N_KV_HEADS, HEAD_DIM))
        cp_l.append(cp)
        mkp_l.append(mk.reshape(bp, mlen, MEM_HEADS, MEM_HEAD_DIM))
        mvp_l.append(mv.reshape(bp, mlen, MEM_HEADS, MEM_HEAD_DIM))

        x1s, qm, ksn, vsn, csn = _mixer_sample(
            xs, _kv_feature_major(cache_swa_k[l]), _kv_feature_major(cache_swa_v[l]),
            jnp.transpose(state_conv[l], (1, 0, 2)),
            mix_consts + (bias_s, sinkrow) + conv_consts + (row(norm_xa_g[l]), wmq, gmq), SEQ_PER_STEP, ts)
        o_s = _mem_attend_sample(qm, _mem_cache_rows(cache_mem_k[l]), _mem_cache_rows(cache_mem_v[l]),
                                 MEM_SEQ_PER_STEP, ts)
        tms = min(256, nb * ts)
        x2s, hns, routes, infos, cnts = _cross_tail_call(o_s, x1s, tail_consts + (_strict_lower(RANK_BLOCK),), tms)
        xs, _ = _moe(x2s, hns, routes, infos, cnts, w_bf16, TM_ROWS_SAMPLE, False)
        ks_l.append(jnp.transpose(ksn.reshape(nb, N_KV_HEADS, HEAD_DIM, wb), (0, 3, 1, 2)))
        vs_l.append(jnp.transpose(vsn.reshape(nb, N_KV_HEADS, HEAD_DIM, wb), (0, 3, 1, 2)))
        cs_l.append(jnp.transpose(csn, (1, 0, 2)))

    st = lambda xs_: jnp.stack(xs_, axis=0)
    return (xp, xs.reshape(nb, ts, d), st(kp_l), st(vp_l), st(cp_l), st(mkp_l), st(mvp_l),
            st(ks_l), st(vs_l), st(cs_l))
```

```python
import functools

import numpy as np
import jax
import jax.numpy as jnp
from jax import lax
from jax.experimental import pallas as pl
from jax.experimental.pallas import tpu as pltpu
from jax.experimental.pallas import tpu_sc as plsc

F32 = jnp.float32
BF16 = jnp.bfloat16

D_MODEL = 1024
D_ATTN = 512
D_CONV = 512
HEAD_DIM = 64
N_HEADS = 8
N_KV_HEADS = 2
KV_DIM = N_KV_HEADS * HEAD_DIM
HEADS_PER_KV = N_HEADS // N_KV_HEADS
WINDOW = 128
BLOCK = 128
CONV_WIDTH = 31
CONV_PAST = CONV_WIDTH - 1
MEM_HEADS = 4
MEM_HEAD_DIM = 256
N_GROUPS = 4
EXPERTS_PER_GROUP = 8
N_EXPERTS = 32
EPS = 1e-6
NEG_INF = -1e30

LANES = 128
HALF = LANES // 2
MXU_TILE = 256
SUBLANES = 8
CONV_PAD = 32
CONV_ROWS = 64
VMEM_LIMIT = 56 * 1024 * 1024

TM_PROMPT = 1024
TM_CROSS = 1024
SEQ_PER_STEP = 32
MEM_SEQ_PER_STEP = 8
TOP_K = 2
RANK_BLOCK = 256
ROUTE_ROWS = 8
SC_CORES = 2
SC_WORKERS = 32
SC_ROWS_PER_STEP = 64
TM_ROWS_PROMPT = 512
TM_ROWS_SAMPLE = 128
TM_COMBINE = 1024


def _rms(x, g):
    ms = jnp.mean(x * x, axis=-1, keepdims=True)
    return x * lax.rsqrt(ms + EPS) * g


def _group_mean_sq(x, bd_ref, width):
    x2 = x * x
    hi = x2.astype(BF16)
    lo = (x2 - hi.astype(F32)).astype(BF16)
    bd = bd_ref[...]
    w = bd.shape[0]
    parts = [jnp.dot(hi[:, c:c + w], bd, preferred_element_type=F32)
             + jnp.dot(lo[:, c:c + w], bd, preferred_element_type=F32) for c in range(0, x.shape[1], w)]
    return jnp.concatenate(parts, axis=1) * (1.0 / width)


def _mixer_proj(x, ng_ref, win_ref, bdq_ref, bdk_ref, gqlo_ref, gqhi_ref, gk_ref):
    h = _rms(x, ng_ref[...]).astype(BF16)
    p = jnp.dot(h, win_ref[...], preferred_element_type=F32)
    q = p[:, :D_ATTN]
    k = p[:, D_ATTN:D_ATTN + KV_DIM]
    v = p[:, D_ATTN + KV_DIM:D_ATTN + 2 * KV_DIM]
    ua = p[:, D_ATTN + 2 * KV_DIM:D_ATTN + 2 * KV_DIM + D_CONV]
    ub = p[:, D_ATTN + 2 * KV_DIM + D_CONV:]
    qn = q * lax.rsqrt(_group_mean_sq(q, bdq_ref, HEAD_DIM) + EPS)
    q_lo = qn * gqlo_ref[...]
    q_hi = qn * gqhi_ref[...]
    kn = k * lax.rsqrt(_group_mean_sq(k, bdk_ref, HEAD_DIM) + EPS) * gk_ref[...]
    u = ua * jax.nn.sigmoid(ub)
    return q_lo, q_hi, kn, v, u


def _dup_halves(x):
    lo = lax.broadcasted_iota(jnp.int32, x.shape, 1) < HALF
    xr = pltpu.roll(x, HALF, axis=1)
    return jnp.where(lo, x, xr), jnp.where(lo, xr, x)


def _sink_softmax(s, sink):
    m = jnp.maximum(jnp.max(s, axis=-1, keepdims=True), sink)
    p = jnp.exp(s - m)
    denom = jnp.sum(p, axis=-1, keepdims=True) + jnp.exp(sink - m)
    return p * (1.0 / denom)


def _conv_ln_silu(y, cb_ref, lg_ref, lb_ref):
    y = y + cb_ref[...]
    mu = jnp.mean(y, axis=-1, keepdims=True)
    yc = y - mu
    yn = yc * lax.rsqrt(jnp.mean(yc * yc, axis=-1, keepdims=True) + EPS)
    z = yn * lg_ref[...] + lb_ref[...]
    return z * jax.nn.sigmoid(z)


def _mem_query(x1, xag_ref, wmq_ref, gmq_ref):
    h = _rms(x1, xag_ref[...]).astype(BF16)
    q = jnp.dot(h, wmq_ref[...], preferred_element_type=F32)
    parts = []
    for hd in range(MEM_HEADS):
        qh = q[:, hd * MEM_HEAD_DIM:(hd + 1) * MEM_HEAD_DIM]
        parts.append(qh * lax.rsqrt(jnp.mean(qh * qh, axis=-1, keepdims=True) + EPS))
    return jnp.concatenate(parts, axis=1) * gmq_ref[...]


def _mem_attend(q, k_heads, v_heads):
    outs = []
    for hd in range(MEM_HEADS):
        sl = slice(hd * MEM_HEAD_DIM, (hd + 1) * MEM_HEAD_DIM)
        s = lax.dot_general(q[:, sl].astype(BF16), k_heads[hd], (((1,), (1,)), ((), ())),
                            preferred_element_type=F32)
        m = jnp.max(s, axis=-1, keepdims=True)
        p = jnp.exp(s - m)
        p = p * (1.0 / jnp.sum(p, axis=-1, keepdims=True))
        outs.append(jnp.dot(p.astype(BF16), v_heads[hd], preferred_element_type=F32))
    return jnp.concatenate(outs, axis=1)


def _split_heads(x):
    return [x[:, hd * MEM_HEAD_DIM:(hd + 1) * MEM_HEAD_DIM] for hd in range(MEM_HEADS)]


def _pack_bf16_pairs(y):
    n = y.shape[1] // 2
    bits = pltpu.bitcast(y.astype(BF16).astype(F32), jnp.int32)
    return (bits[:, :n] & jnp.int32(-65536)) | lax.shift_right_logical(bits[:, n:], jnp.int32(16))


def _unpack_bf16_pairs(w):
    hi = pltpu.bitcast(w & jnp.int32(-65536), F32)
    lo = pltpu.bitcast(lax.shift_left(w, jnp.int32(16)), F32)
    return jnp.concatenate([hi, lo], axis=1)


def _cross_tail(o, x1, wmo_ref, fg_ref, wr_ref, br_ref, tri_ref, base):
    x2 = x1 + jnp.dot(o.astype(BF16), wmo_ref[...], preferred_element_type=F32)
    hn_f = _rms(x2, fg_ref[...])
    hn = hn_f.astype(BF16)
    logits = jnp.dot(hn, wr_ref[...], preferred_element_type=F32) + br_ref[...]
    lane = lax.broadcasted_iota(jnp.int32, logits.shape, 1)
    is_g = (lane >= N_EXPERTS) & (lane < N_EXPERTS + N_GROUPS)
    lg = jnp.where(is_g, logits, NEG_INF)
    gmax = jnp.max(lg, axis=-1, keepdims=True)
    gsel = jnp.min(jnp.where(lg == gmax, lane, 2 * LANES), axis=-1, keepdims=True) - N_EXPERTS
    pg_sel = 1.0 / jnp.sum(jnp.exp(lg - gmax), axis=-1, keepdims=True)
    in_grp = (lane >= gsel * EXPERTS_PER_GROUP) & (lane < (gsel + 1) * EXPERTS_PER_GROUP)
    le = jnp.where(in_grp, logits, NEG_INF)
    top1 = jnp.max(le, axis=-1, keepdims=True)
    idx1 = jnp.min(jnp.where(le == top1, lane, 2 * LANES), axis=-1, keepdims=True)
    le2 = jnp.where(lane == idx1, NEG_INF, le)
    top2 = jnp.max(le2, axis=-1, keepdims=True)
    idx2 = jnp.min(jnp.where(le2 == top2, lane, 2 * LANES), axis=-1, keepdims=True)
    e2 = jnp.exp(top2 - top1)
    inv = 1.0 / (1.0 + e2)
    gate1 = pg_sel * inv
    gate2 = pg_sel * (e2 * inv)
    used = jnp.where((lane == idx1) | (lane == idx2), 1.0, 0.0)
    blk = tri_ref.shape[0]
    used_b = used.astype(BF16)
    parts, run = [], base
    for r0 in range(0, used.shape[0], blk):
        parts.append(jnp.dot(tri_ref[...], used_b[r0:r0 + blk], preferred_element_type=F32) + run)
        run = run + jnp.sum(used[r0:r0 + blk], axis=0, keepdims=True)
    before = jnp.concatenate(parts, axis=0)
    rank1 = jnp.sum(jnp.where(lane == idx1, before, 0.0), axis=-1, keepdims=True)
    rank2 = jnp.sum(jnp.where(lane == idx2, before, 0.0), axis=-1, keepdims=True)
    route = jnp.zeros_like(logits)
    for pos, val in enumerate((idx1.astype(F32), idx2.astype(F32), gate1, gate2, rank1, rank2)):
        route = jnp.where(lane == pos, val, route)
    info = jnp.transpose(route)[0:ROUTE_ROWS, :]
    return x2, _pack_bf16_pairs(hn_f), route, info, run


def _mixer_prompt_kernel(sink_ref, x_ref, ng_ref, win_ref, bdq_ref, bdk_ref, gqlo_ref, gqhi_ref, gk_ref,
                         bias_ref, cw_ref, cb_ref, lg_ref, lb_ref, wout_ref,
                         x1_ref, ko_ref, vo_ref, co_ref,
                         kband, vband, uext, ushift, *, tm, nt):
    t = pl.program_id(1)

    @pl.when(t == 0)
    def _():
        kband[0:BLOCK, :] = jnp.zeros((BLOCK, KV_DIM), F32)
        vband[0:BLOCK, :] = jnp.zeros((BLOCK, KV_DIM), F32)
        uext[0:CONV_PAD, :] = jnp.zeros((CONV_PAD, D_CONV), F32)

    x = x_ref[...]
    q_lo, q_hi, kn, v, u = _mixer_proj(x, ng_ref, win_ref, bdq_ref, bdk_ref, gqlo_ref, gqhi_ref, gk_ref)
    kband[BLOCK:BLOCK + tm, :] = kn
    vband[BLOCK:BLOCK + tm, :] = v
    uext[CONV_PAD:CONV_PAD + tm, :] = u

    @pl.when(t == nt - 1)
    def _():
        ko_ref[...] = kn[tm - BLOCK:, :]
        vo_ref[...] = v[tm - BLOCK:, :]
        co_ref[...] = uext[pl.ds(CONV_PAD + tm - CONV_PAST, CONV_PAST), :]

    kd = [a.astype(BF16) for a in _dup_halves(kband[...])]
    vd = _dup_halves(vband[...])
    lo = lax.broadcasted_iota(jnp.int32, vd[0].shape, 1) < HALF
    v_lo = [jnp.where(lo, a, 0.0).astype(BF16) for a in vd]
    v_hi = [jnp.where(lo, 0.0, a).astype(BF16) for a in vd]
    q_lo = q_lo.astype(BF16)
    q_hi = q_hi.astype(BF16)
    attn_blocks = []
    for j in range(tm // BLOCK):
        rows = slice(j * BLOCK, (j + 1) * BLOCK)
        keys = slice(j * BLOCK, j * BLOCK + 2 * BLOCK)
        bias_base = jnp.where(t == 0, N_HEADS, 0) if j == 0 else 0
        tiles = []
        for c in range(N_KV_HEADS):
            q4 = jnp.concatenate(
                [(q_lo if (HEADS_PER_KV * c + a) % 2 == 0 else q_hi)[rows,
                  ((HEADS_PER_KV * c + a) // 2) * LANES:((HEADS_PER_KV * c + a) // 2 + 1) * LANES]
                 for a in range(HEADS_PER_KV)], axis=0)
            s_all = lax.dot_general(q4, kd[c][keys], (((1,), (1,)), ((), ())), preferred_element_type=F32)
            ps = []
            for a in range(HEADS_PER_KV):
                hd = HEADS_PER_KV * c + a
                s = s_all[a * BLOCK:(a + 1) * BLOCK] + bias_ref[bias_base + hd]
                ps.append(_sink_softmax(s, sink_ref[hd]).astype(BF16))
            vstack = jnp.concatenate([v_lo[c][keys], v_hi[c][keys]], axis=0)
            for i2 in range(HEADS_PER_KV // 2):
                pp = jnp.concatenate([ps[2 * i2], ps[2 * i2 + 1]], axis=1)
                tiles.append(jnp.dot(pp, vstack, preferred_element_type=F32))
        attn_blocks.append(jnp.concatenate(tiles, axis=1))
    attn = jnp.concatenate(attn_blocks, axis=0)

    off = CONV_PAD - CONV_PAST
    span = tm + CONV_PAD - SUBLANES
    for s in range(1, SUBLANES):
        ushift[s - 1] = uext[pl.ds(s, span), :]
    chunks = []
    for lc in range(D_CONV // LANES):
        ls = slice(lc * LANES, (lc + 1) * LANES)
        accs = [jnp.zeros((CONV_ROWS, LANES), F32) for _ in range(tm // CONV_ROWS)]
        for j in range(CONV_WIDTH):
            s, a = (off + j) % SUBLANES, (off + j) // SUBLANES
            wj = cw_ref[j:j + 1, ls]
            for rc in range(tm // CONV_ROWS):
                r0 = rc * CONV_ROWS + a * SUBLANES
                tap = uext[r0:r0 + CONV_ROWS, ls] if s == 0 else ushift[s - 1, r0:r0 + CONV_ROWS, ls]
                accs[rc] = accs[rc] + wj * tap
        chunks.append(jnp.concatenate(accs, axis=0))
    y = _conv_ln_silu(jnp.concatenate(chunks, axis=1), cb_ref, lg_ref, lb_ref)

    x1_ref[...] = (x + jnp.dot(attn.astype(BF16), wout_ref[0:D_ATTN, :], preferred_element_type=F32)
                   + jnp.dot(y.astype(BF16), wout_ref[D_ATTN:, :], preferred_element_type=F32))

    kband[0:BLOCK, :] = kband[tm:tm + BLOCK, :]
    vband[0:BLOCK, :] = vband[tm:tm + BLOCK, :]
    uext[0:CONV_PAD, :] = uext[tm:tm + CONV_PAD, :]


def _const_spec(shape):
    nd = len(shape)
    return pl.BlockSpec(shape, lambda *_: (0,) * nd)


def _mixer_prompt(x, sinks, consts, tm):
    b, s, d = x.shape
    nt = s // tm
    (ng, win, bdq, bdk, gqlo, gqhi, gk, bias_p, cw, cb, lg, lb, wout) = consts
    kern = functools.partial(_mixer_prompt_kernel, tm=tm, nt=nt)
    cspecs = [_const_spec(a.shape) for a in consts]
    return pl.pallas_call(
        kern,
        grid=(b, nt),
        in_specs=[pl.BlockSpec(memory_space=pltpu.SMEM),
                  pl.BlockSpec((None, tm, d), lambda i, j: (i, j, 0))] + cspecs,
        out_specs=[pl.BlockSpec((None, tm, d), lambda i, j: (i, j, 0)),
                   pl.BlockSpec((None, BLOCK, KV_DIM), lambda i, j: (i, 0, 0)),
                   pl.BlockSpec((None, BLOCK, KV_DIM), lambda i, j: (i, 0, 0)),
                   pl.BlockSpec((None, CONV_PAST, D_CONV), lambda i, j: (i, 0, 0))],
        out_shape=[jax.ShapeDtypeStruct((b, s, d), F32),
                   jax.ShapeDtypeStruct((b, BLOCK, KV_DIM), F32),
                   jax.ShapeDtypeStruct((b, BLOCK, KV_DIM), F32),
                   jax.ShapeDtypeStruct((b, CONV_PAST, D_CONV), F32)],
        scratch_shapes=[pltpu.VMEM((BLOCK + tm, KV_DIM), F32),
                        pltpu.VMEM((BLOCK + tm, KV_DIM), F32),
                        pltpu.VMEM((CONV_PAD + tm, D_CONV), F32),
                        pltpu.VMEM((SUBLANES - 1, tm + CONV_PAD - SUBLANES, D_CONV), F32)],
        compiler_params=pltpu.CompilerParams(dimension_semantics=("arbitrary", "arbitrary"),
                                             vmem_limit_bytes=VMEM_LIMIT),
        name="mixer_prompt",
    )(sinks, x, *consts)


def _mixer_sample_kernel(x_ref, ck_ref, cv_ref, st_ref, ng_ref, win_ref, bdq_ref, bdk_ref, gqlo_ref, gqhi_ref,
                         gk_ref, bias_ref, sinkrow_ref, cw_ref, cb_ref, lg_ref, lb_ref, wout_ref,
                         xag_ref, wmq_ref, gmq_ref,
                         x1_ref, qm_ref, ko_ref, vo_ref, co_ref,
                         kall, vall, qs, s_scr, r_scr, u_scr, y_scr, *, g, ts):
    wb = ck_ref.shape[2]
    x = x_ref[...]
    q_lo, q_hi, kn, v, u = _mixer_proj(x, ng_ref, win_ref, bdq_ref, bdk_ref, gqlo_ref, gqhi_ref, gk_ref)
    kall[:, wb:wb + ts, :] = kn.reshape(g, ts, KV_DIM)
    vall[:, wb:wb + ts, :] = v.reshape(g, ts, KV_DIM)
    for n in range(g):
        kall[n, 0:wb, :] = ck_ref[n].T
        vall[n, 0:wb, :] = cv_ref[n].T
        ko_ref[n] = kall[n, ts:wb + ts, :].T
        vo_ref[n] = vall[n, ts:wb + ts, :].T

    for lc in range(D_CONV // LANES):
        ls = slice(lc * LANES, (lc + 1) * LANES)
        u_scr[lc] = u[:, ls]
        u_t = [u_scr[lc, pl.ds(t, g, stride=ts), :] for t in range(ts)]

        def frame(r):
            return st_ref[r, :, ls] if r < CONV_PAST else u_t[r - CONV_PAST]

        for t in range(ts):
            acc = jnp.zeros((g, LANES), F32)
            for j in range(CONV_WIDTH):
                acc = acc + cw_ref[j:j + 1, ls] * frame(t + j)
            y_scr[lc, pl.ds(t, g, stride=ts), :] = acc
        for r in range(CONV_PAST):
            co_ref[r, :, ls] = frame(r + ts)
    y = _conv_ln_silu(jnp.concatenate([y_scr[lc] for lc in range(D_CONV // LANES)], axis=1), cb_ref, lg_ref, lb_ref)

    for hd in range(N_HEADS):
        tile = (q_lo if hd % 2 == 0 else q_hi)[:, (hd // 2) * LANES:(hd // 2 + 1) * LANES]
        if hd % 2 != hd // HEADS_PER_KV:
            tile = pltpu.roll(tile, HALF, axis=1)
        qs[:, hd * ts:(hd + 1) * ts, :] = tile.reshape(g, ts, LANES)
    for n in range(g):
        s_scr[n] = lax.dot_general(qs[n].astype(BF16), kall[n].astype(BF16), (((1,), (1,)), ((), ())),
                                   preferred_element_type=F32)
    s_scr[...] = _sink_softmax(s_scr[...] + bias_ref[...], sinkrow_ref[...])
    for n in range(g):
        r_scr[n] = jnp.dot(s_scr[n].astype(BF16), vall[n].astype(BF16), preferred_element_type=F32)
    r = r_scr[...].reshape(g * N_HEADS * ts, LANES)
    r_sw = pltpu.roll(r, HALF, axis=1).reshape(g, N_HEADS * ts, LANES)
    r = r.reshape(g, N_HEADS * ts, LANES)
    lo = lax.broadcasted_iota(jnp.int32, (g * ts, LANES), 1) < HALF
    heads = [(r if hd // HEADS_PER_KV == hd % 2 else r_sw)[:, hd * ts:(hd + 1) * ts, :].reshape(g * ts, LANES)
             for hd in range(N_HEADS)]
    attn = jnp.concatenate([jnp.where(lo, heads[2 * i], heads[2 * i + 1]) for i in range(N_HEADS // 2)], axis=1)

    x1 = (x + jnp.dot(attn.astype(BF16), wout_ref[0:D_ATTN, :], preferred_element_type=F32)
          + jnp.dot(y.astype(BF16), wout_ref[D_ATTN:, :], preferred_element_type=F32))
    x1_ref[...] = x1
    qm_ref[...] = _mem_query(x1, xag_ref, wmq_ref, gmq_ref)


def _mixer_sample(xs2d, ck, cv, st, consts, g, ts):
    n, d = xs2d.shape
    nb, wb = ck.shape[0], ck.shape[2]
    rows = g * ts
    cspecs = [_const_spec(a.shape) for a in consts]
    kern = functools.partial(_mixer_sample_kernel, g=g, ts=ts)
    return pl.pallas_call(
        kern,
        grid=(nb // g,),
        in_specs=[pl.BlockSpec((rows, d), lambda i: (i, 0)),
                  pl.BlockSpec((g, KV_DIM, wb), lambda i: (i, 0, 0)),
                  pl.BlockSpec((g, KV_DIM, wb), lambda i: (i, 0, 0)),
                  pl.BlockSpec((CONV_PAST, g, D_CONV), lambda i: (0, i, 0))] + cspecs,
        out_specs=[pl.BlockSpec((rows, d), lambda i: (i, 0)),
                   pl.BlockSpec((rows, d), lambda i: (i, 0)),
                   pl.BlockSpec((g, KV_DIM, wb), lambda i: (i, 0, 0)),
                   pl.BlockSpec((g, KV_DIM, wb), lambda i: (i, 0, 0)),
                   pl.BlockSpec((CONV_PAST, g, D_CONV), lambda i: (0, i, 0))],
        out_shape=[jax.ShapeDtypeStruct((n, d), F32),
                   jax.ShapeDtypeStruct((n, d), F32),
                   jax.ShapeDtypeStruct((nb, KV_DIM, wb), F32),
                   jax.ShapeDtypeStruct((nb, KV_DIM, wb), F32),
                   jax.ShapeDtypeStruct((CONV_PAST, nb, D_CONV), F32)],
        scratch_shapes=[pltpu.VMEM((g, wb + ts, KV_DIM), F32),
                        pltpu.VMEM((g, wb + ts, KV_DIM), F32),
                        pltpu.VMEM((g, N_HEADS * ts, LANES), F32),
                        pltpu.VMEM((g, N_HEADS * ts, wb + ts), F32),
                        pltpu.VMEM((g, N_HEADS * ts, LANES), F32),
                        pltpu.VMEM((D_CONV // LANES, rows, LANES), F32),
                        pltpu.VMEM((D_CONV // LANES, rows, LANES), F32)],
        compiler_params=pltpu.CompilerParams(dimension_semantics=("arbitrary",),
                                             vmem_limit_bytes=VMEM_LIMIT),
        name="mixer_sample",
    )(xs2d, ck, cv, st, *consts)


def _memory_kv_kernel(mem_ref, g_ref, wmk_ref, wmv_ref, gk_ref, k_ref, v_ref):
    h = _rms(mem_ref[...], g_ref[...]).astype(BF16)
    k = jnp.dot(h, wmk_ref[...], preferred_element_type=F32)
    parts = []
    for hd in range(MEM_HEADS):
        kh = k[:, hd * MEM_HEAD_DIM:(hd + 1) * MEM_HEAD_DIM]
        parts.append(kh * lax.rsqrt(jnp.mean(kh * kh, axis=-1, keepdims=True) + EPS))
    k_ref[...] = jnp.concatenate(parts, axis=1) * gk_ref[...]
    v_ref[...] = jnp.dot(h, wmv_ref[...], preferred_element_type=F32)


def _memory_kv(mem2d, g, wmk, wmv, gk, tm):
    n, d = mem2d.shape
    consts = (g, wmk, wmv, gk)
    return pl.pallas_call(
        _memory_kv_kernel,
        grid=(n // tm,),
        in_specs=[pl.BlockSpec((tm, d), lambda i: (i, 0))] + [_const_spec(a.shape) for a in consts],
        out_specs=[pl.BlockSpec((tm, d), lambda i: (i, 0)), pl.BlockSpec((tm, d), lambda i: (i, 0))],
        out_shape=[jax.ShapeDtypeStruct((n, d), F32), jax.ShapeDtypeStruct((n, d), F32)],
        compiler_params=pltpu.CompilerParams(dimension_semantics=("arbitrary",),
                                             vmem_limit_bytes=VMEM_LIMIT),
        name="memory_kv",
    )(mem2d, *consts)


def _cross_prompt_kernel(x1_ref, mk_ref, mv_ref, xag_ref, wmq_ref, gmq_ref, wmo_ref, fg_ref, wr_ref, br_ref, tri_ref,
                         x2_ref, hn_ref, route_ref, info_ref, cnt_ref, base):
    @pl.when((pl.program_id(0) == 0) & (pl.program_id(1) == 0))
    def _():
        base[...] = jnp.zeros_like(base)

    x1 = x1_ref[...]
    q = _mem_query(x1, xag_ref, wmq_ref, gmq_ref)
    o = _mem_attend(q, _split_heads(mk_ref[...].astype(BF16)), _split_heads(mv_ref[...].astype(BF16)))
    x2, hn, route, info, new_base = _cross_tail(o, x1, wmo_ref, fg_ref, wr_ref, br_ref, tri_ref, base[...])
    x2_ref[...] = x2
    hn_ref[...] = hn
    route_ref[...] = route
    info_ref[...] = info
    base[...] = new_base
    cnt_ref[...] = new_base


def _cross_prompt(x1, mk, mv, consts, tm):
    b, s, d = x1.shape
    m = mk.shape[1]
    cspecs = [_const_spec(a.shape) for a in consts]
    return pl.pallas_call(
        _cross_prompt_kernel,
        grid=(b, s // tm),
        in_specs=[pl.BlockSpec((None, tm, d), lambda i, j: (i, j, 0)),
                  pl.BlockSpec((None, m, d), lambda i, j: (i, 0, 0)),
                  pl.BlockSpec((None, m, d), lambda i, j: (i, 0, 0))] + cspecs,
        out_specs=[pl.BlockSpec((None, tm, d), lambda i, j: (i, j, 0)),
                   pl.BlockSpec((None, tm, d // 2), lambda i, j: (i, j, 0)),
                   pl.BlockSpec((None, tm, LANES), lambda i, j: (i, j, 0)),
                   pl.BlockSpec((ROUTE_ROWS, tm), lambda i, j: (0, i * (s // tm) + j)),
                   pl.BlockSpec((1, LANES), lambda i, j: (0, 0))],
        out_shape=[jax.ShapeDtypeStruct((b, s, d), F32),
                   jax.ShapeDtypeStruct((b, s, d // 2), jnp.int32),
                   jax.ShapeDtypeStruct((b, s, LANES), F32),
                   jax.ShapeDtypeStruct((ROUTE_ROWS, b * s), F32),
                   jax.ShapeDtypeStruct((1, LANES), F32)],
        scratch_shapes=[pltpu.VMEM((1, LANES), F32)],
        compiler_params=pltpu.CompilerParams(dimension_semantics=("arbitrary", "arbitrary"),
                                             vmem_limit_bytes=VMEM_LIMIT),
        name="cross_prompt",
    )(x1, mk, mv, *consts)


def _mem_attend_sample_kernel(qm_ref, *refs, g, ts):
    k_refs, v_refs, o_ref, s_scr = refs[:g], refs[g:2 * g], refs[2 * g], refs[2 * g + 1]
    halves = MEM_HEAD_DIM // LANES
    rows_per_pos = MEM_HEADS * halves
    mlen = k_refs[0].shape[0] // rows_per_pos

    def all_heads(ref):
        return jnp.concatenate([ref[pl.ds(c * MEM_HEADS + hd, mlen, stride=rows_per_pos), :]
                                for hd in range(MEM_HEADS) for c in range(halves)], axis=1).astype(BF16)

    rows = MEM_HEADS * ts
    own = (lax.broadcasted_iota(jnp.int32, (rows, MEM_HEADS * MEM_HEAD_DIM), 1) // MEM_HEAD_DIM
           == lax.broadcasted_iota(jnp.int32, (rows, MEM_HEADS * MEM_HEAD_DIM), 0) // ts)
    for n in range(g):
        q = qm_ref[n * ts:(n + 1) * ts, :]
        q_bd = jnp.where(own, jnp.concatenate([q] * MEM_HEADS, axis=0), 0.0).astype(BF16)
        s_scr[n] = lax.dot_general(q_bd, all_heads(k_refs[n]), (((1,), (1,)), ((), ())),
                                   preferred_element_type=F32)
    s = s_scr[...]
    p = jnp.exp(s - jnp.max(s, axis=-1, keepdims=True))
    s_scr[...] = p * (1.0 / jnp.sum(p, axis=-1, keepdims=True))
    for n in range(g):
        r = jnp.dot(s_scr[n].astype(BF16), all_heads(v_refs[n]), preferred_element_type=F32)
        o_ref[n * ts:(n + 1) * ts, :] = jnp.concatenate(
            [r[hd * ts:(hd + 1) * ts, hd * MEM_HEAD_DIM:(hd + 1) * MEM_HEAD_DIM] for hd in range(MEM_HEADS)], axis=1)


def _mem_cache_rows(cache):
    nb, mlen = cache.shape[:2]
    halves = MEM_HEAD_DIM // LANES
    return (cache.reshape(nb, mlen, MEM_HEADS, halves, LANES).transpose(0, 1, 3, 2, 4)
            .reshape(nb, mlen * halves * MEM_HEADS, LANES))


def _mem_attend_sample(qm, mk, mv, g, ts):
    n, d = qm.shape
    nb, rows = mk.shape[0], mk.shape[1]
    kern = functools.partial(_mem_attend_sample_kernel, g=g, ts=ts)

    def seq_spec(j):
        return pl.BlockSpec((None, rows, LANES), lambda i: (g * i + j, 0, 0))

    return pl.pallas_call(
        kern,
        grid=(nb // g,),
        in_specs=[pl.BlockSpec((g * ts, d), lambda i: (i, 0))] + [seq_spec(j) for j in range(g)] * 2,
        out_specs=pl.BlockSpec((g * ts, d), lambda i: (i, 0)),
        out_shape=jax.ShapeDtypeStruct((n, d), F32),
        scratch_shapes=[pltpu.VMEM((g, MEM_HEADS * ts, rows // (MEM_HEADS * (MEM_HEAD_DIM // LANES))), F32)],
        compiler_params=pltpu.CompilerParams(dimension_semantics=("arbitrary",),
                                             vmem_limit_bytes=VMEM_LIMIT),
        name="mem_attend_sample",
    )(qm, *([mk] * g), *([mv] * g))


def _cross_tail_kernel(o_ref, x1_ref, wmo_ref, fg_ref, wr_ref, br_ref, tri_ref,
                       x2_ref, hn_ref, route_ref, info_ref, cnt_ref, base):
    @pl.when(pl.program_id(0) == 0)
    def _():
        base[...] = jnp.zeros_like(base)

    x2, hn, route, info, new_base = _cross_tail(o_ref[...], x1_ref[...], wmo_ref, fg_ref, wr_ref, br_ref, tri_ref,
                                                base[...])
    x2_ref[...] = x2
    hn_ref[...] = hn
    route_ref[...] = route
    info_ref[...] = info
    base[...] = new_base
    cnt_ref[...] = new_base


def _cross_tail_call(o, x1, consts, tm):
    n, d = x1.shape
    cspecs = [_const_spec(a.shape) for a in consts]
    return pl.pallas_call(
        _cross_tail_kernel,
        grid=(n // tm,),
        in_specs=[pl.BlockSpec((tm, d), lambda i: (i, 0)), pl.BlockSpec((tm, d), lambda i: (i, 0))] + cspecs,
        out_specs=[pl.BlockSpec((tm, d), lambda i: (i, 0)),
                   pl.BlockSpec((tm, d // 2), lambda i: (i, 0)),
                   pl.BlockSpec((tm, LANES), lambda i: (i, 0)),
                   pl.BlockSpec((ROUTE_ROWS, tm), lambda i: (0, i)),
                   pl.BlockSpec((1, LANES), lambda i: (0, 0))],
        out_shape=[jax.ShapeDtypeStruct((n, d), F32),
                   jax.ShapeDtypeStruct((n, d // 2), jnp.int32),
                   jax.ShapeDtypeStruct((n, LANES), F32),
                   jax.ShapeDtypeStruct((ROUTE_ROWS, n), F32),
                   jax.ShapeDtypeStruct((1, LANES), F32)],
        scratch_shapes=[pltpu.VMEM((1, LANES), F32)],
        compiler_params=pltpu.CompilerParams(dimension_semantics=("arbitrary",),
                                             vmem_limit_bytes=VMEM_LIMIT),
        name="cross_tail",
    )(o, x1, *consts)


def _sc_rows_per_step(per_worker):
    step = min(SC_ROWS_PER_STEP, per_worker)
    assert per_worker % step == 0 and step % 8 == 0
    return step


def _sc_gather_rows(table, idx):
    nrows = idx.shape[0]
    _, width = table.shape
    assert nrows % (8 * SC_WORKERS) == 0
    per_worker = nrows // SC_WORKERS
    step = _sc_rows_per_step(per_worker // 2)
    mesh = plsc.VectorSubcoreMesh(core_axis_name="c", subcore_axis_name="s")

    @functools.partial(
        pl.kernel, mesh=mesh, out_type=jax.ShapeDtypeStruct((nrows, width), table.dtype),
        scratch_types=[pltpu.VMEM((step,), jnp.int32)] * 2 + [pltpu.VMEM((step, width), table.dtype)] * 2
        + [pltpu.SemaphoreType.DMA] * 2)
    def gather(table_hbm, idx_hbm, out_hbm, idx_a, idx_b, rows_a, rows_b, sem_a, sem_b):
        wid = lax.axis_index("s") * SC_CORES + lax.axis_index("c")
        base = wid * per_worker

        @pl.loop(0, per_worker // (2 * step))
        def _(i):
            off = base + i * (2 * step)
            pltpu.sync_copy(idx_hbm.at[pl.ds(off, step)], idx_a)
            pltpu.sync_copy(idx_hbm.at[pl.ds(off + step, step)], idx_b)
            gather_a = pltpu.async_copy(table_hbm.at[idx_a], rows_a, sem_a)
            gather_b = pltpu.async_copy(table_hbm.at[idx_b], rows_b, sem_b)
            gather_a.wait()
            write_a = pltpu.async_copy(rows_a, out_hbm.at[pl.ds(off, step)], sem_a)
            gather_b.wait()
            write_b = pltpu.async_copy(rows_b, out_hbm.at[pl.ds(off + step, step)], sem_b)
            write_a.wait()
            write_b.wait()

    return gather(table, idx)


def _sc_scatter_rows(table, pos, nrows_out):
    n, width = table.shape
    assert n % (8 * SC_WORKERS) == 0 and pos.shape == (TOP_K * n,)
    per_worker = n // SC_WORKERS
    step = _sc_rows_per_step(per_worker)
    mesh = plsc.VectorSubcoreMesh(core_axis_name="c", subcore_axis_name="s")

    @functools.partial(
        pl.kernel, mesh=mesh, out_type=jax.ShapeDtypeStruct((nrows_out, width), table.dtype),
        scratch_types=[pltpu.VMEM((step,), jnp.int32)] * TOP_K
        + [pltpu.VMEM((step, width), table.dtype), pltpu.SemaphoreType.DMA])
    def scatter(table_hbm, pos_hbm, out_hbm, *scratch):
        idx_vs, rows_v, sem = scratch[:TOP_K], scratch[TOP_K], scratch[TOP_K + 1]
        wid = lax.axis_index("s") * SC_CORES + lax.axis_index("c")
        base = wid * per_worker

        @pl.loop(0, per_worker // step)
        def _(i):
            off = base + i * step
            for k in range(TOP_K):
                pltpu.sync_copy(pos_hbm.at[pl.ds(k * n + off, step)], idx_vs[k])
            pltpu.sync_copy(table_hbm.at[pl.ds(off, step)], rows_v)
            for k in range(TOP_K):
                pltpu.async_copy(rows_v, out_hbm.at[idx_vs[k]], sem).wait()

    return scatter(table, pos)


W_CHUNKS = 4
FETCH_AHEAD = 3
FETCH_AHEAD_F32 = 2


def _expert_weight_copies(e, slot, w_hbm, w_slots, sems, chunks):
    copies = []
    for m, (src, dst) in enumerate(zip(w_hbm, w_slots)):
        rows = dst.shape[1] // chunks
        for c in range(chunks):
            sl = pl.ds(c * rows, rows)
            sem = sems.at[(slot * len(w_hbm) + m) * chunks + c]
            copies.append(pltpu.make_async_copy(src.at[e, sl, :], dst.at[slot, sl, :], sem))
    return copies


def _moe_grouped_kernel(te_ref, nv_ref, rv_ref, slot_ref, *refs, publish, ahead, chunks):
    later_refs, (xs_ref, wg_hbm, wu_hbm, wd_hbm, ys_ref), rest = refs[:ahead], refs[ahead:ahead + 5], refs[ahead + 5:]
    i = pl.program_id(0)
    n_slots = ahead + 1
    w_hbm = (wg_hbm, wu_hbm, wd_hbm)
    if publish:
        out_hbm, w_slots, w_bf16, sems, out_sems = rest[0:3], rest[3:6], rest[6:9], rest[9], rest[10]
    else:
        w_slots, sems = rest[0:3], rest[3]

    def publish_copies(e):
        return [pltpu.make_async_copy(src, dst.at[e], out_sems.at[m])
                for m, (src, dst) in enumerate(zip(w_bf16, out_hbm))]

    def fetch(e, slot):
        return _expert_weight_copies(e, slot, w_hbm, w_slots, sems, chunks)

    @pl.when(i == 0)
    def _():
        for cp in fetch(te_ref[0], slot_ref[0]):
            cp.start()
        for k in range(ahead - 1):
            @pl.when(later_refs[k][0] >= 0)
            def _():
                for cp in fetch(later_refs[k][0], (slot_ref[0] + k + 1) % n_slots):
                    cp.start()

    @pl.when(i < nv_ref[0])
    def _():
        slot = slot_ref[i]

        @pl.when((i == 0) | (te_ref[i] != te_ref[jnp.maximum(i - 1, 0)]))
        def _():
            for cp in fetch(te_ref[i], slot):
                cp.wait()
            if publish:
                @pl.when(i > 0)
                def _():
                    for cp in publish_copies(te_ref[jnp.maximum(i - 1, 0)]):
                        cp.wait()

                for src, dst in zip(w_slots, w_bf16):
                    dst[...] = src[slot].astype(BF16)
                for cp in publish_copies(te_ref[i]):
                    cp.start()

            @pl.when(later_refs[ahead - 1][i] >= 0)
            def _():
                for cp in fetch(later_refs[ahead - 1][i], (slot + ahead) % n_slots):
                    cp.start()

        wg, wu, wd = [r[...] for r in w_bf16] if publish else [r[slot] for r in w_slots]
        xs = xs_ref[...]
        row = lax.broadcasted_iota(jnp.int32, xs.shape, 0)
        x = _unpack_bf16_pairs(jnp.where(row < rv_ref[i], xs, 0)).astype(BF16)
        a = jnp.dot(x, wg, preferred_element_type=F32)
        b = jnp.dot(x, wu, preferred_element_type=F32)
        act = (a * jax.nn.sigmoid(a)) * b
        y = jnp.dot(act.astype(BF16), wd, preferred_element_type=F32)
        ys_ref[...] = _pack_bf16_pairs(y)

    if publish:
        @pl.when(i == pl.num_programs(0) - 1)
        def _():
            for cp in publish_copies(te_ref[nv_ref[0] - 1]):
                cp.wait()


def _moe_grouped(tile_expert, n_valid, rows_valid, run_slot, later_experts, xs, wg, wu, wd, tm, publish):
    p, half = xs.shape
    ne, d, f = wg.shape
    ahead = len(later_experts)
    chunks = W_CHUNKS if publish else 1

    def live_tile(i, te, nv, *_):
        return (jnp.minimum(i, nv[0] - 1), 0)

    slot_dtype = F32 if publish else BF16
    n_slots = ahead + 1
    scratch = [pltpu.VMEM((n_slots, d, f), slot_dtype), pltpu.VMEM((n_slots, d, f), slot_dtype),
               pltpu.VMEM((n_slots, f, d), slot_dtype)]
    out_specs = [pl.BlockSpec((tm, half), live_tile)]
    out_shape = [jax.ShapeDtypeStruct((p, half), jnp.int32)]
    if publish:
        scratch += [pltpu.VMEM((d, f), BF16), pltpu.VMEM((d, f), BF16), pltpu.VMEM((f, d), BF16)]
        out_specs += [pl.BlockSpec(memory_space=pl.ANY)] * 3
        out_shape += [jax.ShapeDtypeStruct(w.shape, BF16) for w in (wg, wu, wd)]
    scratch += [pltpu.SemaphoreType.DMA((n_slots * 3 * chunks,))]
    if publish:
        scratch += [pltpu.SemaphoreType.DMA((3,))]
    grid_spec = pltpu.PrefetchScalarGridSpec(
        num_scalar_prefetch=4 + ahead,
        grid=(p // tm,),
        in_specs=[pl.BlockSpec((tm, half), live_tile)] + [pl.BlockSpec(memory_space=pl.ANY)] * 3,
        out_specs=out_specs,
        scratch_shapes=scratch)
    res = pl.pallas_call(
        functools.partial(_moe_grouped_kernel, publish=publish, ahead=ahead, chunks=chunks),
        grid_spec=grid_spec,
        out_shape=out_shape,
        compiler_params=pltpu.CompilerParams(dimension_semantics=("arbitrary",),
                                             vmem_limit_bytes=VMEM_LIMIT),
        name="moe_grouped",
    )(tile_expert, n_valid, rows_valid, run_slot, *later_experts, xs, wg, wu, wd)
    return res[0], tuple(res[1:])


def _moe_combine_kernel(x2_ref, route_ref, y0_ref, y1_ref, out_ref):
    route = route_ref[...]
    out_ref[...] = (x2_ref[...] + route[:, 2:3] * _unpack_bf16_pairs(y0_ref[...])
                    + route[:, 3:4] * _unpack_bf16_pairs(y1_ref[...]))


def _moe_combine(x2, route, gathered, tm):
    n, d = x2.shape
    nt = n // tm
    return pl.pallas_call(
        _moe_combine_kernel,
        grid=(nt,),
        in_specs=[pl.BlockSpec((tm, d), lambda i: (i, 0)),
                  pl.BlockSpec((tm, LANES), lambda i: (i, 0)),
                  pl.BlockSpec((tm, d // 2), lambda i: (i, 0)),
                  pl.BlockSpec((tm, d // 2), lambda i: (i + nt, 0))],
        out_specs=pl.BlockSpec((tm, d), lambda i: (i, 0)),
        out_shape=jax.ShapeDtypeStruct((n, d), F32),
        compiler_params=pltpu.CompilerParams(dimension_semantics=("arbitrary",),
                                             vmem_limit_bytes=VMEM_LIMIT),
        name="moe_combine",
    )(x2, route, gathered, gathered)


def _moe(x2, hn_packed, route, info, counts, weights, tm_rows, publish):
    n = x2.shape[0]
    p = (TOP_K * n // tm_rows + N_EXPERTS) * tm_rows
    n_tiles = p // tm_rows
    cnt = counts[0, :N_EXPERTS].astype(jnp.int32)
    tiles_per_e = jnp.maximum((cnt + tm_rows - 1) // tm_rows, 1 if publish else 0)
    e_ids = jnp.arange(N_EXPERTS, dtype=jnp.int32)
    tile_end = jnp.sum(jnp.where(e_ids[None, :] <= e_ids[:, None], tiles_per_e[None, :], 0), axis=1)
    tile_start = tile_end - tiles_per_e
    n_valid = tile_end[-1:]
    tile_ids = jnp.arange(n_tiles, dtype=jnp.int32)
    tile_expert = jnp.minimum(jnp.sum((tile_end[None, :] <= tile_ids[:, None]).astype(jnp.int32), axis=1),
                              N_EXPERTS - 1)
    mine = tile_expert[:, None] == e_ids[None, :]
    rows_left = jnp.sum(jnp.where(mine, cnt - (tile_ids[:, None] - tile_start) * tm_rows, 0), axis=1)
    rows_valid = jnp.clip(rows_left, 0, tm_rows).astype(jnp.int32)
    ahead = FETCH_AHEAD_F32 if publish else FETCH_AHEAD
    has_tiles = tiles_per_e > 0
    run_of_e = jnp.sum((has_tiles[None, :] & (e_ids[None, :] < e_ids[:, None])).astype(jnp.int32), axis=1)
    run_of_tile = jnp.sum(jnp.where(mine, run_of_e, 0), axis=1)
    run_slot = (run_of_tile % (ahead + 1)).astype(jnp.int32)
    later_experts = []
    for k in range(1, ahead + 1):
        is_run = has_tiles[None, :] & (run_of_e[None, :] == run_of_tile[:, None] + k)
        later_experts.append(jnp.sum(jnp.where(is_run, e_ids[None, :] + 1, 0), axis=1).astype(jnp.int32) - 1)
    eidx = info[0:TOP_K].astype(jnp.int32)
    row_start = jnp.sum(jnp.where(eidx[None] == e_ids[:, None, None], (tile_start * tm_rows)[:, None, None], 0),
                        axis=0)
    pos = (row_start + info[4:4 + TOP_K].astype(jnp.int32)).reshape(-1)

    xs = _sc_scatter_rows(hn_packed, pos, p)
    ys, w_bf16 = _moe_grouped(tile_expert, n_valid, rows_valid, run_slot, later_experts, xs, *weights, tm_rows,
                              publish)
    back = _sc_gather_rows(ys, pos)
    return _moe_combine(x2, route, back, min(TM_COMBINE, n)), w_bf16


def _block_diag(n, width):
    idx = np.arange(n) // width
    return jnp.asarray((idx[:, None] == idx[None, :]).astype(np.float32), dtype=BF16)


def _kv_feature_major(cache):
    nb, wb = cache.shape[:2]
    return jnp.transpose(cache, (0, 2, 3, 1)).reshape(nb, KV_DIM, wb)


def _strict_lower(n):
    r = np.arange(n)
    return jnp.asarray((r[:, None] > r[None, :]).astype(np.float32), dtype=BF16)


def _alibi_slopes():
    return np.exp2(-8.0 * np.arange(1, N_HEADS + 1, dtype=np.float32) / N_HEADS).astype(np.float32)


def _prompt_bias():
    i = np.arange(BLOCK)[:, None]
    s = np.arange(2 * BLOCK)[None, :]
    dist = (i + BLOCK - s).astype(np.float32)
    mask = (dist >= 0) & (dist < WINDOW)
    first = mask & (s >= BLOCK)
    slopes = _alibi_slopes()[:, None, None]
    reg = np.where(mask[None], -slopes * dist[None], np.float32(NEG_INF))
    fst = np.where(first[None], -slopes * dist[None], np.float32(NEG_INF))
    return jnp.asarray(np.concatenate([reg, fst], axis=0).astype(np.float32))


def _sample_bias(ts, wb):
    i = np.arange(ts)[:, None]
    s = np.arange(wb + ts)[None, :]
    dist = (i + wb - s).astype(np.float32)
    mask = (dist >= 0) & (dist < WINDOW)
    slopes = _alibi_slopes()[:, None, None]
    b = np.where(mask[None], -slopes * dist[None], np.float32(NEG_INF)).astype(np.float32)
    return jnp.asarray(b.reshape(N_HEADS * ts, wb + ts))


def kernel(x_prompt, x_sample, cache_swa_k, cache_swa_v, state_conv, cache_mem_k, cache_mem_v, mem_prompt, norm_mix_g, w_in, q_norm_g, k_norm_g, attn_sinks, conv_dw_w, conv_dw_b, conv_ln_g, conv_ln_b, w_out, norm_xa_g, norm_mem_g, w_mq, w_mk, w_mv, mq_norm_g, mk_norm_g, w_mo, norm_ffn_g, w_router_group, b_router_group, w_router_expert, b_router_expert, w_exp_gate, w_exp_up, w_exp_down):
    depth = w_in.shape[0]
    bp, sp, d = x_prompt.shape
    nb, ts, _ = x_sample.shape
    wb = cache_swa_k.shape[2]
    mlen = mem_prompt.shape[1]
    assert d == D_MODEL and wb == WINDOW and sp % TM_PROMPT == 0 and sp % TM_CROSS == 0 and nb % SEQ_PER_STEP == 0

    bdq = _block_diag(MXU_TILE, HEAD_DIM)
    bdk = _block_diag(KV_DIM, HEAD_DIM)
    bias_p = _prompt_bias()
    bias_s = _sample_bias(ts, wb)
    lane_lo = (np.arange(D_ATTN) % LANES) < HALF
    row = lambda a: a.reshape(1, -1).astype(F32)

    xp = x_prompt
    xs = x_sample.reshape(nb * ts, d)
    kp_l, vp_l, cp_l, mkp_l, mvp_l, ks_l, vs_l, cs_l = [], [], [], [], [], [], [], []
    for l in range(depth):
        gq = jnp.tile(q_norm_g[l].astype(F32), N_HEADS) * (HEAD_DIM ** -0.5)
        gqlo = jnp.where(lane_lo, gq, 0.0).reshape(1, -1)
        gqhi = jnp.where(lane_lo, 0.0, gq).reshape(1, -1)
        gk = jnp.tile(k_norm_g[l].astype(F32), N_KV_HEADS).reshape(1, -1)
        sinks = attn_sinks[l].astype(F32)
        sinkrow = jnp.repeat(sinks, ts).reshape(N_HEADS * ts, 1)
        win = w_in[l].astype(BF16)
        wout = w_out[l].astype(BF16)
        wmq = w_mq[l].astype(BF16)
        wmo = w_mo[l].astype(BF16)
        gmq = (jnp.tile(mq_norm_g[l].astype(F32), MEM_HEADS) * (MEM_HEAD_DIM ** -0.5)).reshape(1, -1)
        gmk = jnp.tile(mk_norm_g[l].astype(F32), MEM_HEADS).reshape(1, -1)
        w_r = jnp.concatenate([w_router_expert[l], w_router_group[l],
                               jnp.zeros((d, LANES - N_EXPERTS - N_GROUPS), F32)], axis=1).astype(BF16)
        b_r = jnp.concatenate([b_router_expert[l], b_router_group[l],
                               jnp.zeros((LANES - N_EXPERTS - N_GROUPS,), F32)]).reshape(1, -1).astype(F32)
        wg, wu, wd = w_exp_gate[l], w_exp_up[l], w_exp_down[l]

        mix_consts = (row(norm_mix_g[l]), win, bdq, bdk, gqlo, gqhi, gk)
        conv_consts = (conv_dw_w[l].astype(F32), row(conv_dw_b[l]), row(conv_ln_g[l]), row(conv_ln_b[l]), wout)
        tail_consts = (wmo, row(norm_ffn_g[l]), w_r, b_r)

        x1p, kp, vp, cp = _mixer_prompt(xp, sinks, mix_consts + (bias_p,) + conv_consts, TM_PROMPT)
        mk, mv = _memory_kv(mem_prompt.reshape(bp * mlen, d), row(norm_mem_g[l]),
                            w_mk[l].astype(BF16), w_mv[l].astype(BF16), gmk, min(256, bp * mlen))
        mk = mk.reshape(bp, mlen, d)
        mv = mv.reshape(bp, mlen, d)
        x2p, hnp, routep, infop, cntp = _cross_prompt(x1p, mk, mv, (row(norm_xa_g[l]), wmq, gmq) + tail_consts
                                               + (_strict_lower(RANK_BLOCK),), TM_CROSS)
        xp, w_bf16 = _moe(x2p.reshape(bp * sp, d), hnp.reshape(bp * sp, d // 2), routep.reshape(bp * sp, LANES),
                          infop, cntp, (wg, wu, wd), TM_ROWS_PROMPT, True)
        xp = xp.reshape(bp, sp, d)
        kp_l.append(kp.reshape(bp, BLOCK, N_KV_HEADS, HEAD_DIM))
        vp_l.append(vp.reshape(bp, BLOCK, N_KV_HEADS, HEAD_DIM))
        cp_l.append(cp)
        mkp_l.append(mk.reshape(bp, mlen, MEM_HEADS, MEM_HEAD_DIM))
        mvp_l.append(mv.reshape(bp, mlen, MEM_HEADS, MEM_HEAD_DIM))

        x1s, qm, ksn, vsn, csn = _mixer_sample(
            xs, _kv_feature_major(cache_swa_k[l]), _kv_feature_major(cache_swa_v[l]),
            jnp.transpose(state_conv[l], (1, 0, 2)),
            mix_consts + (bias_s, sinkrow) + conv_consts + (row(norm_xa_g[l]), wmq, gmq), SEQ_PER_STEP, ts)
        o_s = _mem_attend_sample(qm, _mem_cache_rows(cache_mem_k[l]), _mem_cache_rows(cache_mem_v[l]),
                                 MEM_SEQ_PER_STEP, ts)
        tms = min(256, nb * ts)
        x2s, hns, routes, infos, cnts = _cross_tail_call(o_s, x1s, tail_consts + (_strict_lower(RANK_BLOCK),), tms)
        xs, _ = _moe(x2s, hns, routes, infos, cnts, w_bf16, TM_ROWS_SAMPLE, False)
        ks_l.append(jnp.transpose(ksn.reshape(nb, N_KV_HEADS, HEAD_DIM, wb), (0, 3, 1, 2)))
        vs_l.append(jnp.transpose(vsn.reshape(nb, N_KV_HEADS, HEAD_DIM, wb), (0, 3, 1, 2)))
        cs_l.append(jnp.transpose(csn, (1, 0, 2)))

    st = lambda xs_: jnp.stack(xs_, axis=0)
    return (xp, xs.reshape(nb, ts, d), st(kp_l), st(vp_l), st(cp_l), st(mkp_l), st(mvp_l),
            st(ks_l), st(vs_l), st(cs_l))
```

```python
import functools

import numpy as np
import jax
import jax.numpy as jnp
from jax import lax
from jax.experimental import pallas as pl
from jax.experimental.pallas import tpu as pltpu
from jax.experimental.pallas import tpu_sc as plsc

F32 = jnp.float32
BF16 = jnp.bfloat16

D_MODEL = 1024
D_ATTN = 512
D_CONV = 512
HEAD_DIM = 64
N_HEADS = 8
N_KV_HEADS = 2
KV_DIM = N_KV_HEADS * HEAD_DIM
HEADS_PER_KV = N_HEADS // N_KV_HEADS
WINDOW = 128
BLOCK = 128
CONV_WIDTH = 31
CONV_PAST = CONV_WIDTH - 1
MEM_HEADS = 4
MEM_HEAD_DIM = 256
N_GROUPS = 4
EXPERTS_PER_GROUP = 8
N_EXPERTS = 32
EPS = 1e-6
NEG_INF = -1e30

LANES = 128
HALF = LANES // 2
MXU_TILE = 256
SUBLANES = 8
CONV_PAD = 32
CONV_ROWS = 64
VMEM_LIMIT = 56 * 1024 * 1024

TM_PROMPT = 512
TM_CROSS = 1024
SEQ_PER_STEP = 32
MEM_SEQ_PER_STEP = 8
TOP_K = 2
RANK_BLOCK = 256
ROUTE_ROWS = 8
SC_CORES = 2
SC_WORKERS = 32
SC_ROWS_PER_STEP = 64
SC_LANES = 16
SC_PACK_POS = 64
TM_ROWS_PROMPT = 512
TM_ROWS_SAMPLE = 128
TM_COMBINE = 1024


def _rms(x, g):
    ms = jnp.mean(x * x, axis=-1, keepdims=True)
    return x * lax.rsqrt(ms + EPS) * g


def _group_mean_sq(x, bd_ref, width):
    x2 = x * x
    hi = x2.astype(BF16)
    lo = (x2 - hi.astype(F32)).astype(BF16)
    bd = bd_ref[...]
    w = bd.shape[0]
    parts = [jnp.dot(hi[:, c:c + w], bd, preferred_element_type=F32)
             + jnp.dot(lo[:, c:c + w], bd, preferred_element_type=F32) for c in range(0, x.shape[1], w)]
    return jnp.concatenate(parts, axis=1) * (1.0 / width)


def _mixer_proj(x, ng_ref, win_ref, bdq_ref, bdk_ref, gqlo_ref, gqhi_ref, gk_ref):
    h = _rms(x, ng_ref[...]).astype(BF16)
    p = jnp.dot(h, win_ref[...], preferred_element_type=F32)
    q = p[:, :D_ATTN]
    k = p[:, D_ATTN:D_ATTN + KV_DIM]
    v = p[:, D_ATTN + KV_DIM:D_ATTN + 2 * KV_DIM]
    ua = p[:, D_ATTN + 2 * KV_DIM:D_ATTN + 2 * KV_DIM + D_CONV]
    ub = p[:, D_ATTN + 2 * KV_DIM + D_CONV:]
    qn = q * lax.rsqrt(_group_mean_sq(q, bdq_ref, HEAD_DIM) + EPS)
    q_lo = qn * gqlo_ref[...]
    q_hi = qn * gqhi_ref[...]
    kn = k * lax.rsqrt(_group_mean_sq(k, bdk_ref, HEAD_DIM) + EPS) * gk_ref[...]
    u = ua * jax.nn.sigmoid(ub)
    return q_lo, q_hi, kn, v, u


def _dup_halves(x):
    lo = lax.broadcasted_iota(jnp.int32, x.shape, 1) < HALF
    xr = pltpu.roll(x, HALF, axis=1)
    return jnp.where(lo, x, xr), jnp.where(lo, xr, x)


def _sink_softmax(s, sink):
    m = jnp.maximum(jnp.max(s, axis=-1, keepdims=True), sink)
    p = jnp.exp(s - m)
    denom = jnp.sum(p, axis=-1, keepdims=True) + jnp.exp(sink - m)
    return p * (1.0 / denom)


def _conv_ln_silu(y, cb_ref, lg_ref, lb_ref):
    y = y + cb_ref[...]
    mu = jnp.mean(y, axis=-1, keepdims=True)
    yc = y - mu
    yn = yc * lax.rsqrt(jnp.mean(yc * yc, axis=-1, keepdims=True) + EPS)
    z = yn * lg_ref[...] + lb_ref[...]
    return z * jax.nn.sigmoid(z)


def _mem_query(x1, xag_ref, wmq_ref, gmq_ref):
    h = _rms(x1, xag_ref[...]).astype(BF16)
    q = jnp.dot(h, wmq_ref[...], preferred_element_type=F32)
    parts = []
    for hd in range(MEM_HEADS):
        qh = q[:, hd * MEM_HEAD_DIM:(hd + 1) * MEM_HEAD_DIM]
        parts.append(qh * lax.rsqrt(jnp.mean(qh * qh, axis=-1, keepdims=True) + EPS))
    return jnp.concatenate(parts, axis=1) * gmq_ref[...]


def _mem_attend(q, k_heads, v_heads):
    outs = []
    for hd in range(MEM_HEADS):
        sl = slice(hd * MEM_HEAD_DIM, (hd + 1) * MEM_HEAD_DIM)
        s = lax.dot_general(q[:, sl].astype(BF16), k_heads[hd], (((1,), (1,)), ((), ())),
                            preferred_element_type=F32)
        m = jnp.max(s, axis=-1, keepdims=True)
        p = jnp.exp(s - m)
        p = p * (1.0 / jnp.sum(p, axis=-1, keepdims=True))
        outs.append(jnp.dot(p.astype(BF16), v_heads[hd], preferred_element_type=F32))
    return jnp.concatenate(outs, axis=1)


def _split_heads(x):
    return [x[:, hd * MEM_HEAD_DIM:(hd + 1) * MEM_HEAD_DIM] for hd in range(MEM_HEADS)]


def _pack_bf16_pairs(y):
    n = y.shape[1] // 2
    bits = pltpu.bitcast(y.astype(BF16).astype(F32), jnp.int32)
    return (bits[:, :n] & jnp.int32(-65536)) | lax.shift_right_logical(bits[:, n:], jnp.int32(16))


def _unpack_bf16_pairs(w):
    hi = pltpu.bitcast(w & jnp.int32(-65536), F32)
    lo = pltpu.bitcast(lax.shift_left(w, jnp.int32(16)), F32)
    return jnp.concatenate([hi, lo], axis=1)


def _cross_tail(o, x1, wmo_ref, fg_ref, wr_ref, br_ref, tri_ref, base):
    x2 = x1 + jnp.dot(o.astype(BF16), wmo_ref[...], preferred_element_type=F32)
    hn_f = _rms(x2, fg_ref[...])
    hn = hn_f.astype(BF16)
    logits = jnp.dot(hn, wr_ref[...], preferred_element_type=F32) + br_ref[...]
    lane = lax.broadcasted_iota(jnp.int32, logits.shape, 1)
    is_g = (lane >= N_EXPERTS) & (lane < N_EXPERTS + N_GROUPS)
    lg = jnp.where(is_g, logits, NEG_INF)
    gmax = jnp.max(lg, axis=-1, keepdims=True)
    gsel = jnp.min(jnp.where(lg == gmax, lane, 2 * LANES), axis=-1, keepdims=True) - N_EXPERTS
    pg_sel = 1.0 / jnp.sum(jnp.exp(lg - gmax), axis=-1, keepdims=True)
    in_grp = (lane >= gsel * EXPERTS_PER_GROUP) & (lane < (gsel + 1) * EXPERTS_PER_GROUP)
    le = jnp.where(in_grp, logits, NEG_INF)
    top1 = jnp.max(le, axis=-1, keepdims=True)
    idx1 = jnp.min(jnp.where(le == top1, lane, 2 * LANES), axis=-1, keepdims=True)
    le2 = jnp.where(lane == idx1, NEG_INF, le)
    top2 = jnp.max(le2, axis=-1, keepdims=True)
    idx2 = jnp.min(jnp.where(le2 == top2, lane, 2 * LANES), axis=-1, keepdims=True)
    e2 = jnp.exp(top2 - top1)
    inv = 1.0 / (1.0 + e2)
    gate1 = pg_sel * inv
    gate2 = pg_sel * (e2 * inv)
    used = jnp.where((lane == idx1) | (lane == idx2), 1.0, 0.0)
    blk = tri_ref.shape[0]
    used_b = used.astype(BF16)
    parts, run = [], base
    for r0 in range(0, used.shape[0], blk):
        parts.append(jnp.dot(tri_ref[...], used_b[r0:r0 + blk], preferred_element_type=F32) + run)
        run = run + jnp.sum(used[r0:r0 + blk], axis=0, keepdims=True)
    before = jnp.concatenate(parts, axis=0)
    rank1 = jnp.sum(jnp.where(lane == idx1, before, 0.0), axis=-1, keepdims=True)
    rank2 = jnp.sum(jnp.where(lane == idx2, before, 0.0), axis=-1, keepdims=True)
    route = jnp.zeros_like(logits)
    for pos, val in enumerate((idx1.astype(F32), idx2.astype(F32), gate1, gate2, rank1, rank2)):
        route = jnp.where(lane == pos, val, route)
    info = jnp.transpose(route)[0:ROUTE_ROWS, :]
    return x2, _pack_bf16_pairs(hn_f), route, info, run


def _mixer_prompt_kernel(sink_ref, x_ref, ng_ref, win_ref, bdq_ref, bdk_ref, gqlo_ref, gqhi_ref, gk_ref,
                         bias_ref, cw_ref, cb_ref, lg_ref, lb_ref, wout_ref,
                         x1_ref, ko_ref, vo_ref, co_ref,
                         kband, vband, uext, ushift, *, tm, nt):
    t = pl.program_id(1)

    @pl.when(t == 0)
    def _():
        kband[0:BLOCK, :] = jnp.zeros((BLOCK, KV_DIM), F32)
        vband[0:BLOCK, :] = jnp.zeros((BLOCK, KV_DIM), F32)
        uext[0:CONV_PAD, :] = jnp.zeros((CONV_PAD, D_CONV), F32)

    x = x_ref[...]
    q_lo, q_hi, kn, v, u = _mixer_proj(x, ng_ref, win_ref, bdq_ref, bdk_ref, gqlo_ref, gqhi_ref, gk_ref)
    kband[BLOCK:BLOCK + tm, :] = kn
    vband[BLOCK:BLOCK + tm, :] = v
    uext[CONV_PAD:CONV_PAD + tm, :] = u

    @pl.when(t == nt - 1)
    def _():
        ko_ref[...] = kn[tm - BLOCK:, :]
        vo_ref[...] = v[tm - BLOCK:, :]
        co_ref[...] = uext[pl.ds(CONV_PAD + tm - CONV_PAST, CONV_PAST), :]

    kd = [a.astype(BF16) for a in _dup_halves(kband[...])]
    vd = _dup_halves(vband[...])
    lo = lax.broadcasted_iota(jnp.int32, vd[0].shape, 1) < HALF
    v_lo = [jnp.where(lo, a, 0.0).astype(BF16) for a in vd]
    v_hi = [jnp.where(lo, 0.0, a).astype(BF16) for a in vd]
    q_lo = q_lo.astype(BF16)
    q_hi = q_hi.astype(BF16)
    attn_blocks = []
    for j in range(tm // BLOCK):
        rows = slice(j * BLOCK, (j + 1) * BLOCK)
        keys = slice(j * BLOCK, j * BLOCK + 2 * BLOCK)
        bias_base = jnp.where(t == 0, N_HEADS, 0) if j == 0 else 0
        tiles = []
        for c in range(N_KV_HEADS):
            q4 = jnp.concatenate(
                [(q_lo if (HEADS_PER_KV * c + a) % 2 == 0 else q_hi)[rows,
                  ((HEADS_PER_KV * c + a) // 2) * LANES:((HEADS_PER_KV * c + a) // 2 + 1) * LANES]
                 for a in range(HEADS_PER_KV)], axis=0)
            s_all = lax.dot_general(q4, kd[c][keys], (((1,), (1,)), ((), ())), preferred_element_type=F32)
            ps = []
            for a in range(HEADS_PER_KV):
                hd = HEADS_PER_KV * c + a
                s = s_all[a * BLOCK:(a + 1) * BLOCK] + bias_ref[bias_base + hd]
                ps.append(_sink_softmax(s, sink_ref[hd]).astype(BF16))
            vstack = jnp.concatenate([v_lo[c][keys], v_hi[c][keys]], axis=0)
            for i2 in range(HEADS_PER_KV // 2):
                pp = jnp.concatenate([ps[2 * i2], ps[2 * i2 + 1]], axis=1)
                tiles.append(jnp.dot(pp, vstack, preferred_element_type=F32))
        attn_blocks.append(jnp.concatenate(tiles, axis=1))
    attn = jnp.concatenate(attn_blocks, axis=0)

    off = CONV_PAD - CONV_PAST
    span = tm + CONV_PAD - SUBLANES
    for s in range(1, SUBLANES):
        ushift[s - 1] = uext[pl.ds(s, span), :]
    chunks = []
    for lc in range(D_CONV // LANES):
        ls = slice(lc * LANES, (lc + 1) * LANES)
        accs = [jnp.zeros((CONV_ROWS, LANES), F32) for _ in range(tm // CONV_ROWS)]
        for j in range(CONV_WIDTH):
            s, a = (off + j) % SUBLANES, (off + j) // SUBLANES
            wj = cw_ref[j:j + 1, ls]
            for rc in range(tm // CONV_ROWS):
                r0 = rc * CONV_ROWS + a * SUBLANES
                tap = uext[r0:r0 + CONV_ROWS, ls] if s == 0 else ushift[s - 1, r0:r0 + CONV_ROWS, ls]
                accs[rc] = accs[rc] + wj * tap
        chunks.append(jnp.concatenate(accs, axis=0))
    y = _conv_ln_silu(jnp.concatenate(chunks, axis=1), cb_ref, lg_ref, lb_ref)

    x1_ref[...] = (x + jnp.dot(attn.astype(BF16), wout_ref[0:D_ATTN, :], preferred_element_type=F32)
                   + jnp.dot(y.astype(BF16), wout_ref[D_ATTN:, :], preferred_element_type=F32))

    kband[0:BLOCK, :] = kband[tm:tm + BLOCK, :]
    vband[0:BLOCK, :] = vband[tm:tm + BLOCK, :]
    uext[0:CONV_PAD, :] = uext[tm:tm + CONV_PAD, :]


def _const_spec(shape):
    nd = len(shape)
    return pl.BlockSpec(shape, lambda *_: (0,) * nd)


def _mixer_prompt(x, sinks, consts, tm):
    b, s, d = x.shape
    nt = s // tm
    (ng, win, bdq, bdk, gqlo, gqhi, gk, bias_p, cw, cb, lg, lb, wout) = consts
    kern = functools.partial(_mixer_prompt_kernel, tm=tm, nt=nt)
    cspecs = [_const_spec(a.shape) for a in consts]
    return pl.pallas_call(
        kern,
        grid=(b, nt),
        in_specs=[pl.BlockSpec(memory_space=pltpu.SMEM),
                  pl.BlockSpec((None, tm, d), lambda i, j: (i, j, 0))] + cspecs,
        out_specs=[pl.BlockSpec((None, tm, d), lambda i, j: (i, j, 0)),
                   pl.BlockSpec((None, BLOCK, KV_DIM), lambda i, j: (i, 0, 0)),
                   pl.BlockSpec((None, BLOCK, KV_DIM), lambda i, j: (i, 0, 0)),
                   pl.BlockSpec((None, CONV_PAST, D_CONV), lambda i, j: (i, 0, 0))],
        out_shape=[jax.ShapeDtypeStruct((b, s, d), F32),
                   jax.ShapeDtypeStruct((b, BLOCK, KV_DIM), F32),
                   jax.ShapeDtypeStruct((b, BLOCK, KV_DIM), F32),
                   jax.ShapeDtypeStruct((b, CONV_PAST, D_CONV), F32)],
        scratch_shapes=[pltpu.VMEM((BLOCK + tm, KV_DIM), F32),
                        pltpu.VMEM((BLOCK + tm, KV_DIM), F32),
                        pltpu.VMEM((CONV_PAD + tm, D_CONV), F32),
                        pltpu.VMEM((SUBLANES - 1, tm + CONV_PAD - SUBLANES, D_CONV), F32)],
        compiler_params=pltpu.CompilerParams(dimension_semantics=("arbitrary", "arbitrary"),
                                             vmem_limit_bytes=VMEM_LIMIT),
        name="mixer_prompt",
    )(sinks, x, *consts)


def _mixer_sample_kernel(x_ref, ck_ref, cv_ref, st_ref, ng_ref, win_ref, bdq_ref, bdk_ref, gqlo_ref, gqhi_ref,
                         gk_ref, bias_ref, sinkrow_ref, cw_ref, cb_ref, lg_ref, lb_ref, wout_ref,
                         xag_ref, wmq_ref, gmq_ref,
                         x1_ref, qm_ref, ko_ref, vo_ref, co_ref,
                         kall, vall, qs, s_scr, r_scr, u_scr, y_scr, *, g, ts):
    wb = ck_ref.shape[2]
    x = x_ref[...]
    q_lo, q_hi, kn, v, u = _mixer_proj(x, ng_ref, win_ref, bdq_ref, bdk_ref, gqlo_ref, gqhi_ref, gk_ref)
    kall[:, wb:wb + ts, :] = kn.reshape(g, ts, KV_DIM)
    vall[:, wb:wb + ts, :] = v.reshape(g, ts, KV_DIM)
    for n in range(g):
        kall[n, 0:wb, :] = ck_ref[n].T
        vall[n, 0:wb, :] = cv_ref[n].T
        ko_ref[n] = kall[n, ts:wb + ts, :].T
        vo_ref[n] = vall[n, ts:wb + ts, :].T

    for lc in range(D_CONV // LANES):
        ls = slice(lc * LANES, (lc + 1) * LANES)
        u_scr[lc] = u[:, ls]
        u_t = [u_scr[lc, pl.ds(t, g, stride=ts), :] for t in range(ts)]

        def frame(r):
            return st_ref[r, :, ls] if r < CONV_PAST else u_t[r - CONV_PAST]

        for t in range(ts):
            acc = jnp.zeros((g, LANES), F32)
            for j in range(CONV_WIDTH):
                acc = acc + cw_ref[j:j + 1, ls] * frame(t + j)
            y_scr[lc, pl.ds(t, g, stride=ts), :] = acc
        for r in range(CONV_PAST):
            co_ref[r, :, ls] = frame(r + ts)
    y = _conv_ln_silu(jnp.concatenate([y_scr[lc] for lc in range(D_CONV // LANES)], axis=1), cb_ref, lg_ref, lb_ref)

    for hd in range(N_HEADS):
        tile = (q_lo if hd % 2 == 0 else q_hi)[:, (hd // 2) * LANES:(hd // 2 + 1) * LANES]
        if hd % 2 != hd // HEADS_PER_KV:
            tile = pltpu.roll(tile, HALF, axis=1)
        qs[:, hd * ts:(hd + 1) * ts, :] = tile.reshape(g, ts, LANES)
    for n in range(g):
        s_scr[n] = lax.dot_general(qs[n].astype(BF16), kall[n].astype(BF16), (((1,), (1,)), ((), ())),
                                   preferred_element_type=F32)
    s_scr[...] = _sink_softmax(s_scr[...] + bias_ref[...], sinkrow_ref[...])
    for n in range(g):
        r_scr[n] = jnp.dot(s_scr[n].astype(BF16), vall[n].astype(BF16), preferred_element_type=F32)
    r = r_scr[...].reshape(g * N_HEADS * ts, LANES)
    r_sw = pltpu.roll(r, HALF, axis=1).reshape(g, N_HEADS * ts, LANES)
    r = r.reshape(g, N_HEADS * ts, LANES)
    lo = lax.broadcasted_iota(jnp.int32, (g * ts, LANES), 1) < HALF
    heads = [(r if hd // HEADS_PER_KV == hd % 2 else r_sw)[:, hd * ts:(hd + 1) * ts, :].reshape(g * ts, LANES)
             for hd in range(N_HEADS)]
    attn = jnp.concatenate([jnp.where(lo, heads[2 * i], heads[2 * i + 1]) for i in range(N_HEADS // 2)], axis=1)

    x1 = (x + jnp.dot(attn.astype(BF16), wout_ref[0:D_ATTN, :], preferred_element_type=F32)
          + jnp.dot(y.astype(BF16), wout_ref[D_ATTN:, :], preferred_element_type=F32))
    x1_ref[...] = x1
    qm_ref[...] = _mem_query(x1, xag_ref, wmq_ref, gmq_ref)


def _mixer_sample(xs2d, ck, cv, st, consts, g, ts):
    n, d = xs2d.shape
    nb, wb = ck.shape[0], ck.shape[2]
    rows = g * ts
    cspecs = [_const_spec(a.shape) for a in consts]
    kern = functools.partial(_mixer_sample_kernel, g=g, ts=ts)
    return pl.pallas_call(
        kern,
        grid=(nb // g,),
        in_specs=[pl.BlockSpec((rows, d), lambda i: (i, 0)),
                  pl.BlockSpec((g, KV_DIM, wb), lambda i: (i, 0, 0)),
                  pl.BlockSpec((g, KV_DIM, wb), lambda i: (i, 0, 0)),
                  pl.BlockSpec((CONV_PAST, g, D_CONV), lambda i: (0, i, 0))] + cspecs,
        out_specs=[pl.BlockSpec((rows, d), lambda i: (i, 0)),
                   pl.BlockSpec((rows, d), lambda i: (i, 0)),
                   pl.BlockSpec((g, KV_DIM, wb), lambda i: (i, 0, 0)),
                   pl.BlockSpec((g, KV_DIM, wb), lambda i: (i, 0, 0)),
                   pl.BlockSpec((CONV_PAST, g, D_CONV), lambda i: (0, i, 0))],
        out_shape=[jax.ShapeDtypeStruct((n, d), F32),
                   jax.ShapeDtypeStruct((n, d), F32),
                   jax.ShapeDtypeStruct((nb, KV_DIM, wb), F32),
                   jax.ShapeDtypeStruct((nb, KV_DIM, wb), F32),
                   jax.ShapeDtypeStruct((CONV_PAST, nb, D_CONV), F32)],
        scratch_shapes=[pltpu.VMEM((g, wb + ts, KV_DIM), F32),
                        pltpu.VMEM((g, wb + ts, KV_DIM), F32),
                        pltpu.VMEM((g, N_HEADS * ts, LANES), F32),
                        pltpu.VMEM((g, N_HEADS * ts, wb + ts), F32),
                        pltpu.VMEM((g, N_HEADS * ts, LANES), F32),
                        pltpu.VMEM((D_CONV // LANES, rows, LANES), F32),
                        pltpu.VMEM((D_CONV // LANES, rows, LANES), F32)],
        compiler_params=pltpu.CompilerParams(dimension_semantics=("arbitrary",),
                                             vmem_limit_bytes=VMEM_LIMIT),
        name="mixer_sample",
    )(xs2d, ck, cv, st, *consts)


def _memory_kv_kernel(mem_ref, g_ref, wmk_ref, wmv_ref, gk_ref, k_ref, v_ref):
    h = _rms(mem_ref[...], g_ref[...]).astype(BF16)
    k = jnp.dot(h, wmk_ref[...], preferred_element_type=F32)
    parts = []
    for hd in range(MEM_HEADS):
        kh = k[:, hd * MEM_HEAD_DIM:(hd + 1) * MEM_HEAD_DIM]
        parts.append(kh * lax.rsqrt(jnp.mean(kh * kh, axis=-1, keepdims=True) + EPS))
    k_ref[...] = jnp.concatenate(parts, axis=1) * gk_ref[...]
    v_ref[...] = jnp.dot(h, wmv_ref[...], preferred_element_type=F32)


def _memory_kv(mem2d, g, wmk, wmv, gk, tm):
    n, d = mem2d.shape
    consts = (g, wmk, wmv, gk)
    return pl.pallas_call(
        _memory_kv_kernel,
        grid=(n // tm,),
        in_specs=[pl.BlockSpec((tm, d), lambda i: (i, 0))] + [_const_spec(a.shape) for a in consts],
        out_specs=[pl.BlockSpec((tm, d), lambda i: (i, 0)), pl.BlockSpec((tm, d), lambda i: (i, 0))],
        out_shape=[jax.ShapeDtypeStruct((n, d), F32), jax.ShapeDtypeStruct((n, d), F32)],
        compiler_params=pltpu.CompilerParams(dimension_semantics=("arbitrary",),
                                             vmem_limit_bytes=VMEM_LIMIT),
        name="memory_kv",
    )(mem2d, *consts)


def _cross_prompt_kernel(x1_ref, mk_ref, mv_ref, xag_ref, wmq_ref, gmq_ref, wmo_ref, fg_ref, wr_ref, br_ref, tri_ref,
                         x2_ref, hn_ref, route_ref, info_ref, cnt_ref, base):
    @pl.when((pl.program_id(0) == 0) & (pl.program_id(1) == 0))
    def _():
        base[...] = jnp.zeros_like(base)

    x1 = x1_ref[...]
    q = _mem_query(x1, xag_ref, wmq_ref, gmq_ref)
    o = _mem_attend(q, _split_heads(mk_ref[...].astype(BF16)), _split_heads(mv_ref[...].astype(BF16)))
    x2, hn, route, info, new_base = _cross_tail(o, x1, wmo_ref, fg_ref, wr_ref, br_ref, tri_ref, base[...])
    x2_ref[...] = x2
    hn_ref[...] = hn
    route_ref[...] = route
    info_ref[...] = info
    base[...] = new_base
    cnt_ref[...] = new_base


def _cross_prompt(x1, mk, mv, consts, tm):
    b, s, d = x1.shape
    m = mk.shape[1]
    cspecs = [_const_spec(a.shape) for a in consts]
    return pl.pallas_call(
        _cross_prompt_kernel,
        grid=(b, s // tm),
        in_specs=[pl.BlockSpec((None, tm, d), lambda i, j: (i, j, 0)),
                  pl.BlockSpec((None, m, d), lambda i, j: (i, 0, 0)),
                  pl.BlockSpec((None, m, d), lambda i, j: (i, 0, 0))] + cspecs,
        out_specs=[pl.BlockSpec((None, tm, d), lambda i, j: (i, j, 0)),
                   pl.BlockSpec((None, tm, d // 2), lambda i, j: (i, j, 0)),
                   pl.BlockSpec((None, tm, LANES), lambda i, j: (i, j, 0)),
                   pl.BlockSpec((ROUTE_ROWS, tm), lambda i, j: (0, i * (s // tm) + j)),
                   pl.BlockSpec((1, LANES), lambda i, j: (0, 0))],
        out_shape=[jax.ShapeDtypeStruct((b, s, d), F32),
                   jax.ShapeDtypeStruct((b, s, d // 2), jnp.int32),
                   jax.ShapeDtypeStruct((b, s, LANES), F32),
                   jax.ShapeDtypeStruct((ROUTE_ROWS, b * s), F32),
                   jax.ShapeDtypeStruct((1, LANES), F32)],
        scratch_shapes=[pltpu.VMEM((1, LANES), F32)],
        compiler_params=pltpu.CompilerParams(dimension_semantics=("arbitrary", "arbitrary"),
                                             vmem_limit_bytes=VMEM_LIMIT),
        name="cross_prompt",
    )(x1, mk, mv, *consts)


def _mem_attend_sample_kernel(qm_ref, *refs, g, ts):
    k_refs, v_refs, o_ref, s_scr = refs[:g], refs[g:2 * g], refs[2 * g], refs[2 * g + 1]
    mlen = k_refs[0].shape[0] // MEM_HEADS

    def all_heads(ref):
        pieces = []
        for hd in range(MEM_HEADS):
            w = ref[pl.ds(hd, mlen, stride=MEM_HEADS), :]
            pieces.append(pltpu.bitcast(lax.shift_left(w, jnp.int32(16)), F32))
            pieces.append(pltpu.bitcast(w & jnp.int32(-65536), F32))
        return jnp.concatenate(pieces, axis=1).astype(BF16)

    rows = MEM_HEADS * ts
    own = (lax.broadcasted_iota(jnp.int32, (rows, MEM_HEADS * MEM_HEAD_DIM), 1) // MEM_HEAD_DIM
           == lax.broadcasted_iota(jnp.int32, (rows, MEM_HEADS * MEM_HEAD_DIM), 0) // ts)
    for n in range(g):
        q = qm_ref[n * ts:(n + 1) * ts, :]
        q_bd = jnp.where(own, jnp.concatenate([q] * MEM_HEADS, axis=0), 0.0).astype(BF16)
        s_scr[n] = lax.dot_general(q_bd, all_heads(k_refs[n]), (((1,), (1,)), ((), ())),
                                   preferred_element_type=F32)
    s = s_scr[...]
    p = jnp.exp(s - jnp.max(s, axis=-1, keepdims=True))
    s_scr[...] = p * (1.0 / jnp.sum(p, axis=-1, keepdims=True))
    for n in range(g):
        r = jnp.dot(s_scr[n].astype(BF16), all_heads(v_refs[n]), preferred_element_type=F32)
        o_ref[n * ts:(n + 1) * ts, :] = jnp.concatenate(
            [r[hd * ts:(hd + 1) * ts, hd * MEM_HEAD_DIM:(hd + 1) * MEM_HEAD_DIM] for hd in range(MEM_HEADS)], axis=1)


def _mem_cache_rows(cache):
    nb, mlen = cache.shape[:2]
    halves = MEM_HEAD_DIM // LANES
    return (cache.reshape(nb, mlen, MEM_HEADS, halves, LANES).transpose(0, 1, 3, 2, 4)
            .reshape(nb, mlen * halves * MEM_HEADS, LANES))


def _sc_pack_cache(rows):
    nb, nrows, _ = rows.shape
    total = nb * nrows
    per_worker = total // SC_WORKERS
    chunk = SC_PACK_POS * SUBLANES
    assert MEM_HEAD_DIM == 2 * LANES and 2 * MEM_HEADS == SUBLANES and per_worker % chunk == 0
    tiles_in = rows.reshape(total // SUBLANES, SUBLANES, LANES)
    mesh = plsc.VectorSubcoreMesh(core_axis_name="c", subcore_axis_name="s")

    @functools.partial(
        pl.kernel, mesh=mesh,
        out_type=jax.ShapeDtypeStruct((total // (2 * SUBLANES), SUBLANES, LANES), jnp.int32),
        scratch_types=[pltpu.VMEM((SC_PACK_POS, SUBLANES, LANES), F32),
                       pltpu.VMEM((SC_PACK_POS // 2, SUBLANES, LANES), jnp.int32)],
        compiler_params=pltpu.CompilerParams(needs_layout_passes=False))
    def pack(in_hbm, out_hbm, in_v, out_v):
        wid = lax.axis_index("s") * SC_CORES + lax.axis_index("c")
        base = wid * (per_worker // SUBLANES)

        @pl.loop(0, per_worker // chunk)
        def _(ci):
            off = base + ci * SC_PACK_POS
            pltpu.sync_copy(in_hbm.at[pl.ds(off, SC_PACK_POS)], in_v)

            @pl.loop(0, SC_PACK_POS // 2)
            def _(q):
                for pp in range(2):
                    for hd in range(MEM_HEADS):
                        for l0 in range(0, LANES, SC_LANES):
                            lo = in_v[2 * q + pp, hd, pl.ds(l0, SC_LANES)]
                            hi = in_v[2 * q + pp, MEM_HEADS + hd, pl.ds(l0, SC_LANES)]
                            word = plsc.bitcast(plsc.pack(lo, hi, format=plsc.PackFormat.INTERLEAVED), jnp.int32)
                            out_v[q, pp * MEM_HEADS + hd, pl.ds(l0, SC_LANES)] = word

            pltpu.sync_copy(out_v, out_hbm.at[pl.ds(off // 2, SC_PACK_POS // 2)])

    return pack(tiles_in).reshape(nb, nrows // 2, LANES)


def _mem_attend_sample(qm, mk, mv, g, ts):
    n, d = qm.shape
    nb, rows = mk.shape[0], mk.shape[1]
    kern = functools.partial(_mem_attend_sample_kernel, g=g, ts=ts)

    def seq_spec(j):
        return pl.BlockSpec((None, rows, LANES), lambda i: (g * i + j, 0, 0))

    return pl.pallas_call(
        kern,
        grid=(nb // g,),
        in_specs=[pl.BlockSpec((g * ts, d), lambda i: (i, 0))] + [seq_spec(j) for j in range(g)] * 2,
        out_specs=pl.BlockSpec((g * ts, d), lambda i: (i, 0)),
        out_shape=jax.ShapeDtypeStruct((n, d), F32),
        scratch_shapes=[pltpu.VMEM((g, MEM_HEADS * ts, rows // MEM_HEADS), F32)],
        compiler_params=pltpu.CompilerParams(dimension_semantics=("arbitrary",),
                                             vmem_limit_bytes=VMEM_LIMIT),
        name="mem_attend_sample",
    )(qm, *([mk] * g), *([mv] * g))


def _cross_tail_kernel(o_ref, x1_ref, wmo_ref, fg_ref, wr_ref, br_ref, tri_ref,
                       x2_ref, hn_ref, route_ref, info_ref, cnt_ref, base):
    @pl.when(pl.program_id(0) == 0)
    def _():
        base[...] = jnp.zeros_like(base)

    x2, hn, route, info, new_base = _cross_tail(o_ref[...], x1_ref[...], wmo_ref, fg_ref, wr_ref, br_ref, tri_ref,
                                                base[...])
    x2_ref[...] = x2
    hn_ref[...] = hn
    route_ref[...] = route
    info_ref[...] = info
    base[...] = new_base
    cnt_ref[...] = new_base


def _cross_tail_call(o, x1, consts, tm):
    n, d = x1.shape
    cspecs = [_const_spec(a.shape) for a in consts]
    return pl.pallas_call(
        _cross_tail_kernel,
        grid=(n // tm,),
        in_specs=[pl.BlockSpec((tm, d), lambda i: (i, 0)), pl.BlockSpec((tm, d), lambda i: (i, 0))] + cspecs,
        out_specs=[pl.BlockSpec((tm, d), lambda i: (i, 0)),
                   pl.BlockSpec((tm, d // 2), lambda i: (i, 0)),
                   pl.BlockSpec((tm, LANES), lambda i: (i, 0)),
                   pl.BlockSpec((ROUTE_ROWS, tm), lambda i: (0, i)),
                   pl.BlockSpec((1, LANES), lambda i: (0, 0))],
        out_shape=[jax.ShapeDtypeStruct((n, d), F32),
                   jax.ShapeDtypeStruct((n, d // 2), jnp.int32),
                   jax.ShapeDtypeStruct((n, LANES), F32),
                   jax.ShapeDtypeStruct((ROUTE_ROWS, n), F32),
                   jax.ShapeDtypeStruct((1, LANES), F32)],
        scratch_shapes=[pltpu.VMEM((1, LANES), F32)],
        compiler_params=pltpu.CompilerParams(dimension_semantics=("arbitrary",),
                                             vmem_limit_bytes=VMEM_LIMIT),
        name="cross_tail",
    )(o, x1, *consts)


def _sc_rows_per_step(per_worker):
    step = min(SC_ROWS_PER_STEP, per_worker)
    assert per_worker % step == 0 and step % 8 == 0
    return step


def _sc_gather_rows(table, idx):
    nrows = idx.shape[0]
    _, width = table.shape
    assert nrows % (8 * SC_WORKERS) == 0
    per_worker = nrows // SC_WORKERS
    step = _sc_rows_per_step(per_worker // 2)
    mesh = plsc.VectorSubcoreMesh(core_axis_name="c", subcore_axis_name="s")

    @functools.partial(
        pl.kernel, mesh=mesh, out_type=jax.ShapeDtypeStruct((nrows, width), table.dtype),
        scratch_types=[pltpu.VMEM((step,), jnp.int32)] * 2 + [pltpu.VMEM((step, width), table.dtype)] * 2
        + [pltpu.SemaphoreType.DMA] * 2)
    def gather(table_hbm, idx_hbm, out_hbm, idx_a, idx_b, rows_a, rows_b, sem_a, sem_b):
        wid = lax.axis_index("s") * SC_CORES + lax.axis_index("c")
        base = wid * per_worker

        @pl.loop(0, per_worker // (2 * step))
        def _(i):
            off = base + i * (2 * step)
            pltpu.sync_copy(idx_hbm.at[pl.ds(off, step)], idx_a)
            pltpu.sync_copy(idx_hbm.at[pl.ds(off + step, step)], idx_b)
            gather_a = pltpu.async_copy(table_hbm.at[idx_a], rows_a, sem_a)
            gather_b = pltpu.async_copy(table_hbm.at[idx_b], rows_b, sem_b)
            gather_a.wait()
            write_a = pltpu.async_copy(rows_a, out_hbm.at[pl.ds(off, step)], sem_a)
            gather_b.wait()
            write_b = pltpu.async_copy(rows_b, out_hbm.at[pl.ds(off + step, step)], sem_b)
            write_a.wait()
            write_b.wait()

    return gather(table, idx)


def _sc_scatter_rows(table, pos, nrows_out):
    n, width = table.shape
    assert n % (8 * SC_WORKERS) == 0 and pos.shape == (TOP_K * n,)
    per_worker = n // SC_WORKERS
    step = _sc_rows_per_step(per_worker)
    mesh = plsc.VectorSubcoreMesh(core_axis_name="c", subcore_axis_name="s")

    @functools.partial(
        pl.kernel, mesh=mesh, out_type=jax.ShapeDtypeStruct((nrows_out, width), table.dtype),
        scratch_types=[pltpu.VMEM((step,), jnp.int32)] * TOP_K
        + [pltpu.VMEM((step, width), table.dtype), pltpu.SemaphoreType.DMA])
    def scatter(table_hbm, pos_hbm, out_hbm, *scratch):
        idx_vs, rows_v, sem = scratch[:TOP_K], scratch[TOP_K], scratch[TOP_K + 1]
        wid = lax.axis_index("s") * SC_CORES + lax.axis_index("c")
        base = wid * per_worker

        @pl.loop(0, per_worker // step)
        def _(i):
            off = base + i * step
            for k in range(TOP_K):
                pltpu.sync_copy(pos_hbm.at[pl.ds(k * n + off, step)], idx_vs[k])
            pltpu.sync_copy(table_hbm.at[pl.ds(off, step)], rows_v)
            for k in range(TOP_K):
                pltpu.async_copy(rows_v, out_hbm.at[idx_vs[k]], sem).wait()

    return scatter(table, pos)


W_CHUNKS = 4
FETCH_AHEAD = 3
FETCH_AHEAD_F32 = 2


def _expert_weight_copies(e, slot, w_hbm, w_slots, sems, chunks):
    copies = []
    for m, (src, dst) in enumerate(zip(w_hbm, w_slots)):
        rows = dst.shape[1] // chunks
        for c in range(chunks):
            sl = pl.ds(c * rows, rows)
            sem = sems.at[(slot * len(w_hbm) + m) * chunks + c]
            copies.append(pltpu.make_async_copy(src.at[e, sl, :], dst.at[slot, sl, :], sem))
    return copies


def _moe_grouped_kernel(te_ref, nv_ref, rv_ref, slot_ref, *refs, publish, ahead, chunks):
    later_refs, (xs_ref, wg_hbm, wu_hbm, wd_hbm, ys_ref), rest = refs[:ahead], refs[ahead:ahead + 5], refs[ahead + 5:]
    i = pl.program_id(0)
    n_slots = ahead + 1
    w_hbm = (wg_hbm, wu_hbm, wd_hbm)
    if publish:
        out_hbm, w_slots, w_bf16, sems, out_sems = rest[0:3], rest[3:6], rest[6:9], rest[9], rest[10]
    else:
        w_slots, sems = rest[0:3], rest[3]

    def publish_copies(e):
        return [pltpu.make_async_copy(src, dst.at[e], out_sems.at[m])
                for m, (src, dst) in enumerate(zip(w_bf16, out_hbm))]

    def fetch(e, slot):
        return _expert_weight_copies(e, slot, w_hbm, w_slots, sems, chunks)

    @pl.when(i == 0)
    def _():
        for cp in fetch(te_ref[0], slot_ref[0]):
            cp.start()
        for k in range(ahead - 1):
            @pl.when(later_refs[k][0] >= 0)
            def _():
                for cp in fetch(later_refs[k][0], (slot_ref[0] + k + 1) % n_slots):
                    cp.start()

    @pl.when(i < nv_ref[0])
    def _():
        slot = slot_ref[i]

        @pl.when((i == 0) | (te_ref[i] != te_ref[jnp.maximum(i - 1, 0)]))
        def _():
            for cp in fetch(te_ref[i], slot):
                cp.wait()
            if publish:
                @pl.when(i > 0)
                def _():
                    for cp in publish_copies(te_ref[jnp.maximum(i - 1, 0)]):
                        cp.wait()

                for src, dst in zip(w_slots, w_bf16):
                    dst[...] = src[slot].astype(BF16)
                for cp in publish_copies(te_ref[i]):
                    cp.start()

            @pl.when(later_refs[ahead - 1][i] >= 0)
            def _():
                for cp in fetch(later_refs[ahead - 1][i], (slot + ahead) % n_slots):
                    cp.start()

        wg, wu, wd = [r[...] for r in w_bf16] if publish else [r[slot] for r in w_slots]
        xs = xs_ref[...]
        row = lax.broadcasted_iota(jnp.int32, xs.shape, 0)
        x = _unpack_bf16_pairs(jnp.where(row < rv_ref[i], xs, 0)).astype(BF16)
        a = jnp.dot(x, wg, preferred_element_type=F32)
        b = jnp.dot(x, wu, preferred_element_type=F32)
        act = (a * jax.nn.sigmoid(a)) * b
        y = jnp.dot(act.astype(BF16), wd, preferred_element_type=F32)
        ys_ref[...] = _pack_bf16_pairs(y)

    if publish:
        @pl.when(i == pl.num_programs(0) - 1)
        def _():
            for cp in publish_copies(te_ref[nv_ref[0] - 1]):
                cp.wait()


def _moe_grouped(tile_expert, n_valid, rows_valid, run_slot, later_experts, xs, wg, wu, wd, tm, publish):
    p, half = xs.shape
    ne, d, f = wg.shape
    ahead = len(later_experts)
    chunks = W_CHUNKS if publish else 1

    def live_tile(i, te, nv, *_):
        return (jnp.minimum(i, nv[0] - 1), 0)

    slot_dtype = F32 if publish else BF16
    n_slots = ahead + 1
    scratch = [pltpu.VMEM((n_slots, d, f), slot_dtype), pltpu.VMEM((n_slots, d, f), slot_dtype),
               pltpu.VMEM((n_slots, f, d), slot_dtype)]
    out_specs = [pl.BlockSpec((tm, half), live_tile)]
    out_shape = [jax.ShapeDtypeStruct((p, half), jnp.int32)]
    if publish:
        scratch += [pltpu.VMEM((d, f), BF16), pltpu.VMEM((d, f), BF16), pltpu.VMEM((f, d), BF16)]
        out_specs += [pl.BlockSpec(memory_space=pl.ANY)] * 3
        out_shape += [jax.ShapeDtypeStruct(w.shape, BF16) for w in (wg, wu, wd)]
    scratch += [pltpu.SemaphoreType.DMA((n_slots * 3 * chunks,))]
    if publish:
        scratch += [pltpu.SemaphoreType.DMA((3,))]
    grid_spec = pltpu.PrefetchScalarGridSpec(
        num_scalar_prefetch=4 + ahead,
        grid=(p // tm,),
        in_specs=[pl.BlockSpec((tm, half), live_tile)] + [pl.BlockSpec(memory_space=pl.ANY)] * 3,
        out_specs=out_specs,
        scratch_shapes=scratch)
    res = pl.pallas_call(
        functools.partial(_moe_grouped_kernel, publish=publish, ahead=ahead, chunks=chunks),
        grid_spec=grid_spec,
        out_shape=out_shape,
        compiler_params=pltpu.CompilerParams(dimension_semantics=("arbitrary",),
                                             vmem_limit_bytes=VMEM_LIMIT),
        name="moe_grouped",
    )(tile_expert, n_valid, rows_valid, run_slot, *later_experts, xs, wg, wu, wd)
    return res[0], tuple(res[1:])


def _moe_combine_kernel(x2_ref, route_ref, y0_ref, y1_ref, out_ref):
    route = route_ref[...]
    out_ref[...] = (x2_ref[...] + route[:, 2:3] * _unpack_bf16_pairs(y0_ref[...])
                    + route[:, 3:4] * _unpack_bf16_pairs(y1_ref[...]))


def _moe_combine(x2, route, gathered, tm):
    n, d = x2.shape
    nt = n // tm
    return pl.pallas_call(
        _moe_combine_kernel,
        grid=(nt,),
        in_specs=[pl.BlockSpec((tm, d), lambda i: (i, 0)),
                  pl.BlockSpec((tm, LANES), lambda i: (i, 0)),
                  pl.BlockSpec((tm, d // 2), lambda i: (i, 0)),
                  pl.BlockSpec((tm, d // 2), lambda i: (i + nt, 0))],
        out_specs=pl.BlockSpec((tm, d), lambda i: (i, 0)),
        out_shape=jax.ShapeDtypeStruct((n, d), F32),
        compiler_params=pltpu.CompilerParams(dimension_semantics=("arbitrary",),
                                             vmem_limit_bytes=VMEM_LIMIT),
        name="moe_combine",
    )(x2, route, gathered, gathered)


def _moe(x2, hn_packed, route, info, counts, weights, tm_rows, publish):
    n = x2.shape[0]
    p = (TOP_K * n // tm_rows + N_EXPERTS) * tm_rows
    n_tiles = p // tm_rows
    cnt = counts[0, :N_EXPERTS].astype(jnp.int32)
    tiles_per_e = jnp.maximum((cnt + tm_rows - 1) // tm_rows, 1 if publish else 0)
    e_ids = jnp.arange(N_EXPERTS, dtype=jnp.int32)
    tile_end = jnp.sum(jnp.where(e_ids[None, :] <= e_ids[:, None], tiles_per_e[None, :], 0), axis=1)
    tile_start = tile_end - tiles_per_e
    n_valid = tile_end[-1:]
    tile_ids = jnp.arange(n_tiles, dtype=jnp.int32)
    tile_expert = jnp.minimum(jnp.sum((tile_end[None, :] <= tile_ids[:, None]).astype(jnp.int32), axis=1),
                              N_EXPERTS - 1)
    mine = tile_expert[:, None] == e_ids[None, :]
    rows_left = jnp.sum(jnp.where(mine, cnt - (tile_ids[:, None] - tile_start) * tm_rows, 0), axis=1)
    rows_valid = jnp.clip(rows_left, 0, tm_rows).astype(jnp.int32)
    ahead = FETCH_AHEAD_F32 if publish else FETCH_AHEAD
    has_tiles = tiles_per_e > 0
    run_of_e = jnp.sum((has_tiles[None, :] & (e_ids[None, :] < e_ids[:, None])).astype(jnp.int32), axis=1)
    run_of_tile = jnp.sum(jnp.where(mine, run_of_e, 0), axis=1)
    run_slot = (run_of_tile % (ahead + 1)).astype(jnp.int32)
    later_experts = []
    for k in range(1, ahead + 1):
        is_run = has_tiles[None, :] & (run_of_e[None, :] == run_of_tile[:, None] + k)
        later_experts.append(jnp.sum(jnp.where(is_run, e_ids[None, :] + 1, 0), axis=1).astype(jnp.int32) - 1)
    eidx = info[0:TOP_K].astype(jnp.int32)
    row_start = jnp.sum(jnp.where(eidx[None] == e_ids[:, None, None], (tile_start * tm_rows)[:, None, None], 0),
                        axis=0)
    pos = (row_start + info[4:4 + TOP_K].astype(jnp.int32)).reshape(-1)

    xs = _sc_scatter_rows(hn_packed, pos, p)
    ys, w_bf16 = _moe_grouped(tile_expert, n_valid, rows_valid, run_slot, later_experts, xs, *weights, tm_rows,
                              publish)
    back = _sc_gather_rows(ys, pos)
    return _moe_combine(x2, route, back, min(TM_COMBINE, n)), w_bf16


def _block_diag(n, width):
    idx = np.arange(n) // width
    return jnp.asarray((idx[:, None] == idx[None, :]).astype(np.float32), dtype=BF16)


def _kv_feature_major(cache):
    nb, wb = cache.shape[:2]
    return jnp.transpose(cache, (0, 2, 3, 1)).reshape(nb, KV_DIM, wb)


def _strict_lower(n):
    r = np.arange(n)
    return jnp.asarray((r[:, None] > r[None, :]).astype(np.float32), dtype=BF16)


def _alibi_slopes():
    return np.exp2(-8.0 * np.arange(1, N_HEADS + 1, dtype=np.float32) / N_HEADS).astype(np.float32)


def _prompt_bias():
    i = np.arange(BLOCK)[:, None]
    s = np.arange(2 * BLOCK)[None, :]
    dist = (i + BLOCK - s).astype(np.float32)
    mask = (dist >= 0) & (dist < WINDOW)
    first = mask & (s >= BLOCK)
    slopes = _alibi_slopes()[:, None, None]
    reg = np.where(mask[None], -slopes * dist[None], np.float32(NEG_INF))
    fst = np.where(first[None], -slopes * dist[None], np.float32(NEG_INF))
    return jnp.asarray(np.concatenate([reg, fst], axis=0).astype(np.float32))


def _sample_bias(ts, wb):
    i = np.arange(ts)[:, None]
    s = np.arange(wb + ts)[None, :]
    dist = (i + wb - s).astype(np.float32)
    mask = (dist >= 0) & (dist < WINDOW)
    slopes = _alibi_slopes()[:, None, None]
    b = np.where(mask[None], -slopes * dist[None], np.float32(NEG_INF)).astype(np.float32)
    return jnp.asarray(b.reshape(N_HEADS * ts, wb + ts))


def kernel(x_prompt, x_sample, cache_swa_k, cache_swa_v, state_conv, cache_mem_k, cache_mem_v, mem_prompt, norm_mix_g, w_in, q_norm_g, k_norm_g, attn_sinks, conv_dw_w, conv_dw_b, conv_ln_g, conv_ln_b, w_out, norm_xa_g, norm_mem_g, w_mq, w_mk, w_mv, mq_norm_g, mk_norm_g, w_mo, norm_ffn_g, w_router_group, b_router_group, w_router_expert, b_router_expert, w_exp_gate, w_exp_up, w_exp_down):
    depth = w_in.shape[0]
    bp, sp, d = x_prompt.shape
    nb, ts, _ = x_sample.shape
    wb = cache_swa_k.shape[2]
    mlen = mem_prompt.shape[1]
    assert d == D_MODEL and wb == WINDOW and sp % TM_PROMPT == 0 and sp % TM_CROSS == 0 and nb % SEQ_PER_STEP == 0

    bdq = _block_diag(MXU_TILE, HEAD_DIM)
    bdk = _block_diag(KV_DIM, HEAD_DIM)
    bias_p = _prompt_bias()
    bias_s = _sample_bias(ts, wb)
    lane_lo = (np.arange(D_ATTN) % LANES) < HALF
    row = lambda a: a.reshape(1, -1).astype(F32)

    xp = x_prompt
    xs = x_sample.reshape(nb * ts, d)
    kp_l, vp_l, cp_l, mkp_l, mvp_l, ks_l, vs_l, cs_l = [], [], [], [], [], [], [], []
    for l in range(depth):
        gq = jnp.tile(q_norm_g[l].astype(F32), N_HEADS) * (HEAD_DIM ** -0.5)
        gqlo = jnp.where(lane_lo, gq, 0.0).reshape(1, -1)
        gqhi = jnp.where(lane_lo, 0.0, gq).reshape(1, -1)
        gk = jnp.tile(k_norm_g[l].astype(F32), N_KV_HEADS).reshape(1, -1)
        sinks = attn_sinks[l].astype(F32)
        sinkrow = jnp.repeat(sinks, ts).reshape(N_HEADS * ts, 1)
        win = w_in[l].astype(BF16)
        wout = w_out[l].astype(BF16)
        wmq = w_mq[l].astype(BF16)
        wmo = w_mo[l].astype(BF16)
        gmq = (jnp.tile(mq_norm_g[l].astype(F32), MEM_HEADS) * (MEM_HEAD_DIM ** -0.5)).reshape(1, -1)
        gmk = jnp.tile(mk_norm_g[l].astype(F32), MEM_HEADS).reshape(1, -1)
        w_r = jnp.concatenate([w_router_expert[l], w_router_group[l],
                               jnp.zeros((d, LANES - N_EXPERTS - N_GROUPS), F32)], axis=1).astype(BF16)
        b_r = jnp.concatenate([b_router_expert[l], b_router_group[l],
                               jnp.zeros((LANES - N_EXPERTS - N_GROUPS,), F32)]).reshape(1, -1).astype(F32)
        wg, wu, wd = w_exp_gate[l], w_exp_up[l], w_exp_down[l]

        mix_consts = (row(norm_mix_g[l]), win, bdq, bdk, gqlo, gqhi, gk)
        conv_consts = (conv_dw_w[l].astype(F32), row(conv_dw_b[l]), row(conv_ln_g[l]), row(conv_ln_b[l]), wout)
        tail_consts = (wmo, row(norm_ffn_g[l]), w_r, b_r)

        x1p, kp, vp, cp = _mixer_prompt(xp, sinks, mix_consts + (bias_p,) + conv_consts, TM_PROMPT)
        mk, mv = _memory_kv(mem_prompt.reshape(bp * mlen, d), row(norm_mem_g[l]),
                            w_mk[l].astype(BF16), w_mv[l].astype(BF16), gmk, min(256, bp * mlen))
        mk = mk.reshape(bp, mlen, d)
        mv = mv.reshape(bp, mlen, d)
        x2p, hnp, routep, infop, cntp = _cross_prompt(x1p, mk, mv, (row(norm_xa_g[l]), wmq, gmq) + tail_consts
                                               + (_strict_lower(RANK_BLOCK),), TM_CROSS)
        xp, w_bf16 = _moe(x2p.reshape(bp * sp, d), hnp.reshape(bp * sp, d // 2), routep.reshape(bp * sp, LANES),
                          infop, cntp, (wg, wu, wd), TM_ROWS_PROMPT, True)
        xp = xp.reshape(bp, sp, d)
        kp_l.append(kp.reshape(bp, BLOCK, N_KV_HEADS, HEAD_DIM))
        vp_l.append(vp.reshape(bp, BLOCK, N_KV_HEADS, HEAD_DIM))
        cp_l.append(cp)
        mkp_l.append(mk.reshape(bp, mlen, MEM_HEADS, MEM_HEAD_DIM))
        mvp_l.append(mv.reshape(bp, mlen, MEM_HEADS, MEM_HEAD_DIM))

        x1s, qm, ksn, vsn, csn = _mixer_sample(
            xs, _kv_feature_major(cache_swa_k[l]), _kv_feature_major(cache_swa_v[l]),
            jnp.transpose(state_conv[l], (1, 0, 2)),
            mix_consts + (bias_s, sinkrow) + conv_consts + (row(norm_xa_g[l]), wmq, gmq), SEQ_PER_STEP, ts)
        o_s = _mem_attend_sample(qm, _sc_pack_cache(_mem_cache_rows(cache_mem_k[l])),
                                 _sc_pack_cache(_mem_cache_rows(cache_mem_v[l])),
                                 MEM_SEQ_PER_STEP, ts)
        tms = min(256, nb * ts)
        x2s, hns, routes, infos, cnts = _cross_tail_call(o_s, x1s, tail_consts + (_strict_lower(RANK_BLOCK),), tms)
        xs, _ = _moe(x2s, hns, routes, infos, cnts, w_bf16, TM_ROWS_SAMPLE, False)
        ks_l.append(jnp.transpose(ksn.reshape(nb, N_KV_HEADS, HEAD_DIM, wb), (0, 3, 1, 2)))
        vs_l.append(jnp.transpose(vsn.reshape(nb, N_KV_HEADS, HEAD_DIM, wb), (0, 3, 1, 2)))
        cs_l.append(jnp.transpose(csn, (1, 0, 2)))

    st = lambda xs_: jnp.stack(xs_, axis=0)
    return (xp, xs.reshape(nb, ts, d), st(kp_l), st(vp_l), st(cp_l), st(mkp_l), st(mvp_l),
            st(ks_l), st(vs_l), st(cs_l))
```

```python
import functools

import numpy as np
import jax
import jax.numpy as jnp
from jax import lax
from jax.experimental import pallas as pl
from jax.experimental.pallas import tpu as pltpu
from jax.experimental.pallas import tpu_sc as plsc

F32 = jnp.float32
BF16 = jnp.bfloat16

D_MODEL = 1024
D_ATTN = 512
D_CONV = 512
HEAD_DIM = 64
N_HEADS = 8
N_KV_HEADS = 2
KV_DIM = N_KV_HEADS * HEAD_DIM
HEADS_PER_KV = N_HEADS // N_KV_HEADS
WINDOW = 128
BLOCK = 128
CONV_WIDTH = 31
CONV_PAST = CONV_WIDTH - 1
MEM_HEADS = 4
MEM_HEAD_DIM = 256
N_GROUPS = 4
EXPERTS_PER_GROUP = 8
N_EXPERTS = 32
EPS = 1e-6
NEG_INF = -1e30

LANES = 128
HALF = LANES // 2
MXU_TILE = 256
SUBLANES = 8
CONV_PAD = 32
CONV_ROWS = 64
VMEM_LIMIT = 56 * 1024 * 1024

TM_PROMPT = 512
TM_CROSS = 1024
SEQ_PER_STEP = 32
MEM_SEQ_PER_STEP = 16
TOP_K = 2
RANK_BLOCK = 256
ROUTE_ROWS = 8
SC_CORES = 2
SC_WORKERS = 32
SC_ROWS_PER_STEP = 64
SC_LANES = 16
SC_PACK_POS = 64
TM_ROWS_PROMPT = 512
TM_ROWS_SAMPLE = 128
TM_COMBINE = 1024


def _rms(x, g):
    ms = jnp.mean(x * x, axis=-1, keepdims=True)
    return x * lax.rsqrt(ms + EPS) * g


def _group_mean_sq(x, bd_ref, width):
    x2 = x * x
    hi = x2.astype(BF16)
    lo = (x2 - hi.astype(F32)).astype(BF16)
    bd = bd_ref[...]
    w = bd.shape[0]
    parts = [jnp.dot(hi[:, c:c + w], bd, preferred_element_type=F32)
             + jnp.dot(lo[:, c:c + w], bd, preferred_element_type=F32) for c in range(0, x.shape[1], w)]
    return jnp.concatenate(parts, axis=1) * (1.0 / width)


def _mixer_proj(x, ng_ref, win_ref, bdq_ref, bdk_ref, gqlo_ref, gqhi_ref, gk_ref):
    h = _rms(x, ng_ref[...]).astype(BF16)
    p = jnp.dot(h, win_ref[...], preferred_element_type=F32)
    q = p[:, :D_ATTN]
    k = p[:, D_ATTN:D_ATTN + KV_DIM]
    v = p[:, D_ATTN + KV_DIM:D_ATTN + 2 * KV_DIM]
    ua = p[:, D_ATTN + 2 * KV_DIM:D_ATTN + 2 * KV_DIM + D_CONV]
    ub = p[:, D_ATTN + 2 * KV_DIM + D_CONV:]
    qn = q * lax.rsqrt(_group_mean_sq(q, bdq_ref, HEAD_DIM) + EPS)
    q_lo = qn * gqlo_ref[...]
    q_hi = qn * gqhi_ref[...]
    kn = k * lax.rsqrt(_group_mean_sq(k, bdk_ref, HEAD_DIM) + EPS) * gk_ref[...]
    u = ua * jax.nn.sigmoid(ub)
    return q_lo, q_hi, kn, v, u


def _dup_halves(x):
    lo = lax.broadcasted_iota(jnp.int32, x.shape, 1) < HALF
    xr = pltpu.roll(x, HALF, axis=1)
    return jnp.where(lo, x, xr), jnp.where(lo, xr, x)


def _sink_softmax(s, sink):
    m = jnp.maximum(jnp.max(s, axis=-1, keepdims=True), sink)
    p = jnp.exp(s - m)
    denom = jnp.sum(p, axis=-1, keepdims=True) + jnp.exp(sink - m)
    return p * (1.0 / denom)


def _conv_ln_silu(y, cb_ref, lg_ref, lb_ref):
    y = y + cb_ref[...]
    mu = jnp.mean(y, axis=-1, keepdims=True)
    yc = y - mu
    yn = yc * lax.rsqrt(jnp.mean(yc * yc, axis=-1, keepdims=True) + EPS)
    z = yn * lg_ref[...] + lb_ref[...]
    return z * jax.nn.sigmoid(z)


def _mem_query(x1, xag_ref, wmq_ref, gmq_ref):
    h = _rms(x1, xag_ref[...]).astype(BF16)
    q = jnp.dot(h, wmq_ref[...], preferred_element_type=F32)
    parts = []
    for hd in range(MEM_HEADS):
        qh = q[:, hd * MEM_HEAD_DIM:(hd + 1) * MEM_HEAD_DIM]
        parts.append(qh * lax.rsqrt(jnp.mean(qh * qh, axis=-1, keepdims=True) + EPS))
    return jnp.concatenate(parts, axis=1) * gmq_ref[...]


def _mem_attend(q, k_heads, v_heads):
    outs = []
    for hd in range(MEM_HEADS):
        sl = slice(hd * MEM_HEAD_DIM, (hd + 1) * MEM_HEAD_DIM)
        s = lax.dot_general(q[:, sl].astype(BF16), k_heads[hd], (((1,), (1,)), ((), ())),
                            preferred_element_type=F32)
        m = jnp.max(s, axis=-1, keepdims=True)
        p = jnp.exp(s - m)
        p = p * (1.0 / jnp.sum(p, axis=-1, keepdims=True))
        outs.append(jnp.dot(p.astype(BF16), v_heads[hd], preferred_element_type=F32))
    return jnp.concatenate(outs, axis=1)


def _split_heads(x):
    return [x[:, hd * MEM_HEAD_DIM:(hd + 1) * MEM_HEAD_DIM] for hd in range(MEM_HEADS)]


def _pack_bf16_pairs(y):
    n = y.shape[1] // 2
    bits = pltpu.bitcast(y.astype(BF16).astype(F32), jnp.int32)
    return (bits[:, :n] & jnp.int32(-65536)) | lax.shift_right_logical(bits[:, n:], jnp.int32(16))


def _unpack_bf16_pairs(w):
    hi = pltpu.bitcast(w & jnp.int32(-65536), F32)
    lo = pltpu.bitcast(lax.shift_left(w, jnp.int32(16)), F32)
    return jnp.concatenate([hi, lo], axis=1)


def _cross_tail(o, x1, wmo_ref, fg_ref, wr_ref, br_ref, tri_ref, base):
    x2 = x1 + jnp.dot(o.astype(BF16), wmo_ref[...], preferred_element_type=F32)
    hn_f = _rms(x2, fg_ref[...])
    hn = hn_f.astype(BF16)
    logits = jnp.dot(hn, wr_ref[...], preferred_element_type=F32) + br_ref[...]
    lane = lax.broadcasted_iota(jnp.int32, logits.shape, 1)
    is_g = (lane >= N_EXPERTS) & (lane < N_EXPERTS + N_GROUPS)
    lg = jnp.where(is_g, logits, NEG_INF)
    gmax = jnp.max(lg, axis=-1, keepdims=True)
    gsel = jnp.min(jnp.where(lg == gmax, lane, 2 * LANES), axis=-1, keepdims=True) - N_EXPERTS
    pg_sel = 1.0 / jnp.sum(jnp.exp(lg - gmax), axis=-1, keepdims=True)
    in_grp = (lane >= gsel * EXPERTS_PER_GROUP) & (lane < (gsel + 1) * EXPERTS_PER_GROUP)
    le = jnp.where(in_grp, logits, NEG_INF)
    top1 = jnp.max(le, axis=-1, keepdims=True)
    idx1 = jnp.min(jnp.where(le == top1, lane, 2 * LANES), axis=-1, keepdims=True)
    le2 = jnp.where(lane == idx1, NEG_INF, le)
    top2 = jnp.max(le2, axis=-1, keepdims=True)
    idx2 = jnp.min(jnp.where(le2 == top2, lane, 2 * LANES), axis=-1, keepdims=True)
    e2 = jnp.exp(top2 - top1)
    inv = 1.0 / (1.0 + e2)
    gate1 = pg_sel * inv
    gate2 = pg_sel * (e2 * inv)
    used = jnp.where((lane == idx1) | (lane == idx2), 1.0, 0.0)
    blk = tri_ref.shape[0]
    used_b = used.astype(BF16)
    parts, run = [], base
    for r0 in range(0, used.shape[0], blk):
        parts.append(jnp.dot(tri_ref[...], used_b[r0:r0 + blk], preferred_element_type=F32) + run)
        run = run + jnp.sum(used[r0:r0 + blk], axis=0, keepdims=True)
    before = jnp.concatenate(parts, axis=0)
    rank1 = jnp.sum(jnp.where(lane == idx1, before, 0.0), axis=-1, keepdims=True)
    rank2 = jnp.sum(jnp.where(lane == idx2, before, 0.0), axis=-1, keepdims=True)
    route = jnp.zeros_like(logits)
    for pos, val in enumerate((idx1.astype(F32), idx2.astype(F32), gate1, gate2, rank1, rank2)):
        route = jnp.where(lane == pos, val, route)
    info = jnp.transpose(route)[0:ROUTE_ROWS, :]
    return x2, _pack_bf16_pairs(hn_f), route, info, run


def _mixer_prompt_kernel(sink_ref, x_ref, ng_ref, win_ref, bdq_ref, bdk_ref, gqlo_ref, gqhi_ref, gk_ref,
                         bias_ref, cw_ref, cb_ref, lg_ref, lb_ref, wout_ref,
                         x1_ref, ko_ref, vo_ref, co_ref,
                         kband, vband, uext, ushift, *, tm, nt):
    t = pl.program_id(1)

    @pl.when(t == 0)
    def _():
        kband[0:BLOCK, :] = jnp.zeros((BLOCK, KV_DIM), F32)
        vband[0:BLOCK, :] = jnp.zeros((BLOCK, KV_DIM), F32)
        uext[0:CONV_PAD, :] = jnp.zeros((CONV_PAD, D_CONV), F32)

    x = x_ref[...]
    q_lo, q_hi, kn, v, u = _mixer_proj(x, ng_ref, win_ref, bdq_ref, bdk_ref, gqlo_ref, gqhi_ref, gk_ref)
    kband[BLOCK:BLOCK + tm, :] = kn
    vband[BLOCK:BLOCK + tm, :] = v
    uext[CONV_PAD:CONV_PAD + tm, :] = u

    @pl.when(t == nt - 1)
    def _():
        ko_ref[...] = kn[tm - BLOCK:, :]
        vo_ref[...] = v[tm - BLOCK:, :]
        co_ref[...] = uext[pl.ds(CONV_PAD + tm - CONV_PAST, CONV_PAST), :]

    kd = [a.astype(BF16) for a in _dup_halves(kband[...])]
    vd = _dup_halves(vband[...])
    lo = lax.broadcasted_iota(jnp.int32, vd[0].shape, 1) < HALF
    v_lo = [jnp.where(lo, a, 0.0).astype(BF16) for a in vd]
    v_hi = [jnp.where(lo, 0.0, a).astype(BF16) for a in vd]
    q_lo = q_lo.astype(BF16)
    q_hi = q_hi.astype(BF16)
    attn_blocks = []
    for j in range(tm // BLOCK):
        rows = slice(j * BLOCK, (j + 1) * BLOCK)
        keys = slice(j * BLOCK, j * BLOCK + 2 * BLOCK)
        bias_base = jnp.where(t == 0, N_HEADS, 0) if j == 0 else 0
        tiles = []
        for c in range(N_KV_HEADS):
            q4 = jnp.concatenate(
                [(q_lo if (HEADS_PER_KV * c + a) % 2 == 0 else q_hi)[rows,
                  ((HEADS_PER_KV * c + a) // 2) * LANES:((HEADS_PER_KV * c + a) // 2 + 1) * LANES]
                 for a in range(HEADS_PER_KV)], axis=0)
            s_all = lax.dot_general(q4, kd[c][keys], (((1,), (1,)), ((), ())), preferred_element_type=F32)
            ps = []
            for a in range(HEADS_PER_KV):
                hd = HEADS_PER_KV * c + a
                s = s_all[a * BLOCK:(a + 1) * BLOCK] + bias_ref[bias_base + hd]
                ps.append(_sink_softmax(s, sink_ref[hd]).astype(BF16))
            vstack = jnp.concatenate([v_lo[c][keys], v_hi[c][keys]], axis=0)
            for i2 in range(HEADS_PER_KV // 2):
                pp = jnp.concatenate([ps[2 * i2], ps[2 * i2 + 1]], axis=1)
                tiles.append(jnp.dot(pp, vstack, preferred_element_type=F32))
        attn_blocks.append(jnp.concatenate(tiles, axis=1))
    attn = jnp.concatenate(attn_blocks, axis=0)

    off = CONV_PAD - CONV_PAST
    span = tm + CONV_PAD - SUBLANES
    for s in range(1, SUBLANES):
        ushift[s - 1] = uext[pl.ds(s, span), :]
    chunks = []
    for lc in range(D_CONV // LANES):
        ls = slice(lc * LANES, (lc + 1) * LANES)
        accs = [jnp.zeros((CONV_ROWS, LANES), F32) for _ in range(tm // CONV_ROWS)]
        for j in range(CONV_WIDTH):
            s, a = (off + j) % SUBLANES, (off + j) // SUBLANES
            wj = cw_ref[j:j + 1, ls]
            for rc in range(tm // CONV_ROWS):
                r0 = rc * CONV_ROWS + a * SUBLANES
                tap = uext[r0:r0 + CONV_ROWS, ls] if s == 0 else ushift[s - 1, r0:r0 + CONV_ROWS, ls]
                accs[rc] = accs[rc] + wj * tap
        chunks.append(jnp.concatenate(accs, axis=0))
    y = _conv_ln_silu(jnp.concatenate(chunks, axis=1), cb_ref, lg_ref, lb_ref)

    x1_ref[...] = (x + jnp.dot(attn.astype(BF16), wout_ref[0:D_ATTN, :], preferred_element_type=F32)
                   + jnp.dot(y.astype(BF16), wout_ref[D_ATTN:, :], preferred_element_type=F32))

    kband[0:BLOCK, :] = kband[tm:tm + BLOCK, :]
    vband[0:BLOCK, :] = vband[tm:tm + BLOCK, :]
    uext[0:CONV_PAD, :] = uext[tm:tm + CONV_PAD, :]


def _const_spec(shape):
    nd = len(shape)
    return pl.BlockSpec(shape, lambda *_: (0,) * nd)


def _mixer_prompt(x, sinks, consts, tm):
    b, s, d = x.shape
    nt = s // tm
    (ng, win, bdq, bdk, gqlo, gqhi, gk, bias_p, cw, cb, lg, lb, wout) = consts
    kern = functools.partial(_mixer_prompt_kernel, tm=tm, nt=nt)
    cspecs = [_const_spec(a.shape) for a in consts]
    return pl.pallas_call(
        kern,
        grid=(b, nt),
        in_specs=[pl.BlockSpec(memory_space=pltpu.SMEM),
                  pl.BlockSpec((None, tm, d), lambda i, j: (i, j, 0))] + cspecs,
        out_specs=[pl.BlockSpec((None, tm, d), lambda i, j: (i, j, 0)),
                   pl.BlockSpec((None, BLOCK, KV_DIM), lambda i, j: (i, 0, 0)),
                   pl.BlockSpec((None, BLOCK, KV_DIM), lambda i, j: (i, 0, 0)),
                   pl.BlockSpec((None, CONV_PAST, D_CONV), lambda i, j: (i, 0, 0))],
        out_shape=[jax.ShapeDtypeStruct((b, s, d), F32),
                   jax.ShapeDtypeStruct((b, BLOCK, KV_DIM), F32),
                   jax.ShapeDtypeStruct((b, BLOCK, KV_DIM), F32),
                   jax.ShapeDtypeStruct((b, CONV_PAST, D_CONV), F32)],
        scratch_shapes=[pltpu.VMEM((BLOCK + tm, KV_DIM), F32),
                        pltpu.VMEM((BLOCK + tm, KV_DIM), F32),
                        pltpu.VMEM((CONV_PAD + tm, D_CONV), F32),
                        pltpu.VMEM((SUBLANES - 1, tm + CONV_PAD - SUBLANES, D_CONV), F32)],
        compiler_params=pltpu.CompilerParams(dimension_semantics=("arbitrary", "arbitrary"),
                                             vmem_limit_bytes=VMEM_LIMIT),
        name="mixer_prompt",
    )(sinks, x, *consts)


def _mixer_sample_kernel(x_ref, ck_ref, cv_ref, st_ref, ng_ref, win_ref, bdq_ref, bdk_ref, gqlo_ref, gqhi_ref,
                         gk_ref, bias_ref, sinkrow_ref, cw_ref, cb_ref, lg_ref, lb_ref, wout_ref,
                         xag_ref, wmq_ref, gmq_ref,
                         x1_ref, qm_ref, ko_ref, vo_ref, co_ref,
                         kall, vall, qs, s_scr, r_scr, u_scr, y_scr, *, g, ts):
    wb = ck_ref.shape[2]
    x = x_ref[...]
    q_lo, q_hi, kn, v, u = _mixer_proj(x, ng_ref, win_ref, bdq_ref, bdk_ref, gqlo_ref, gqhi_ref, gk_ref)
    kall[:, wb:wb + ts, :] = kn.reshape(g, ts, KV_DIM)
    vall[:, wb:wb + ts, :] = v.reshape(g, ts, KV_DIM)
    for n in range(g):
        kall[n, 0:wb, :] = ck_ref[n].T
        vall[n, 0:wb, :] = cv_ref[n].T
        ko_ref[n] = kall[n, ts:wb + ts, :].T
        vo_ref[n] = vall[n, ts:wb + ts, :].T

    for lc in range(D_CONV // LANES):
        ls = slice(lc * LANES, (lc + 1) * LANES)
        u_scr[lc] = u[:, ls]
        u_t = [u_scr[lc, pl.ds(t, g, stride=ts), :] for t in range(ts)]

        def frame(r):
            return st_ref[r, :, ls] if r < CONV_PAST else u_t[r - CONV_PAST]

        for t in range(ts):
            acc = jnp.zeros((g, LANES), F32)
            for j in range(CONV_WIDTH):
                acc = acc + cw_ref[j:j + 1, ls] * frame(t + j)
            y_scr[lc, pl.ds(t, g, stride=ts), :] = acc
        for r in range(CONV_PAST):
            co_ref[r, :, ls] = frame(r + ts)
    y = _conv_ln_silu(jnp.concatenate([y_scr[lc] for lc in range(D_CONV // LANES)], axis=1), cb_ref, lg_ref, lb_ref)

    for hd in range(N_HEADS):
        tile = (q_lo if hd % 2 == 0 else q_hi)[:, (hd // 2) * LANES:(hd // 2 + 1) * LANES]
        if hd % 2 != hd // HEADS_PER_KV:
            tile = pltpu.roll(tile, HALF, axis=1)
        qs[:, hd * ts:(hd + 1) * ts, :] = tile.reshape(g, ts, LANES)
    for n in range(g):
        s_scr[n] = lax.dot_general(qs[n].astype(BF16), kall[n].astype(BF16), (((1,), (1,)), ((), ())),
                                   preferred_element_type=F32)
    s_scr[...] = _sink_softmax(s_scr[...] + bias_ref[...], sinkrow_ref[...])
    for n in range(g):
        r_scr[n] = jnp.dot(s_scr[n].astype(BF16), vall[n].astype(BF16), preferred_element_type=F32)
    r = r_scr[...].reshape(g * N_HEADS * ts, LANES)
    r_sw = pltpu.roll(r, HALF, axis=1).reshape(g, N_HEADS * ts, LANES)
    r = r.reshape(g, N_HEADS * ts, LANES)
    lo = lax.broadcasted_iota(jnp.int32, (g * ts, LANES), 1) < HALF
    heads = [(r if hd // HEADS_PER_KV == hd % 2 else r_sw)[:, hd * ts:(hd + 1) * ts, :].reshape(g * ts, LANES)
             for hd in range(N_HEADS)]
    attn = jnp.concatenate([jnp.where(lo, heads[2 * i], heads[2 * i + 1]) for i in range(N_HEADS // 2)], axis=1)

    x1 = (x + jnp.dot(attn.astype(BF16), wout_ref[0:D_ATTN, :], preferred_element_type=F32)
          + jnp.dot(y.astype(BF16), wout_ref[D_ATTN:, :], preferred_element_type=F32))
    x1_ref[...] = x1
    qm_ref[...] = _mem_query(x1, xag_ref, wmq_ref, gmq_ref)


def _mixer_sample(xs2d, ck, cv, st, consts, g, ts):
    n, d = xs2d.shape
    nb, wb = ck.shape[0], ck.shape[2]
    rows = g * ts
    cspecs = [_const_spec(a.shape) for a in consts]
    kern = functools.partial(_mixer_sample_kernel, g=g, ts=ts)
    return pl.pallas_call(
        kern,
        grid=(nb // g,),
        in_specs=[pl.BlockSpec((rows, d), lambda i: (i, 0)),
                  pl.BlockSpec((g, KV_DIM, wb), lambda i: (i, 0, 0)),
                  pl.BlockSpec((g, KV_DIM, wb), lambda i: (i, 0, 0)),
                  pl.BlockSpec((CONV_PAST, g, D_CONV), lambda i: (0, i, 0))] + cspecs,
        out_specs=[pl.BlockSpec((rows, d), lambda i: (i, 0)),
                   pl.BlockSpec((rows, d), lambda i: (i, 0)),
                   pl.BlockSpec((g, KV_DIM, wb), lambda i: (i, 0, 0)),
                   pl.BlockSpec((g, KV_DIM, wb), lambda i: (i, 0, 0)),
                   pl.BlockSpec((CONV_PAST, g, D_CONV), lambda i: (0, i, 0))],
        out_shape=[jax.ShapeDtypeStruct((n, d), F32),
                   jax.ShapeDtypeStruct((n, d), F32),
                   jax.ShapeDtypeStruct((nb, KV_DIM, wb), F32),
                   jax.ShapeDtypeStruct((nb, KV_DIM, wb), F32),
                   jax.ShapeDtypeStruct((CONV_PAST, nb, D_CONV), F32)],
        scratch_shapes=[pltpu.VMEM((g, wb + ts, KV_DIM), F32),
                        pltpu.VMEM((g, wb + ts, KV_DIM), F32),
                        pltpu.VMEM((g, N_HEADS * ts, LANES), F32),
                        pltpu.VMEM((g, N_HEADS * ts, wb + ts), F32),
                        pltpu.VMEM((g, N_HEADS * ts, LANES), F32),
                        pltpu.VMEM((D_CONV // LANES, rows, LANES), F32),
                        pltpu.VMEM((D_CONV // LANES, rows, LANES), F32)],
        compiler_params=pltpu.CompilerParams(dimension_semantics=("arbitrary",),
                                             vmem_limit_bytes=VMEM_LIMIT),
        name="mixer_sample",
    )(xs2d, ck, cv, st, *consts)


def _memory_kv_kernel(mem_ref, g_ref, wmk_ref, wmv_ref, gk_ref, k_ref, v_ref):
    h = _rms(mem_ref[...], g_ref[...]).astype(BF16)
    k = jnp.dot(h, wmk_ref[...], preferred_element_type=F32)
    parts = []
    for hd in range(MEM_HEADS):
        kh = k[:, hd * MEM_HEAD_DIM:(hd + 1) * MEM_HEAD_DIM]
        parts.append(kh * lax.rsqrt(jnp.mean(kh * kh, axis=-1, keepdims=True) + EPS))
    k_ref[...] = jnp.concatenate(parts, axis=1) * gk_ref[...]
    v_ref[...] = jnp.dot(h, wmv_ref[...], preferred_element_type=F32)


def _memory_kv(mem2d, g, wmk, wmv, gk, tm):
    n, d = mem2d.shape
    consts = (g, wmk, wmv, gk)
    return pl.pallas_call(
        _memory_kv_kernel,
        grid=(n // tm,),
        in_specs=[pl.BlockSpec((tm, d), lambda i: (i, 0))] + [_const_spec(a.shape) for a in consts],
        out_specs=[pl.BlockSpec((tm, d), lambda i: (i, 0)), pl.BlockSpec((tm, d), lambda i: (i, 0))],
        out_shape=[jax.ShapeDtypeStruct((n, d), F32), jax.ShapeDtypeStruct((n, d), F32)],
        compiler_params=pltpu.CompilerParams(dimension_semantics=("arbitrary",),
                                             vmem_limit_bytes=VMEM_LIMIT),
        name="memory_kv",
    )(mem2d, *consts)


def _cross_prompt_kernel(x1_ref, mk_ref, mv_ref, xag_ref, wmq_ref, gmq_ref, wmo_ref, fg_ref, wr_ref, br_ref, tri_ref,
                         x2_ref, hn_ref, route_ref, info_ref, cnt_ref, base):
    @pl.when((pl.program_id(0) == 0) & (pl.program_id(1) == 0))
    def _():
        base[...] = jnp.zeros_like(base)

    x1 = x1_ref[...]
    q = _mem_query(x1, xag_ref, wmq_ref, gmq_ref)
    o = _mem_attend(q, _split_heads(mk_ref[...].astype(BF16)), _split_heads(mv_ref[...].astype(BF16)))
    x2, hn, route, info, new_base = _cross_tail(o, x1, wmo_ref, fg_ref, wr_ref, br_ref, tri_ref, base[...])
    x2_ref[...] = x2
    hn_ref[...] = hn
    route_ref[...] = route
    info_ref[...] = info
    base[...] = new_base
    cnt_ref[...] = new_base


def _cross_prompt(x1, mk, mv, consts, tm):
    b, s, d = x1.shape
    m = mk.shape[1]
    cspecs = [_const_spec(a.shape) for a in consts]
    return pl.pallas_call(
        _cross_prompt_kernel,
        grid=(b, s // tm),
        in_specs=[pl.BlockSpec((None, tm, d), lambda i, j: (i, j, 0)),
                  pl.BlockSpec((None, m, d), lambda i, j: (i, 0, 0)),
                  pl.BlockSpec((None, m, d), lambda i, j: (i, 0, 0))] + cspecs,
        out_specs=[pl.BlockSpec((None, tm, d), lambda i, j: (i, j, 0)),
                   pl.BlockSpec((None, tm, d // 2), lambda i, j: (i, j, 0)),
                   pl.BlockSpec((None, tm, LANES), lambda i, j: (i, j, 0)),
                   pl.BlockSpec((ROUTE_ROWS, tm), lambda i, j: (0, i * (s // tm) + j)),
                   pl.BlockSpec((1, LANES), lambda i, j: (0, 0))],
        out_shape=[jax.ShapeDtypeStruct((b, s, d), F32),
                   jax.ShapeDtypeStruct((b, s, d // 2), jnp.int32),
                   jax.ShapeDtypeStruct((b, s, LANES), F32),
                   jax.ShapeDtypeStruct((ROUTE_ROWS, b * s), F32),
                   jax.ShapeDtypeStruct((1, LANES), F32)],
        scratch_shapes=[pltpu.VMEM((1, LANES), F32)],
        compiler_params=pltpu.CompilerParams(dimension_semantics=("arbitrary", "arbitrary"),
                                             vmem_limit_bytes=VMEM_LIMIT),
        name="cross_prompt",
    )(x1, mk, mv, *consts)


def _mem_attend_sample_kernel(qm_ref, *refs, g, ts):
    k_refs, v_refs, o_ref, s_scr = refs[:g], refs[g:2 * g], refs[2 * g], refs[2 * g + 1]
    mlen = k_refs[0].shape[0] // MEM_HEADS

    def all_heads(ref):
        pieces = []
        for hd in range(MEM_HEADS):
            w = ref[pl.ds(hd, mlen, stride=MEM_HEADS), :]
            pieces.append(pltpu.bitcast(lax.shift_left(w, jnp.int32(16)), F32))
            pieces.append(pltpu.bitcast(w & jnp.int32(-65536), F32))
        return jnp.concatenate(pieces, axis=1).astype(BF16)

    rows = MEM_HEADS * ts
    own = (lax.broadcasted_iota(jnp.int32, (rows, MEM_HEADS * MEM_HEAD_DIM), 1) // MEM_HEAD_DIM
           == lax.broadcasted_iota(jnp.int32, (rows, MEM_HEADS * MEM_HEAD_DIM), 0) // ts)
    for n in range(g):
        q = qm_ref[n * ts:(n + 1) * ts, :]
        q_bd = jnp.where(own, jnp.concatenate([q] * MEM_HEADS, axis=0), 0.0).astype(BF16)
        s_scr[n] = lax.dot_general(q_bd, all_heads(k_refs[n]), (((1,), (1,)), ((), ())),
                                   preferred_element_type=F32)
    s = s_scr[...]
    p = jnp.exp(s - jnp.max(s, axis=-1, keepdims=True))
    s_scr[...] = p * (1.0 / jnp.sum(p, axis=-1, keepdims=True))
    for n in range(g):
        r = jnp.dot(s_scr[n].astype(BF16), all_heads(v_refs[n]), preferred_element_type=F32)
        o_ref[n * ts:(n + 1) * ts, :] = jnp.concatenate(
            [r[hd * ts:(hd + 1) * ts, hd * MEM_HEAD_DIM:(hd + 1) * MEM_HEAD_DIM] for hd in range(MEM_HEADS)], axis=1)


def _mem_cache_rows(cache):
    nb, mlen = cache.shape[:2]
    halves = MEM_HEAD_DIM // LANES
    return (cache.reshape(nb, mlen, MEM_HEADS, halves, LANES).transpose(0, 1, 3, 2, 4)
            .reshape(nb, mlen * halves * MEM_HEADS, LANES))


def _sc_pack_cache(rows):
    nb, nrows, _ = rows.shape
    total = nb * nrows
    per_worker = total // SC_WORKERS
    chunk = SC_PACK_POS * SUBLANES
    assert MEM_HEAD_DIM == 2 * LANES and 2 * MEM_HEADS == SUBLANES and per_worker % chunk == 0
    tiles_in = rows.reshape(total // SUBLANES, SUBLANES, LANES)
    mesh = plsc.VectorSubcoreMesh(core_axis_name="c", subcore_axis_name="s")

    @functools.partial(
        pl.kernel, mesh=mesh,
        out_type=jax.ShapeDtypeStruct((total // (2 * SUBLANES), SUBLANES, LANES), jnp.int32),
        scratch_types=[pltpu.VMEM((SC_PACK_POS, SUBLANES, LANES), F32),
                       pltpu.VMEM((SC_PACK_POS // 2, SUBLANES, LANES), jnp.int32)],
        compiler_params=pltpu.CompilerParams(needs_layout_passes=False))
    def pack(in_hbm, out_hbm, in_v, out_v):
        wid = lax.axis_index("s") * SC_CORES + lax.axis_index("c")
        base = wid * (per_worker // SUBLANES)

        @pl.loop(0, per_worker // chunk)
        def _(ci):
            off = base + ci * SC_PACK_POS
            pltpu.sync_copy(in_hbm.at[pl.ds(off, SC_PACK_POS)], in_v)

            @pl.loop(0, SC_PACK_POS // 2)
            def _(q):
                for pp in range(2):
                    for hd in range(MEM_HEADS):
                        for l0 in range(0, LANES, SC_LANES):
                            lo = in_v[2 * q + pp, hd, pl.ds(l0, SC_LANES)]
                            hi = in_v[2 * q + pp, MEM_HEADS + hd, pl.ds(l0, SC_LANES)]
                            word = plsc.bitcast(plsc.pack(lo, hi, format=plsc.PackFormat.INTERLEAVED), jnp.int32)
                            out_v[q, pp * MEM_HEADS + hd, pl.ds(l0, SC_LANES)] = word

            pltpu.sync_copy(out_v, out_hbm.at[pl.ds(off // 2, SC_PACK_POS // 2)])

    return pack(tiles_in).reshape(nb, nrows // 2, LANES)


def _mem_attend_sample(qm, mk, mv, g, ts):
    n, d = qm.shape
    nb, rows = mk.shape[0], mk.shape[1]
    kern = functools.partial(_mem_attend_sample_kernel, g=g, ts=ts)

    def seq_spec(j):
        return pl.BlockSpec((None, rows, LANES), lambda i: (g * i + j, 0, 0))

    return pl.pallas_call(
        kern,
        grid=(nb // g,),
        in_specs=[pl.BlockSpec((g * ts, d), lambda i: (i, 0))] + [seq_spec(j) for j in range(g)] * 2,
        out_specs=pl.BlockSpec((g * ts, d), lambda i: (i, 0)),
        out_shape=jax.ShapeDtypeStruct((n, d), F32),
        scratch_shapes=[pltpu.VMEM((g, MEM_HEADS * ts, rows // MEM_HEADS), F32)],
        compiler_params=pltpu.CompilerParams(dimension_semantics=("arbitrary",),
                                             vmem_limit_bytes=VMEM_LIMIT),
        name="mem_attend_sample",
    )(qm, *([mk] * g), *([mv] * g))


def _cross_tail_kernel(o_ref, x1_ref, wmo_ref, fg_ref, wr_ref, br_ref, tri_ref,
                       x2_ref, hn_ref, route_ref, info_ref, cnt_ref, base):
    @pl.when(pl.program_id(0) == 0)
    def _():
        base[...] = jnp.zeros_like(base)

    x2, hn, route, info, new_base = _cross_tail(o_ref[...], x1_ref[...], wmo_ref, fg_ref, wr_ref, br_ref, tri_ref,
                                                base[...])
    x2_ref[...] = x2
    hn_ref[...] = hn
    route_ref[...] = route
    info_ref[...] = info
    base[...] = new_base
    cnt_ref[...] = new_base


def _cross_tail_call(o, x1, consts, tm):
    n, d = x1.shape
    cspecs = [_const_spec(a.shape) for a in consts]
    return pl.pallas_call(
        _cross_tail_kernel,
        grid=(n // tm,),
        in_specs=[pl.BlockSpec((tm, d), lambda i: (i, 0)), pl.BlockSpec((tm, d), lambda i: (i, 0))] + cspecs,
        out_specs=[pl.BlockSpec((tm, d), lambda i: (i, 0)),
                   pl.BlockSpec((tm, d // 2), lambda i: (i, 0)),
                   pl.BlockSpec((tm, LANES), lambda i: (i, 0)),
                   pl.BlockSpec((ROUTE_ROWS, tm), lambda i: (0, i)),
                   pl.BlockSpec((1, LANES), lambda i: (0, 0))],
        out_shape=[jax.ShapeDtypeStruct((n, d), F32),
                   jax.ShapeDtypeStruct((n, d // 2), jnp.int32),
                   jax.ShapeDtypeStruct((n, LANES), F32),
                   jax.ShapeDtypeStruct((ROUTE_ROWS, n), F32),
                   jax.ShapeDtypeStruct((1, LANES), F32)],
        scratch_shapes=[pltpu.VMEM((1, LANES), F32)],
        compiler_params=pltpu.CompilerParams(dimension_semantics=("arbitrary",),
                                             vmem_limit_bytes=VMEM_LIMIT),
        name="cross_tail",
    )(o, x1, *consts)


def _sc_rows_per_step(per_worker):
    step = min(SC_ROWS_PER_STEP, per_worker)
    assert per_worker % step == 0 and step % 8 == 0
    return step


def _sc_gather_rows(table, idx):
    nrows = idx.shape[0]
    _, width = table.shape
    assert nrows % (8 * SC_WORKERS) == 0
    per_worker = nrows // SC_WORKERS
    step = _sc_rows_per_step(per_worker // 2)
    mesh = plsc.VectorSubcoreMesh(core_axis_name="c", subcore_axis_name="s")

    @functools.partial(
        pl.kernel, mesh=mesh, out_type=jax.ShapeDtypeStruct((nrows, width), table.dtype),
        scratch_types=[pltpu.VMEM((step,), jnp.int32)] * 2 + [pltpu.VMEM((step, width), table.dtype)] * 2
        + [pltpu.SemaphoreType.DMA] * 2)
    def gather(table_hbm, idx_hbm, out_hbm, idx_a, idx_b, rows_a, rows_b, sem_a, sem_b):
        wid = lax.axis_index("s") * SC_CORES + lax.axis_index("c")
        base = wid * per_worker

        @pl.loop(0, per_worker // (2 * step))
        def _(i):
            off = base + i * (2 * step)
            pltpu.sync_copy(idx_hbm.at[pl.ds(off, step)], idx_a)
            pltpu.sync_copy(idx_hbm.at[pl.ds(off + step, step)], idx_b)
            gather_a = pltpu.async_copy(table_hbm.at[idx_a], rows_a, sem_a)
            gather_b = pltpu.async_copy(table_hbm.at[idx_b], rows_b, sem_b)
            gather_a.wait()
            write_a = pltpu.async_copy(rows_a, out_hbm.at[pl.ds(off, step)], sem_a)
            gather_b.wait()
            write_b = pltpu.async_copy(rows_b, out_hbm.at[pl.ds(off + step, step)], sem_b)
            write_a.wait()
            write_b.wait()

    return gather(table, idx)


def _sc_scatter_rows(table, pos, nrows_out):
    n, width = table.shape
    assert n % (8 * SC_WORKERS) == 0 and pos.shape == (TOP_K * n,)
    per_worker = n // SC_WORKERS
    step = _sc_rows_per_step(per_worker)
    mesh = plsc.VectorSubcoreMesh(core_axis_name="c", subcore_axis_name="s")

    @functools.partial(
        pl.kernel, mesh=mesh, out_type=jax.ShapeDtypeStruct((nrows_out, width), table.dtype),
        scratch_types=[pltpu.VMEM((step,), jnp.int32)] * TOP_K
        + [pltpu.VMEM((step, width), table.dtype), pltpu.SemaphoreType.DMA])
    def scatter(table_hbm, pos_hbm, out_hbm, *scratch):
        idx_vs, rows_v, sem = scratch[:TOP_K], scratch[TOP_K], scratch[TOP_K + 1]
        wid = lax.axis_index("s") * SC_CORES + lax.axis_index("c")
        base = wid * per_worker

        @pl.loop(0, per_worker // step)
        def _(i):
            off = base + i * step
            for k in range(TOP_K):
                pltpu.sync_copy(pos_hbm.at[pl.ds(k * n + off, step)], idx_vs[k])
            pltpu.sync_copy(table_hbm.at[pl.ds(off, step)], rows_v)
            for k in range(TOP_K):
                pltpu.async_copy(rows_v, out_hbm.at[idx_vs[k]], sem).wait()

    return scatter(table, pos)


W_CHUNKS = 4
FETCH_AHEAD = 3
FETCH_AHEAD_F32 = 2


def _expert_weight_copies(e, slot, w_hbm, w_slots, sems, chunks):
    copies = []
    for m, (src, dst) in enumerate(zip(w_hbm, w_slots)):
        rows = dst.shape[1] // chunks
        for c in range(chunks):
            sl = pl.ds(c * rows, rows)
            sem = sems.at[(slot * len(w_hbm) + m) * chunks + c]
            copies.append(pltpu.make_async_copy(src.at[e, sl, :], dst.at[slot, sl, :], sem))
    return copies


def _moe_grouped_kernel(te_ref, nv_ref, rv_ref, slot_ref, *refs, publish, ahead, chunks):
    later_refs, (xs_ref, wg_hbm, wu_hbm, wd_hbm, ys_ref), rest = refs[:ahead], refs[ahead:ahead + 5], refs[ahead + 5:]
    i = pl.program_id(0)
    n_slots = ahead + 1
    w_hbm = (wg_hbm, wu_hbm, wd_hbm)
    if publish:
        out_hbm, w_slots, w_bf16, sems, out_sems = rest[0:3], rest[3:6], rest[6:9], rest[9], rest[10]
    else:
        w_slots, sems = rest[0:3], rest[3]

    def publish_copies(e):
        return [pltpu.make_async_copy(src, dst.at[e], out_sems.at[m])
                for m, (src, dst) in enumerate(zip(w_bf16, out_hbm))]

    def fetch(e, slot):
        return _expert_weight_copies(e, slot, w_hbm, w_slots, sems, chunks)

    @pl.when(i == 0)
    def _():
        for cp in fetch(te_ref[0], slot_ref[0]):
            cp.start()
        for k in range(ahead - 1):
            @pl.when(later_refs[k][0] >= 0)
            def _():
                for cp in fetch(later_refs[k][0], (slot_ref[0] + k + 1) % n_slots):
                    cp.start()

    @pl.when(i < nv_ref[0])
    def _():
        slot = slot_ref[i]

        @pl.when((i == 0) | (te_ref[i] != te_ref[jnp.maximum(i - 1, 0)]))
        def _():
            for cp in fetch(te_ref[i], slot):
                cp.wait()
            if publish:
                @pl.when(i > 0)
                def _():
                    for cp in publish_copies(te_ref[jnp.maximum(i - 1, 0)]):
                        cp.wait()

                for src, dst in zip(w_slots, w_bf16):
                    dst[...] = src[slot].astype(BF16)
                for cp in publish_copies(te_ref[i]):
                    cp.start()

            @pl.when(later_refs[ahead - 1][i] >= 0)
            def _():
                for cp in fetch(later_refs[ahead - 1][i], (slot + ahead) % n_slots):
                    cp.start()

        wg, wu, wd = [r[...] for r in w_bf16] if publish else [r[slot] for r in w_slots]
        xs = xs_ref[...]
        row = lax.broadcasted_iota(jnp.int32, xs.shape, 0)
        x = _unpack_bf16_pairs(jnp.where(row < rv_ref[i], xs, 0)).astype(BF16)
        a = jnp.dot(x, wg, preferred_element_type=F32)
        b = jnp.dot(x, wu, preferred_element_type=F32)
        act = (a * jax.nn.sigmoid(a)) * b
        y = jnp.dot(act.astype(BF16), wd, preferred_element_type=F32)
        ys_ref[...] = _pack_bf16_pairs(y)

    if publish:
        @pl.when(i == pl.num_programs(0) - 1)
        def _():
            for cp in publish_copies(te_ref[nv_ref[0] - 1]):
                cp.wait()


def _moe_grouped(tile_expert, n_valid, rows_valid, run_slot, later_experts, xs, wg, wu, wd, tm, publish):
    p, half = xs.shape
    ne, d, f = wg.shape
    ahead = len(later_experts)
    chunks = W_CHUNKS if publish else 1

    def live_tile(i, te, nv, *_):
        return (jnp.minimum(i, nv[0] - 1), 0)

    slot_dtype = F32 if publish else BF16
    n_slots = ahead + 1
    scratch = [pltpu.VMEM((n_slots, d, f), slot_dtype), pltpu.VMEM((n_slots, d, f), slot_dtype),
               pltpu.VMEM((n_slots, f, d), slot_dtype)]
    out_specs = [pl.BlockSpec((tm, half), live_tile)]
    out_shape = [jax.ShapeDtypeStruct((p, half), jnp.int32)]
    if publish:
        scratch += [pltpu.VMEM((d, f), BF16), pltpu.VMEM((d, f), BF16), pltpu.VMEM((f, d), BF16)]
        out_specs += [pl.BlockSpec(memory_space=pl.ANY)] * 3
        out_shape += [jax.ShapeDtypeStruct(w.shape, BF16) for w in (wg, wu, wd)]
    scratch += [pltpu.SemaphoreType.DMA((n_slots * 3 * chunks,))]
    if publish:
        scratch += [pltpu.SemaphoreType.DMA((3,))]
    grid_spec = pltpu.PrefetchScalarGridSpec(
        num_scalar_prefetch=4 + ahead,
        grid=(p // tm,),
        in_specs=[pl.BlockSpec((tm, half), live_tile)] + [pl.BlockSpec(memory_space=pl.ANY)] * 3,
        out_specs=out_specs,
        scratch_shapes=scratch)
    res = pl.pallas_call(
        functools.partial(_moe_grouped_kernel, publish=publish, ahead=ahead, chunks=chunks),
        grid_spec=grid_spec,
        out_shape=out_shape,
        compiler_params=pltpu.CompilerParams(dimension_semantics=("arbitrary",),
                                             vmem_limit_bytes=VMEM_LIMIT),
        name="moe_grouped",
    )(tile_expert, n_valid, rows_valid, run_slot, *later_experts, xs, wg, wu, wd)
    return res[0], tuple(res[1:])


def _moe_combine_kernel(x2_ref, route_ref, y0_ref, y1_ref, out_ref):
    route = route_ref[...]
    out_ref[...] = (x2_ref[...] + route[:, 2:3] * _unpack_bf16_pairs(y0_ref[...])
                    + route[:, 3:4] * _unpack_bf16_pairs(y1_ref[...]))


def _moe_combine(x2, route, gathered, tm):
    n, d = x2.shape
    nt = n // tm
    return pl.pallas_call(
        _moe_combine_kernel,
        grid=(nt,),
        in_specs=[pl.BlockSpec((tm, d), lambda i: (i, 0)),
                  pl.BlockSpec((tm, LANES), lambda i: (i, 0)),
                  pl.BlockSpec((tm, d // 2), lambda i: (i, 0)),
                  pl.BlockSpec((tm, d // 2), lambda i: (i + nt, 0))],
        out_specs=pl.BlockSpec((tm, d), lambda i: (i, 0)),
        out_shape=jax.ShapeDtypeStruct((n, d), F32),
        compiler_params=pltpu.CompilerParams(dimension_semantics=("arbitrary",),
                                             vmem_limit_bytes=VMEM_LIMIT),
        name="moe_combine",
    )(x2, route, gathered, gathered)


def _moe(x2, hn_packed, route, info, counts, weights, tm_rows, publish):
    n = x2.shape[0]
    p = (TOP_K * n // tm_rows + N_EXPERTS) * tm_rows
    n_tiles = p // tm_rows
    cnt = counts[0, :N_EXPERTS].astype(jnp.int32)
    tiles_per_e = jnp.maximum((cnt + tm_rows - 1) // tm_rows, 1 if publish else 0)
    e_ids = jnp.arange(N_EXPERTS, dtype=jnp.int32)
    tile_end = jnp.sum(jnp.where(e_ids[None, :] <= e_ids[:, None], tiles_per_e[None, :], 0), axis=1)
    tile_start = tile_end - tiles_per_e
    n_valid = tile_end[-1:]
    tile_ids = jnp.arange(n_tiles, dtype=jnp.int32)
    tile_expert = jnp.minimum(jnp.sum((tile_end[None, :] <= tile_ids[:, None]).astype(jnp.int32), axis=1),
                              N_EXPERTS - 1)
    mine = tile_expert[:, None] == e_ids[None, :]
    rows_left = jnp.sum(jnp.where(mine, cnt - (tile_ids[:, None] - tile_start) * tm_rows, 0), axis=1)
    rows_valid = jnp.clip(rows_left, 0, tm_rows).astype(jnp.int32)
    ahead = FETCH_AHEAD_F32 if publish else FETCH_AHEAD
    has_tiles = tiles_per_e > 0
    run_of_e = jnp.sum((has_tiles[None, :] & (e_ids[None, :] < e_ids[:, None])).astype(jnp.int32), axis=1)
    run_of_tile = jnp.sum(jnp.where(mine, run_of_e, 0), axis=1)
    run_slot = (run_of_tile % (ahead + 1)).astype(jnp.int32)
    later_experts = []
    for k in range(1, ahead + 1):
        is_run = has_tiles[None, :] & (run_of_e[None, :] == run_of_tile[:, None] + k)
        later_experts.append(jnp.sum(jnp.where(is_run, e_ids[None, :] + 1, 0), axis=1).astype(jnp.int32) - 1)
    eidx = info[0:TOP_K].astype(jnp.int32)
    row_start = jnp.sum(jnp.where(eidx[None] == e_ids[:, None, None], (tile_start * tm_rows)[:, None, None], 0),
                        axis=0)
    pos = (row_start + info[4:4 + TOP_K].astype(jnp.int32)).reshape(-1)

    xs = _sc_scatter_rows(hn_packed, pos, p)
    ys, w_bf16 = _moe_grouped(tile_expert, n_valid, rows_valid, run_slot, later_experts, xs, *weights, tm_rows,
                              publish)
    back = _sc_gather_rows(ys, pos)
    return _moe_combine(x2, route, back, min(TM_COMBINE, n)), w_bf16


def _block_diag(n, width):
    idx = np.arange(n) // width
    return jnp.asarray((idx[:, None] == idx[None, :]).astype(np.float32), dtype=BF16)


def _kv_feature_major(cache):
    nb, wb = cache.shape[:2]
    return jnp.transpose(cache, (0, 2, 3, 1)).reshape(nb, KV_DIM, wb)


def _strict_lower(n):
    r = np.arange(n)
    return jnp.asarray((r[:, None] > r[None, :]).astype(np.float32), dtype=BF16)


def _alibi_slopes():
    return np.exp2(-8.0 * np.arange(1, N_HEADS + 1, dtype=np.float32) / N_HEADS).astype(np.float32)


def _prompt_bias():
    i = np.arange(BLOCK)[:, None]
    s = np.arange(2 * BLOCK)[None, :]
    dist = (i + BLOCK - s).astype(np.float32)
    mask = (dist >= 0) & (dist < WINDOW)
    first = mask & (s >= BLOCK)
    slopes = _alibi_slopes()[:, None, None]
    reg = np.where(mask[None], -slopes * dist[None], np.float32(NEG_INF))
    fst = np.where(first[None], -slopes * dist[None], np.float32(NEG_INF))
    return jnp.asarray(np.concatenate([reg, fst], axis=0).astype(np.float32))


def _sample_bias(ts, wb):
    i = np.arange(ts)[:, None]
    s = np.arange(wb + ts)[None, :]
    dist = (i + wb - s).astype(np.float32)
    mask = (dist >= 0) & (dist < WINDOW)
    slopes = _alibi_slopes()[:, None, None]
    b = np.where(mask[None], -slopes * dist[None], np.float32(NEG_INF)).astype(np.float32)
    return jnp.asarray(b.reshape(N_HEADS * ts, wb + ts))


def kernel(x_prompt, x_sample, cache_swa_k, cache_swa_v, state_conv, cache_mem_k, cache_mem_v, mem_prompt, norm_mix_g, w_in, q_norm_g, k_norm_g, attn_sinks, conv_dw_w, conv_dw_b, conv_ln_g, conv_ln_b, w_out, norm_xa_g, norm_mem_g, w_mq, w_mk, w_mv, mq_norm_g, mk_norm_g, w_mo, norm_ffn_g, w_router_group, b_router_group, w_router_expert, b_router_expert, w_exp_gate, w_exp_up, w_exp_down):
    depth = w_in.shape[0]
    bp, sp, d = x_prompt.shape
    nb, ts, _ = x_sample.shape
    wb = cache_swa_k.shape[2]
    mlen = mem_prompt.shape[1]
    assert d == D_MODEL and wb == WINDOW and sp % TM_PROMPT == 0 and sp % TM_CROSS == 0 and nb % SEQ_PER_STEP == 0

    bdq = _block_diag(MXU_TILE, HEAD_DIM)
    bdk = _block_diag(KV_DIM, HEAD_DIM)
    bias_p = _prompt_bias()
    bias_s = _sample_bias(ts, wb)
    lane_lo = (np.arange(D_ATTN) % LANES) < HALF
    row = lambda a: a.reshape(1, -1).astype(F32)

    xp = x_prompt
    xs = x_sample.reshape(nb * ts, d)
    kp_l, vp_l, cp_l, mkp_l, mvp_l, ks_l, vs_l, cs_l = [], [], [], [], [], [], [], []
    for l in range(depth):
        gq = jnp.tile(q_norm_g[l].astype(F32), N_HEADS) * (HEAD_DIM ** -0.5)
        gqlo = jnp.where(lane_lo, gq, 0.0).reshape(1, -1)
        gqhi = jnp.where(lane_lo, 0.0, gq).reshape(1, -1)
        gk = jnp.tile(k_norm_g[l].astype(F32), N_KV_HEADS).reshape(1, -1)
        sinks = attn_sinks[l].astype(F32)
        sinkrow = jnp.repeat(sinks, ts).reshape(N_HEADS * ts, 1)
        win = w_in[l].astype(BF16)
        wout = w_out[l].astype(BF16)
        wmq = w_mq[l].astype(BF16)
        wmo = w_mo[l].astype(BF16)
        gmq = (jnp.tile(mq_norm_g[l].astype(F32), MEM_HEADS) * (MEM_HEAD_DIM ** -0.5)).reshape(1, -1)
        gmk = jnp.tile(mk_norm_g[l].astype(F32), MEM_HEADS).reshape(1, -1)
        w_r = jnp.concatenate([w_router_expert[l], w_router_group[l],
                               jnp.zeros((d, LANES - N_EXPERTS - N_GROUPS), F32)], axis=1).astype(BF16)
        b_r = jnp.concatenate([b_router_expert[l], b_router_group[l],
                               jnp.zeros((LANES - N_EXPERTS - N_GROUPS,), F32)]).reshape(1, -1).astype(F32)
        wg, wu, wd = w_exp_gate[l], w_exp_up[l], w_exp_down[l]

        mix_consts = (row(norm_mix_g[l]), win, bdq, bdk, gqlo, gqhi, gk)
        conv_consts = (conv_dw_w[l].astype(F32), row(conv_dw_b[l]), row(conv_ln_g[l]), row(conv_ln_b[l]), wout)
        tail_consts = (wmo, row(norm_ffn_g[l]), w_r, b_r)

        x1p, kp, vp, cp = _mixer_prompt(xp, sinks, mix_consts + (bias_p,) + conv_consts, TM_PROMPT)
        mk, mv = _memory_kv(mem_prompt.reshape(bp * mlen, d), row(norm_mem_g[l]),
                            w_mk[l].astype(BF16), w_mv[l].astype(BF16), gmk, min(256, bp * mlen))
        mk = mk.reshape(bp, mlen, d)
        mv = mv.reshape(bp, mlen, d)
        x2p, hnp, routep, infop, cntp = _cross_prompt(x1p, mk, mv, (row(norm_xa_g[l]), wmq, gmq) + tail_consts
                                               + (_strict_lower(RANK_BLOCK),), TM_CROSS)
        xp, w_bf16 = _moe(x2p.reshape(bp * sp, d), hnp.reshape(bp * sp, d // 2), routep.reshape(bp * sp, LANES),
                          infop, cntp, (wg, wu, wd), TM_ROWS_PROMPT, True)
        xp = xp.reshape(bp, sp, d)
        kp_l.append(kp.reshape(bp, BLOCK, N_KV_HEADS, HEAD_DIM))
        vp_l.append(vp.reshape(bp, BLOCK, N_KV_HEADS, HEAD_DIM))
        cp_l.append(cp)
        mkp_l.append(mk.reshape(bp, mlen, MEM_HEADS, MEM_HEAD_DIM))
        mvp_l.append(mv.reshape(bp, mlen, MEM_HEADS, MEM_HEAD_DIM))

        x1s, qm, ksn, vsn, csn = _mixer_sample(
            xs, _kv_feature_major(cache_swa_k[l]), _kv_feature_major(cache_swa_v[l]),
            jnp.transpose(state_conv[l], (1, 0, 2)),
            mix_consts + (bias_s, sinkrow) + conv_consts + (row(norm_xa_g[l]), wmq, gmq), SEQ_PER_STEP, ts)
        o_s = _mem_attend_sample(qm, _sc_pack_cache(_mem_cache_rows(cache_mem_k[l])),
                                 _sc_pack_cache(_mem_cache_rows(cache_mem_v[l])),
                                 MEM_SEQ_PER_STEP, ts)
        tms = min(256, nb * ts)
        x2s, hns, routes, infos, cnts = _cross_tail_call(o_s, x1s, tail_consts + (_strict_lower(RANK_BLOCK),), tms)
        xs, _ = _moe(x2s, hns, routes, infos, cnts, w_bf16, TM_ROWS_SAMPLE, False)
        ks_l.append(jnp.transpose(ksn.reshape(nb, N_KV_HEADS, HEAD_DIM, wb), (0, 3, 1, 2)))
        vs_l.append(jnp.transpose(vsn.reshape(nb, N_KV_HEADS, HEAD_DIM, wb), (0, 3, 1, 2)))
        cs_l.append(jnp.transpose(csn, (1, 0, 2)))

    st = lambda xs_: jnp.stack(xs_, axis=0)
    return (xp, xs.reshape(nb, ts, d), st(kp_l), st(vp_l), st(cp_l), st(mkp_l), st(mvp_l),
            st(ks_l), st(vs_l), st(cs_l))
```

```python
import functools

import numpy as np
import jax
import jax.numpy as jnp
from jax import lax
from jax.experimental import pallas as pl
from jax.experimental.pallas import tpu as pltpu
from jax.experimental.pallas import tpu_sc as plsc

F32 = jnp.float32
BF16 = jnp.bfloat16

D_MODEL = 1024
D_ATTN = 512
D_CONV = 512
HEAD_DIM = 64
N_HEADS = 8
N_KV_HEADS = 2
KV_DIM = N_KV_HEADS * HEAD_DIM
HEADS_PER_KV = N_HEADS // N_KV_HEADS
WINDOW = 128
BLOCK = 128
CONV_WIDTH = 31
CONV_PAST = CONV_WIDTH - 1
MEM_HEADS = 4
MEM_HEAD_DIM = 256
N_GROUPS = 4
EXPERTS_PER_GROUP = 8
N_EXPERTS = 32
EPS = 1e-6
NEG_INF = -1e30

LANES = 128
HALF = LANES // 2
MXU_TILE = 256
SUBLANES = 8
CONV_PAD = 32
CONV_ROWS = 64
VMEM_LIMIT = 56 * 1024 * 1024

TM_PROMPT = 512
TM_CROSS = 1024
SEQ_PER_STEP = 32
MEM_SEQ_PER_STEP = 16
TOP_K = 2
RANK_BLOCK = 256
ROUTE_ROWS = 8
SC_CORES = 2
SC_WORKERS = 32
SC_ROWS_PER_STEP = 64
SC_LANES = 16
SC_PACK_POS = 64
TM_ROWS_PROMPT = 512
TM_ROWS_SAMPLE = 128
TM_COMBINE = 1024


def _rms(x, g):
    ms = jnp.mean(x * x, axis=-1, keepdims=True)
    return x * lax.rsqrt(ms + EPS) * g


def _group_mean_sq(x, bd_ref, width):
    x2 = x * x
    hi = x2.astype(BF16)
    lo = (x2 - hi.astype(F32)).astype(BF16)
    bd = bd_ref[...]
    w = bd.shape[0]
    parts = [jnp.dot(hi[:, c:c + w], bd, preferred_element_type=F32)
             + jnp.dot(lo[:, c:c + w], bd, preferred_element_type=F32) for c in range(0, x.shape[1], w)]
    return jnp.concatenate(parts, axis=1) * (1.0 / width)


def _mixer_proj(x, ng_ref, win_ref, bdq_ref, bdk_ref, gqlo_ref, gqhi_ref, gk_ref):
    h = _rms(x, ng_ref[...]).astype(BF16)
    p = jnp.dot(h, win_ref[...], preferred_element_type=F32)
    q = p[:, :D_ATTN]
    k = p[:, D_ATTN:D_ATTN + KV_DIM]
    v = p[:, D_ATTN + KV_DIM:D_ATTN + 2 * KV_DIM]
    ua = p[:, D_ATTN + 2 * KV_DIM:D_ATTN + 2 * KV_DIM + D_CONV]
    ub = p[:, D_ATTN + 2 * KV_DIM + D_CONV:]
    qn = q * lax.rsqrt(_group_mean_sq(q, bdq_ref, HEAD_DIM) + EPS)
    q_lo = qn * gqlo_ref[...]
    q_hi = qn * gqhi_ref[...]
    kn = k * lax.rsqrt(_group_mean_sq(k, bdk_ref, HEAD_DIM) + EPS) * gk_ref[...]
    u = ua * jax.nn.sigmoid(ub)
    return q_lo, q_hi, kn, v, u


def _dup_halves(x):
    lo = lax.broadcasted_iota(jnp.int32, x.shape, 1) < HALF
    xr = pltpu.roll(x, HALF, axis=1)
    return jnp.where(lo, x, xr), jnp.where(lo, xr, x)


def _sink_softmax(s, sink):
    m = jnp.maximum(jnp.max(s, axis=-1, keepdims=True), sink)
    p = jnp.exp(s - m)
    denom = jnp.sum(p, axis=-1, keepdims=True) + jnp.exp(sink - m)
    return p * (1.0 / denom)


def _conv_ln_silu(y, cb_ref, lg_ref, lb_ref):
    y = y + cb_ref[...]
    mu = jnp.mean(y, axis=-1, keepdims=True)
    yc = y - mu
    yn = yc * lax.rsqrt(jnp.mean(yc * yc, axis=-1, keepdims=True) + EPS)
    z = yn * lg_ref[...] + lb_ref[...]
    return z * jax.nn.sigmoid(z)


def _mem_query(x1, xag_ref, wmq_ref, gmq_ref):
    h = _rms(x1, xag_ref[...]).astype(BF16)
    q = jnp.dot(h, wmq_ref[...], preferred_element_type=F32)
    parts = []
    for hd in range(MEM_HEADS):
        qh = q[:, hd * MEM_HEAD_DIM:(hd + 1) * MEM_HEAD_DIM]
        parts.append(qh * lax.rsqrt(jnp.mean(qh * qh, axis=-1, keepdims=True) + EPS))
    return jnp.concatenate(parts, axis=1) * gmq_ref[...]


def _mem_attend(q, k_heads, v_heads):
    outs = []
    for hd in range(MEM_HEADS):
        sl = slice(hd * MEM_HEAD_DIM, (hd + 1) * MEM_HEAD_DIM)
        s = lax.dot_general(q[:, sl].astype(BF16), k_heads[hd], (((1,), (1,)), ((), ())),
                            preferred_element_type=F32)
        m = jnp.max(s, axis=-1, keepdims=True)
        p = jnp.exp(s - m)
        p = p * (1.0 / jnp.sum(p, axis=-1, keepdims=True))
        outs.append(jnp.dot(p.astype(BF16), v_heads[hd], preferred_element_type=F32))
    return jnp.concatenate(outs, axis=1)


def _split_heads(x):
    return [x[:, hd * MEM_HEAD_DIM:(hd + 1) * MEM_HEAD_DIM] for hd in range(MEM_HEADS)]


def _pack_bf16_pairs(y):
    n = y.shape[1] // 2
    bits = pltpu.bitcast(y.astype(BF16).astype(F32), jnp.int32)
    return (bits[:, :n] & jnp.int32(-65536)) | lax.shift_right_logical(bits[:, n:], jnp.int32(16))


def _unpack_bf16_pairs(w):
    hi = pltpu.bitcast(w & jnp.int32(-65536), F32)
    lo = pltpu.bitcast(lax.shift_left(w, jnp.int32(16)), F32)
    return jnp.concatenate([hi, lo], axis=1)


def _cross_tail(o, x1, wmo_ref, fg_ref, wr_ref, br_ref, tri_ref, base):
    x2 = x1 + jnp.dot(o.astype(BF16), wmo_ref[...], preferred_element_type=F32)
    hn_f = _rms(x2, fg_ref[...])
    hn = hn_f.astype(BF16)
    logits = jnp.dot(hn, wr_ref[...], preferred_element_type=F32) + br_ref[...]
    lane = lax.broadcasted_iota(jnp.int32, logits.shape, 1)
    is_g = (lane >= N_EXPERTS) & (lane < N_EXPERTS + N_GROUPS)
    lg = jnp.where(is_g, logits, NEG_INF)
    gmax = jnp.max(lg, axis=-1, keepdims=True)
    gsel = jnp.min(jnp.where(lg == gmax, lane, 2 * LANES), axis=-1, keepdims=True) - N_EXPERTS
    pg_sel = 1.0 / jnp.sum(jnp.exp(lg - gmax), axis=-1, keepdims=True)
    in_grp = (lane >= gsel * EXPERTS_PER_GROUP) & (lane < (gsel + 1) * EXPERTS_PER_GROUP)
    le = jnp.where(in_grp, logits, NEG_INF)
    top1 = jnp.max(le, axis=-1, keepdims=True)
    idx1 = jnp.min(jnp.where(le == top1, lane, 2 * LANES), axis=-1, keepdims=True)
    le2 = jnp.where(lane == idx1, NEG_INF, le)
    top2 = jnp.max(le2, axis=-1, keepdims=True)
    idx2 = jnp.min(jnp.where(le2 == top2, lane, 2 * LANES), axis=-1, keepdims=True)
    e2 = jnp.exp(top2 - top1)
    inv = 1.0 / (1.0 + e2)
    gate1 = pg_sel * inv
    gate2 = pg_sel * (e2 * inv)
    used = jnp.where((lane == idx1) | (lane == idx2), 1.0, 0.0)
    blk = tri_ref.shape[0]
    used_b = used.astype(BF16)
    parts, run = [], base
    for r0 in range(0, used.shape[0], blk):
        parts.append(jnp.dot(tri_ref[...], used_b[r0:r0 + blk], preferred_element_type=F32) + run)
        run = run + jnp.sum(used[r0:r0 + blk], axis=0, keepdims=True)
    before = jnp.concatenate(parts, axis=0)
    rank1 = jnp.sum(jnp.where(lane == idx1, before, 0.0), axis=-1, keepdims=True)
    rank2 = jnp.sum(jnp.where(lane == idx2, before, 0.0), axis=-1, keepdims=True)
    route = jnp.zeros_like(logits)
    for pos, val in enumerate((idx1.astype(F32), idx2.astype(F32), gate1, gate2, rank1, rank2)):
        route = jnp.where(lane == pos, val, route)
    info = jnp.transpose(route)[0:ROUTE_ROWS, :]
    return x2, _pack_bf16_pairs(hn_f), route, info, run


def _mixer_prompt_kernel(sink_ref, x_ref, ng_ref, win_ref, bdq_ref, bdk_ref, gqlo_ref, gqhi_ref, gk_ref,
                         bias_ref, cw_ref, cb_ref, lg_ref, lb_ref, wout_ref,
                         x1_ref, ko_ref, vo_ref, co_ref,
                         kband, vband, uext, ushift, *, tm, nt):
    t = pl.program_id(1)

    @pl.when(t == 0)
    def _():
        kband[0:BLOCK, :] = jnp.zeros((BLOCK, KV_DIM), F32)
        vband[0:BLOCK, :] = jnp.zeros((BLOCK, KV_DIM), F32)
        uext[0:CONV_PAD, :] = jnp.zeros((CONV_PAD, D_CONV), F32)

    x = x_ref[...]
    q_lo, q_hi, kn, v, u = _mixer_proj(x, ng_ref, win_ref, bdq_ref, bdk_ref, gqlo_ref, gqhi_ref, gk_ref)
    kband[BLOCK:BLOCK + tm, :] = kn
    vband[BLOCK:BLOCK + tm, :] = v
    uext[CONV_PAD:CONV_PAD + tm, :] = u

    @pl.when(t == nt - 1)
    def _():
        ko_ref[...] = kn[tm - BLOCK:, :]
        vo_ref[...] = v[tm - BLOCK:, :]
        co_ref[...] = uext[pl.ds(CONV_PAD + tm - CONV_PAST, CONV_PAST), :]

    kd = [a.astype(BF16) for a in _dup_halves(kband[...])]
    vd = _dup_halves(vband[...])
    lo = lax.broadcasted_iota(jnp.int32, vd[0].shape, 1) < HALF
    v_lo = [jnp.where(lo, a, 0.0).astype(BF16) for a in vd]
    v_hi = [jnp.where(lo, 0.0, a).astype(BF16) for a in vd]
    q_lo = q_lo.astype(BF16)
    q_hi = q_hi.astype(BF16)
    attn_blocks = []
    for j in range(tm // BLOCK):
        rows = slice(j * BLOCK, (j + 1) * BLOCK)
        keys = slice(j * BLOCK, j * BLOCK + 2 * BLOCK)
        bias_base = jnp.where(t == 0, N_HEADS, 0) if j == 0 else 0
        tiles = []
        for c in range(N_KV_HEADS):
            q4 = jnp.concatenate(
                [(q_lo if (HEADS_PER_KV * c + a) % 2 == 0 else q_hi)[rows,
                  ((HEADS_PER_KV * c + a) // 2) * LANES:((HEADS_PER_KV * c + a) // 2 + 1) * LANES]
                 for a in range(HEADS_PER_KV)], axis=0)
            s_all = lax.dot_general(q4, kd[c][keys], (((1,), (1,)), ((), ())), preferred_element_type=F32)
            ps = []
            for a in range(HEADS_PER_KV):
                hd = HEADS_PER_KV * c + a
                s = s_all[a * BLOCK:(a + 1) * BLOCK] + bias_ref[bias_base + hd]
                ps.append(_sink_softmax(s, sink_ref[hd]).astype(BF16))
            vstack = jnp.concatenate([v_lo[c][keys], v_hi[c][keys]], axis=0)
            for i2 in range(HEADS_PER_KV // 2):
                pp = jnp.concatenate([ps[2 * i2], ps[2 * i2 + 1]], axis=1)
                tiles.append(jnp.dot(pp, vstack, preferred_element_type=F32))
        attn_blocks.append(jnp.concatenate(tiles, axis=1))
    attn = jnp.concatenate(attn_blocks, axis=0)

    off = CONV_PAD - CONV_PAST
    span = tm + CONV_PAD - SUBLANES
    for s in range(1, SUBLANES):
        ushift[s - 1] = uext[pl.ds(s, span), :]
    chunks = []
    for lc in range(D_CONV // LANES):
        ls = slice(lc * LANES, (lc + 1) * LANES)
        accs = [jnp.zeros((CONV_ROWS, LANES), F32) for _ in range(tm // CONV_ROWS)]
        for j in range(CONV_WIDTH):
            s, a = (off + j) % SUBLANES, (off + j) // SUBLANES
            wj = cw_ref[j:j + 1, ls]
            for rc in range(tm // CONV_ROWS):
                r0 = rc * CONV_ROWS + a * SUBLANES
                tap = uext[r0:r0 + CONV_ROWS, ls] if s == 0 else ushift[s - 1, r0:r0 + CONV_ROWS, ls]
                accs[rc] = accs[rc] + wj * tap
        chunks.append(jnp.concatenate(accs, axis=0))
    y = _conv_ln_silu(jnp.concatenate(chunks, axis=1), cb_ref, lg_ref, lb_ref)

    x1_ref[...] = (x + jnp.dot(attn.astype(BF16), wout_ref[0:D_ATTN, :], preferred_element_type=F32)
                   + jnp.dot(y.astype(BF16), wout_ref[D_ATTN:, :], preferred_element_type=F32))

    kband[0:BLOCK, :] = kband[tm:tm + BLOCK, :]
    vband[0:BLOCK, :] = vband[tm:tm + BLOCK, :]
    uext[0:CONV_PAD, :] = uext[tm:tm + CONV_PAD, :]


def _const_spec(shape):
    nd = len(shape)
    return pl.BlockSpec(shape, lambda *_: (0,) * nd)


def _mixer_prompt(x, sinks, consts, tm):
    b, s, d = x.shape
    nt = s // tm
    (ng, win, bdq, bdk, gqlo, gqhi, gk, bias_p, cw, cb, lg, lb, wout) = consts
    kern = functools.partial(_mixer_prompt_kernel, tm=tm, nt=nt)
    cspecs = [_const_spec(a.shape) for a in consts]
    return pl.pallas_call(
        kern,
        grid=(b, nt),
        in_specs=[pl.BlockSpec(memory_space=pltpu.SMEM),
                  pl.BlockSpec((None, tm, d), lambda i, j: (i, j, 0))] + cspecs,
        out_specs=[pl.BlockSpec((None, tm, d), lambda i, j: (i, j, 0)),
                   pl.BlockSpec((None, BLOCK, KV_DIM), lambda i, j: (i, 0, 0)),
                   pl.BlockSpec((None, BLOCK, KV_DIM), lambda i, j: (i, 0, 0)),
                   pl.BlockSpec((None, CONV_PAST, D_CONV), lambda i, j: (i, 0, 0))],
        out_shape=[jax.ShapeDtypeStruct((b, s, d), F32),
                   jax.ShapeDtypeStruct((b, BLOCK, KV_DIM), F32),
                   jax.ShapeDtypeStruct((b, BLOCK, KV_DIM), F32),
                   jax.ShapeDtypeStruct((b, CONV_PAST, D_CONV), F32)],
        scratch_shapes=[pltpu.VMEM((BLOCK + tm, KV_DIM), F32),
                        pltpu.VMEM((BLOCK + tm, KV_DIM), F32),
                        pltpu.VMEM((CONV_PAD + tm, D_CONV), F32),
                        pltpu.VMEM((SUBLANES - 1, tm + CONV_PAD - SUBLANES, D_CONV), F32)],
        compiler_params=pltpu.CompilerParams(dimension_semantics=("arbitrary", "arbitrary"),
                                             vmem_limit_bytes=VMEM_LIMIT),
        name="mixer_prompt",
    )(sinks, x, *consts)


def _mixer_sample_kernel(x_ref, ck_ref, cv_ref, st_ref, ng_ref, win_ref, bdq_ref, bdk_ref, gqlo_ref, gqhi_ref,
                         gk_ref, bias_ref, sinkrow_ref, cw_ref, cb_ref, lg_ref, lb_ref, wout_ref,
                         xag_ref, wmq_ref, gmq_ref,
                         x1_ref, qm_ref, ko_ref, vo_ref, co_ref,
                         kall, vall, qs, s_scr, r_scr, u_scr, y_scr, *, g, ts):
    wb = ck_ref.shape[2]
    x = x_ref[...]
    q_lo, q_hi, kn, v, u = _mixer_proj(x, ng_ref, win_ref, bdq_ref, bdk_ref, gqlo_ref, gqhi_ref, gk_ref)
    kall[:, wb:wb + ts, :] = kn.reshape(g, ts, KV_DIM)
    vall[:, wb:wb + ts, :] = v.reshape(g, ts, KV_DIM)
    for n in range(g):
        kall[n, 0:wb, :] = ck_ref[n].T
        vall[n, 0:wb, :] = cv_ref[n].T
        ko_ref[n] = kall[n, ts:wb + ts, :].T
        vo_ref[n] = vall[n, ts:wb + ts, :].T

    for lc in range(D_CONV // LANES):
        ls = slice(lc * LANES, (lc + 1) * LANES)
        u_scr[lc] = u[:, ls]
        u_t = [u_scr[lc, pl.ds(t, g, stride=ts), :] for t in range(ts)]

        def frame(r):
            return st_ref[r, :, ls] if r < CONV_PAST else u_t[r - CONV_PAST]

        for t in range(ts):
            acc = jnp.zeros((g, LANES), F32)
            for j in range(CONV_WIDTH):
                acc = acc + cw_ref[j:j + 1, ls] * frame(t + j)
            y_scr[lc, pl.ds(t, g, stride=ts), :] = acc
        for r in range(CONV_PAST):
            co_ref[r, :, ls] = frame(r + ts)
    y = _conv_ln_silu(jnp.concatenate([y_scr[lc] for lc in range(D_CONV // LANES)], axis=1), cb_ref, lg_ref, lb_ref)

    for hd in range(N_HEADS):
        tile = (q_lo if hd % 2 == 0 else q_hi)[:, (hd // 2) * LANES:(hd // 2 + 1) * LANES]
        if hd % 2 != hd // HEADS_PER_KV:
            tile = pltpu.roll(tile, HALF, axis=1)
        qs[:, hd * ts:(hd + 1) * ts, :] = tile.reshape(g, ts, LANES)
    for n in range(g):
        s_scr[n] = lax.dot_general(qs[n].astype(BF16), kall[n].astype(BF16), (((1,), (1,)), ((), ())),
                                   preferred_element_type=F32)
    s_scr[...] = _sink_softmax(s_scr[...] + bias_ref[...], sinkrow_ref[...])
    for n in range(g):
        r_scr[n] = jnp.dot(s_scr[n].astype(BF16), vall[n].astype(BF16), preferred_element_type=F32)
    r = r_scr[...].reshape(g * N_HEADS * ts, LANES)
    r_sw = pltpu.roll(r, HALF, axis=1).reshape(g, N_HEADS * ts, LANES)
    r = r.reshape(g, N_HEADS * ts, LANES)
    lo = lax.broadcasted_iota(jnp.int32, (g * ts, LANES), 1) < HALF
    heads = [(r if hd // HEADS_PER_KV == hd % 2 else r_sw)[:, hd * ts:(hd + 1) * ts, :].reshape(g * ts, LANES)
             for hd in range(N_HEADS)]
    attn = jnp.concatenate([jnp.where(lo, heads[2 * i], heads[2 * i + 1]) for i in range(N_HEADS // 2)], axis=1)

    x1 = (x + jnp.dot(attn.astype(BF16), wout_ref[0:D_ATTN, :], preferred_element_type=F32)
          + jnp.dot(y.astype(BF16), wout_ref[D_ATTN:, :], preferred_element_type=F32))
    x1_ref[...] = x1
    qm_ref[...] = _mem_query(x1, xag_ref, wmq_ref, gmq_ref)


def _mixer_sample(xs2d, ck, cv, st, consts, g, ts):
    n, d = xs2d.shape
    nb, wb = ck.shape[0], ck.shape[2]
    rows = g * ts
    cspecs = [_const_spec(a.shape) for a in consts]
    kern = functools.partial(_mixer_sample_kernel, g=g, ts=ts)
    return pl.pallas_call(
        kern,
        grid=(nb // g,),
        in_specs=[pl.BlockSpec((rows, d), lambda i: (i, 0)),
                  pl.BlockSpec((g, KV_DIM, wb), lambda i: (i, 0, 0)),
                  pl.BlockSpec((g, KV_DIM, wb), lambda i: (i, 0, 0)),
                  pl.BlockSpec((CONV_PAST, g, D_CONV), lambda i: (0, i, 0))] + cspecs,
        out_specs=[pl.BlockSpec((rows, d), lambda i: (i, 0)),
                   pl.BlockSpec((rows, d), lambda i: (i, 0)),
                   pl.BlockSpec((g, KV_DIM, wb), lambda i: (i, 0, 0)),
                   pl.BlockSpec((g, KV_DIM, wb), lambda i: (i, 0, 0)),
                   pl.BlockSpec((CONV_PAST, g, D_CONV), lambda i: (0, i, 0))],
        out_shape=[jax.ShapeDtypeStruct((n, d), F32),
                   jax.ShapeDtypeStruct((n, d), F32),
                   jax.ShapeDtypeStruct((nb, KV_DIM, wb), F32),
                   jax.ShapeDtypeStruct((nb, KV_DIM, wb), F32),
                   jax.ShapeDtypeStruct((CONV_PAST, nb, D_CONV), F32)],
        scratch_shapes=[pltpu.VMEM((g, wb + ts, KV_DIM), F32),
                        pltpu.VMEM((g, wb + ts, KV_DIM), F32),
                        pltpu.VMEM((g, N_HEADS * ts, LANES), F32),
                        pltpu.VMEM((g, N_HEADS * ts, wb + ts), F32),
                        pltpu.VMEM((g, N_HEADS * ts, LANES), F32),
                        pltpu.VMEM((D_CONV // LANES, rows, LANES), F32),
                        pltpu.VMEM((D_CONV // LANES, rows, LANES), F32)],
        compiler_params=pltpu.CompilerParams(dimension_semantics=("arbitrary",),
                                             vmem_limit_bytes=VMEM_LIMIT),
        name="mixer_sample",
    )(xs2d, ck, cv, st, *consts)


def _memory_kv_kernel(mem_ref, g_ref, wmk_ref, wmv_ref, gk_ref, k_ref, v_ref):
    h = _rms(mem_ref[...], g_ref[...]).astype(BF16)
    k = jnp.dot(h, wmk_ref[...].astype(BF16), preferred_element_type=F32)
    parts = []
    for hd in range(MEM_HEADS):
        kh = k[:, hd * MEM_HEAD_DIM:(hd + 1) * MEM_HEAD_DIM]
        parts.append(kh * lax.rsqrt(jnp.mean(kh * kh, axis=-1, keepdims=True) + EPS))
    k_ref[...] = jnp.concatenate(parts, axis=1) * gk_ref[...]
    v_ref[...] = jnp.dot(h, wmv_ref[...].astype(BF16), preferred_element_type=F32)


def _memory_kv(mem2d, g, wmk, wmv, gk, tm):
    n, d = mem2d.shape
    consts = (g, wmk, wmv, gk)
    return pl.pallas_call(
        _memory_kv_kernel,
        grid=(n // tm,),
        in_specs=[pl.BlockSpec((tm, d), lambda i: (i, 0))] + [_const_spec(a.shape) for a in consts],
        out_specs=[pl.BlockSpec((tm, d), lambda i: (i, 0)), pl.BlockSpec((tm, d), lambda i: (i, 0))],
        out_shape=[jax.ShapeDtypeStruct((n, d), F32), jax.ShapeDtypeStruct((n, d), F32)],
        compiler_params=pltpu.CompilerParams(dimension_semantics=("arbitrary",),
                                             vmem_limit_bytes=VMEM_LIMIT),
        name="memory_kv",
    )(mem2d, *consts)


def _cross_prompt_kernel(x1_ref, mk_ref, mv_ref, xag_ref, wmq_ref, gmq_ref, wmo_ref, fg_ref, wr_ref, br_ref, tri_ref,
                         x2_ref, hn_ref, route_ref, info_ref, cnt_ref, base):
    @pl.when((pl.program_id(0) == 0) & (pl.program_id(1) == 0))
    def _():
        base[...] = jnp.zeros_like(base)

    x1 = x1_ref[...]
    q = _mem_query(x1, xag_ref, wmq_ref, gmq_ref)
    o = _mem_attend(q, _split_heads(mk_ref[...].astype(BF16)), _split_heads(mv_ref[...].astype(BF16)))
    x2, hn, route, info, new_base = _cross_tail(o, x1, wmo_ref, fg_ref, wr_ref, br_ref, tri_ref, base[...])
    x2_ref[...] = x2
    hn_ref[...] = hn
    route_ref[...] = route
    info_ref[...] = info
    base[...] = new_base
    cnt_ref[...] = new_base


def _cross_prompt(x1, mk, mv, consts, tm):
    b, s, d = x1.shape
    m = mk.shape[1]
    cspecs = [_const_spec(a.shape) for a in consts]
    return pl.pallas_call(
        _cross_prompt_kernel,
        grid=(b, s // tm),
        in_specs=[pl.BlockSpec((None, tm, d), lambda i, j: (i, j, 0)),
                  pl.BlockSpec((None, m, d), lambda i, j: (i, 0, 0)),
                  pl.BlockSpec((None, m, d), lambda i, j: (i, 0, 0))] + cspecs,
        out_specs=[pl.BlockSpec((None, tm, d), lambda i, j: (i, j, 0)),
                   pl.BlockSpec((None, tm, d // 2), lambda i, j: (i, j, 0)),
                   pl.BlockSpec((None, tm, LANES), lambda i, j: (i, j, 0)),
                   pl.BlockSpec((ROUTE_ROWS, tm), lambda i, j: (0, i * (s // tm) + j)),
                   pl.BlockSpec((1, LANES), lambda i, j: (0, 0))],
        out_shape=[jax.ShapeDtypeStruct((b, s, d), F32),
                   jax.ShapeDtypeStruct((b, s, d // 2), jnp.int32),
                   jax.ShapeDtypeStruct((b, s, LANES), F32),
                   jax.ShapeDtypeStruct((ROUTE_ROWS, b * s), F32),
                   jax.ShapeDtypeStruct((1, LANES), F32)],
        scratch_shapes=[pltpu.VMEM((1, LANES), F32)],
        compiler_params=pltpu.CompilerParams(dimension_semantics=("arbitrary", "arbitrary"),
                                             vmem_limit_bytes=VMEM_LIMIT),
        name="cross_prompt",
    )(x1, mk, mv, *consts)


def _mem_attend_sample_kernel(qm_ref, *refs, g, ts):
    k_refs, v_refs, o_ref, s_scr = refs[:g], refs[g:2 * g], refs[2 * g], refs[2 * g + 1]
    mlen = k_refs[0].shape[0] // MEM_HEADS

    def all_heads(ref):
        pieces = []
        for hd in range(MEM_HEADS):
            w = ref[pl.ds(hd, mlen, stride=MEM_HEADS), :]
            pieces.append(pltpu.bitcast(lax.shift_left(w, jnp.int32(16)), F32))
            pieces.append(pltpu.bitcast(w & jnp.int32(-65536), F32))
        return jnp.concatenate(pieces, axis=1).astype(BF16)

    rows = MEM_HEADS * ts
    own = (lax.broadcasted_iota(jnp.int32, (rows, MEM_HEADS * MEM_HEAD_DIM), 1) // MEM_HEAD_DIM
           == lax.broadcasted_iota(jnp.int32, (rows, MEM_HEADS * MEM_HEAD_DIM), 0) // ts)
    for n in range(g):
        q = qm_ref[n * ts:(n + 1) * ts, :]
        q_bd = jnp.where(own, jnp.concatenate([q] * MEM_HEADS, axis=0), 0.0).astype(BF16)
        s_scr[n] = lax.dot_general(q_bd, all_heads(k_refs[n]), (((1,), (1,)), ((), ())),
                                   preferred_element_type=F32)
    s = s_scr[...]
    p = jnp.exp(s - jnp.max(s, axis=-1, keepdims=True))
    s_scr[...] = p * (1.0 / jnp.sum(p, axis=-1, keepdims=True))
    for n in range(g):
        r = jnp.dot(s_scr[n].astype(BF16), all_heads(v_refs[n]), preferred_element_type=F32)
        o_ref[n * ts:(n + 1) * ts, :] = jnp.concatenate(
            [r[hd * ts:(hd + 1) * ts, hd * MEM_HEAD_DIM:(hd + 1) * MEM_HEAD_DIM] for hd in range(MEM_HEADS)], axis=1)


def _mem_cache_rows(cache):
    nb, mlen = cache.shape[:2]
    halves = MEM_HEAD_DIM // LANES
    return (cache.reshape(nb, mlen, MEM_HEADS, halves, LANES).transpose(0, 1, 3, 2, 4)
            .reshape(nb, mlen * halves * MEM_HEADS, LANES))


def _sc_pack_cache(rows):
    nb, nrows, _ = rows.shape
    total = nb * nrows
    per_worker = total // SC_WORKERS
    chunk = SC_PACK_POS * SUBLANES
    assert MEM_HEAD_DIM == 2 * LANES and 2 * MEM_HEADS == SUBLANES and per_worker % chunk == 0
    tiles_in = rows.reshape(total // SUBLANES, SUBLANES, LANES)
    mesh = plsc.VectorSubcoreMesh(core_axis_name="c", subcore_axis_name="s")

    @functools.partial(
        pl.kernel, mesh=mesh,
        out_type=jax.ShapeDtypeStruct((total // (2 * SUBLANES), SUBLANES, LANES), jnp.int32),
        scratch_types=[pltpu.VMEM((SC_PACK_POS, SUBLANES, LANES), F32),
                       pltpu.VMEM((SC_PACK_POS // 2, SUBLANES, LANES), jnp.int32)],
        compiler_params=pltpu.CompilerParams(needs_layout_passes=False))
    def pack(in_hbm, out_hbm, in_v, out_v):
        wid = lax.axis_index("s") * SC_CORES + lax.axis_index("c")
        base = wid * (per_worker // SUBLANES)

        @pl.loop(0, per_worker // chunk)
        def _(ci):
            off = base + ci * SC_PACK_POS
            pltpu.sync_copy(in_hbm.at[pl.ds(off, SC_PACK_POS)], in_v)

            @pl.loop(0, SC_PACK_POS // 2)
            def _(q):
                for pp in range(2):
                    for hd in range(MEM_HEADS):
                        for l0 in range(0, LANES, SC_LANES):
                            lo = in_v[2 * q + pp, hd, pl.ds(l0, SC_LANES)]
                            hi = in_v[2 * q + pp, MEM_HEADS + hd, pl.ds(l0, SC_LANES)]
                            word = plsc.bitcast(plsc.pack(lo, hi, format=plsc.PackFormat.INTERLEAVED), jnp.int32)
                            out_v[q, pp * MEM_HEADS + hd, pl.ds(l0, SC_LANES)] = word

            pltpu.sync_copy(out_v, out_hbm.at[pl.ds(off // 2, SC_PACK_POS // 2)])

    return pack(tiles_in).reshape(nb, nrows // 2, LANES)


def _mem_attend_sample(qm, mk, mv, g, ts):
    n, d = qm.shape
    nb, rows = mk.shape[0], mk.shape[1]
    kern = functools.partial(_mem_attend_sample_kernel, g=g, ts=ts)

    def seq_spec(j):
        return pl.BlockSpec((None, rows, LANES), lambda i: (g * i + j, 0, 0))

    return pl.pallas_call(
        kern,
        grid=(nb // g,),
        in_specs=[pl.BlockSpec((g * ts, d), lambda i: (i, 0))] + [seq_spec(j) for j in range(g)] * 2,
        out_specs=pl.BlockSpec((g * ts, d), lambda i: (i, 0)),
        out_shape=jax.ShapeDtypeStruct((n, d), F32),
        scratch_shapes=[pltpu.VMEM((g, MEM_HEADS * ts, rows // MEM_HEADS), F32)],
        compiler_params=pltpu.CompilerParams(dimension_semantics=("arbitrary",),
                                             vmem_limit_bytes=VMEM_LIMIT),
        name="mem_attend_sample",
    )(qm, *([mk] * g), *([mv] * g))


def _cross_tail_kernel(o_ref, x1_ref, wmo_ref, fg_ref, wr_ref, br_ref, tri_ref,
                       x2_ref, hn_ref, route_ref, info_ref, cnt_ref, base):
    @pl.when(pl.program_id(0) == 0)
    def _():
        base[...] = jnp.zeros_like(base)

    x2, hn, route, info, new_base = _cross_tail(o_ref[...], x1_ref[...], wmo_ref, fg_ref, wr_ref, br_ref, tri_ref,
                                                base[...])
    x2_ref[...] = x2
    hn_ref[...] = hn
    route_ref[...] = route
    info_ref[...] = info
    base[...] = new_base
    cnt_ref[...] = new_base


def _cross_tail_call(o, x1, consts, tm):
    n, d = x1.shape
    cspecs = [_const_spec(a.shape) for a in consts]
    return pl.pallas_call(
        _cross_tail_kernel,
        grid=(n // tm,),
        in_specs=[pl.BlockSpec((tm, d), lambda i: (i, 0)), pl.BlockSpec((tm, d), lambda i: (i, 0))] + cspecs,
        out_specs=[pl.BlockSpec((tm, d), lambda i: (i, 0)),
                   pl.BlockSpec((tm, d // 2), lambda i: (i, 0)),
                   pl.BlockSpec((tm, LANES), lambda i: (i, 0)),
                   pl.BlockSpec((ROUTE_ROWS, tm), lambda i: (0, i)),
                   pl.BlockSpec((1, LANES), lambda i: (0, 0))],
        out_shape=[jax.ShapeDtypeStruct((n, d), F32),
                   jax.ShapeDtypeStruct((n, d // 2), jnp.int32),
                   jax.ShapeDtypeStruct((n, LANES), F32),
                   jax.ShapeDtypeStruct((ROUTE_ROWS, n), F32),
                   jax.ShapeDtypeStruct((1, LANES), F32)],
        scratch_shapes=[pltpu.VMEM((1, LANES), F32)],
        compiler_params=pltpu.CompilerParams(dimension_semantics=("arbitrary",),
                                             vmem_limit_bytes=VMEM_LIMIT),
        name="cross_tail",
    )(o, x1, *consts)


def _sc_rows_per_step(per_worker):
    step = min(SC_ROWS_PER_STEP, per_worker)
    assert per_worker % step == 0 and step % 8 == 0
    return step


def _sc_gather_rows(table, idx):
    nrows = idx.shape[0]
    _, width = table.shape
    assert nrows % (8 * SC_WORKERS) == 0
    per_worker = nrows // SC_WORKERS
    step = _sc_rows_per_step(per_worker // 2)
    mesh = plsc.VectorSubcoreMesh(core_axis_name="c", subcore_axis_name="s")

    @functools.partial(
        pl.kernel, mesh=mesh, out_type=jax.ShapeDtypeStruct((nrows, width), table.dtype),
        scratch_types=[pltpu.VMEM((step,), jnp.int32)] * 2 + [pltpu.VMEM((step, width), table.dtype)] * 2
        + [pltpu.SemaphoreType.DMA] * 2)
    def gather(table_hbm, idx_hbm, out_hbm, idx_a, idx_b, rows_a, rows_b, sem_a, sem_b):
        wid = lax.axis_index("s") * SC_CORES + lax.axis_index("c")
        base = wid * per_worker

        @pl.loop(0, per_worker // (2 * step))
        def _(i):
            off = base + i * (2 * step)
            pltpu.sync_copy(idx_hbm.at[pl.ds(off, step)], idx_a)
            pltpu.sync_copy(idx_hbm.at[pl.ds(off + step, step)], idx_b)
            gather_a = pltpu.async_copy(table_hbm.at[idx_a], rows_a, sem_a)
            gather_b = pltpu.async_copy(table_hbm.at[idx_b], rows_b, sem_b)
            gather_a.wait()
            write_a = pltpu.async_copy(rows_a, out_hbm.at[pl.ds(off, step)], sem_a)
            gather_b.wait()
            write_b = pltpu.async_copy(rows_b, out_hbm.at[pl.ds(off + step, step)], sem_b)
            write_a.wait()
            write_b.wait()

    return gather(table, idx)


def _sc_scatter_rows(table, pos, nrows_out):
    n, width = table.shape
    assert n % (8 * SC_WORKERS) == 0 and pos.shape == (TOP_K * n,)
    per_worker = n // SC_WORKERS
    step = _sc_rows_per_step(per_worker)
    mesh = plsc.VectorSubcoreMesh(core_axis_name="c", subcore_axis_name="s")

    @functools.partial(
        pl.kernel, mesh=mesh, out_type=jax.ShapeDtypeStruct((nrows_out, width), table.dtype),
        scratch_types=[pltpu.VMEM((step,), jnp.int32)] * TOP_K
        + [pltpu.VMEM((step, width), table.dtype), pltpu.SemaphoreType.DMA])
    def scatter(table_hbm, pos_hbm, out_hbm, *scratch):
        idx_vs, rows_v, sem = scratch[:TOP_K], scratch[TOP_K], scratch[TOP_K + 1]
        wid = lax.axis_index("s") * SC_CORES + lax.axis_index("c")
        base = wid * per_worker

        @pl.loop(0, per_worker // step)
        def _(i):
            off = base + i * step
            for k in range(TOP_K):
                pltpu.sync_copy(pos_hbm.at[pl.ds(k * n + off, step)], idx_vs[k])
            pltpu.sync_copy(table_hbm.at[pl.ds(off, step)], rows_v)
            for k in range(TOP_K):
                pltpu.async_copy(rows_v, out_hbm.at[idx_vs[k]], sem).wait()

    return scatter(table, pos)


W_CHUNKS = 4
FETCH_AHEAD = 3
FETCH_AHEAD_F32 = 2


def _expert_weight_copies(e, slot, w_hbm, w_slots, sems, chunks):
    copies = []
    for m, (src, dst) in enumerate(zip(w_hbm, w_slots)):
        rows = dst.shape[1] // chunks
        for c in range(chunks):
            sl = pl.ds(c * rows, rows)
            sem = sems.at[(slot * len(w_hbm) + m) * chunks + c]
            copies.append(pltpu.make_async_copy(src.at[e, sl, :], dst.at[slot, sl, :], sem))
    return copies


def _moe_grouped_kernel(te_ref, nv_ref, rv_ref, slot_ref, *refs, publish, ahead, chunks):
    later_refs, (xs_ref, wg_hbm, wu_hbm, wd_hbm, ys_ref), rest = refs[:ahead], refs[ahead:ahead + 5], refs[ahead + 5:]
    i = pl.program_id(0)
    n_slots = ahead + 1
    w_hbm = (wg_hbm, wu_hbm, wd_hbm)
    if publish:
        out_hbm, w_slots, w_bf16, sems, out_sems = rest[0:3], rest[3:6], rest[6:9], rest[9], rest[10]
    else:
        w_slots, sems = rest[0:3], rest[3]

    def publish_copies(e):
        return [pltpu.make_async_copy(src, dst.at[e], out_sems.at[m])
                for m, (src, dst) in enumerate(zip(w_bf16, out_hbm))]

    def fetch(e, slot):
        return _expert_weight_copies(e, slot, w_hbm, w_slots, sems, chunks)

    @pl.when(i == 0)
    def _():
        for cp in fetch(te_ref[0], slot_ref[0]):
            cp.start()
        for k in range(ahead - 1):
            @pl.when(later_refs[k][0] >= 0)
            def _():
                for cp in fetch(later_refs[k][0], (slot_ref[0] + k + 1) % n_slots):
                    cp.start()

    @pl.when(i < nv_ref[0])
    def _():
        slot = slot_ref[i]

        @pl.when((i == 0) | (te_ref[i] != te_ref[jnp.maximum(i - 1, 0)]))
        def _():
            for cp in fetch(te_ref[i], slot):
                cp.wait()
            if publish:
                @pl.when(i > 0)
                def _():
                    for cp in publish_copies(te_ref[jnp.maximum(i - 1, 0)]):
                        cp.wait()

                for src, dst in zip(w_slots, w_bf16):
                    dst[...] = src[slot].astype(BF16)
                for cp in publish_copies(te_ref[i]):
                    cp.start()

            @pl.when(later_refs[ahead - 1][i] >= 0)
            def _():
                for cp in fetch(later_refs[ahead - 1][i], (slot + ahead) % n_slots):
                    cp.start()

        wg, wu, wd = [r[...] for r in w_bf16] if publish else [r[slot] for r in w_slots]
        xs = xs_ref[...]
        row = lax.broadcasted_iota(jnp.int32, xs.shape, 0)
        x = _unpack_bf16_pairs(jnp.where(row < rv_ref[i], xs, 0)).astype(BF16)
        a = jnp.dot(x, wg, preferred_element_type=F32)
        b = jnp.dot(x, wu, preferred_element_type=F32)
        act = (a * jax.nn.sigmoid(a)) * b
        y = jnp.dot(act.astype(BF16), wd, preferred_element_type=F32)
        ys_ref[...] = _pack_bf16_pairs(y)

    if publish:
        @pl.when(i == pl.num_programs(0) - 1)
        def _():
            for cp in publish_copies(te_ref[nv_ref[0] - 1]):
                cp.wait()


def _moe_grouped(tile_expert, n_valid, rows_valid, run_slot, later_experts, xs, wg, wu, wd, tm, publish):
    p, half = xs.shape
    ne, d, f = wg.shape
    ahead = len(later_experts)
    chunks = W_CHUNKS if publish else 1

    def live_tile(i, te, nv, *_):
        return (jnp.minimum(i, nv[0] - 1), 0)

    slot_dtype = F32 if publish else BF16
    n_slots = ahead + 1
    scratch = [pltpu.VMEM((n_slots, d, f), slot_dtype), pltpu.VMEM((n_slots, d, f), slot_dtype),
               pltpu.VMEM((n_slots, f, d), slot_dtype)]
    out_specs = [pl.BlockSpec((tm, half), live_tile)]
    out_shape = [jax.ShapeDtypeStruct((p, half), jnp.int32)]
    if publish:
        scratch += [pltpu.VMEM((d, f), BF16), pltpu.VMEM((d, f), BF16), pltpu.VMEM((f, d), BF16)]
        out_specs += [pl.BlockSpec(memory_space=pl.ANY)] * 3
        out_shape += [jax.ShapeDtypeStruct(w.shape, BF16) for w in (wg, wu, wd)]
    scratch += [pltpu.SemaphoreType.DMA((n_slots * 3 * chunks,))]
    if publish:
        scratch += [pltpu.SemaphoreType.DMA((3,))]
    grid_spec = pltpu.PrefetchScalarGridSpec(
        num_scalar_prefetch=4 + ahead,
        grid=(p // tm,),
        in_specs=[pl.BlockSpec((tm, half), live_tile)] + [pl.BlockSpec(memory_space=pl.ANY)] * 3,
        out_specs=out_specs,
        scratch_shapes=scratch)
    res = pl.pallas_call(
        functools.partial(_moe_grouped_kernel, publish=publish, ahead=ahead, chunks=chunks),
        grid_spec=grid_spec,
        out_shape=out_shape,
        compiler_params=pltpu.CompilerParams(dimension_semantics=("arbitrary",),
                                             vmem_limit_bytes=VMEM_LIMIT),
        name="moe_grouped",
    )(tile_expert, n_valid, rows_valid, run_slot, *later_experts, xs, wg, wu, wd)
    return res[0], tuple(res[1:])


def _moe_combine_kernel(x2_ref, route_ref, y0_ref, y1_ref, out_ref):
    route = route_ref[...]
    out_ref[...] = (x2_ref[...] + route[:, 2:3] * _unpack_bf16_pairs(y0_ref[...])
                    + route[:, 3:4] * _unpack_bf16_pairs(y1_ref[...]))


def _moe_combine(x2, route, gathered, tm):
    n, d = x2.shape
    nt = n // tm
    return pl.pallas_call(
        _moe_combine_kernel,
        grid=(nt,),
        in_specs=[pl.BlockSpec((tm, d), lambda i: (i, 0)),
                  pl.BlockSpec((tm, LANES), lambda i: (i, 0)),
                  pl.BlockSpec((tm, d // 2), lambda i: (i, 0)),
                  pl.BlockSpec((tm, d // 2), lambda i: (i + nt, 0))],
        out_specs=pl.BlockSpec((tm, d), lambda i: (i, 0)),
        out_shape=jax.ShapeDtypeStruct((n, d), F32),
        compiler_params=pltpu.CompilerParams(dimension_semantics=("arbitrary",),
                                             vmem_limit_bytes=VMEM_LIMIT),
        name="moe_combine",
    )(x2, route, gathered, gathered)


def _moe(x2, hn_packed, route, info, counts, weights, tm_rows, publish):
    n = x2.shape[0]
    p = (TOP_K * n // tm_rows + N_EXPERTS) * tm_rows
    n_tiles = p // tm_rows
    cnt = counts[0, :N_EXPERTS].astype(jnp.int32)
    tiles_per_e = jnp.maximum((cnt + tm_rows - 1) // tm_rows, 1 if publish else 0)
    e_ids = jnp.arange(N_EXPERTS, dtype=jnp.int32)
    tile_end = jnp.sum(jnp.where(e_ids[None, :] <= e_ids[:, None], tiles_per_e[None, :], 0), axis=1)
    tile_start = tile_end - tiles_per_e
    n_valid = tile_end[-1:]
    tile_ids = jnp.arange(n_tiles, dtype=jnp.int32)
    tile_expert = jnp.minimum(jnp.sum((tile_end[None, :] <= tile_ids[:, None]).astype(jnp.int32), axis=1),
                              N_EXPERTS - 1)
    mine = tile_expert[:, None] == e_ids[None, :]
    rows_left = jnp.sum(jnp.where(mine, cnt - (tile_ids[:, None] - tile_start) * tm_rows, 0), axis=1)
    rows_valid = jnp.clip(rows_left, 0, tm_rows).astype(jnp.int32)
    ahead = FETCH_AHEAD_F32 if publish else FETCH_AHEAD
    has_tiles = tiles_per_e > 0
    run_of_e = jnp.sum((has_tiles[None, :] & (e_ids[None, :] < e_ids[:, None])).astype(jnp.int32), axis=1)
    run_of_tile = jnp.sum(jnp.where(mine, run_of_e, 0), axis=1)
    run_slot = (run_of_tile % (ahead + 1)).astype(jnp.int32)
    later_experts = []
    for k in range(1, ahead + 1):
        is_run = has_tiles[None, :] & (run_of_e[None, :] == run_of_tile[:, None] + k)
        later_experts.append(jnp.sum(jnp.where(is_run, e_ids[None, :] + 1, 0), axis=1).astype(jnp.int32) - 1)
    eidx = info[0:TOP_K].astype(jnp.int32)
    row_start = jnp.sum(jnp.where(eidx[None] == e_ids[:, None, None], (tile_start * tm_rows)[:, None, None], 0),
                        axis=0)
    pos = (row_start + info[4:4 + TOP_K].astype(jnp.int32)).reshape(-1)

    xs = _sc_scatter_rows(hn_packed, pos, p)
    ys, w_bf16 = _moe_grouped(tile_expert, n_valid, rows_valid, run_slot, later_experts, xs, *weights, tm_rows,
                              publish)
    back = _sc_gather_rows(ys, pos)
    return _moe_combine(x2, route, back, min(TM_COMBINE, n)), w_bf16


def _block_diag(n, width):
    idx = np.arange(n) // width
    return jnp.asarray((idx[:, None] == idx[None, :]).astype(np.float32), dtype=BF16)


def _kv_feature_major(cache):
    nb, wb = cache.shape[:2]
    return jnp.transpose(cache, (0, 2, 3, 1)).reshape(nb, KV_DIM, wb)


def _strict_lower(n):
    r = np.arange(n)
    return jnp.asarray((r[:, None] > r[None, :]).astype(np.float32), dtype=BF16)


def _alibi_slopes():
    return np.exp2(-8.0 * np.arange(1, N_HEADS + 1, dtype=np.float32) / N_HEADS).astype(np.float32)


def _prompt_bias():
    i = np.arange(BLOCK)[:, None]
    s = np.arange(2 * BLOCK)[None, :]
    dist = (i + BLOCK - s).astype(np.float32)
    mask = (dist >= 0) & (dist < WINDOW)
    first = mask & (s >= BLOCK)
    slopes = _alibi_slopes()[:, None, None]
    reg = np.where(mask[None], -slopes * dist[None], np.float32(NEG_INF))
    fst = np.where(first[None], -slopes * dist[None], np.float32(NEG_INF))
    return jnp.asarray(np.concatenate([reg, fst], axis=0).astype(np.float32))


def _sample_bias(ts, wb):
    i = np.arange(ts)[:, None]
    s = np.arange(wb + ts)[None, :]
    dist = (i + wb - s).astype(np.float32)
    mask = (dist >= 0) & (dist < WINDOW)
    slopes = _alibi_slopes()[:, None, None]
    b = np.where(mask[None], -slopes * dist[None], np.float32(NEG_INF)).astype(np.float32)
    return jnp.asarray(b.reshape(N_HEADS * ts, wb + ts))


def kernel(x_prompt, x_sample, cache_swa_k, cache_swa_v, state_conv, cache_mem_k, cache_mem_v, mem_prompt, norm_mix_g, w_in, q_norm_g, k_norm_g, attn_sinks, conv_dw_w, conv_dw_b, conv_ln_g, conv_ln_b, w_out, norm_xa_g, norm_mem_g, w_mq, w_mk, w_mv, mq_norm_g, mk_norm_g, w_mo, norm_ffn_g, w_router_group, b_router_group, w_router_expert, b_router_expert, w_exp_gate, w_exp_up, w_exp_down):
    depth = w_in.shape[0]
    bp, sp, d = x_prompt.shape
    nb, ts, _ = x_sample.shape
    wb = cache_swa_k.shape[2]
    mlen = mem_prompt.shape[1]
    assert d == D_MODEL and wb == WINDOW and sp % TM_PROMPT == 0 and sp % TM_CROSS == 0 and nb % SEQ_PER_STEP == 0

    bdq = _block_diag(MXU_TILE, HEAD_DIM)
    bdk = _block_diag(KV_DIM, HEAD_DIM)
    bias_p = _prompt_bias()
    bias_s = _sample_bias(ts, wb)
    lane_lo = (np.arange(D_ATTN) % LANES) < HALF
    row = lambda a: a.reshape(1, -1).astype(F32)

    xp = x_prompt
    xs = x_sample.reshape(nb * ts, d)
    kp_l, vp_l, cp_l, mkp_l, mvp_l, ks_l, vs_l, cs_l = [], [], [], [], [], [], [], []
    for l in range(depth):
        gq = jnp.tile(q_norm_g[l].astype(F32), N_HEADS) * (HEAD_DIM ** -0.5)
        gqlo = jnp.where(lane_lo, gq, 0.0).reshape(1, -1)
        gqhi = jnp.where(lane_lo, 0.0, gq).reshape(1, -1)
        gk = jnp.tile(k_norm_g[l].astype(F32), N_KV_HEADS).reshape(1, -1)
        sinks = attn_sinks[l].astype(F32)
        sinkrow = jnp.repeat(sinks, ts).reshape(N_HEADS * ts, 1)
        win = w_in[l].astype(BF16)
        wout = w_out[l].astype(BF16)
        wmq = w_mq[l].astype(BF16)
        wmo = w_mo[l].astype(BF16)
        gmq = (jnp.tile(mq_norm_g[l].astype(F32), MEM_HEADS) * (MEM_HEAD_DIM ** -0.5)).reshape(1, -1)
        gmk = jnp.tile(mk_norm_g[l].astype(F32), MEM_HEADS).reshape(1, -1)
        w_r = jnp.concatenate([w_router_expert[l], w_router_group[l],
                               jnp.zeros((d, LANES - N_EXPERTS - N_GROUPS), F32)], axis=1).astype(BF16)
        b_r = jnp.concatenate([b_router_expert[l], b_router_group[l],
                               jnp.zeros((LANES - N_EXPERTS - N_GROUPS,), F32)]).reshape(1, -1).astype(F32)
        wg, wu, wd = w_exp_gate[l], w_exp_up[l], w_exp_down[l]

        mix_consts = (row(norm_mix_g[l]), win, bdq, bdk, gqlo, gqhi, gk)
        conv_consts = (conv_dw_w[l].astype(F32), row(conv_dw_b[l]), row(conv_ln_g[l]), row(conv_ln_b[l]), wout)
        tail_consts = (wmo, row(norm_ffn_g[l]), w_r, b_r)

        x1p, kp, vp, cp = _mixer_prompt(xp, sinks, mix_consts + (bias_p,) + conv_consts, TM_PROMPT)
        mk, mv = _memory_kv(mem_prompt.reshape(bp * mlen, d), row(norm_mem_g[l]),
                            w_mk[l], w_mv[l], gmk, bp * mlen)
        mk = mk.reshape(bp, mlen, d)
        mv = mv.reshape(bp, mlen, d)
        x2p, hnp, routep, infop, cntp = _cross_prompt(x1p, mk, mv, (row(norm_xa_g[l]), wmq, gmq) + tail_consts
                                               + (_strict_lower(RANK_BLOCK),), TM_CROSS)
        xp, w_bf16 = _moe(x2p.reshape(bp * sp, d), hnp.reshape(bp * sp, d // 2), routep.reshape(bp * sp, LANES),
                          infop, cntp, (wg, wu, wd), TM_ROWS_PROMPT, True)
        xp = xp.reshape(bp, sp, d)
        kp_l.append(kp.reshape(bp, BLOCK, N_KV_HEADS, HEAD_DIM))
        vp_l.append(vp.reshape(bp, BLOCK, N_KV_HEADS, HEAD_DIM))
        cp_l.append(cp)
        mkp_l.append(mk.reshape(bp, mlen, MEM_HEADS, MEM_HEAD_DIM))
        mvp_l.append(mv.reshape(bp, mlen, MEM_HEADS, MEM_HEAD_DIM))

        x1s, qm, ksn, vsn, csn = _mixer_sample(
            xs, _kv_feature_major(cache_swa_k[l]), _kv_feature_major(cache_swa_v[l]),
            jnp.transpose(state_conv[l], (1, 0, 2)),
            mix_consts + (bias_s, sinkrow) + conv_consts + (row(norm_xa_g[l]), wmq, gmq), SEQ_PER_STEP, ts)
        o_s = _mem_attend_sample(qm, _sc_pack_cache(_mem_cache_rows(cache_mem_k[l])),
                                 _sc_pack_cache(_mem_cache_rows(cache_mem_v[l])),
                                 MEM_SEQ_PER_STEP, ts)
        tms = min(256, nb * ts)
        x2s, hns, routes, infos, cnts = _cross_tail_call(o_s, x1s, tail_consts + (_strict_lower(RANK_BLOCK),), tms)
        xs, _ = _moe(x2s, hns, routes, infos, cnts, w_bf16, TM_ROWS_SAMPLE, False)
        ks_l.append(jnp.transpose(ksn.reshape(nb, N_KV_HEADS, HEAD_DIM, wb), (0, 3, 1, 2)))
        vs_l.append(jnp.transpose(vsn.reshape(nb, N_KV_HEADS, HEAD_DIM, wb), (0, 3, 1, 2)))
        cs_l.append(jnp.transpose(csn, (1, 0, 2)))

    st = lambda xs_: jnp.stack(xs_, axis=0)
    return (xp, xs.reshape(nb, ts, d), st(kp_l), st(vp_l), st(cp_l), st(mkp_l), st(mvp_l),
            st(ks_l), st(vs_l), st(cs_l))
```
